```python
import jax, jax.numpy as jnp
from jax import lax
import numpy as np

D_MODEL = 1024
BATCH = 8
SEQ = 4096
DEPTH = 1

MEM_LEN = 256
MIX_WIDTH = D_MODEL
SGU_WIDTH = MIX_WIDTH // 2
SGU_GROUPS = 4
SGU_GROUP_DIM = SGU_WIDTH // SGU_GROUPS
CHUNK = 128
SB_WIDTH = MIX_WIDTH - SGU_WIDTH
SB_HEAD_DIM = 64
SB_HEADS = SB_WIDTH // SB_HEAD_DIM
Q_BLOCK = 128
XA_HEADS = 4
XA_HEAD_DIM = D_MODEL // XA_HEADS
D_FF = ((8 * D_MODEL // 3 + 127) // 128) * 128
IN_COLS = 2 * SGU_WIDTH + 3 * SB_WIDTH
EPS = 1e-6

kernel_name = "hybrid_sgu_stickbreaking_macaron_block"


def rmsnorm(x, g):
    xf = x.astype(jnp.float32)
    y = xf * lax.rsqrt(jnp.mean(xf * xf, axis=-1, keepdims=True) + EPS)
    return (y * g.astype(jnp.float32)).astype(x.dtype)


def swiglu(x, w_gate, w_up, w_down):
    return (jax.nn.silu(x @ w_gate) * (x @ w_up)) @ w_down


def chunked_sgu(u, v, norm_g, norm_b, w_s, b_s):
    B, S, G, Dg = v.shape
    vf = v.astype(jnp.float32)
    mu = jnp.mean(vf, axis=-1, keepdims=True)
    var = jnp.mean((vf - mu) ** 2, axis=-1, keepdims=True)
    vn = ((vf - mu) * lax.rsqrt(var + EPS) * norm_g.astype(jnp.float32)
          + norm_b.astype(jnp.float32)).astype(v.dtype)
    vc = vn.reshape(B, S // CHUNK, CHUNK, G, Dg)
    causal = jnp.tril(jnp.ones((CHUNK, CHUNK), dtype=bool))
    w = jnp.where(causal[None], w_s, jnp.zeros_like(w_s)).astype(v.dtype)
    mixed = jnp.einsum('gts,bcsgd->bctgd', w, vc) + b_s.T.astype(v.dtype)[None, None, :, :, None]
    return u * mixed.reshape(B, S, G, Dg)


def stick_breaking_attention(q, k, v):
    B, H, S, Dh = q.shape
    scale = Dh ** -0.5
    outs = []
    for blk in range(S // Q_BLOCK):
        t0 = blk * Q_BLOCK
        t1 = t0 + Q_BLOCK
        qb = q[:, :, t0:t1]
        kb = k[:, :, :t1]
        vb = v[:, :, :t1]
        z = jnp.einsum('bhtd,bhsd->bhts', qb, kb,
                       preferred_element_type=jnp.float32) * scale
        t_idx = t0 + jnp.arange(Q_BLOCK)[:, None]
        s_idx = jnp.arange(t1)[None, :]
        strict = s_idx < t_idx
        log_beta = jax.nn.log_sigmoid(z)
        log_1m = jnp.where(strict, jax.nn.log_sigmoid(-z), 0.0)
        suffix = lax.cumsum(log_1m, axis=log_1m.ndim - 1, reverse=True) - log_1m
        a = jnp.where(strict, jnp.exp(log_beta + suffix), 0.0)
        outs.append(jnp.einsum('bhts,bhsd->bhtd', a.astype(v.dtype), vb))
    return jnp.concatenate(outs, axis=2)


def memory_cross_attention(x, memn, w_q, w_kv, w_o):
    B, S, D = x.shape
    M = memn.shape[1]
    q = (x @ w_q).reshape(B, S, XA_HEADS, XA_HEAD_DIM)
    kv = (memn @ w_kv).reshape(B, M, 2, XA_HEADS, XA_HEAD_DIM)
    k, v = kv[:, :, 0], kv[:, :, 1]
    logits = jnp.einsum('bshd,bmhd->bhsm', q, k,
                        preferred_element_type=jnp.float32) * (XA_HEAD_DIM ** -0.5)
    p = jax.nn.softmax(logits, axis=-1).astype(v.dtype)
    o = jnp.einsum('bhsm,bmhd->bshd', p, v).reshape(B, S, XA_HEADS * XA_HEAD_DIM)
    return o @ w_o


def _fwd_setup_inputs(seed: int = 0) -> dict:
    key = jax.random.key(seed)
    ks = iter(jax.random.split(key, 40))
    L, D, F = DEPTH, D_MODEL, D_FF

    def nrm(shape, scale):
        return jax.random.normal(next(ks), shape, jnp.float32) * scale

    def gain(shape):
        return 1.0 + nrm(shape, 0.02)

    return {
        "x": nrm((BATCH, SEQ, D), 1.0),
        "mem": nrm((BATCH, MEM_LEN, D), 1.0),
        "ffn1_pre_g": gain((L, D)),
        "ffn1_post_g": gain((L, D)),
        "ffn1_w_gate": nrm((L, D, F), D ** -0.5),
        "ffn1_w_up": nrm((L, D, F), D ** -0.5),
        "ffn1_w_down": nrm((L, F, D), F ** -0.5),
        "mix_pre_g": gain((L, D)),
        "mix_post_g": gain((L, D)),
        "w_in": nrm((L, D, IN_COLS), D ** -0.5),
        "sgu_norm_g": gain((L, SGU_GROUPS, SGU_GROUP_DIM)),
        "sgu_norm_b": nrm((L, SGU_GROUPS, SGU_GROUP_DIM), 0.02),
        "sgu_w_s": nrm((L, SGU_GROUPS, CHUNK, CHUNK), CHUNK ** -0.5),
        "sgu_b_s": 1.0 + nrm((L, SGU_GROUPS, CHUNK), 0.02),
        "sgu_out_g": gain((L, SGU_WIDTH)),
        "sb_out_g": gain((L, SB_WIDTH)),
        "w_out": nrm((L, MIX_WIDTH, D), MIX_WIDTH ** -0.5),
        "xa_pre_g": gain((L, D)),
        "xa_post_g": gain((L, D)),
        "mem_norm_g": gain((L, D)),
        "xa_w_q": nrm((L, D, XA_HEADS * XA_HEAD_DIM), D ** -0.5),
        "xa_w_kv": nrm((L, D, 2 * XA_HEADS * XA_HEAD_DIM), D ** -0.5),
        "xa_w_o": nrm((L, XA_HEADS * XA_HEAD_DIM, D), (XA_HEADS * XA_HEAD_DIM) ** -0.5),
        "ffn2_pre_g": gain((L, D)),
        "ffn2_post_g": gain((L, D)),
        "ffn2_w_gate": nrm((L, D, F), D ** -0.5),
        "ffn2_w_up": nrm((L, D, F), D ** -0.5),
        "ffn2_w_down": nrm((L, F, D), F ** -0.5),
        "final_norm_g": gain((L, D)),
    }


def _fwd_reference(x, mem, ffn1_pre_g, ffn1_post_g, ffn1_w_gate, ffn1_w_up, ffn1_w_down,
              mix_pre_g, mix_post_g, w_in, sgu_norm_g, sgu_norm_b, sgu_w_s, sgu_b_s,
              sgu_out_g, sb_out_g, w_out, xa_pre_g, xa_post_g, mem_norm_g, xa_w_q,
              xa_w_kv, xa_w_o, ffn2_pre_g, ffn2_post_g, ffn2_w_gate, ffn2_w_up,
              ffn2_w_down, final_norm_g):
    B, S, D = x.shape
    splits = [SGU_WIDTH, 2 * SGU_WIDTH, 2 * SGU_WIDTH + SB_WIDTH,
              2 * SGU_WIDTH + 2 * SB_WIDTH]
    h = x
    for l in range(DEPTH):
        f = swiglu(rmsnorm(h, ffn1_pre_g[l]), ffn1_w_gate[l], ffn1_w_up[l], ffn1_w_down[l])
        h = h + 0.5 * rmsnorm(f, ffn1_post_g[l])

        n = rmsnorm(h, mix_pre_g[l])
        proj = n @ w_in[l]
        u, vg, q, k, vs = jnp.split(proj, splits, axis=-1)
        u = jax.nn.gelu(u).reshape(B, S, SGU_GROUPS, SGU_GROUP_DIM)
        vg = jax.nn.gelu(vg).reshape(B, S, SGU_GROUPS, SGU_GROUP_DIM)
        out_a = chunked_sgu(u, vg, sgu_norm_g[l], sgu_norm_b[l],
                            sgu_w_s[l], sgu_b_s[l]).reshape(B, S, SGU_WIDTH)

        def heads(t):
            return t.reshape(B, S, SB_HEADS, SB_HEAD_DIM).transpose(0, 2, 1, 3)
        out_b = stick_breaking_attention(heads(q), heads(k), heads(vs))
        out_b = out_b.transpose(0, 2, 1, 3).reshape(B, S, SB_WIDTH)

        merged = jnp.concatenate([rmsnorm(out_a, sgu_out_g[l]),
                                  rmsnorm(out_b, sb_out_g[l])], axis=-1)
        h = h + rmsnorm(merged @ w_out[l], mix_post_g[l])

        c = memory_cross_attention(rmsnorm(h, xa_pre_g[l]), rmsnorm(mem, mem_norm_g[l]),
                                   xa_w_q[l], xa_w_kv[l], xa_w_o[l])
        h = h + rmsnorm(c, xa_post_g[l])

        f = swiglu(rmsnorm(h, ffn2_pre_g[l]), ffn2_w_gate[l], ffn2_w_up[l], ffn2_w_down[l])
        h = h + 0.5 * rmsnorm(f, ffn2_post_g[l])

        h = rmsnorm(h, final_norm_g[l])
    return h


import jax as _jax
import jax.numpy as _jnp

TWIN_FORMAT = 'train_step'
FWD_PARAMS = ['x', 'mem', 'ffn1_pre_g', 'ffn1_post_g', 'ffn1_w_gate', 'ffn1_w_up', 'ffn1_w_down', 'mix_pre_g', 'mix_post_g', 'w_in', 'sgu_norm_g', 'sgu_norm_b', 'sgu_w_s', 'sgu_b_s', 'sgu_out_g', 'sb_out_g', 'w_out', 'xa_pre_g', 'xa_post_g', 'mem_norm_g', 'xa_w_q', 'xa_w_kv', 'xa_w_o', 'ffn2_pre_g', 'ffn2_post_g', 'ffn2_w_gate', 'ffn2_w_up', 'ffn2_w_down', 'final_norm_g']
TWIN_WEIGHTS = ['ffn1_pre_g', 'ffn1_post_g', 'ffn1_w_gate', 'ffn1_w_up', 'ffn1_w_down', 'mix_pre_g', 'mix_post_g', 'w_in', 'sgu_norm_g', 'sgu_norm_b', 'sgu_w_s', 'sgu_b_s', 'sgu_out_g', 'sb_out_g', 'w_out', 'xa_pre_g', 'xa_post_g', 'mem_norm_g', 'xa_w_q', 'xa_w_kv', 'xa_w_o', 'ffn2_pre_g', 'ffn2_post_g', 'ffn2_w_gate', 'ffn2_w_up', 'ffn2_w_down', 'final_norm_g']
TWIN_DIFF_INPUT = 'x'
TWIN_INPUTS = ['x', 'mem', 'ffn1_pre_g', 'ffn1_post_g', 'ffn1_w_gate', 'ffn1_w_up', 'ffn1_w_down', 'mix_pre_g', 'mix_post_g', 'w_in', 'sgu_norm_g', 'sgu_norm_b', 'sgu_w_s', 'sgu_b_s', 'sgu_out_g', 'sb_out_g', 'w_out', 'xa_pre_g', 'xa_post_g', 'mem_norm_g', 'xa_w_q', 'xa_w_kv', 'xa_w_o', 'ffn2_pre_g', 'ffn2_post_g', 'ffn2_w_gate', 'ffn2_w_up', 'ffn2_w_down', 'final_norm_g', 'loss_target', 'm_ffn1_pre_g', 'm_ffn1_post_g', 'm_ffn1_w_gate', 'm_ffn1_w_up', 'm_ffn1_w_down', 'm_mix_pre_g', 'm_mix_post_g', 'm_w_in', 'm_sgu_norm_g', 'm_sgu_norm_b', 'm_sgu_w_s', 'm_sgu_b_s', 'm_sgu_out_g', 'm_sb_out_g', 'm_w_out', 'm_xa_pre_g', 'm_xa_post_g', 'm_mem_norm_g', 'm_xa_w_q', 'm_xa_w_kv', 'm_xa_w_o', 'm_ffn2_pre_g', 'm_ffn2_post_g', 'm_ffn2_w_gate', 'm_ffn2_w_up', 'm_ffn2_w_down', 'm_final_norm_g', 'v_ffn1_pre_g', 'v_ffn1_post_g', 'v_ffn1_w_gate', 'v_ffn1_w_up', 'v_ffn1_w_down', 'v_mix_pre_g', 'v_mix_post_g', 'v_w_in', 'v_sgu_norm_g', 'v_sgu_norm_b', 'v_sgu_w_s', 'v_sgu_b_s', 'v_sgu_out_g', 'v_sb_out_g', 'v_w_out', 'v_xa_pre_g', 'v_xa_post_g', 'v_mem_norm_g', 'v_xa_w_q', 'v_xa_w_kv', 'v_xa_w_o', 'v_ffn2_pre_g', 'v_ffn2_post_g', 'v_ffn2_w_gate', 'v_ffn2_w_up', 'v_ffn2_w_down', 'v_final_norm_g']
TWIN_OUTPUTS = ['loss', 'grad_x', 'grad_ffn1_pre_g', 'grad_ffn1_post_g', 'grad_ffn1_w_gate', 'grad_ffn1_w_up', 'grad_ffn1_w_down', 'grad_mix_pre_g', 'grad_mix_post_g', 'grad_w_in', 'grad_sgu_norm_g', 'grad_sgu_norm_b', 'grad_sgu_w_s', 'grad_sgu_b_s', 'grad_sgu_out_g', 'grad_sb_out_g', 'grad_w_out', 'grad_xa_pre_g', 'grad_xa_post_g', 'grad_mem_norm_g', 'grad_xa_w_q', 'grad_xa_w_kv', 'grad_xa_w_o', 'grad_ffn2_pre_g', 'grad_ffn2_post_g', 'grad_ffn2_w_gate', 'grad_ffn2_w_up', 'grad_ffn2_w_down', 'grad_final_norm_g', 'delta_ffn1_pre_g', 'delta_ffn1_post_g', 'delta_ffn1_w_gate', 'delta_ffn1_w_up', 'delta_ffn1_w_down', 'delta_mix_pre_g', 'delta_mix_post_g', 'delta_w_in', 'delta_sgu_norm_g', 'delta_sgu_norm_b', 'delta_sgu_w_s', 'delta_sgu_b_s', 'delta_sgu_out_g', 'delta_sb_out_g', 'delta_w_out', 'delta_xa_pre_g', 'delta_xa_post_g', 'delta_mem_norm_g', 'delta_xa_w_q', 'delta_xa_w_kv', 'delta_xa_w_o', 'delta_ffn2_pre_g', 'delta_ffn2_post_g', 'delta_ffn2_w_gate', 'delta_ffn2_w_up', 'delta_ffn2_w_down', 'delta_final_norm_g', 'new_m_ffn1_pre_g', 'new_m_ffn1_post_g', 'new_m_ffn1_w_gate', 'new_m_ffn1_w_up', 'new_m_ffn1_w_down', 'new_m_mix_pre_g', 'new_m_mix_post_g', 'new_m_w_in', 'new_m_sgu_norm_g', 'new_m_sgu_norm_b', 'new_m_sgu_w_s', 'new_m_sgu_b_s', 'new_m_sgu_out_g', 'new_m_sb_out_g', 'new_m_w_out', 'new_m_xa_pre_g', 'new_m_xa_post_g', 'new_m_mem_norm_g', 'new_m_xa_w_q', 'new_m_xa_w_kv', 'new_m_xa_w_o', 'new_m_ffn2_pre_g', 'new_m_ffn2_post_g', 'new_m_ffn2_w_gate', 'new_m_ffn2_w_up', 'new_m_ffn2_w_down', 'new_m_final_norm_g', 'new_v_ffn1_pre_g', 'new_v_ffn1_post_g', 'new_v_ffn1_w_gate', 'new_v_ffn1_w_up', 'new_v_ffn1_w_down', 'new_v_mix_pre_g', 'new_v_mix_post_g', 'new_v_w_in', 'new_v_sgu_norm_g', 'new_v_sgu_norm_b', 'new_v_sgu_w_s', 'new_v_sgu_b_s', 'new_v_sgu_out_g', 'new_v_sb_out_g', 'new_v_w_out', 'new_v_xa_pre_g', 'new_v_xa_post_g', 'new_v_mem_norm_g', 'new_v_xa_w_q', 'new_v_xa_w_kv', 'new_v_xa_w_o', 'new_v_ffn2_pre_g', 'new_v_ffn2_post_g', 'new_v_ffn2_w_gate', 'new_v_ffn2_w_up', 'new_v_ffn2_w_down', 'new_v_final_norm_g']
TWIN_LEAF_KINDS = {'loss': 'loss', 'grad_x': 'grad_x', 'grad_ffn1_pre_g': 'grad_w', 'grad_ffn1_post_g': 'grad_w', 'grad_ffn1_w_gate': 'grad_w', 'grad_ffn1_w_up': 'grad_w', 'grad_ffn1_w_down': 'grad_w', 'grad_mix_pre_g': 'grad_w', 'grad_mix_post_g': 'grad_w', 'grad_w_in': 'grad_w', 'grad_sgu_norm_g': 'grad_w', 'grad_sgu_norm_b': 'grad_w', 'grad_sgu_w_s': 'grad_w', 'grad_sgu_b_s': 'grad_w', 'grad_sgu_out_g': 'grad_w', 'grad_sb_out_g': 'grad_w', 'grad_w_out': 'grad_w', 'grad_xa_pre_g': 'grad_w', 'grad_xa_post_g': 'grad_w', 'grad_mem_norm_g': 'grad_w', 'grad_xa_w_q': 'grad_w', 'grad_xa_w_kv': 'grad_w', 'grad_xa_w_o': 'grad_w', 'grad_ffn2_pre_g': 'grad_w', 'grad_ffn2_post_g': 'grad_w', 'grad_ffn2_w_gate': 'grad_w', 'grad_ffn2_w_up': 'grad_w', 'grad_ffn2_w_down': 'grad_w', 'grad_final_norm_g': 'grad_w', 'delta_ffn1_pre_g': 'delta_w', 'delta_ffn1_post_g': 'delta_w', 'delta_ffn1_w_gate': 'delta_w', 'delta_ffn1_w_up': 'delta_w', 'delta_ffn1_w_down': 'delta_w', 'delta_mix_pre_g': 'delta_w', 'delta_mix_post_g': 'delta_w', 'delta_w_in': 'delta_w', 'delta_sgu_norm_g': 'delta_w', 'delta_sgu_norm_b': 'delta_w', 'delta_sgu_w_s': 'delta_w', 'delta_sgu_b_s': 'delta_w', 'delta_sgu_out_g': 'delta_w', 'delta_sb_out_g': 'delta_w', 'delta_w_out': 'delta_w', 'delta_xa_pre_g': 'delta_w', 'delta_xa_post_g': 'delta_w', 'delta_mem_norm_g': 'delta_w', 'delta_xa_w_q': 'delta_w', 'delta_xa_w_kv': 'delta_w', 'delta_xa_w_o': 'delta_w', 'delta_ffn2_pre_g': 'delta_w', 'delta_ffn2_post_g': 'delta_w', 'delta_ffn2_w_gate': 'delta_w', 'delta_ffn2_w_up': 'delta_w', 'delta_ffn2_w_down': 'delta_w', 'delta_final_norm_g': 'delta_w', 'new_m_ffn1_pre_g': 'new_m', 'new_m_ffn1_post_g': 'new_m', 'new_m_ffn1_w_gate': 'new_m', 'new_m_ffn1_w_up': 'new_m', 'new_m_ffn1_w_down': 'new_m', 'new_m_mix_pre_g': 'new_m', 'new_m_mix_post_g': 'new_m', 'new_m_w_in': 'new_m', 'new_m_sgu_norm_g': 'new_m', 'new_m_sgu_norm_b': 'new_m', 'new_m_sgu_w_s': 'new_m', 'new_m_sgu_b_s': 'new_m', 'new_m_sgu_out_g': 'new_m', 'new_m_sb_out_g': 'new_m', 'new_m_w_out': 'new_m', 'new_m_xa_pre_g': 'new_m', 'new_m_xa_post_g': 'new_m', 'new_m_mem_norm_g': 'new_m', 'new_m_xa_w_q': 'new_m', 'new_m_xa_w_kv': 'new_m', 'new_m_xa_w_o': 'new_m', 'new_m_ffn2_pre_g': 'new_m', 'new_m_ffn2_post_g': 'new_m', 'new_m_ffn2_w_gate': 'new_m', 'new_m_ffn2_w_up': 'new_m', 'new_m_ffn2_w_down': 'new_m', 'new_m_final_norm_g': 'new_m', 'new_v_ffn1_pre_g': 'new_v', 'new_v_ffn1_post_g': 'new_v', 'new_v_ffn1_w_gate': 'new_v', 'new_v_ffn1_w_up': 'new_v', 'new_v_ffn1_w_down': 'new_v', 'new_v_mix_pre_g': 'new_v', 'new_v_mix_post_g': 'new_v', 'new_v_w_in': 'new_v', 'new_v_sgu_norm_g': 'new_v', 'new_v_sgu_norm_b': 'new_v', 'new_v_sgu_w_s': 'new_v', 'new_v_sgu_b_s': 'new_v', 'new_v_sgu_out_g': 'new_v', 'new_v_sb_out_g': 'new_v', 'new_v_w_out': 'new_v', 'new_v_xa_pre_g': 'new_v', 'new_v_xa_post_g': 'new_v', 'new_v_mem_norm_g': 'new_v', 'new_v_xa_w_q': 'new_v', 'new_v_xa_w_kv': 'new_v', 'new_v_xa_w_o': 'new_v', 'new_v_ffn2_pre_g': 'new_v', 'new_v_ffn2_post_g': 'new_v', 'new_v_ffn2_w_gate': 'new_v', 'new_v_ffn2_w_up': 'new_v', 'new_v_ffn2_w_down': 'new_v', 'new_v_final_norm_g': 'new_v'}


def _forward(args):
    return _fwd_reference(*[args[k] for k in FWD_PARAMS])


def _output_shape():
    def fwd():
        inp = _fwd_setup_inputs(0)
        return _fwd_reference(*[inp[k] for k in FWD_PARAMS])
    out = _jax.eval_shape(fwd)
    return out.shape, out.dtype

N_MICROBATCH = 1
ADAM_LR = 0.001
ADAM_B1 = 0.9
ADAM_B2 = 0.999
ADAM_EPS = 1e-08
ADAM_WD = 0.01
ADAM_STEP = 10
PER_EXAMPLE_BATCH_AXIS = {'x': 0, 'mem': 0, 'loss_target': 0}
SHARED_INPUTS = []
_WEIGHT_DTYPES = {'ffn1_pre_g': _jnp.float32, 'ffn1_post_g': _jnp.float32, 'ffn1_w_gate': _jnp.float32, 'ffn1_w_up': _jnp.float32, 'ffn1_w_down': _jnp.float32, 'mix_pre_g': _jnp.float32, 'mix_post_g': _jnp.float32, 'w_in': _jnp.float32, 'sgu_norm_g': _jnp.float32, 'sgu_norm_b': _jnp.float32, 'sgu_w_s': _jnp.float32, 'sgu_b_s': _jnp.float32, 'sgu_out_g': _jnp.float32, 'sb_out_g': _jnp.float32, 'w_out': _jnp.float32, 'xa_pre_g': _jnp.float32, 'xa_post_g': _jnp.float32, 'mem_norm_g': _jnp.float32, 'xa_w_q': _jnp.float32, 'xa_w_kv': _jnp.float32, 'xa_w_o': _jnp.float32, 'ffn2_pre_g': _jnp.float32, 'ffn2_post_g': _jnp.float32, 'ffn2_w_gate': _jnp.float32, 'ffn2_w_up': _jnp.float32, 'ffn2_w_down': _jnp.float32, 'final_norm_g': _jnp.float32}
MOMENT_SCALE = {'ffn1_pre_g': 1.338146e-01, 'ffn1_post_g': 9.883798e-02, 'ffn1_w_gate': 5.590860e-02, 'ffn1_w_up': 5.406417e-02, 'ffn1_w_down': 8.961814e-02, 'mix_pre_g': 1.467427e-01, 'mix_post_g': 3.119095e-01, 'w_in': 9.324200e-02, 'sgu_norm_g': 6.905904e-02, 'sgu_norm_b': 7.109363e-02, 'sgu_w_s': 6.746692e-02, 'sgu_b_s': 9.083312e-02, 'sgu_out_g': 1.288201e-01, 'sb_out_g': 1.210293e-01, 'w_out': 1.244877e-01, 'xa_pre_g': 1.005191e-01, 'xa_post_g': 3.224249e-01, 'mem_norm_g': 1.632741e-01, 'xa_w_q': 9.968138e-02, 'xa_w_kv': 1.039835e-01, 'xa_w_o': 1.082893e-01, 'ffn2_pre_g': 7.037792e-02, 'ffn2_post_g': 8.208632e-02, 'ffn2_w_gate': 2.952231e-02, 'ffn2_w_up': 2.877565e-02, 'ffn2_w_down': 4.772754e-02, 'final_norm_g': 3.205982e+01}


def _to_microbatches(a, axis):
    t = _jnp.moveaxis(a, axis, 0)
    t = t.reshape((N_MICROBATCH, t.shape[0] // N_MICROBATCH) + t.shape[1:])
    return _jnp.moveaxis(t, 1, axis + 1)


def setup_inputs(seed: int = 0) -> dict:
    inp = _fwd_setup_inputs(seed)
    key = _jax.random.fold_in(_jax.random.key(seed), 7919)
    shape, _ = _output_shape()
    out = dict(inp)
    out["loss_target"] = _jax.random.normal(_jax.random.fold_in(key, 0), shape, _jnp.float32)
    for i, name in enumerate(TWIN_WEIGHTS):
        w = inp[name].astype(_jnp.float32)
        if MOMENT_SCALE is None:
            s = _jnp.sqrt(_jnp.mean(_jnp.square(w)) + 1e-30)
        else:
            s = MOMENT_SCALE[name]
        km, kv = _jax.random.split(_jax.random.fold_in(key, i + 1))
        out[name] = w
        out["m_" + name] = s * _jax.random.normal(km, w.shape, _jnp.float32)
        out["v_" + name] = (s * s) * _jax.random.uniform(kv, w.shape, _jnp.float32, 0.5, 1.5)
    if N_MICROBATCH > 1:
        for name, axis in PER_EXAMPLE_BATCH_AXIS.items():
            out[name] = _to_microbatches(out[name], axis)
    return {'x': out['x'], 'mem': out['mem'], 'ffn1_pre_g': out['ffn1_pre_g'], 'ffn1_post_g': out['ffn1_post_g'], 'ffn1_w_gate': out['ffn1_w_gate'], 'ffn1_w_up': out['ffn1_w_up'], 'ffn1_w_down': out['ffn1_w_down'], 'mix_pre_g': out['mix_pre_g'], 'mix_post_g': out['mix_post_g'], 'w_in': out['w_in'], 'sgu_norm_g': out['sgu_norm_g'], 'sgu_norm_b': out['sgu_norm_b'], 'sgu_w_s': out['sgu_w_s'], 'sgu_b_s': out['sgu_b_s'], 'sgu_out_g': out['sgu_out_g'], 'sb_out_g': out['sb_out_g'], 'w_out': out['w_out'], 'xa_pre_g': out['xa_pre_g'], 'xa_post_g': out['xa_post_g'], 'mem_norm_g': out['mem_norm_g'], 'xa_w_q': out['xa_w_q'], 'xa_w_kv': out['xa_w_kv'], 'xa_w_o': out['xa_w_o'], 'ffn2_pre_g': out['ffn2_pre_g'], 'ffn2_post_g': out['ffn2_post_g'], 'ffn2_w_gate': out['ffn2_w_gate'], 'ffn2_w_up': out['ffn2_w_up'], 'ffn2_w_down': out['ffn2_w_down'], 'final_norm_g': out['final_norm_g'], 'loss_target': out['loss_target'], 'm_ffn1_pre_g': out['m_ffn1_pre_g'], 'm_ffn1_post_g': out['m_ffn1_post_g'], 'm_ffn1_w_gate': out['m_ffn1_w_gate'], 'm_ffn1_w_up': out['m_ffn1_w_up'], 'm_ffn1_w_down': out['m_ffn1_w_down'], 'm_mix_pre_g': out['m_mix_pre_g'], 'm_mix_post_g': out['m_mix_post_g'], 'm_w_in': out['m_w_in'], 'm_sgu_norm_g': out['m_sgu_norm_g'], 'm_sgu_norm_b': out['m_sgu_norm_b'], 'm_sgu_w_s': out['m_sgu_w_s'], 'm_sgu_b_s': out['m_sgu_b_s'], 'm_sgu_out_g': out['m_sgu_out_g'], 'm_sb_out_g': out['m_sb_out_g'], 'm_w_out': out['m_w_out'], 'm_xa_pre_g': out['m_xa_pre_g'], 'm_xa_post_g': out['m_xa_post_g'], 'm_mem_norm_g': out['m_mem_norm_g'], 'm_xa_w_q': out['m_xa_w_q'], 'm_xa_w_kv': out['m_xa_w_kv'], 'm_xa_w_o': out['m_xa_w_o'], 'm_ffn2_pre_g': out['m_ffn2_pre_g'], 'm_ffn2_post_g': out['m_ffn2_post_g'], 'm_ffn2_w_gate': out['m_ffn2_w_gate'], 'm_ffn2_w_up': out['m_ffn2_w_up'], 'm_ffn2_w_down': out['m_ffn2_w_down'], 'm_final_norm_g': out['m_final_norm_g'], 'v_ffn1_pre_g': out['v_ffn1_pre_g'], 'v_ffn1_post_g': out['v_ffn1_post_g'], 'v_ffn1_w_gate': out['v_ffn1_w_gate'], 'v_ffn1_w_up': out['v_ffn1_w_up'], 'v_ffn1_w_down': out['v_ffn1_w_down'], 'v_mix_pre_g': out['v_mix_pre_g'], 'v_mix_post_g': out['v_mix_post_g'], 'v_w_in': out['v_w_in'], 'v_sgu_norm_g': out['v_sgu_norm_g'], 'v_sgu_norm_b': out['v_sgu_norm_b'], 'v_sgu_w_s': out['v_sgu_w_s'], 'v_sgu_b_s': out['v_sgu_b_s'], 'v_sgu_out_g': out['v_sgu_out_g'], 'v_sb_out_g': out['v_sb_out_g'], 'v_w_out': out['v_w_out'], 'v_xa_pre_g': out['v_xa_pre_g'], 'v_xa_post_g': out['v_xa_post_g'], 'v_mem_norm_g': out['v_mem_norm_g'], 'v_xa_w_q': out['v_xa_w_q'], 'v_xa_w_kv': out['v_xa_w_kv'], 'v_xa_w_o': out['v_xa_w_o'], 'v_ffn2_pre_g': out['v_ffn2_pre_g'], 'v_ffn2_post_g': out['v_ffn2_post_g'], 'v_ffn2_w_gate': out['v_ffn2_w_gate'], 'v_ffn2_w_up': out['v_ffn2_w_up'], 'v_ffn2_w_down': out['v_ffn2_w_down'], 'v_final_norm_g': out['v_final_norm_g']}


def _loss(weights, diff, rest, loss_target):
    with _jax.named_scope("forward"):
        args = {**rest, TWIN_DIFF_INPUT: diff, **{k: w.astype(_WEIGHT_DTYPES[k]) for k, w in weights.items()}}
        y = _forward(args)
    with _jax.named_scope("loss_head"):
        err = _jnp.square(y.astype(_jnp.float32) - loss_target)
        return 0.5 * _jnp.sum(_jnp.mean(err, axis=-1)) if err.ndim else 0.5 * err


def _adamw(w, g, m, v):
    m = ADAM_B1 * m + (1.0 - ADAM_B1) * g
    v = ADAM_B2 * v + (1.0 - ADAM_B2) * _jnp.square(g)
    m_hat = m / (1.0 - ADAM_B1 ** ADAM_STEP)
    v_hat = v / (1.0 - ADAM_B2 ** ADAM_STEP)
    delta = -ADAM_LR * (m_hat / (_jnp.sqrt(v_hat) + ADAM_EPS) + ADAM_WD * w)
    return delta, m, v


def reference(x, mem, ffn1_pre_g, ffn1_post_g, ffn1_w_gate, ffn1_w_up, ffn1_w_down, mix_pre_g, mix_post_g, w_in, sgu_norm_g, sgu_norm_b, sgu_w_s, sgu_b_s, sgu_out_g, sb_out_g, w_out, xa_pre_g, xa_post_g, mem_norm_g, xa_w_q, xa_w_kv, xa_w_o, ffn2_pre_g, ffn2_post_g, ffn2_w_gate, ffn2_w_up, ffn2_w_down, final_norm_g, loss_target, m_ffn1_pre_g, m_ffn1_post_g, m_ffn1_w_gate, m_ffn1_w_up, m_ffn1_w_down, m_mix_pre_g, m_mix_post_g, m_w_in, m_sgu_norm_g, m_sgu_norm_b, m_sgu_w_s, m_sgu_b_s, m_sgu_out_g, m_sb_out_g, m_w_out, m_xa_pre_g, m_xa_post_g, m_mem_norm_g, m_xa_w_q, m_xa_w_kv, m_xa_w_o, m_ffn2_pre_g, m_ffn2_post_g, m_ffn2_w_gate, m_ffn2_w_up, m_ffn2_w_down, m_final_norm_g, v_ffn1_pre_g, v_ffn1_post_g, v_ffn1_w_gate, v_ffn1_w_up, v_ffn1_w_down, v_mix_pre_g, v_mix_post_g, v_w_in, v_sgu_norm_g, v_sgu_norm_b, v_sgu_w_s, v_sgu_b_s, v_sgu_out_g, v_sb_out_g, v_w_out, v_xa_pre_g, v_xa_post_g, v_mem_norm_g, v_xa_w_q, v_xa_w_kv, v_xa_w_o, v_ffn2_pre_g, v_ffn2_post_g, v_ffn2_w_gate, v_ffn2_w_up, v_ffn2_w_down, v_final_norm_g):
    given = dict(x=x, mem=mem, ffn1_pre_g=ffn1_pre_g, ffn1_post_g=ffn1_post_g, ffn1_w_gate=ffn1_w_gate, ffn1_w_up=ffn1_w_up, ffn1_w_down=ffn1_w_down, mix_pre_g=mix_pre_g, mix_post_g=mix_post_g, w_in=w_in, sgu_norm_g=sgu_norm_g, sgu_norm_b=sgu_norm_b, sgu_w_s=sgu_w_s, sgu_b_s=sgu_b_s, sgu_out_g=sgu_out_g, sb_out_g=sb_out_g, w_out=w_out, xa_pre_g=xa_pre_g, xa_post_g=xa_post_g, mem_norm_g=mem_norm_g, xa_w_q=xa_w_q, xa_w_kv=xa_w_kv, xa_w_o=xa_w_o, ffn2_pre_g=ffn2_pre_g, ffn2_post_g=ffn2_post_g, ffn2_w_gate=ffn2_w_gate, ffn2_w_up=ffn2_w_up, ffn2_w_down=ffn2_w_down, final_norm_g=final_norm_g, loss_target=loss_target, m_ffn1_pre_g=m_ffn1_pre_g, m_ffn1_post_g=m_ffn1_post_g, m_ffn1_w_gate=m_ffn1_w_gate, m_ffn1_w_up=m_ffn1_w_up, m_ffn1_w_down=m_ffn1_w_down, m_mix_pre_g=m_mix_pre_g, m_mix_post_g=m_mix_post_g, m_w_in=m_w_in, m_sgu_norm_g=m_sgu_norm_g, m_sgu_norm_b=m_sgu_norm_b, m_sgu_w_s=m_sgu_w_s, m_sgu_b_s=m_sgu_b_s, m_sgu_out_g=m_sgu_out_g, m_sb_out_g=m_sb_out_g, m_w_out=m_w_out, m_xa_pre_g=m_xa_pre_g, m_xa_post_g=m_xa_post_g, m_mem_norm_g=m_mem_norm_g, m_xa_w_q=m_xa_w_q, m_xa_w_kv=m_xa_w_kv, m_xa_w_o=m_xa_w_o, m_ffn2_pre_g=m_ffn2_pre_g, m_ffn2_post_g=m_ffn2_post_g, m_ffn2_w_gate=m_ffn2_w_gate, m_ffn2_w_up=m_ffn2_w_up, m_ffn2_w_down=m_ffn2_w_down, m_final_norm_g=m_final_norm_g, v_ffn1_pre_g=v_ffn1_pre_g, v_ffn1_post_g=v_ffn1_post_g, v_ffn1_w_gate=v_ffn1_w_gate, v_ffn1_w_up=v_ffn1_w_up, v_ffn1_w_down=v_ffn1_w_down, v_mix_pre_g=v_mix_pre_g, v_mix_post_g=v_mix_post_g, v_w_in=v_w_in, v_sgu_norm_g=v_sgu_norm_g, v_sgu_norm_b=v_sgu_norm_b, v_sgu_w_s=v_sgu_w_s, v_sgu_b_s=v_sgu_b_s, v_sgu_out_g=v_sgu_out_g, v_sb_out_g=v_sb_out_g, v_w_out=v_w_out, v_xa_pre_g=v_xa_pre_g, v_xa_post_g=v_xa_post_g, v_mem_norm_g=v_mem_norm_g, v_xa_w_q=v_xa_w_q, v_xa_w_kv=v_xa_w_kv, v_xa_w_o=v_xa_w_o, v_ffn2_pre_g=v_ffn2_pre_g, v_ffn2_post_g=v_ffn2_post_g, v_ffn2_w_gate=v_ffn2_w_gate, v_ffn2_w_up=v_ffn2_w_up, v_ffn2_w_down=v_ffn2_w_down, v_final_norm_g=v_final_norm_g)
    weights = {n: given[n] for n in TWIN_WEIGHTS}
    shared = {n: given[n] for n in SHARED_INPUTS}
    per_example = {n: given[n] for n in ['x', 'mem']}
    grad_fn = _jax.value_and_grad(_loss, argnums=(0, 1))

    def one_microbatch(ex, loss_target):
        ex = dict(ex)
        diff = ex.pop(TWIN_DIFF_INPUT)
        return grad_fn(weights, diff, {**shared, **ex}, loss_target)

    if N_MICROBATCH == 1:
        loss, (grad_w, grad_x) = one_microbatch(per_example, given["loss_target"])
    else:
        def body(carry, xs):
            loss_sum, grad_sum = carry
            l_k, (gw_k, gx_k) = one_microbatch(xs[0], xs[1])
            with _jax.named_scope("update"):
                return (loss_sum + l_k, _jax.tree.map(_jnp.add, grad_sum, gw_k)), gx_k

        init = (_jnp.zeros((), _jnp.float32), _jax.tree.map(_jnp.zeros_like, weights))
        (loss, grad_w), grad_x = _jax.lax.scan(body, init, (per_example, given["loss_target"]))
    with _jax.named_scope("update"):
        delta_w, new_m, new_v = {}, {}, {}
        for n in TWIN_WEIGHTS:
            delta_w[n], new_m[n], new_v[n] = _adamw(weights[n], grad_w[n], given["m_" + n], given["v_" + n])
    return (loss, grad_x, *[grad_w[n] for n in TWIN_WEIGHTS], *[delta_w[n] for n in TWIN_WEIGHTS],
            *[new_m[n] for n in TWIN_WEIGHTS], *[new_v[n] for n in TWIN_WEIGHTS])
```

```python
import functools
import math

import jax
import jax.numpy as jnp
from jax import lax
from jax.experimental import pallas as pl
from jax.experimental.pallas import tpu as pltpu

F32 = jnp.float32
BF = jnp.bfloat16
EPS = 1e-6
D_MODEL = 1024
N_CHIPS = 4
FF_BLOCK = 704
IN_BLOCK = 640
KV_BLOCK = 512
ROW_BLOCK = 256
SGU_GROUPS = 4
CHUNK = 128
SB_HEAD_DIM = 64
SB_SCALE = SB_HEAD_DIM ** -0.5
XA_HEADS = 4
XA_HEAD_DIM = 256
XA_SCALE = XA_HEAD_DIM ** -0.5
LANES = 128
VMEM_LIMIT = 56 * 1024 * 1024
MESH = pl.DeviceIdType.MESH

ADAM_LR = 0.001
ADAM_B1 = 0.9
ADAM_B2 = 0.999
ADAM_EPS = 1e-08
ADAM_WD = 0.01
ADAM_STEP = 10

_GELU_C = math.sqrt(2.0 / math.pi)
_GELU_A = 0.044715


def _cp(*sem):
    return pltpu.CompilerParams(dimension_semantics=sem, vmem_limit_bytes=VMEM_LIMIT)


def _call(body, **kw):
    return pl.pallas_call(body, **kw)


def _dot(a, b):
    return jnp.dot(a, b, preferred_element_type=F32)


def _dot_nt(a, b):
    return lax.dot_general(a, b, (((1,), (1,)), ((), ())), preferred_element_type=F32)


def _dot_tn(a, b):
    return lax.dot_general(a, b, (((0,), (0,)), ((), ())), preferred_element_type=F32)


def _rstd(x):
    return lax.rsqrt(jnp.mean(x * x, axis=-1, keepdims=True) + EPS)


def _rms_bwd(x, g, dy):
    r = _rstd(x)
    xh = x * r
    gd = dy * g
    dx = r * (gd - xh * jnp.mean(gd * xh, axis=-1, keepdims=True))
    dg = jnp.sum(dy * xh, axis=0, keepdims=True)
    return dx, dg


def _gelu(x):
    return 0.5 * x * (1.0 + jnp.tanh(_GELU_C * (x + _GELU_A * (x * x * x))))


def _gelu_grad(x):
    t = jnp.tanh(_GELU_C * (x + _GELU_A * (x * x * x)))
    return 0.5 * (1.0 + t) + 0.5 * x * (1.0 - t * t) * (_GELU_C * (1.0 + 3.0 * _GELU_A * x * x))


def _split3(x):
    hi = x.astype(BF)
    r1 = x - hi.astype(F32)
    mid = r1.astype(BF)
    lo = (r1 - mid.astype(F32)).astype(BF)
    return hi, mid, lo


def _dot3(x, ones_mat):
    hi, mid, lo = _split3(x)
    return _dot(hi, ones_mat) + _dot(mid, ones_mat) + _dot(lo, ones_mat)


def _row_tile(m, want=512):
    return min(want, m)


def rms_fwd(x, g):
    m, d = x.shape
    tm = _row_tile(m)

    def body(x_ref, g_ref, o_ref):
        xv = x_ref[...]
        o_ref[...] = (xv * _rstd(xv) * g_ref[...]).astype(BF)

    return _call(
        body, name="rms_fwd", out_shape=jax.ShapeDtypeStruct((m, d), BF), grid=(m // tm,),
        in_specs=[pl.BlockSpec((tm, d), lambda i: (i, 0)), pl.BlockSpec((1, d), lambda i: (0, 0))],
        out_specs=pl.BlockSpec((tm, d), lambda i: (i, 0)), compiler_params=_cp("parallel"),
    )(x, g)


def ffn_up(a, ga, ig, iu):
    m, d = a.shape
    tm = _row_tile(m)
    nb = ga.shape[-1]

    def body(a_ref, wg_ref, wu_ref, g_ref, u_ref, h_ref):
        av = a_ref[...]
        g = _dot(av, wg_ref[...])
        u = _dot(av, wu_ref[...])
        g_ref[...] = g.astype(BF)
        u_ref[...] = u.astype(BF)
        h_ref[...] = (g * jax.nn.sigmoid(g) * u).astype(BF)

    blk = jax.ShapeDtypeStruct((N_CHIPS, m, nb), BF)
    ospec = pl.BlockSpec((None, tm, nb), lambda k, i: (k, i, 0))
    return _call(
        body, name="ffn_up", out_shape=(blk, blk, blk), grid=(N_CHIPS, m // tm),
        in_specs=[pl.BlockSpec((tm, d), lambda k, i: (i, 0)),
                  pl.BlockSpec((None, None, d, nb), lambda k, i: (k, ig, 0, 0)),
                  pl.BlockSpec((None, None, d, nb), lambda k, i: (k, iu, 0, 0))],
        out_specs=(ospec, ospec, ospec), compiler_params=_cp("parallel", "parallel"),
    )(a, ga, ga)


def mm_res(lhs, w, which, h, gp, alpha, gn):
    blocked = lhs.ndim == 3
    m = lhs.shape[1] if blocked else lhs.shape[0]
    kb, n = w.shape[2], w.shape[3]
    tm = _row_tile(m)

    def body(l_ref, w_ref, h_ref, gp_ref, gn_ref, f_ref, hn_ref, an_ref):
        acc = None
        for k in range(N_CHIPS):
            lk = l_ref[k] if blocked else l_ref[:, k * kb:(k + 1) * kb]
            t = _dot(lk, w_ref[k])
            acc = t if acc is None else acc + t
        f_ref[...] = acc
        hn = h_ref[...] + alpha * (acc * _rstd(acc) * gp_ref[...])
        hn_ref[...] = hn
        an_ref[...] = (hn * _rstd(hn) * gn_ref[...]).astype(BF)

    lspec = (pl.BlockSpec((N_CHIPS, tm, kb), lambda i: (0, i, 0)) if blocked
             else pl.BlockSpec((tm, N_CHIPS * kb), lambda i: (i, 0)))
    row = pl.BlockSpec((tm, n), lambda i: (i, 0))
    vec = pl.BlockSpec((1, n), lambda i: (0, 0))
    return _call(
        body, name="mm_res", grid=(m // tm,),
        out_shape=(jax.ShapeDtypeStruct((m, n), F32), jax.ShapeDtypeStruct((m, n), F32),
                   jax.ShapeDtypeStruct((m, n), BF)),
        in_specs=[lspec, pl.BlockSpec((N_CHIPS, None, kb, n), lambda i: (0, which, 0, 0)), row, vec, vec],
        out_specs=(row, row, row), compiler_params=_cp("parallel"),
    )(lhs, w, h, gp, gn)


def mm_cb(a, w, out_dtype=BF):
    m, kd = a.shape
    nb = w.shape[-1]
    tm = _row_tile(m)

    def body(a_ref, w_ref, o_ref):
        o_ref[...] = _dot(a_ref[...], w_ref[...]).astype(out_dtype)

    return _call(
        body, name="mm_cb", out_shape=jax.ShapeDtypeStruct((m, N_CHIPS * nb), out_dtype),
        grid=(N_CHIPS, m // tm),
        in_specs=[pl.BlockSpec((tm, kd), lambda k, i: (i, 0)), pl.BlockSpec((None, kd, nb), lambda k, i: (k, 0, 0))],
        out_specs=pl.BlockSpec((tm, nb), lambda k, i: (i, k)), compiler_params=_cp("parallel", "parallel"),
    )(a, w)


def _sgu_core(u_pre, vg_pre, wm_ref, ng_ref, nb_ref, bst_ref, g, c):
    rs = slice(c * CHUNK, (c + 1) * CHUNK)
    cs = slice(g * CHUNK, (g + 1) * CHUNK)
    ug = _gelu(u_pre[rs, cs].astype(F32))
    vgl = _gelu(vg_pre[rs, cs].astype(F32))
    mu = jnp.mean(vgl, axis=-1, keepdims=True)
    cen = vgl - mu
    rstd = lax.rsqrt(jnp.mean(cen * cen, axis=-1, keepdims=True) + EPS)
    xh = cen * rstd
    vn = xh * ng_ref[g:g + 1, :] + nb_ref[g:g + 1, :]
    mixed = _dot(wm_ref[g], vn.astype(BF)) + bst_ref[g]
    return ug, xh, rstd, vn, mixed


def sgu_fwd(proj, wm, ng, nb, bst):
    m = proj.shape[0]
    tm = _row_tile(m)
    wd = SGU_GROUPS * CHUNK

    def body(u_ref, v_ref, wm_ref, ng_ref, nb_ref, bst_ref, o_ref):
        for c in range(tm // CHUNK):
            for g in range(SGU_GROUPS):
                ug, _, _, _, mixed = _sgu_core(u_ref, v_ref, wm_ref, ng_ref, nb_ref, bst_ref, g, c)
                o_ref[c * CHUNK:(c + 1) * CHUNK, g * CHUNK:(g + 1) * CHUNK] = (ug * mixed).astype(BF)

    full = lambda shape: pl.BlockSpec(shape, lambda i: (0,) * len(shape))
    return _call(
        body, name="sgu_fwd", out_shape=jax.ShapeDtypeStruct((m, wd), BF), grid=(m // tm,),
        in_specs=[pl.BlockSpec((tm, wd), lambda i: (i, 0)), pl.BlockSpec((tm, wd), lambda i: (i, 1)),
                  full(wm.shape), full(ng.shape), full(nb.shape), full(bst.shape)],
        out_specs=pl.BlockSpec((tm, wd), lambda i: (i, 0)), compiler_params=_cp("parallel"),
    )(proj, proj, wm, ng, nb, bst)


def _sb_tiles(m):
    tq = min(512, m)
    tk = min(256, m)
    return tq, tk


def _log_sigmoid_pair(z, mask):
    ls = jnp.minimum(z, 0.0) - jnp.log(1.0 + jnp.exp(-jnp.abs(z)))
    return ls, jnp.where(mask, ls - z, 0.0)


def sb_fwd(proj):
    m = proj.shape[0]
    tq, tk = _sb_tiles(m)

    def body(q_ref, k_ref, v_ref, o_ref, tot_ref, acc, car):
        i = pl.program_id(1)
        nblk = (i + 1) * (tq // tk)
        lane = lax.broadcasted_iota(jnp.int32, (1, LANES), 1)
        rows = i * tq + lax.broadcasted_iota(jnp.int32, (tq, tk), 0)
        cols = lax.broadcasted_iota(jnp.int32, (tq, tk), 1)
        upper = (lax.broadcasted_iota(jnp.int32, (tk, tk), 0) > lax.broadcasted_iota(jnp.int32, (tk, tk), 1)).astype(BF)
        qv = q_ref[...]
        o_all = jnp.zeros((tq, LANES), F32)
        t_all = jnp.zeros((tq, LANES), F32)
        for hd in range(2):
            mh = (lane >= SB_HEAD_DIM) if hd else (lane < SB_HEAD_DIM)
            qm = jnp.where(mh, qv, jnp.zeros_like(qv)) * SB_SCALE
            acc[...] = jnp.zeros_like(acc)
            car[...] = jnp.zeros_like(car)

            def step(jj, carry):
                ks = pl.multiple_of((nblk - 1 - jj) * tk, tk)
                kb = k_ref[pl.ds(ks, tk), :]
                vb = v_ref[pl.ds(ks, tk), :]
                z = _dot_nt(qm, kb)
                mask = (cols + ks) < rows
                ls, l1 = _log_sigmoid_pair(z, mask)
                suffix = _dot3(l1, upper) + car[...]
                a = jnp.where(mask, jnp.exp(ls + suffix), 0.0)
                acc[...] += _dot(a.astype(BF), vb)
                car[...] += jnp.sum(l1, axis=-1, keepdims=True)
                return carry

            lax.fori_loop(0, nblk, step, 0)
            o_all = jnp.where(mh, acc[...], o_all)
            t_all = jnp.where(mh, car[...], t_all)
        o_ref[...] = o_all.astype(BF)
        tot_ref[...] = t_all

    qb = 2 * 512 // LANES
    return _call(
        body, name="sb_fwd", grid=(4, m // tq),
        out_shape=(jax.ShapeDtypeStruct((m, 512), BF), jax.ShapeDtypeStruct((m, 512), F32)),
        in_specs=[pl.BlockSpec((tq, LANES), lambda p, i: (i, qb + p)),
                  pl.BlockSpec((m, LANES), lambda p, i: (0, qb + 4 + p)),
                  pl.BlockSpec((m, LANES), lambda p, i: (0, qb + 8 + p))],
        out_specs=(pl.BlockSpec((tq, LANES), lambda p, i: (i, p)), pl.BlockSpec((tq, LANES), lambda p, i: (i, p))),
        scratch_shapes=[pltpu.VMEM((tq, LANES), F32), pltpu.VMEM((tq, 1), F32)],
        compiler_params=_cp("parallel", "arbitrary"),
    )(proj, proj, proj)


def merge_norm(oa, ob, ga, gb):
    m, w = oa.shape
    tm = _row_tile(m)

    def body(a_ref, b_ref, ga_ref, gb_ref, o_ref):
        av = a_ref[...].astype(F32)
        bv = b_ref[...].astype(F32)
        o_ref[:, :w] = (av * _rstd(av) * ga_ref[...]).astype(BF)
        o_ref[:, w:] = (bv * _rstd(bv) * gb_ref[...]).astype(BF)

    row = pl.BlockSpec((tm, w), lambda i: (i, 0))
    vec = pl.BlockSpec((1, w), lambda i: (0, 0))
    return _call(
        body, name="merge_norm", out_shape=jax.ShapeDtypeStruct((m, 2 * w), BF), grid=(m // tm,),
        in_specs=[row, row, vec, vec], out_specs=pl.BlockSpec((tm, 2 * w), lambda i: (i, 0)),
        compiler_params=_cp("parallel"),
    )(oa, ob, ga, gb)


def _xa_probs(qh, kh):
    logits = _dot_nt(qh, kh) * XA_SCALE
    e = jnp.exp(logits - jnp.max(logits, axis=-1, keepdims=True))
    return e / jnp.sum(e, axis=-1, keepdims=True)


def xa_fwd(xq, kv):
    m, d = xq.shape
    mm = kv.shape[0]
    tm = _row_tile(m)

    def body(q_ref, kv_ref, o_ref):
        for hd in range(XA_HEADS):
            cs = slice(hd * XA_HEAD_DIM, (hd + 1) * XA_HEAD_DIM)
            p = _xa_probs(q_ref[:, cs], kv_ref[:, cs])
            vh = kv_ref[:, d + hd * XA_HEAD_DIM:d + (hd + 1) * XA_HEAD_DIM]
            o_ref[:, cs] = _dot(p.astype(BF), vh).astype(BF)

    return _call(
        body, name="xa_fwd", out_shape=jax.ShapeDtypeStruct((m, d), BF), grid=(m // tm,),
        in_specs=[pl.BlockSpec((tm, d), lambda i: (i, 0)), pl.BlockSpec((mm, 2 * d), lambda i: (0, 0))],
        out_specs=pl.BlockSpec((tm, d), lambda i: (i, 0)), compiler_params=_cp("parallel"),
    )(xq, kv)


def norm_bwd(h, gn, d_a=None, d_res=None, target=None, f_prev=None, gp_prev=None, alpha_prev=1.0):
    m, d = h.shape
    tm = _row_tile(m)
    has_loss = target is not None
    has_res = d_res is not None
    has_prev = f_prev is not None

    def body(*refs):
        refs = list(refs)
        h_ref, gn_ref = refs[0], refs[1]
        pos = 2
        da_ref = dres_ref = t_ref = f_ref = gp_ref = None
        if has_loss:
            t_ref = refs[pos]; pos += 1
        else:
            da_ref = refs[pos]; pos += 1
        if has_res:
            dres_ref = refs[pos]; pos += 1
        if has_prev:
            f_ref, gp_ref = refs[pos], refs[pos + 1]; pos += 2
        dh_ref, dgn_ref = refs[pos], refs[pos + 1]; pos += 2
        df_ref = dgp_ref = loss_ref = None
        if has_prev:
            df_ref, dgp_ref = refs[pos], refs[pos + 1]; pos += 2
        if has_loss:
            loss_ref = refs[pos]

        first = pl.program_id(0) == 0
        hv = h_ref[...]
        gn = gn_ref[...]
        if has_loss:
            err = hv * _rstd(hv) * gn - t_ref[...]
            da = err * (1.0 / d)
            part = 0.5 * jnp.sum(jnp.sum(err * err, axis=-1, keepdims=True) * (1.0 / d))

            @pl.when(first)
            def _():
                loss_ref[...] = jnp.zeros_like(loss_ref)

            loss_ref[...] += part
        else:
            da = da_ref[...].astype(F32)
        dx, dgn = _rms_bwd(hv, gn, da)
        dh = dx + dres_ref[...] if has_res else dx
        dh_ref[...] = dh

        @pl.when(first)
        def _():
            dgn_ref[...] = jnp.zeros_like(dgn_ref)

        dgn_ref[...] += dgn
        if has_prev:
            dfv, dgp = _rms_bwd(f_ref[...], gp_ref[...], dh)
            df_ref[...] = (alpha_prev * dfv).astype(BF)

            @pl.when(first)
            def _():
                dgp_ref[...] = jnp.zeros_like(dgp_ref)

            dgp_ref[...] += alpha_prev * dgp

    row = pl.BlockSpec((tm, d), lambda i: (i, 0))
    vec = pl.BlockSpec((1, d), lambda i: (0, 0))
    ins, in_specs = [h, gn], [row, vec]
    ins.append(target if has_loss else d_a); in_specs.append(row)
    if has_res:
        ins.append(d_res); in_specs.append(row)
    if has_prev:
        ins += [f_prev, gp_prev]; in_specs += [row, vec]
    outs = [jax.ShapeDtypeStruct((m, d), F32), jax.ShapeDtypeStruct((1, d), F32)]
    out_specs = [row, vec]
    names = ["d_h", "d_gn"]
    if has_prev:
        outs += [jax.ShapeDtypeStruct((m, d), BF), jax.ShapeDtypeStruct((1, d), F32)]
        out_specs += [row, vec]
        names += ["d_f", "d_gp"]
    if has_loss:
        outs.append(jax.ShapeDtypeStruct((8, LANES), F32))
        out_specs.append(pl.BlockSpec((8, LANES), lambda i: (0, 0)))
        names.append("loss")
    res = _call(
        body, name="norm_bwd", out_shape=tuple(outs), grid=(m // tm,), in_specs=in_specs,
        out_specs=tuple(out_specs), compiler_params=_cp("arbitrary"),
    )(*ins)
    return dict(zip(names, res))


def ffn_bwd_act(df, gb, which, g, u):
    m, d = df.shape
    nb = g.shape[-1]
    tm = _row_tile(m)

    def body(df_ref, w_ref, g_ref, u_ref, dg_ref, du_ref):
        dh = _dot_nt(df_ref[...], w_ref[...])
        gv = g_ref[...].astype(F32)
        uv = u_ref[...].astype(F32)
        s = jax.nn.sigmoid(gv)
        dg_ref[...] = (dh * uv * (s * (1.0 + gv * (1.0 - s)))).astype(BF)
        du_ref[...] = (dh * gv * s).astype(BF)

    blk = jax.ShapeDtypeStruct((N_CHIPS, m, nb), BF)
    aspec = pl.BlockSpec((None, tm, nb), lambda k, i: (k, i, 0))
    return _call(
        body, name="ffn_bwd_act", out_shape=(blk, blk), grid=(N_CHIPS, m // tm),
        in_specs=[pl.BlockSpec((tm, d), lambda k, i: (i, 0)),
                  pl.BlockSpec((None, None, nb, d), lambda k, i: (k, which, 0, 0)), aspec, aspec],
        out_specs=(aspec, aspec), compiler_params=_cp("parallel", "parallel"),
    )(df, gb, g, u)


def mm_tn(a, b, dest, a_spec, b_spec, o_spec, acc_shape, msteps):
    def body(a_ref, b_ref, dest_ref, o_ref, acc):
        del dest_ref
        ms = pl.program_id(1)

        @pl.when(ms == 0)
        def _():
            acc[...] = jnp.zeros_like(acc)

        acc[...] += _dot_tn(a_ref[...], b_ref[...])

        @pl.when(ms == msteps - 1)
        def _():
            o_ref[...] = acc[...].astype(o_ref.dtype)

    return _call(
        body, name="mm_tn", out_shape=jax.ShapeDtypeStruct(dest.shape, dest.dtype), grid=(N_CHIPS, msteps),
        in_specs=[a_spec, b_spec, pl.BlockSpec(memory_space=pl.ANY)], out_specs=o_spec,
        scratch_shapes=[pltpu.VMEM(acc_shape, F32)], input_output_aliases={2: 0},
        compiler_params=_cp("parallel", "arbitrary"),
    )(a, b, dest)


def _act_spec(arr, tm, nb):
    if arr.ndim == 3:
        return pl.BlockSpec((None, tm, nb), lambda k, ms: (k, ms, 0))
    return pl.BlockSpec((tm, nb), lambda k, ms: (ms, k))


def grad_cb(a, dout, dest, which=None):
    m, kd = a.shape
    nb = dest.shape[-1]
    tm = _row_tile(m)
    if which is None:
        o_spec = pl.BlockSpec((None, kd, nb), lambda k, ms: (k, 0, 0))
    else:
        o_spec = pl.BlockSpec((None, None, kd, nb), lambda k, ms: (k, which, 0, 0))
    return mm_tn(a, dout, dest, pl.BlockSpec((tm, kd), lambda k, ms: (ms, 0)), _act_spec(dout, tm, nb), o_spec,
                 (kd, nb), m // tm)


def grad_rb(a, dout, dest, which):
    m, n = dout.shape
    kb = dest.shape[-2]
    tm = _row_tile(m)
    o_spec = pl.BlockSpec((None, None, kb, n), lambda k, ms: (k, which, 0, 0))
    return mm_tn(a, dout, dest, _act_spec(a, tm, kb), pl.BlockSpec((tm, n), lambda k, ms: (ms, 0)), o_spec,
                 (kb, n), m // tm)


def mm_nt_cb(pairs, n, out_dtype):
    d0 = pairs[0][0]
    m = d0.shape[1] if d0.ndim == 3 else d0.shape[0]
    tm = _row_tile(m)
    npair = len(pairs)

    def body(*refs):
        o_ref, acc = refs[2 * npair], refs[2 * npair + 1]
        k = pl.program_id(1)

        @pl.when(k == 0)
        def _():
            acc[...] = jnp.zeros_like(acc)

        for p in range(npair):
            acc[...] += _dot_nt(refs[2 * p][...], refs[2 * p + 1][...])

        @pl.when(k == N_CHIPS - 1)
        def _():
            o_ref[...] = acc[...].astype(out_dtype)

    ins, in_specs = [], []
    for dout, w, which in pairs:
        nb = w.shape[-1]
        if dout.ndim == 3:
            in_specs.append(pl.BlockSpec((None, tm, nb), lambda i, k: (k, i, 0)))
        else:
            in_specs.append(pl.BlockSpec((tm, nb), lambda i, k: (i, k)))
        if w.ndim == 4:
            in_specs.append(pl.BlockSpec((None, None, n, nb), lambda i, k, which=which: (k, which, 0, 0)))
        else:
            in_specs.append(pl.BlockSpec((None, n, nb), lambda i, k: (k, 0, 0)))
        ins += [dout, w]
    return _call(
        body, name="mm_nt_cb", out_shape=jax.ShapeDtypeStruct((m, n), out_dtype), grid=(m // tm, N_CHIPS),
        in_specs=in_specs, out_specs=pl.BlockSpec((tm, n), lambda i, k: (i, 0)),
        scratch_shapes=[pltpu.VMEM((tm, n), F32)], compiler_params=_cp("parallel", "arbitrary"),
    )(*ins)


def mm_nt_rb(dout, w, which, out_dtype):
    m, n = dout.shape
    kb = w.shape[2]
    tm = _row_tile(m)

    def body(d_ref, w_ref, o_ref):
        o_ref[...] = _dot_nt(d_ref[...], w_ref[...]).astype(out_dtype)

    return _call(
        body, name="mm_nt_rb", out_shape=jax.ShapeDtypeStruct((m, N_CHIPS * kb), out_dtype), grid=(N_CHIPS, m // tm),
        in_specs=[pl.BlockSpec((tm, n), lambda k, i: (i, 0)),
                  pl.BlockSpec((None, None, kb, n), lambda k, i: (k, which, 0, 0))],
        out_specs=pl.BlockSpec((tm, kb), lambda k, i: (i, k)), compiler_params=_cp("parallel", "parallel"),
    )(dout, w)


def merge_norm_bwd(oa, ob, dmerged, ga, gb):
    m, w = oa.shape
    tm = _row_tile(m)

    def body(a_ref, b_ref, dm_ref, ga_ref, gb_ref, da_ref, db_ref, dga_ref, dgb_ref):
        @pl.when(pl.program_id(0) == 0)
        def _():
            dga_ref[...] = jnp.zeros_like(dga_ref)
            dgb_ref[...] = jnp.zeros_like(dgb_ref)

        da, dga = _rms_bwd(a_ref[...].astype(F32), ga_ref[...], dm_ref[:, :w].astype(F32))
        db, dgb = _rms_bwd(b_ref[...].astype(F32), gb_ref[...], dm_ref[:, w:].astype(F32))
        da_ref[...] = da
        db_ref[...] = db
        dga_ref[...] += dga
        dgb_ref[...] += dgb

    row = pl.BlockSpec((tm, w), lambda i: (i, 0))
    vec = pl.BlockSpec((1, w), lambda i: (0, 0))
    return _call(
        body, name="merge_norm_bwd", grid=(m // tm,),
        out_shape=(jax.ShapeDtypeStruct((m, w), F32), jax.ShapeDtypeStruct((m, w), F32),
                   jax.ShapeDtypeStruct((1, w), F32), jax.ShapeDtypeStruct((1, w), F32)),
        in_specs=[row, row, pl.BlockSpec((tm, 2 * w), lambda i: (i, 0)), vec, vec],
        out_specs=(row, row, vec, vec), compiler_params=_cp("arbitrary"),
    )(oa, ob, dmerged, ga, gb)


def sgu_bwd(proj, d_oa, wm, wmt, ng, nb, bst):
    m = proj.shape[0]
    tm = _row_tile(m)
    wd = SGU_GROUPS * CHUNK

    def body(u_ref, v_ref, do_ref, wm_ref, wmt_ref, ng_ref, nb_ref, bst_ref,
             dp_ref, dw_ref, dbt_ref, dng_ref, dnb_ref):
        @pl.when(pl.program_id(0) == 0)
        def _():
            dw_ref[...] = jnp.zeros_like(dw_ref)
            dbt_ref[...] = jnp.zeros_like(dbt_ref)
            dng_ref[...] = jnp.zeros_like(dng_ref)
            dnb_ref[...] = jnp.zeros_like(dnb_ref)

        causal = lax.broadcasted_iota(jnp.int32, (CHUNK, CHUNK), 0) >= lax.broadcasted_iota(jnp.int32, (CHUNK, CHUNK), 1)
        for c in range(tm // CHUNK):
            rs = slice(c * CHUNK, (c + 1) * CHUNK)
            for g in range(SGU_GROUPS):
                cs = slice(g * CHUNK, (g + 1) * CHUNK)
                ug, xh, rstd, vn, mixed = _sgu_core(u_ref, v_ref, wm_ref, ng_ref, nb_ref, bst_ref, g, c)
                do = do_ref[rs, cs]
                dug = do * mixed
                dmix = do * ug
                dmb = dmix.astype(BF)
                dbt_ref[g] += jnp.sum(dmix, axis=-1, keepdims=True)
                dw_ref[g] += jnp.where(causal, _dot_nt(dmb, vn.astype(BF)), 0.0)
                dvn = _dot(wmt_ref[g], dmb)
                dng_ref[g:g + 1, :] += jnp.sum(dvn * xh, axis=0, keepdims=True)
                dnb_ref[g:g + 1, :] += jnp.sum(dvn, axis=0, keepdims=True)
                dxh = dvn * ng_ref[g:g + 1, :]
                dvg = rstd * (dxh - jnp.mean(dxh, axis=-1, keepdims=True)
                              - xh * jnp.mean(dxh * xh, axis=-1, keepdims=True))
                dp_ref[rs, cs] = (dug * _gelu_grad(u_ref[rs, cs].astype(F32))).astype(BF)
                dp_ref[rs, wd + g * CHUNK:wd + (g + 1) * CHUNK] = (dvg * _gelu_grad(v_ref[rs, cs].astype(F32))).astype(BF)

    full = lambda shape: pl.BlockSpec(shape, lambda i: (0,) * len(shape))
    return _call(
        body, name="sgu_bwd", grid=(m // tm,),
        out_shape=(jax.ShapeDtypeStruct((m, 2 * wd), BF), jax.ShapeDtypeStruct(wm.shape, F32),
                   jax.ShapeDtypeStruct(bst.shape, F32), jax.ShapeDtypeStruct(ng.shape, F32),
                   jax.ShapeDtypeStruct(nb.shape, F32)),
        in_specs=[pl.BlockSpec((tm, wd), lambda i: (i, 0)), pl.BlockSpec((tm, wd), lambda i: (i, 1)),
                  pl.BlockSpec((tm, wd), lambda i: (i, 0)),
                  full(wm.shape), full(wmt.shape), full(ng.shape), full(nb.shape), full(bst.shape)],
        out_specs=(pl.BlockSpec((tm, 2 * wd), lambda i: (i, 0)), full(wm.shape), full(bst.shape), full(ng.shape),
                   full(nb.shape)),
        compiler_params=_cp("arbitrary"),
    )(proj, proj, d_oa, wm, wmt, ng, nb, bst)


def sb_bwd(proj, tot, d_ob):
    m = proj.shape[0]
    tq, tk = _sb_tiles(m)

    def body(q_ref, k_ref, v_ref, tot_ref, do_ref, dq_ref, dk_ref, dv_ref, dqa, cl1, cg):
        i = pl.program_id(1)

        @pl.when(i == 0)
        def _():
            dk_ref[...] = jnp.zeros_like(dk_ref)
            dv_ref[...] = jnp.zeros_like(dv_ref)

        nblk = (i + 1) * (tq // tk)
        lane = lax.broadcasted_iota(jnp.int32, (1, LANES), 1)
        rows = i * tq + lax.broadcasted_iota(jnp.int32, (tq, tk), 0)
        cols = lax.broadcasted_iota(jnp.int32, (tq, tk), 1)
        r_io = lax.broadcasted_iota(jnp.int32, (tk, tk), 0)
        c_io = lax.broadcasted_iota(jnp.int32, (tk, tk), 1)
        incl = (r_io <= c_io).astype(BF)
        excl = (r_io < c_io).astype(BF)
        qv = q_ref[...]
        dov = do_ref[...].astype(BF)
        totv = tot_ref[...]
        dq_all = jnp.zeros((tq, LANES), F32)
        for hd in range(2):
            mh = (lane >= SB_HEAD_DIM) if hd else (lane < SB_HEAD_DIM)
            qm = jnp.where(mh, qv, jnp.zeros_like(qv)) * SB_SCALE
            dom = jnp.where(mh, dov, jnp.zeros_like(dov))
            tot_h = jnp.max(jnp.where(mh, totv, -jnp.inf), axis=-1, keepdims=True)
            dqa[...] = jnp.zeros_like(dqa)
            cl1[...] = jnp.zeros_like(cl1)
            cg[...] = jnp.zeros_like(cg)

            def step(j, carry):
                ks = pl.multiple_of(j * tk, tk)
                kb = k_ref[pl.ds(ks, tk), :]
                vb = v_ref[pl.ds(ks, tk), :]
                z = _dot_nt(qm, kb)
                mask = (cols + ks) < rows
                ls, l1 = _log_sigmoid_pair(z, mask)
                suffix = tot_h - (_dot3(l1, incl) + cl1[...])
                a = jnp.where(mask, jnp.exp(ls + suffix), 0.0)
                gmat = a * _dot_nt(dom, vb)
                pref = _dot3(gmat, excl) + cg[...]
                sg = jnp.exp(ls)
                dz = jnp.where(mask, gmat * (1.0 - sg) - pref * sg, 0.0).astype(BF)
                dqa[...] += _dot(dz, kb)
                dk_ref[pl.ds(ks, tk), :] += _dot_tn(dz, qm)
                dv_ref[pl.ds(ks, tk), :] += _dot_tn(a.astype(BF), dom)
                cl1[...] += jnp.sum(l1, axis=-1, keepdims=True)
                cg[...] += jnp.sum(gmat, axis=-1, keepdims=True)
                return carry

            lax.fori_loop(0, nblk, step, 0)
            dq_all = jnp.where(mh, dqa[...] * SB_SCALE, dq_all)
        dq_ref[...] = dq_all

    qb = 2 * 512 // LANES
    tile = pl.BlockSpec((tq, LANES), lambda p, i: (i, p))
    seq = pl.BlockSpec((m, LANES), lambda p, i: (0, p))
    out = jax.ShapeDtypeStruct((m, 512), F32)
    return _call(
        body, name="sb_bwd", grid=(4, m // tq), out_shape=(out, out, out),
        in_specs=[pl.BlockSpec((tq, LANES), lambda p, i: (i, qb + p)),
                  pl.BlockSpec((m, LANES), lambda p, i: (0, qb + 4 + p)),
                  pl.BlockSpec((m, LANES), lambda p, i: (0, qb + 8 + p)), tile, tile],
        out_specs=(tile, seq, seq),
        scratch_shapes=[pltpu.VMEM((tq, LANES), F32), pltpu.VMEM((tq, 1), F32), pltpu.VMEM((tq, 1), F32)],
        compiler_params=_cp("parallel", "arbitrary"),
    )(proj, proj, proj, tot, d_ob)


def xa_bwd(xq, kv, d_o):
    m, d = xq.shape
    mm = kv.shape[0]
    tm = _row_tile(m)

    def body(q_ref, kv_ref, do_ref, dq_ref, dkv_ref):
        @pl.when(pl.program_id(0) == 0)
        def _():
            dkv_ref[...] = jnp.zeros_like(dkv_ref)

        for hd in range(XA_HEADS):
            cs = slice(hd * XA_HEAD_DIM, (hd + 1) * XA_HEAD_DIM)
            vs = slice(d + hd * XA_HEAD_DIM, d + (hd + 1) * XA_HEAD_DIM)
            qh = q_ref[:, cs]
            kh = kv_ref[:, cs]
            doh = do_ref[:, cs]
            p = _xa_probs(qh, kh)
            dp = _dot_nt(doh, kv_ref[:, vs])
            ds = (p * (dp - jnp.sum(p * dp, axis=-1, keepdims=True))).astype(BF)
            dq_ref[:, cs] = (_dot(ds, kh) * XA_SCALE).astype(BF)
            dkv_ref[:, cs] += _dot_tn(ds, qh) * XA_SCALE
            dkv_ref[:, vs] += _dot_tn(p.astype(BF), doh)

    row = pl.BlockSpec((tm, d), lambda i: (i, 0))
    whole = pl.BlockSpec((mm, 2 * d), lambda i: (0, 0))
    return _call(
        body, name="xa_bwd", grid=(m // tm,),
        out_shape=(jax.ShapeDtypeStruct((m, d), BF), jax.ShapeDtypeStruct((mm, 2 * d), F32)),
        in_specs=[row, whole, row], out_specs=(row, whole), compiler_params=_cp("arbitrary"),
    )(xq, kv, d_o)


def adamw(w, g, mom, vel):
    r, c = w.shape
    tr = r
    for cand in (512, 256, 128, 64, 32, 16, 8):
        if r % cand == 0:
            tr = cand
            break

    def body(w_ref, g_ref, m_ref, v_ref, d_ref, nm_ref, nv_ref):
        gv = g_ref[...]
        mn = ADAM_B1 * m_ref[...] + (1.0 - ADAM_B1) * gv
        vn = ADAM_B2 * v_ref[...] + (1.0 - ADAM_B2) * (gv * gv)
        m_hat = mn / (1.0 - ADAM_B1 ** ADAM_STEP)
        v_hat = vn / (1.0 - ADAM_B2 ** ADAM_STEP)
        d_ref[...] = -ADAM_LR * (m_hat / (jnp.sqrt(v_hat) + ADAM_EPS) + ADAM_WD * w_ref[...])
        nm_ref[...] = mn
        nv_ref[...] = vn

    spec = pl.BlockSpec((tr, c), lambda i: (i, 0))
    out = jax.ShapeDtypeStruct((r, c), F32)
    return _call(
        body, name="adamw", out_shape=(out, out, out), grid=(r // tr,), in_specs=[spec] * 4,
        out_specs=(spec, spec, spec), compiler_params=_cp("parallel"),
    )(w, g, mom, vel)


def _place():
    x, y, c = lax.axis_index("x"), lax.axis_index("y"), lax.axis_index("c")
    others = [(1 - x, y), (x, 1 - y), (1 - x, 1 - y)]
    return x, y, c, others


_HALF = {"A": (0, 2), "B": (0, 1), "C": (1, 128), "D": (0, 512), "E": (0, 512)}
_GROUPS = ("A", "B", "C", "D", "E")


def _half_of(ref, name, hc, lead=0):
    axis, size = _HALF[name]
    idx = [slice(None)] * (lead + axis) + [pl.ds(hc * size, size)]
    return ref.at[tuple(idx)]


def gather_weights(loc):
    names = list(_GROUPS)
    n = len(names)

    def body(*refs):
        src = dict(zip(names, refs[:n]))
        out = dict(zip(names, refs[n:2 * n]))
        send, recv, lsem = refs[2 * n], refs[2 * n + 1], refs[2 * n + 2]
        x, y, c, others = _place()
        me = 2 * x + y

        def cp(a, j, chip_idx, hc, to, from_src, stage):
            dst = _half_of(out[names[a]].at[chip_idx], names[a], hc)
            s = _half_of(src[names[a]], names[a], hc) if from_src else dst
            k = a * 6 + stage * 3 + j
            return pltpu.make_async_remote_copy(src_ref=s, dst_ref=dst, send_sem=send.at[k], recv_sem=recv.at[k],
                                                device_id=to, device_id_type=MESH)

        local = [pltpu.make_async_copy(src[nm], out[nm].at[me], lsem.at[a]) for a, nm in enumerate(names)]
        for l in local:
            l.start()
        first = [cp(a, j, me, c, (ox, oy, c), True, 0) for a in range(n) for j, (ox, oy) in enumerate(others)]
        for f in first:
            f.start()
        passed = []
        for j, (ox, oy) in enumerate(others):
            for a in range(n):
                cp(a, j, 2 * ox + oy, c, (x, y, c), False, 0).wait_recv()
                p = cp(a, j, 2 * ox + oy, c, (x, y, 1 - c), False, 1)
                p.start()
                passed.append(p)
        for j, (ox, oy) in enumerate(others):
            for a in range(n):
                cp(a, j, 2 * ox + oy, 1 - c, (x, y, c), False, 1).wait_recv()
        for f in first + passed:
            f.wait_send()
        for l in local:
            l.wait()

    hbm = pl.BlockSpec(memory_space=pl.ANY)
    outs = tuple(jax.ShapeDtypeStruct((N_CHIPS,) + loc[nm].shape, loc[nm].dtype) for nm in names)
    res = _call(
        body, name="gather_weights", out_shape=outs, in_specs=[hbm] * n, out_specs=(hbm,) * n,
        scratch_shapes=[pltpu.SemaphoreType.DMA((6 * n,)), pltpu.SemaphoreType.DMA((6 * n,)),
                        pltpu.SemaphoreType.DMA((n,))],
    )(*[loc[nm] for nm in names])
    return dict(zip(names, res))


def _half_shape(name, shape):
    axis, size = _HALF[name]
    s = list(shape)
    s[axis] = size
    return tuple(s)


def rs_to_sibling(grads):
    names = list(_GROUPS)
    n = len(names)

    def body(*refs):
        src = dict(zip(names, refs[:n]))
        out = dict(zip(names, refs[n:2 * n]))
        send, recv = refs[2 * n], refs[2 * n + 1]
        x, y, c, _ = _place()
        copies = []
        for a, nm in enumerate(names):
            copies.append(pltpu.make_async_remote_copy(
                src_ref=_half_of(src[nm], nm, 1 - c, lead=1), dst_ref=out[nm], send_sem=send.at[a], recv_sem=recv.at[a],
                device_id=(x, y, 1 - c), device_id_type=MESH))
        for cpy in copies:
            cpy.start()
        for cpy in copies:
            cpy.wait()

    hbm = pl.BlockSpec(memory_space=pl.ANY)
    outs = tuple(jax.ShapeDtypeStruct((N_CHIPS,) + _half_shape(nm, grads[nm].shape[1:]), grads[nm].dtype)
                 for nm in names)
    res = _call(
        body, name="rs_to_sibling", out_shape=outs, in_specs=[hbm] * n, out_specs=(hbm,) * n,
        scratch_shapes=[pltpu.SemaphoreType.DMA((n,)), pltpu.SemaphoreType.DMA((n,))],
    )(*[grads[nm] for nm in names])
    return dict(zip(names, res))


def _tile2(shape):
    lead = shape[:-2]
    return lead, shape[-2:]


def add_halves(name, mine, got, c_idx):
    axis, size = _HALF[name]
    hshape = got.shape
    lead, last2 = hshape[:-2], hshape[-2:]
    nlead = len(lead)
    haxis = 1 + axis

    def body(c_ref, m_ref, g_ref, o_ref):
        del c_ref
        o_ref[...] = (m_ref[...].astype(F32) + g_ref[...].astype(F32)).astype(o_ref.dtype)

    blk = (None,) * nlead + last2

    def got_map(*idx):
        return tuple(idx[:nlead]) + (0, 0)

    def mine_map(*idx):
        lead_idx = list(idx[:nlead])
        c = idx[nlead][0]
        if haxis < nlead:
            lead_idx[haxis] = lead_idx[haxis] + c * size
            return tuple(lead_idx) + (0, 0)
        return tuple(lead_idx) + (c, 0)

    grid_spec = pltpu.PrefetchScalarGridSpec(
        num_scalar_prefetch=1, grid=lead,
        in_specs=[pl.BlockSpec(blk, mine_map), pl.BlockSpec(blk, got_map)],
        out_specs=pl.BlockSpec(blk, got_map))
    return _call(
        body, name="add_halves", out_shape=jax.ShapeDtypeStruct(hshape, got.dtype), grid_spec=grid_spec,
        compiler_params=_cp(*(("parallel",) * nlead)),
    )(c_idx, mine, got)


def rs_to_chips(summed):
    names = list(_GROUPS)
    n = len(names)

    def body(*refs):
        src = dict(zip(names, refs[:n]))
        out = dict(zip(names, refs[n:2 * n]))
        send, recv = refs[2 * n], refs[2 * n + 1]
        x, y, c, others = _place()
        copies = []
        for a, nm in enumerate(names):
            for j, (ox, oy) in enumerate(others):
                copies.append(pltpu.make_async_remote_copy(
                    src_ref=src[nm].at[2 * ox + oy], dst_ref=out[nm].at[j], send_sem=send.at[3 * a + j],
                    recv_sem=recv.at[3 * a + j], device_id=(ox, oy, c), device_id_type=MESH))
        for cpy in copies:
            cpy.start()
        for cpy in copies:
            cpy.wait()

    hbm = pl.BlockSpec(memory_space=pl.ANY)
    outs = tuple(jax.ShapeDtypeStruct((3,) + summed[nm].shape[1:], summed[nm].dtype) for nm in names)
    res = _call(
        body, name="rs_to_chips", out_shape=outs, in_specs=[hbm] * n, out_specs=(hbm,) * n,
        scratch_shapes=[pltpu.SemaphoreType.DMA((3 * n,)), pltpu.SemaphoreType.DMA((3 * n,))],
    )(*[summed[nm] for nm in names])
    return dict(zip(names, res))


def add_chips(summed, got, k_idx):
    hshape = summed.shape[1:]
    lead, last2 = hshape[:-2], hshape[-2:]
    nlead = len(lead)

    def body(k_ref, s_ref, g0_ref, g1_ref, g2_ref, o_ref):
        del k_ref
        o_ref[...] = ((s_ref[...].astype(F32) + g0_ref[...].astype(F32)) + g1_ref[...].astype(F32)) + g2_ref[...].astype(F32)

    blk = (None,) * (nlead + 1) + last2
    oblk = (None,) * nlead + last2

    def got_map(slot):
        return lambda *idx: (slot,) + tuple(idx[:nlead]) + (0, 0)

    grid_spec = pltpu.PrefetchScalarGridSpec(
        num_scalar_prefetch=1, grid=lead if nlead else (1,),
        in_specs=[pl.BlockSpec(blk, lambda *idx: (idx[-1][0],) + tuple(idx[:nlead]) + (0, 0)),
                  pl.BlockSpec(blk, got_map(0)), pl.BlockSpec(blk, got_map(1)), pl.BlockSpec(blk, got_map(2))],
        out_specs=pl.BlockSpec(oblk, lambda *idx: tuple(idx[:nlead]) + (0, 0)))
    return _call(
        body, name="add_chips", out_shape=jax.ShapeDtypeStruct(hshape, F32), grid_spec=grid_spec,
        compiler_params=_cp(*(("parallel",) * max(nlead, 1))),
    )(k_idx, summed, got, got, got)


def rs_replicate(final_half, full_shapes):
    names = list(_GROUPS)
    n = len(names)

    def body(*refs):
        src = dict(zip(names, refs[:n]))
        out = dict(zip(names, refs[n:2 * n]))
        send, recv, lsem = refs[2 * n], refs[2 * n + 1], refs[2 * n + 2]
        x, y, c, _ = _place()
        copies, local = [], []
        for a, nm in enumerate(names):
            dst = _half_of(out[nm], nm, c)
            copies.append(pltpu.make_async_remote_copy(src_ref=src[nm], dst_ref=dst, send_sem=send.at[a],
                                                       recv_sem=recv.at[a], device_id=(x, y, 1 - c), device_id_type=MESH))
            local.append(pltpu.make_async_copy(src[nm], dst, lsem.at[a]))
        for cpy in copies + local:
            cpy.start()
        for a, nm in enumerate(names):
            other = _half_of(out[nm], nm, 1 - c)
            pltpu.make_async_remote_copy(src_ref=src[nm], dst_ref=other, send_sem=send.at[a], recv_sem=recv.at[a],
                                         device_id=(x, y, 1 - c), device_id_type=MESH).wait_recv()
        for cpy in copies:
            cpy.wait_send()
        for l in local:
            l.wait()

    hbm = pl.BlockSpec(memory_space=pl.ANY)
    outs = tuple(jax.ShapeDtypeStruct(full_shapes[nm], F32) for nm in names)
    res = _call(
        body, name="rs_replicate", out_shape=outs, in_specs=[hbm] * n, out_specs=(hbm,) * n,
        scratch_shapes=[pltpu.SemaphoreType.DMA((n,)), pltpu.SemaphoreType.DMA((n,)), pltpu.SemaphoreType.DMA((n,))],
    )(*[final_half[nm] for nm in names])
    return dict(zip(names, res))


def allreduce_small(v):
    r = v.shape[0]

    def body(v_ref, o_ref, slots, send, recv):
        x, y, c, _ = _place()
        me = 4 * x + 2 * y + c
        slots[me] = v_ref[...]
        copies = []
        for rel in range(1, 8):
            fx, fy, fc = (rel >> 2) & 1, (rel >> 1) & 1, rel & 1
            peer = (x ^ fx, y ^ fy, c ^ fc)
            copies.append(pltpu.make_async_remote_copy(
                src_ref=v_ref, dst_ref=slots.at[me], send_sem=send.at[rel - 1], recv_sem=recv.at[rel - 1],
                device_id=peer, device_id_type=MESH))
        for cpy in copies:
            cpy.start()
        for rel in range(1, 8):
            fx, fy, fc = (rel >> 2) & 1, (rel >> 1) & 1, rel & 1
            src_id = 4 * (x ^ fx) + 2 * (y ^ fy) + (c ^ fc)
            pltpu.make_async_remote_copy(
                src_ref=v_ref, dst_ref=slots.at[src_id], send_sem=send.at[rel - 1], recv_sem=recv.at[rel - 1],
                device_id=(x, y, c), device_id_type=MESH).wait_recv()
        for cpy in copies:
            cpy.wait_send()
        acc = slots[0]
        for s in range(1, 8):
            acc = acc + slots[s]
        o_ref[...] = acc

    vm = pl.BlockSpec(memory_space=pltpu.VMEM)
    return _call(
        body, name="allreduce_small", out_shape=jax.ShapeDtypeStruct(v.shape, F32), in_specs=[vm], out_specs=vm,
        scratch_shapes=[pltpu.VMEM((8, r, LANES), F32), pltpu.SemaphoreType.DMA((7,)), pltpu.SemaphoreType.DMA((7,))],
    )(v)


def local_step(x, mem, target, wts, small):
    s = x.shape[0]
    ga, gb, gc, gd, ge = wts["A"], wts["B"], wts["C"], wts["D"], wts["E"]
    causal = jnp.tril(jnp.ones((CHUNK, CHUNK), dtype=bool))
    w_s = jnp.where(causal[None], small["sgu_w_s"], 0.0)
    wm = w_s.astype(BF)
    wmt = jnp.swapaxes(w_s, 1, 2).astype(BF)
    bst = small["sgu_b_s"].reshape(SGU_GROUPS, CHUNK, 1)
    ng, nbias = small["sgu_norm_g"], small["sgu_norm_b"]

    a1 = rms_fwd(x, small["ffn1_pre_g"])
    g1, u1, hid1 = ffn_up(a1, ga, 0, 1)
    f1, h1, n1 = mm_res(hid1, gb, 0, x, small["ffn1_post_g"], 0.5, small["mix_pre_g"])
    proj = mm_cb(n1, gd)
    oa = sgu_fwd(proj, wm, ng, nbias, bst)
    ob, tot = sb_fwd(proj)
    merged = merge_norm(oa, ob, small["sgu_out_g"], small["sb_out_g"])
    mo, h2, xn = mm_res(merged, gc, 0, h1, small["mix_post_g"], 1.0, small["xa_pre_g"])
    memn = rms_fwd(mem, small["mem_norm_g"])
    kv = mm_cb(memn, ge)
    xq = mm_res_plain(xn, gc, 1)
    o = xa_fwd(xq, kv)
    cc, h3, a2 = mm_res(o, gc, 2, h2, small["xa_post_g"], 1.0, small["ffn2_pre_g"])
    g2, u2, hid2 = ffn_up(a2, ga, 2, 3)
    f2, h4, _ = mm_res(hid2, gb, 1, h3, small["ffn2_post_g"], 0.5, small["final_norm_g"])

    dga = lax.empty(ga.shape, BF)
    dgb = lax.empty(gb.shape, BF)
    dgc = lax.empty(gc.shape, BF)
    dgd = lax.empty(gd.shape, BF)
    dge = lax.empty(ge.shape, BF)
    sg = {}

    r = norm_bwd(h4, small["final_norm_g"], target=target, f_prev=f2, gp_prev=small["ffn2_post_g"], alpha_prev=0.5)
    loss_tile, dh4, df2 = r["loss"], r["d_h"], r["d_f"]
    sg["final_norm_g"], sg["ffn2_post_g"] = r["d_gn"], r["d_gp"]

    dg2, du2 = ffn_bwd_act(df2, gb, 1, g2, u2)
    dgb = grad_rb(hid2, df2, dgb, 1)
    dga = grad_cb(a2, dg2, dga, 2)
    dga = grad_cb(a2, du2, dga, 3)
    da2 = mm_nt_cb([(dg2, ga, 2), (du2, ga, 3)], D_MODEL, F32)
    r = norm_bwd(h3, small["ffn2_pre_g"], d_a=da2, d_res=dh4, f_prev=cc, gp_prev=small["xa_post_g"], alpha_prev=1.0)
    dh3, dc = r["d_h"], r["d_f"]
    sg["ffn2_pre_g"], sg["xa_post_g"] = r["d_gn"], r["d_gp"]

    d_o = mm_nt_rb(dc, gc, 2, BF)
    dgc = grad_rb(o, dc, dgc, 2)
    dxq, dkv = xa_bwd(xq, kv, d_o)
    dkvb = dkv.astype(BF)
    dge = grad_cb(memn, dkvb, dge)
    dmemn = mm_nt_cb([(dkvb, ge, None)], D_MODEL, F32)
    sg["mem_norm_g"] = norm_bwd(mem, small["mem_norm_g"], d_a=dmemn)["d_gn"]
    dgc = grad_rb(xn, dxq, dgc, 1)
    dxn = mm_nt_rb(dxq, gc, 1, F32)
    r = norm_bwd(h2, small["xa_pre_g"], d_a=dxn, d_res=dh3, f_prev=mo, gp_prev=small["mix_post_g"], alpha_prev=1.0)
    dh2, dmo = r["d_h"], r["d_f"]
    sg["xa_pre_g"], sg["mix_post_g"] = r["d_gn"], r["d_gp"]

    dmerged = mm_nt_rb(dmo, gc, 0, BF)
    dgc = grad_rb(merged, dmo, dgc, 0)
    d_oa, d_ob, sg["sgu_out_g"], sg["sb_out_g"] = merge_norm_bwd(oa, ob, dmerged, small["sgu_out_g"], small["sb_out_g"])
    dp_uv, dws, dbt, sg["sgu_norm_g"], sg["sgu_norm_b"] = sgu_bwd(proj, d_oa, wm, wmt, ng, nbias, bst)
    sg["sgu_w_s"] = dws
    sg["sgu_b_s"] = dbt.reshape(SGU_GROUPS, CHUNK)
    dq, dk, dv = sb_bwd(proj, tot, d_ob)
    dproj = jnp.concatenate([dp_uv, dq.astype(BF), dk.astype(BF), dv.astype(BF)], axis=1)
    dgd = grad_cb(n1, dproj, dgd)
    dn1 = mm_nt_cb([(dproj, gd, None)], D_MODEL, F32)
    r = norm_bwd(h1, small["mix_pre_g"], d_a=dn1, d_res=dh2, f_prev=f1, gp_prev=small["ffn1_post_g"], alpha_prev=0.5)
    dh1, df1 = r["d_h"], r["d_f"]
    sg["mix_pre_g"], sg["ffn1_post_g"] = r["d_gn"], r["d_gp"]

    dg1, du1 = ffn_bwd_act(df1, gb, 0, g1, u1)
    dgb = grad_rb(hid1, df1, dgb, 0)
    dga = grad_cb(a1, dg1, dga, 0)
    dga = grad_cb(a1, du1, dga, 1)
    da1 = mm_nt_cb([(dg1, ga, 0), (du1, ga, 1)], D_MODEL, F32)
    r = norm_bwd(x, small["ffn1_pre_g"], d_a=da1, d_res=dh1)
    grad_x = r["d_h"]
    sg["ffn1_pre_g"] = r["d_gn"]
    return loss_tile, grad_x, {"A": dga, "B": dgb, "C": dgc, "D": dgd, "E": dge}, sg


def mm_res_plain(a, w, which):
    m = a.shape[0]
    kb, n = w.shape[2], w.shape[3]
    tm = _row_tile(m)

    def body(a_ref, w_ref, o_ref):
        acc = None
        for k in range(N_CHIPS):
            t = _dot(a_ref[:, k * kb:(k + 1) * kb], w_ref[k])
            acc = t if acc is None else acc + t
        o_ref[...] = acc.astype(BF)

    return _call(
        body, name="mm_rb", out_shape=jax.ShapeDtypeStruct((m, n), BF), grid=(m // tm,),
        in_specs=[pl.BlockSpec((tm, N_CHIPS * kb), lambda i: (i, 0)),
                  pl.BlockSpec((N_CHIPS, None, kb, n), lambda i: (0, which, 0, 0))],
        out_specs=pl.BlockSpec((tm, n), lambda i: (i, 0)), compiler_params=_cp("parallel"),
    )(a, w)


_BIG = ("ffn1_w_gate", "ffn1_w_up", "ffn1_w_down", "w_in", "w_out", "xa_w_q", "xa_w_kv", "xa_w_o",
        "ffn2_w_gate", "ffn2_w_up", "ffn2_w_down")
_SMALL = ("ffn1_pre_g", "ffn1_post_g", "mix_pre_g", "mix_post_g", "sgu_norm_g", "sgu_norm_b", "sgu_w_s", "sgu_b_s",
          "sgu_out_g", "sb_out_g", "xa_pre_g", "xa_post_g", "mem_norm_g", "ffn2_pre_g", "ffn2_post_g", "final_norm_g")
_WEIGHTS = ("ffn1_pre_g", "ffn1_post_g", "ffn1_w_gate", "ffn1_w_up", "ffn1_w_down", "mix_pre_g", "mix_post_g", "w_in",
            "sgu_norm_g", "sgu_norm_b", "sgu_w_s", "sgu_b_s", "sgu_out_g", "sb_out_g", "w_out", "xa_pre_g", "xa_post_g",
            "mem_norm_g", "xa_w_q", "xa_w_kv", "xa_w_o", "ffn2_pre_g", "ffn2_post_g", "ffn2_w_gate", "ffn2_w_up",
            "ffn2_w_down", "final_norm_g")
_SLOT = {"ffn1_w_gate": ("A", 0), "ffn1_w_up": ("A", 1), "ffn2_w_gate": ("A", 2), "ffn2_w_up": ("A", 3),
         "ffn1_w_down": ("B", 0), "ffn2_w_down": ("B", 1), "w_out": ("C", 0), "xa_w_q": ("C", 1), "xa_w_o": ("C", 2),
         "w_in": ("D", None), "xa_w_kv": ("E", None)}


def _pack_small(vals):
    return jnp.concatenate([vals[nm].reshape(-1, LANES) for nm in _SMALL], axis=0)


def _unpack_small(packed, shapes):
    out, pos = {}, 0
    for nm in _SMALL:
        rows = math.prod(shapes[nm]) // LANES
        out[nm] = packed[pos:pos + rows].reshape(shapes[nm])
        pos += rows
    return out


def kernel(x, mem, ffn1_pre_g, ffn1_post_g, ffn1_w_gate, ffn1_w_up, ffn1_w_down, mix_pre_g, mix_post_g, w_in, sgu_norm_g, sgu_norm_b, sgu_w_s, sgu_b_s, sgu_out_g, sb_out_g, w_out, xa_pre_g, xa_post_g, mem_norm_g, xa_w_q, xa_w_kv, xa_w_o, ffn2_pre_g, ffn2_post_g, ffn2_w_gate, ffn2_w_up, ffn2_w_down, final_norm_g, loss_target, m_ffn1_pre_g, m_ffn1_post_g, m_ffn1_w_gate, m_ffn1_w_up, m_ffn1_w_down, m_mix_pre_g, m_mix_post_g, m_w_in, m_sgu_norm_g, m_sgu_norm_b, m_sgu_w_s, m_sgu_b_s, m_sgu_out_g, m_sb_out_g, m_w_out, m_xa_pre_g, m_xa_post_g, m_mem_norm_g, m_xa_w_q, m_xa_w_kv, m_xa_w_o, m_ffn2_pre_g, m_ffn2_post_g, m_ffn2_w_gate, m_ffn2_w_up, m_ffn2_w_down, m_final_norm_g, v_ffn1_pre_g, v_ffn1_post_g, v_ffn1_w_gate, v_ffn1_w_up, v_ffn1_w_down, v_mix_pre_g, v_mix_post_g, v_w_in, v_sgu_norm_g, v_sgu_norm_b, v_sgu_w_s, v_sgu_b_s, v_sgu_out_g, v_sb_out_g, v_w_out, v_xa_pre_g, v_xa_post_g, v_mem_norm_g, v_xa_w_q, v_xa_w_kv, v_xa_w_o, v_ffn2_pre_g, v_ffn2_post_g, v_ffn2_w_gate, v_ffn2_w_up, v_ffn2_w_down, v_final_norm_g):
    env = dict(locals())
    w = {nm: env[nm] for nm in _WEIGHTS}
    mom = {nm: env["m_" + nm] for nm in _WEIGHTS}
    vel = {nm: env["v_" + nm] for nm in _WEIGHTS}

    loc = {
        "A": jnp.stack([w["ffn1_w_gate"][0], w["ffn1_w_up"][0], w["ffn2_w_gate"][0], w["ffn2_w_up"][0]]).astype(BF),
        "B": jnp.stack([w["ffn1_w_down"][0], w["ffn2_w_down"][0]]).astype(BF),
        "C": jnp.stack([w["w_out"][0], w["xa_w_q"][0], w["xa_w_o"][0]]).astype(BF),
        "D": w["w_in"][0].astype(BF),
        "E": w["xa_w_kv"][0].astype(BF),
    }
    wts = gather_weights(loc)

    small = {nm: w[nm][0] for nm in _SMALL}
    for nm in ("ffn1_pre_g", "ffn1_post_g", "mix_pre_g", "mix_post_g", "sgu_out_g", "sb_out_g", "xa_pre_g", "xa_post_g",
               "mem_norm_g", "ffn2_pre_g", "ffn2_post_g", "final_norm_g"):
        small[nm] = w[nm]
    loss_tile, grad_x, big_g, small_g = local_step(x[0], mem[0], loss_target[0], wts, small)

    small_shapes = {nm: w[nm].shape for nm in _SMALL}
    flat = _pack_small(small_g)
    n_small = flat.shape[0]
    pad = jnp.zeros((-n_small % 8, LANES), F32)
    packed = allreduce_small(jnp.concatenate([flat, pad, loss_tile], axis=0))
    loss = packed[-8, 0]
    g_small = _unpack_small(packed[:n_small], small_shapes)

    c_idx = lax.axis_index("c").astype(jnp.int32).reshape(1)
    k_idx = (2 * lax.axis_index("x") + lax.axis_index("y")).astype(jnp.int32).reshape(1)
    from_sib = rs_to_sibling(big_g)
    pair_sum = {nm: add_halves(nm, big_g[nm], from_sib[nm], c_idx) for nm in _GROUPS}
    from_chips = rs_to_chips(pair_sum)
    final_half = {nm: add_chips(pair_sum[nm], from_chips[nm], k_idx) for nm in _GROUPS}
    shard = rs_replicate(final_half, {nm: loc[nm].shape for nm in _GROUPS})

    grads, delta, new_m, new_v = {}, {}, {}, {}
    for nm in _BIG:
        grp, idx = _SLOT[nm]
        g2d = shard[grp] if idx is None else shard[grp][idx]
        shape = w[nm].shape
        d, nm_, nv_ = adamw(w[nm][0], g2d, mom[nm][0], vel[nm][0])
        grads[nm], delta[nm], new_m[nm], new_v[nm] = (g2d.reshape(shape), d.reshape(shape), nm_.reshape(shape),
                                                     nv_.reshape(shape))
    d, nm_, nv_ = adamw(_pack_small(w), _pack_small(g_small), _pack_small(mom), _pack_small(vel))
    d, nm_, nv_ = (_unpack_small(t, small_shapes) for t in (d, nm_, nv_))
    for nm in _SMALL:
        grads[nm], delta[nm], new_m[nm], new_v[nm] = g_small[nm], d[nm], nm_[nm], nv_[nm]

    return (loss, grad_x[None], *[grads[nm] for nm in _WEIGHTS], *[delta[nm] for nm in _WEIGHTS],
            *[new_m[nm] for nm in _WEIGHTS], *[new_v[nm] for nm in _WEIGHTS])
```

```python
import functools
import math

import jax
import jax.numpy as jnp
from jax import lax
from jax.experimental import pallas as pl
from jax.experimental.pallas import tpu as pltpu

F32 = jnp.float32
BF = jnp.bfloat16
EPS = 1e-6
D_MODEL = 1024
N_CHIPS = 4
FF_BLOCK = 704
IN_BLOCK = 640
KV_BLOCK = 512
ROW_BLOCK = 256
SGU_GROUPS = 4
CHUNK = 128
SB_HEAD_DIM = 64
SB_SCALE = SB_HEAD_DIM ** -0.5
XA_HEADS = 4
XA_HEAD_DIM = 256
XA_SCALE = XA_HEAD_DIM ** -0.5
LANES = 128
VMEM_LIMIT = 56 * 1024 * 1024
MESH = pl.DeviceIdType.MESH

ADAM_LR = 0.001
ADAM_B1 = 0.9
ADAM_B2 = 0.999
ADAM_EPS = 1e-08
ADAM_WD = 0.01
ADAM_STEP = 10

_GELU_C = math.sqrt(2.0 / math.pi)
_GELU_A = 0.044715


def _cp(*sem):
    return pltpu.CompilerParams(dimension_semantics=sem, vmem_limit_bytes=VMEM_LIMIT)


def _call(body, **kw):
    return pl.pallas_call(body, **kw)


def _dot(a, b):
    return jnp.dot(a, b, preferred_element_type=F32)


def _dot_nt(a, b):
    return lax.dot_general(a, b, (((1,), (1,)), ((), ())), preferred_element_type=F32)


def _dot_tn(a, b):
    return lax.dot_general(a, b, (((0,), (0,)), ((), ())), preferred_element_type=F32)


def _rstd(x):
    return lax.rsqrt(jnp.mean(x * x, axis=-1, keepdims=True) + EPS)


def _rms_bwd(x, g, dy):
    r = _rstd(x)
    xh = x * r
    gd = dy * g
    dx = r * (gd - xh * jnp.mean(gd * xh, axis=-1, keepdims=True))
    dg = jnp.sum(dy * xh, axis=0, keepdims=True)
    return dx, dg


def _gelu(x):
    return 0.5 * x * (1.0 + jnp.tanh(_GELU_C * (x + _GELU_A * (x * x * x))))


def _gelu_grad(x):
    t = jnp.tanh(_GELU_C * (x + _GELU_A * (x * x * x)))
    return 0.5 * (1.0 + t) + 0.5 * x * (1.0 - t * t) * (_GELU_C * (1.0 + 3.0 * _GELU_A * x * x))


def _dot2(x, ones_mat):
    hi = x.astype(BF)
    lo = (x - hi.astype(F32)).astype(BF)
    return _dot(hi, ones_mat) + _dot(lo, ones_mat)


def _row_tile(m, want=512):
    return min(want, m)


def rms_fwd(x, g):
    m, d = x.shape
    tm = _row_tile(m)

    def body(x_ref, g_ref, o_ref):
        xv = x_ref[...]
        o_ref[...] = (xv * _rstd(xv) * g_ref[...]).astype(BF)

    return _call(
        body, name="rms_fwd", out_shape=jax.ShapeDtypeStruct((m, d), BF), grid=(m // tm,),
        in_specs=[pl.BlockSpec((tm, d), lambda i: (i, 0)), pl.BlockSpec((1, d), lambda i: (0, 0))],
        out_specs=pl.BlockSpec((tm, d), lambda i: (i, 0)), compiler_params=_cp("parallel"),
    )(x, g)


def ffn_up(a, ga, ig, iu):
    m, d = a.shape
    tm = _row_tile(m)
    nb = ga.shape[-1]

    def body(a_ref, wg_ref, wu_ref, g_ref, u_ref, h_ref):
        av = a_ref[...]
        g = _dot(av, wg_ref[...])
        u = _dot(av, wu_ref[...])
        g_ref[...] = g.astype(BF)
        u_ref[...] = u.astype(BF)
        h_ref[...] = (g * jax.nn.sigmoid(g) * u).astype(BF)

    blk = jax.ShapeDtypeStruct((N_CHIPS, m, nb), BF)
    ospec = pl.BlockSpec((None, tm, nb), lambda k, i: (k, i, 0))
    return _call(
        body, name="ffn_up", out_shape=(blk, blk, blk), grid=(N_CHIPS, m // tm),
        in_specs=[pl.BlockSpec((tm, d), lambda k, i: (i, 0)),
                  pl.BlockSpec((None, None, d, nb), lambda k, i: (k, ig, 0, 0)),
                  pl.BlockSpec((None, None, d, nb), lambda k, i: (k, iu, 0, 0))],
        out_specs=(ospec, ospec, ospec), compiler_params=_cp("parallel", "parallel"),
    )(a, ga, ga)


def mm_res(lhs, w, which, h, gp, alpha, gn):
    blocked = lhs.ndim == 3
    m = lhs.shape[1] if blocked else lhs.shape[0]
    kb, n = w.shape[2], w.shape[3]
    tm = _row_tile(m)

    def body(l_ref, w_ref, h_ref, gp_ref, gn_ref, f_ref, hn_ref, an_ref):
        acc = None
        for k in range(N_CHIPS):
            lk = l_ref[k] if blocked else l_ref[:, k * kb:(k + 1) * kb]
            t = _dot(lk, w_ref[k])
            acc = t if acc is None else acc + t
        f_ref[...] = acc
        hn = h_ref[...] + alpha * (acc * _rstd(acc) * gp_ref[...])
        hn_ref[...] = hn
        an_ref[...] = (hn * _rstd(hn) * gn_ref[...]).astype(BF)

    lspec = (pl.BlockSpec((N_CHIPS, tm, kb), lambda i: (0, i, 0)) if blocked
             else pl.BlockSpec((tm, N_CHIPS * kb), lambda i: (i, 0)))
    row = pl.BlockSpec((tm, n), lambda i: (i, 0))
    vec = pl.BlockSpec((1, n), lambda i: (0, 0))
    return _call(
        body, name="mm_res", grid=(m // tm,),
        out_shape=(jax.ShapeDtypeStruct((m, n), F32), jax.ShapeDtypeStruct((m, n), F32),
                   jax.ShapeDtypeStruct((m, n), BF)),
        in_specs=[lspec, pl.BlockSpec((N_CHIPS, None, kb, n), lambda i: (0, which, 0, 0)), row, vec, vec],
        out_specs=(row, row, row), compiler_params=_cp("parallel"),
    )(lhs, w, h, gp, gn)


def mm_cb(a, w, out_dtype=BF):
    m, kd = a.shape
    nb = w.shape[-1]
    tm = _row_tile(m)

    def body(a_ref, w_ref, o_ref):
        o_ref[...] = _dot(a_ref[...], w_ref[...]).astype(out_dtype)

    return _call(
        body, name="mm_cb", out_shape=jax.ShapeDtypeStruct((m, N_CHIPS * nb), out_dtype),
        grid=(N_CHIPS, m // tm),
        in_specs=[pl.BlockSpec((tm, kd), lambda k, i: (i, 0)), pl.BlockSpec((None, kd, nb), lambda k, i: (k, 0, 0))],
        out_specs=pl.BlockSpec((tm, nb), lambda k, i: (i, k)), compiler_params=_cp("parallel", "parallel"),
    )(a, w)


def _sgu_core(u_pre, vg_pre, wm_ref, ng_ref, nb_ref, bst_ref, g, c):
    rs = slice(c * CHUNK, (c + 1) * CHUNK)
    cs = slice(g * CHUNK, (g + 1) * CHUNK)
    ug = _gelu(u_pre[rs, cs].astype(F32))
    vgl = _gelu(vg_pre[rs, cs].astype(F32))
    mu = jnp.mean(vgl, axis=-1, keepdims=True)
    cen = vgl - mu
    rstd = lax.rsqrt(jnp.mean(cen * cen, axis=-1, keepdims=True) + EPS)
    xh = cen * rstd
    vn = xh * ng_ref[g:g + 1, :] + nb_ref[g:g + 1, :]
    mixed = _dot(wm_ref[g], vn.astype(BF)) + bst_ref[g]
    return ug, xh, rstd, vn, mixed


def sgu_fwd(proj, wm, ng, nb, bst):
    m = proj.shape[0]
    tm = _row_tile(m)
    wd = SGU_GROUPS * CHUNK

    def body(u_ref, v_ref, wm_ref, ng_ref, nb_ref, bst_ref, o_ref):
        for c in range(tm // CHUNK):
            for g in range(SGU_GROUPS):
                ug, _, _, _, mixed = _sgu_core(u_ref, v_ref, wm_ref, ng_ref, nb_ref, bst_ref, g, c)
                o_ref[c * CHUNK:(c + 1) * CHUNK, g * CHUNK:(g + 1) * CHUNK] = (ug * mixed).astype(BF)

    full = lambda shape: pl.BlockSpec(shape, lambda i: (0,) * len(shape))
    return _call(
        body, name="sgu_fwd", out_shape=jax.ShapeDtypeStruct((m, wd), BF), grid=(m // tm,),
        in_specs=[pl.BlockSpec((tm, wd), lambda i: (i, 0)), pl.BlockSpec((tm, wd), lambda i: (i, 1)),
                  full(wm.shape), full(ng.shape), full(nb.shape), full(bst.shape)],
        out_specs=pl.BlockSpec((tm, wd), lambda i: (i, 0)), compiler_params=_cp("parallel"),
    )(proj, proj, wm, ng, nb, bst)


def _sb_tiles(m):
    tq = min(512, m)
    tk = min(256, m)
    return tq, tk


def _log_sigmoid_pair(z, mask):
    ls = jnp.minimum(z, 0.0) - jnp.log(1.0 + jnp.exp(-jnp.abs(z)))
    l1 = ls - z
    return ls, (l1 if mask is None else jnp.where(mask, l1, 0.0))


def _sb_diag_mask(r0, r1, d, tk):
    rows = r0 + lax.broadcasted_iota(jnp.int32, (r1 - r0, tk), 0)
    cols = d * tk + lax.broadcasted_iota(jnp.int32, (r1 - r0, tk), 1)
    return cols < rows


def _head_masks():
    lane = lax.broadcasted_iota(jnp.int32, (1, LANES), 1)
    return [lane < SB_HEAD_DIM, lane >= SB_HEAD_DIM]


def sb_fwd(proj):
    m = proj.shape[0]
    tq, tk = _sb_tiles(m)
    ndiag = tq // tk

    def body(q_ref, k_ref, v_ref, o_ref, tot_ref, qs, acc, car):
        i = pl.program_id(1)
        nfull = i * ndiag
        heads = _head_masks()
        upper = (lax.broadcasted_iota(jnp.int32, (tk, tk), 0) > lax.broadcasted_iota(jnp.int32, (tk, tk), 1)).astype(BF)
        qv = q_ref[...]
        for hd in range(2):
            qs[hd] = jnp.where(heads[hd], qv, jnp.zeros_like(qv)) * SB_SCALE
        acc[...] = jnp.zeros_like(acc)
        car[...] = jnp.zeros_like(car)

        def block(ks, r0, mask):
            rs = slice(r0, tq)
            kb = k_ref[pl.ds(ks, tk), :]
            vb = v_ref[pl.ds(ks, tk), :]
            zs = [_dot_nt(qs[hd, rs, :], kb) for hd in range(2)]
            mid = []
            for hd in range(2):
                ls, l1 = _log_sigmoid_pair(zs[hd], mask)
                mid.append((ls, l1, _dot2(l1, upper)))
            pvs = []
            for hd in range(2):
                ls, l1, cum = mid[hd]
                a = jnp.exp(ls + (cum + car[hd, rs, :]))
                if mask is not None:
                    a = jnp.where(mask, a, 0.0)
                pvs.append(_dot(a.astype(BF), vb))
            for hd in range(2):
                acc[hd, rs, :] += pvs[hd]
                car[hd, rs, :] += jnp.sum(mid[hd][1], axis=-1, keepdims=True)

        for d in reversed(range(ndiag)):
            block(pl.multiple_of((nfull + d) * tk, tk), d * tk, _sb_diag_mask(d * tk, tq, d, tk))

        def step(jj, carry):
            block(pl.multiple_of((nfull - 1 - jj) * tk, tk), 0, None)
            return carry

        lax.fori_loop(0, nfull, step, 0)
        o_ref[...] = jnp.where(heads[0], acc[0], acc[1]).astype(BF)
        tot_ref[...] = jnp.where(heads[0], car[0], car[1])

    qb = 2 * 512 // LANES
    return _call(
        body, name="sb_fwd", grid=(4, m // tq),
        out_shape=(jax.ShapeDtypeStruct((m, 512), BF), jax.ShapeDtypeStruct((m, 512), F32)),
        in_specs=[pl.BlockSpec((tq, LANES), lambda p, i: (i, qb + p)),
                  pl.BlockSpec((m, LANES), lambda p, i: (0, qb + 4 + p)),
                  pl.BlockSpec((m, LANES), lambda p, i: (0, qb + 8 + p))],
        out_specs=(pl.BlockSpec((tq, LANES), lambda p, i: (i, p)), pl.BlockSpec((tq, LANES), lambda p, i: (i, p))),
        scratch_shapes=[pltpu.VMEM((2, tq, LANES), BF), pltpu.VMEM((2, tq, LANES), F32), pltpu.VMEM((2, tq, 1), F32)],
        compiler_params=_cp("parallel", "arbitrary"),
    )(proj, proj, proj)


def merge_norm(oa, ob, ga, gb):
    m, w = oa.shape
    tm = _row_tile(m)

    def body(a_ref, b_ref, ga_ref, gb_ref, o_ref):
        av = a_ref[...].astype(F32)
        bv = b_ref[...].astype(F32)
        o_ref[:, :w] = (av * _rstd(av) * ga_ref[...]).astype(BF)
        o_ref[:, w:] = (bv * _rstd(bv) * gb_ref[...]).astype(BF)

    row = pl.BlockSpec((tm, w), lambda i: (i, 0))
    vec = pl.BlockSpec((1, w), lambda i: (0, 0))
    return _call(
        body, name="merge_norm", out_shape=jax.ShapeDtypeStruct((m, 2 * w), BF), grid=(m // tm,),
        in_specs=[row, row, vec, vec], out_specs=pl.BlockSpec((tm, 2 * w), lambda i: (i, 0)),
        compiler_params=_cp("parallel"),
    )(oa, ob, ga, gb)


def _xa_probs(qh, kh):
    logits = _dot_nt(qh, kh) * XA_SCALE
    e = jnp.exp(logits - jnp.max(logits, axis=-1, keepdims=True))
    return e / jnp.sum(e, axis=-1, keepdims=True)


def xa_fwd(xq, kv):
    m, d = xq.shape
    mm = kv.shape[0]
    tm = _row_tile(m)

    def body(q_ref, kv_ref, o_ref):
        for hd in range(XA_HEADS):
            cs = slice(hd * XA_HEAD_DIM, (hd + 1) * XA_HEAD_DIM)
            p = _xa_probs(q_ref[:, cs], kv_ref[:, cs])
            vh = kv_ref[:, d + hd * XA_HEAD_DIM:d + (hd + 1) * XA_HEAD_DIM]
            o_ref[:, cs] = _dot(p.astype(BF), vh).astype(BF)

    return _call(
        body, name="xa_fwd", out_shape=jax.ShapeDtypeStruct((m, d), BF), grid=(m // tm,),
        in_specs=[pl.BlockSpec((tm, d), lambda i: (i, 0)), pl.BlockSpec((mm, 2 * d), lambda i: (0, 0))],
        out_specs=pl.BlockSpec((tm, d), lambda i: (i, 0)), compiler_params=_cp("parallel"),
    )(xq, kv)


def norm_bwd(h, gn, d_a=None, d_res=None, target=None, f_prev=None, gp_prev=None, alpha_prev=1.0):
    m, d = h.shape
    tm = _row_tile(m)
    has_loss = target is not None
    has_res = d_res is not None
    has_prev = f_prev is not None

    def body(*refs):
        refs = list(refs)
        h_ref, gn_ref = refs[0], refs[1]
        pos = 2
        da_ref = dres_ref = t_ref = f_ref = gp_ref = None
        if has_loss:
            t_ref = refs[pos]; pos += 1
        else:
            da_ref = refs[pos]; pos += 1
        if has_res:
            dres_ref = refs[pos]; pos += 1
        if has_prev:
            f_ref, gp_ref = refs[pos], refs[pos + 1]; pos += 2
        dh_ref, dgn_ref = refs[pos], refs[pos + 1]; pos += 2
        df_ref = dgp_ref = loss_ref = None
        if has_prev:
            df_ref, dgp_ref = refs[pos], refs[pos + 1]; pos += 2
        if has_loss:
            loss_ref = refs[pos]

        first = pl.program_id(0) == 0
        hv = h_ref[...]
        gn = gn_ref[...]
        if has_loss:
            err = hv * _rstd(hv) * gn - t_ref[...]
            da = err * (1.0 / d)
            part = 0.5 * jnp.sum(jnp.sum(err * err, axis=-1, keepdims=True) * (1.0 / d))

            @pl.when(first)
            def _():
                loss_ref[...] = jnp.zeros_like(loss_ref)

            loss_ref[...] += part
        else:
            da = da_ref[...].astype(F32)
        dx, dgn = _rms_bwd(hv, gn, da)
        dh = dx + dres_ref[...] if has_res else dx
        dh_ref[...] = dh

        @pl.when(first)
        def _():
            dgn_ref[...] = jnp.zeros_like(dgn_ref)

        dgn_ref[...] += dgn
        if has_prev:
            dfv, dgp = _rms_bwd(f_ref[...], gp_ref[...], dh)
            df_ref[...] = (alpha_prev * dfv).astype(BF)

            @pl.when(first)
            def _():
                dgp_ref[...] = jnp.zeros_like(dgp_ref)

            dgp_ref[...] += alpha_prev * dgp

    row = pl.BlockSpec((tm, d), lambda i: (i, 0))
    vec = pl.BlockSpec((1, d), lambda i: (0, 0))
    ins, in_specs = [h, gn], [row, vec]
    ins.append(target if has_loss else d_a); in_specs.append(row)
    if has_res:
        ins.append(d_res); in_specs.append(row)
    if has_prev:
        ins += [f_prev, gp_prev]; in_specs += [row, vec]
    outs = [jax.ShapeDtypeStruct((m, d), F32), jax.ShapeDtypeStruct((1, d), F32)]
    out_specs = [row, vec]
    names = ["d_h", "d_gn"]
    if has_prev:
        outs += [jax.ShapeDtypeStruct((m, d), BF), jax.ShapeDtypeStruct((1, d), F32)]
        out_specs += [row, vec]
        names += ["d_f", "d_gp"]
    if has_loss:
        outs.append(jax.ShapeDtypeStruct((8, LANES), F32))
        out_specs.append(pl.BlockSpec((8, LANES), lambda i: (0, 0)))
        names.append("loss")
    res = _call(
        body, name="norm_bwd", out_shape=tuple(outs), grid=(m // tm,), in_specs=in_specs,
        out_specs=tuple(out_specs), compiler_params=_cp("arbitrary"),
    )(*ins)
    return dict(zip(names, res))


def ffn_bwd_act(df, gb, which, g, u):
    m, d = df.shape
    nb = g.shape[-1]
    tm = _row_tile(m)

    def body(df_ref, w_ref, g_ref, u_ref, dg_ref, du_ref):
        dh = _dot_nt(df_ref[...], w_ref[...])
        gv = g_ref[...].astype(F32)
        uv = u_ref[...].astype(F32)
        s = jax.nn.sigmoid(gv)
        dg_ref[...] = (dh * uv * (s * (1.0 + gv * (1.0 - s)))).astype(BF)
        du_ref[...] = (dh * gv * s).astype(BF)

    blk = jax.ShapeDtypeStruct((N_CHIPS, m, nb), BF)
    aspec = pl.BlockSpec((None, tm, nb), lambda k, i: (k, i, 0))
    return _call(
        body, name="ffn_bwd_act", out_shape=(blk, blk), grid=(N_CHIPS, m // tm),
        in_specs=[pl.BlockSpec((tm, d), lambda k, i: (i, 0)),
                  pl.BlockSpec((None, None, nb, d), lambda k, i: (k, which, 0, 0)), aspec, aspec],
        out_specs=(aspec, aspec), compiler_params=_cp("parallel", "parallel"),
    )(df, gb, g, u)


def mm_tn(a, b, dest, a_spec, b_spec, o_spec, acc_shape, msteps):
    def body(a_ref, b_ref, dest_ref, o_ref, acc):
        del dest_ref
        ms = pl.program_id(1)

        @pl.when(ms == 0)
        def _():
            acc[...] = jnp.zeros_like(acc)

        acc[...] += _dot_tn(a_ref[...], b_ref[...])

        @pl.when(ms == msteps - 1)
        def _():
            o_ref[...] = acc[...].astype(o_ref.dtype)

    return _call(
        body, name="mm_tn", out_shape=jax.ShapeDtypeStruct(dest.shape, dest.dtype), grid=(N_CHIPS, msteps),
        in_specs=[a_spec, b_spec, pl.BlockSpec(memory_space=pl.ANY)], out_specs=o_spec,
        scratch_shapes=[pltpu.VMEM(acc_shape, F32)], input_output_aliases={2: 0},
        compiler_params=_cp("parallel", "arbitrary"),
    )(a, b, dest)


def _act_spec(arr, tm, nb):
    if arr.ndim == 3:
        return pl.BlockSpec((None, tm, nb), lambda k, ms: (k, ms, 0))
    return pl.BlockSpec((tm, nb), lambda k, ms: (ms, k))


def grad_cb(a, dout, dest, which=None):
    m, kd = a.shape
    nb = dest.shape[-1]
    tm = _row_tile(m)
    if which is None:
        o_spec = pl.BlockSpec((None, kd, nb), lambda k, ms: (k, 0, 0))
    else:
        o_spec = pl.BlockSpec((None, None, kd, nb), lambda k, ms: (k, which, 0, 0))
    return mm_tn(a, dout, dest, pl.BlockSpec((tm, kd), lambda k, ms: (ms, 0)), _act_spec(dout, tm, nb), o_spec,
                 (kd, nb), m // tm)


def grad_rb(a, dout, dest, which):
    m, n = dout.shape
    kb = dest.shape[-2]
    tm = _row_tile(m)
    o_spec = pl.BlockSpec((None, None, kb, n), lambda k, ms: (k, which, 0, 0))
    return mm_tn(a, dout, dest, _act_spec(a, tm, kb), pl.BlockSpec((tm, n), lambda k, ms: (ms, 0)), o_spec,
                 (kb, n), m // tm)


def mm_nt_cb(pairs, n, out_dtype):
    d0 = pairs[0][0]
    m = d0.shape[1] if d0.ndim == 3 else d0.shape[0]
    tm = _row_tile(m)
    npair = len(pairs)

    def body(*refs):
        o_ref, acc = refs[2 * npair], refs[2 * npair + 1]
        k = pl.program_id(1)

        @pl.when(k == 0)
        def _():
            acc[...] = jnp.zeros_like(acc)

        for p in range(npair):
            acc[...] += _dot_nt(refs[2 * p][...], refs[2 * p + 1][...])

        @pl.when(k == N_CHIPS - 1)
        def _():
            o_ref[...] = acc[...].astype(out_dtype)

    ins, in_specs = [], []
    for dout, w, which in pairs:
        nb = w.shape[-1]
        if dout.ndim == 3:
            in_specs.append(pl.BlockSpec((None, tm, nb), lambda i, k: (k, i, 0)))
        else:
            in_specs.append(pl.BlockSpec((tm, nb), lambda i, k: (i, k)))
        if w.ndim == 4:
            in_specs.append(pl.BlockSpec((None, None, n, nb), lambda i, k, which=which: (k, which, 0, 0)))
        else:
            in_specs.append(pl.BlockSpec((None, n, nb), lambda i, k: (k, 0, 0)))
        ins += [dout, w]
    return _call(
        body, name="mm_nt_cb", out_shape=jax.ShapeDtypeStruct((m, n), out_dtype), grid=(m // tm, N_CHIPS),
        in_specs=in_specs, out_specs=pl.BlockSpec((tm, n), lambda i, k: (i, 0)),
        scratch_shapes=[pltpu.VMEM((tm, n), F32)], compiler_params=_cp("parallel", "arbitrary"),
    )(*ins)


def mm_nt_rb(dout, w, which, out_dtype):
    m, n = dout.shape
    kb = w.shape[2]
    tm = _row_tile(m)

    def body(d_ref, w_ref, o_ref):
        o_ref[...] = _dot_nt(d_ref[...], w_ref[...]).astype(out_dtype)

    return _call(
        body, name="mm_nt_rb", out_shape=jax.ShapeDtypeStruct((m, N_CHIPS * kb), out_dtype), grid=(N_CHIPS, m // tm),
        in_specs=[pl.BlockSpec((tm, n), lambda k, i: (i, 0)),
                  pl.BlockSpec((None, None, kb, n), lambda k, i: (k, which, 0, 0))],
        out_specs=pl.BlockSpec((tm, kb), lambda k, i: (i, k)), compiler_params=_cp("parallel", "parallel"),
    )(dout, w)


def merge_norm_bwd(oa, ob, dmerged, ga, gb):
    m, w = oa.shape
    tm = _row_tile(m)

    def body(a_ref, b_ref, dm_ref, ga_ref, gb_ref, da_ref, db_ref, dga_ref, dgb_ref):
        @pl.when(pl.program_id(0) == 0)
        def _():
            dga_ref[...] = jnp.zeros_like(dga_ref)
            dgb_ref[...] = jnp.zeros_like(dgb_ref)

        da, dga = _rms_bwd(a_ref[...].astype(F32), ga_ref[...], dm_ref[:, :w].astype(F32))
        db, dgb = _rms_bwd(b_ref[...].astype(F32), gb_ref[...], dm_ref[:, w:].astype(F32))
        da_ref[...] = da
        db_ref[...] = db
        dga_ref[...] += dga
        dgb_ref[...] += dgb

    row = pl.BlockSpec((tm, w), lambda i: (i, 0))
    vec = pl.BlockSpec((1, w), lambda i: (0, 0))
    return _call(
        body, name="merge_norm_bwd", grid=(m // tm,),
        out_shape=(jax.ShapeDtypeStruct((m, w), F32), jax.ShapeDtypeStruct((m, w), F32),
                   jax.ShapeDtypeStruct((1, w), F32), jax.ShapeDtypeStruct((1, w), F32)),
        in_specs=[row, row, pl.BlockSpec((tm, 2 * w), lambda i: (i, 0)), vec, vec],
        out_specs=(row, row, vec, vec), compiler_params=_cp("arbitrary"),
    )(oa, ob, dmerged, ga, gb)


def sgu_bwd(proj, d_oa, wm, wmt, ng, nb, bst):
    m = proj.shape[0]
    tm = _row_tile(m)
    wd = SGU_GROUPS * CHUNK

    def body(u_ref, v_ref, do_ref, wm_ref, wmt_ref, ng_ref, nb_ref, bst_ref,
             dp_ref, dw_ref, dbt_ref, dng_ref, dnb_ref):
        @pl.when(pl.program_id(0) == 0)
        def _():
            dw_ref[...] = jnp.zeros_like(dw_ref)
            dbt_ref[...] = jnp.zeros_like(dbt_ref)
            dng_ref[...] = jnp.zeros_like(dng_ref)
            dnb_ref[...] = jnp.zeros_like(dnb_ref)

        causal = lax.broadcasted_iota(jnp.int32, (CHUNK, CHUNK), 0) >= lax.broadcasted_iota(jnp.int32, (CHUNK, CHUNK), 1)
        for c in range(tm // CHUNK):
            rs = slice(c * CHUNK, (c + 1) * CHUNK)
            for g in range(SGU_GROUPS):
                cs = slice(g * CHUNK, (g + 1) * CHUNK)
                ug, xh, rstd, vn, mixed = _sgu_core(u_ref, v_ref, wm_ref, ng_ref, nb_ref, bst_ref, g, c)
                do = do_ref[rs, cs]
                dug = do * mixed
                dmix = do * ug
                dmb = dmix.astype(BF)
                dbt_ref[g] += jnp.sum(dmix, axis=-1, keepdims=True)
                dw_ref[g] += jnp.where(causal, _dot_nt(dmb, vn.astype(BF)), 0.0)
                dvn = _dot(wmt_ref[g], dmb)
                dng_ref[g:g + 1, :] += jnp.sum(dvn * xh, axis=0, keepdims=True)
                dnb_ref[g:g + 1, :] += jnp.sum(dvn, axis=0, keepdims=True)
                dxh = dvn * ng_ref[g:g + 1, :]
                dvg = rstd * (dxh - jnp.mean(dxh, axis=-1, keepdims=True)
                              - xh * jnp.mean(dxh * xh, axis=-1, keepdims=True))
                dp_ref[rs, cs] = (dug * _gelu_grad(u_ref[rs, cs].astype(F32))).astype(BF)
                dp_ref[rs, wd + g * CHUNK:wd + (g + 1) * CHUNK] = (dvg * _gelu_grad(v_ref[rs, cs].astype(F32))).astype(BF)

    full = lambda shape: pl.BlockSpec(shape, lambda i: (0,) * len(shape))
    return _call(
        body, name="sgu_bwd", grid=(m // tm,),
        out_shape=(jax.ShapeDtypeStruct((m, 2 * wd), BF), jax.ShapeDtypeStruct(wm.shape, F32),
                   jax.ShapeDtypeStruct(bst.shape, F32), jax.ShapeDtypeStruct(ng.shape, F32),
                   jax.ShapeDtypeStruct(nb.shape, F32)),
        in_specs=[pl.BlockSpec((tm, wd), lambda i: (i, 0)), pl.BlockSpec((tm, wd), lambda i: (i, 1)),
                  pl.BlockSpec((tm, wd), lambda i: (i, 0)),
                  full(wm.shape), full(wmt.shape), full(ng.shape), full(nb.shape), full(bst.shape)],
        out_specs=(pl.BlockSpec((tm, 2 * wd), lambda i: (i, 0)), full(wm.shape), full(bst.shape), full(ng.shape),
                   full(nb.shape)),
        compiler_params=_cp("arbitrary"),
    )(proj, proj, d_oa, wm, wmt, ng, nb, bst)


def sb_bwd(proj, tot, d_ob):
    m = proj.shape[0]
    tq, tk = _sb_tiles(m)

    ndiag = tq // tk

    def body(q_ref, k_ref, v_ref, tot_ref, do_ref, dq_ref, dk_ref, dv_ref, qs, dos, tots, dqa, cl1, cg):
        i = pl.program_id(1)

        @pl.when(i == 0)
        def _():
            dk_ref[...] = jnp.zeros_like(dk_ref)
            dv_ref[...] = jnp.zeros_like(dv_ref)

        nfull = i * ndiag
        heads = _head_masks()
        r_io = lax.broadcasted_iota(jnp.int32, (tk, tk), 0)
        c_io = lax.broadcasted_iota(jnp.int32, (tk, tk), 1)
        incl = (r_io <= c_io).astype(BF)
        excl = (r_io < c_io).astype(BF)
        qv = q_ref[...]
        dov = do_ref[...].astype(BF)
        totv = tot_ref[...]
        for hd in range(2):
            qs[hd] = jnp.where(heads[hd], qv, jnp.zeros_like(qv)) * SB_SCALE
            dos[hd] = jnp.where(heads[hd], dov, jnp.zeros_like(dov))
            tots[hd] = jnp.max(jnp.where(heads[hd], totv, -jnp.inf), axis=-1, keepdims=True)
        dqa[...] = jnp.zeros_like(dqa)
        cl1[...] = jnp.zeros_like(cl1)
        cg[...] = jnp.zeros_like(cg)

        def block(ks, r0, mask):
            rs = slice(r0, tq)
            kb = k_ref[pl.ds(ks, tk), :]
            vb = v_ref[pl.ds(ks, tk), :]
            qc = [qs[hd, rs, :] for hd in range(2)]
            doc = [dos[hd, rs, :] for hd in range(2)]
            zs = [_dot_nt(qc[hd], kb) for hd in range(2)]
            das = [_dot_nt(doc[hd], vb) for hd in range(2)]
            s1 = []
            for hd in range(2):
                ls, l1 = _log_sigmoid_pair(zs[hd], mask)
                s1.append((ls, l1, _dot2(l1, incl)))
            s2 = []
            for hd in range(2):
                ls, l1, pre = s1[hd]
                a = jnp.exp(ls + (tots[hd, rs, :] - (pre + cl1[hd, rs, :])))
                if mask is not None:
                    a = jnp.where(mask, a, 0.0)
                gmat = a * das[hd]
                s2.append((a, gmat, _dot2(gmat, excl)))
            dk_sum = dv_sum = None
            for hd in range(2):
                a, gmat, pref = s2[hd]
                sg = jnp.exp(s1[hd][0])
                dz = gmat * (1.0 - sg) - (pref + cg[hd, rs, :]) * sg
                if mask is not None:
                    dz = jnp.where(mask, dz, 0.0)
                dz = dz.astype(BF)
                dqa[hd, rs, :] += _dot(dz, kb)
                dk_t = _dot_tn(dz, qc[hd])
                dv_t = _dot_tn(a.astype(BF), doc[hd])
                dk_sum = dk_t if dk_sum is None else dk_sum + dk_t
                dv_sum = dv_t if dv_sum is None else dv_sum + dv_t
            for hd in range(2):
                cl1[hd, rs, :] += jnp.sum(s1[hd][1], axis=-1, keepdims=True)
                cg[hd, rs, :] += jnp.sum(s2[hd][1], axis=-1, keepdims=True)
            dk_ref[pl.ds(ks, tk), :] += dk_sum
            dv_ref[pl.ds(ks, tk), :] += dv_sum

        def step(j, carry):
            block(pl.multiple_of(j * tk, tk), 0, None)
            return carry

        lax.fori_loop(0, nfull, step, 0)
        for d in range(ndiag):
            block(pl.multiple_of((nfull + d) * tk, tk), d * tk, _sb_diag_mask(d * tk, tq, d, tk))
        dq_ref[...] = jnp.where(heads[0], dqa[0], dqa[1]) * SB_SCALE

    qb = 2 * 512 // LANES
    tile = pl.BlockSpec((tq, LANES), lambda p, i: (i, p))
    seq = pl.BlockSpec((m, LANES), lambda p, i: (0, p))
    out = jax.ShapeDtypeStruct((m, 512), F32)
    return _call(
        body, name="sb_bwd", grid=(4, m // tq), out_shape=(out, out, out),
        in_specs=[pl.BlockSpec((tq, LANES), lambda p, i: (i, qb + p)),
                  pl.BlockSpec((m, LANES), lambda p, i: (0, qb + 4 + p)),
                  pl.BlockSpec((m, LANES), lambda p, i: (0, qb + 8 + p)), tile, tile],
        out_specs=(tile, seq, seq),
        scratch_shapes=[pltpu.VMEM((2, tq, LANES), BF), pltpu.VMEM((2, tq, LANES), BF), pltpu.VMEM((2, tq, 1), F32),
                        pltpu.VMEM((2, tq, LANES), F32), pltpu.VMEM((2, tq, 1), F32), pltpu.VMEM((2, tq, 1), F32)],
        compiler_params=_cp("parallel", "arbitrary"),
    )(proj, proj, proj, tot, d_ob)


def xa_bwd(xq, kv, d_o):
    m, d = xq.shape
    mm = kv.shape[0]
    tm = _row_tile(m)

    def body(q_ref, kv_ref, do_ref, dq_ref, dkv_ref):
        @pl.when(pl.program_id(0) == 0)
        def _():
            dkv_ref[...] = jnp.zeros_like(dkv_ref)

        for hd in range(XA_HEADS):
            cs = slice(hd * XA_HEAD_DIM, (hd + 1) * XA_HEAD_DIM)
            vs = slice(d + hd * XA_HEAD_DIM, d + (hd + 1) * XA_HEAD_DIM)
            qh = q_ref[:, cs]
            kh = kv_ref[:, cs]
            doh = do_ref[:, cs]
            p = _xa_probs(qh, kh)
            dp = _dot_nt(doh, kv_ref[:, vs])
            ds = (p * (dp - jnp.sum(p * dp, axis=-1, keepdims=True))).astype(BF)
            dq_ref[:, cs] = (_dot(ds, kh) * XA_SCALE).astype(BF)
            dkv_ref[:, cs] += _dot_tn(ds, qh) * XA_SCALE
            dkv_ref[:, vs] += _dot_tn(p.astype(BF), doh)

    row = pl.BlockSpec((tm, d), lambda i: (i, 0))
    whole = pl.BlockSpec((mm, 2 * d), lambda i: (0, 0))
    return _call(
        body, name="xa_bwd", grid=(m // tm,),
        out_shape=(jax.ShapeDtypeStruct((m, d), BF), jax.ShapeDtypeStruct((mm, 2 * d), F32)),
        in_specs=[row, whole, row], out_specs=(row, whole), compiler_params=_cp("arbitrary"),
    )(xq, kv, d_o)


def adamw(w, g, mom, vel):
    r, c = w.shape
    tr = r
    for cand in (512, 256, 128, 64, 32, 16, 8):
        if r % cand == 0:
            tr = cand
            break

    def body(w_ref, g_ref, m_ref, v_ref, d_ref, nm_ref, nv_ref):
        gv = g_ref[...]
        mn = ADAM_B1 * m_ref[...] + (1.0 - ADAM_B1) * gv
        vn = ADAM_B2 * v_ref[...] + (1.0 - ADAM_B2) * (gv * gv)
        m_hat = mn / (1.0 - ADAM_B1 ** ADAM_STEP)
        v_hat = vn / (1.0 - ADAM_B2 ** ADAM_STEP)
        d_ref[...] = -ADAM_LR * (m_hat / (jnp.sqrt(v_hat) + ADAM_EPS) + ADAM_WD * w_ref[...])
        nm_ref[...] = mn
        nv_ref[...] = vn

    spec = pl.BlockSpec((tr, c), lambda i: (i, 0))
    out = jax.ShapeDtypeStruct((r, c), F32)
    return _call(
        body, name="adamw", out_shape=(out, out, out), grid=(r // tr,), in_specs=[spec] * 4,
        out_specs=(spec, spec, spec), compiler_params=_cp("parallel"),
    )(w, g, mom, vel)


def _place():
    x, y, c = lax.axis_index("x"), lax.axis_index("y"), lax.axis_index("c")
    others = [(1 - x, y), (x, 1 - y), (1 - x, 1 - y)]
    return x, y, c, others


_HALF = {"A": (0, 2), "B": (0, 1), "C": (1, 128), "D": (0, 512), "E": (0, 512)}
_GROUPS = ("A", "B", "C", "D", "E")


def _half_of(ref, name, hc, lead=0):
    axis, size = _HALF[name]
    idx = [slice(None)] * (lead + axis) + [pl.ds(hc * size, size)]
    return ref.at[tuple(idx)]


def gather_weights(loc):
    names = list(_GROUPS)
    n = len(names)

    def body(*refs):
        src = dict(zip(names, refs[:n]))
        out = dict(zip(names, refs[2 * n:3 * n]))
        send, recv = refs[3 * n], refs[3 * n + 1]
        x, y, c, others = _place()
        me = 2 * x + y

        def cp(a, j, chip_idx, hc, to, from_src, stage):
            dst = _half_of(out[names[a]].at[chip_idx], names[a], hc)
            s = _half_of(src[names[a]], names[a], hc) if from_src else dst
            k = a * 6 + stage * 3 + j
            return pltpu.make_async_remote_copy(src_ref=s, dst_ref=dst, send_sem=send.at[k], recv_sem=recv.at[k],
                                                device_id=to, device_id_type=MESH)

        first = [cp(a, j, me, c, (ox, oy, c), True, 0) for a in range(n) for j, (ox, oy) in enumerate(others)]
        for f in first:
            f.start()
        passed = []
        for j, (ox, oy) in enumerate(others):
            for a in range(n):
                cp(a, j, 2 * ox + oy, c, (x, y, c), False, 0).wait_recv()
                p = cp(a, j, 2 * ox + oy, c, (x, y, 1 - c), False, 1)
                p.start()
                passed.append(p)
        for j, (ox, oy) in enumerate(others):
            for a in range(n):
                cp(a, j, 2 * ox + oy, 1 - c, (x, y, c), False, 1).wait_recv()
        for f in first + passed:
            f.wait_send()

    me = 2 * lax.axis_index("x") + lax.axis_index("y")
    init = []
    for nm in names:
        full = lax.empty((N_CHIPS,) + loc[nm].shape, loc[nm].dtype)
        init.append(lax.dynamic_update_slice(full, loc[nm][None], (me,) + (0,) * loc[nm].ndim))
    hbm = pl.BlockSpec(memory_space=pl.ANY)
    outs = tuple(jax.ShapeDtypeStruct(t.shape, t.dtype) for t in init)
    res = _call(
        body, name="gather_weights", out_shape=outs, in_specs=[hbm] * (2 * n), out_specs=(hbm,) * n,
        input_output_aliases={n + a: a for a in range(n)},
        scratch_shapes=[pltpu.SemaphoreType.DMA((6 * n,)), pltpu.SemaphoreType.DMA((6 * n,))],
    )(*[loc[nm] for nm in names], *init)
    return dict(zip(names, res))


def _half_shape(name, shape):
    axis, size = _HALF[name]
    s = list(shape)
    s[axis] = size
    return tuple(s)


def rs_to_sibling(grads):
    names = list(_GROUPS)
    n = len(names)

    def body(*refs):
        src = dict(zip(names, refs[:n]))
        out = dict(zip(names, refs[n:2 * n]))
        send, recv = refs[2 * n], refs[2 * n + 1]
        x, y, c, _ = _place()
        copies = []
        for a, nm in enumerate(names):
            copies.append(pltpu.make_async_remote_copy(
                src_ref=_half_of(src[nm], nm, 1 - c, lead=1), dst_ref=out[nm], send_sem=send.at[a], recv_sem=recv.at[a],
                device_id=(x, y, 1 - c), device_id_type=MESH))
        for cpy in copies:
            cpy.start()
        for cpy in copies:
            cpy.wait()

    hbm = pl.BlockSpec(memory_space=pl.ANY)
    outs = tuple(jax.ShapeDtypeStruct((N_CHIPS,) + _half_shape(nm, grads[nm].shape[1:]), grads[nm].dtype)
                 for nm in names)
    res = _call(
        body, name="rs_to_sibling", out_shape=outs, in_specs=[hbm] * n, out_specs=(hbm,) * n,
        scratch_shapes=[pltpu.SemaphoreType.DMA((n,)), pltpu.SemaphoreType.DMA((n,))],
    )(*[grads[nm] for nm in names])
    return dict(zip(names, res))


def _tile2(shape):
    lead = shape[:-2]
    return lead, shape[-2:]


def add_halves(name, mine, got, c_idx):
    axis, size = _HALF[name]
    hshape = got.shape
    lead, last2 = hshape[:-2], hshape[-2:]
    nlead = len(lead)
    haxis = 1 + axis

    def body(c_ref, m_ref, g_ref, o_ref):
        del c_ref
        o_ref[...] = (m_ref[...].astype(F32) + g_ref[...].astype(F32)).astype(o_ref.dtype)

    blk = (None,) * nlead + last2

    def got_map(*idx):
        return tuple(idx[:nlead]) + (0, 0)

    def mine_map(*idx):
        lead_idx = list(idx[:nlead])
        c = idx[nlead][0]
        if haxis < nlead:
            lead_idx[haxis] = lead_idx[haxis] + c * size
            return tuple(lead_idx) + (0, 0)
        return tuple(lead_idx) + (c, 0)

    grid_spec = pltpu.PrefetchScalarGridSpec(
        num_scalar_prefetch=1, grid=lead,
        in_specs=[pl.BlockSpec(blk, mine_map), pl.BlockSpec(blk, got_map)],
        out_specs=pl.BlockSpec(blk, got_map))
    return _call(
        body, name="add_halves", out_shape=jax.ShapeDtypeStruct(hshape, got.dtype), grid_spec=grid_spec,
        compiler_params=_cp(*(("parallel",) * nlead)),
    )(c_idx, mine, got)


def rs_to_chips(summed):
    names = list(_GROUPS)
    n = len(names)

    def body(*refs):
        src = dict(zip(names, refs[:n]))
        out = dict(zip(names, refs[n:2 * n]))
        send, recv = refs[2 * n], refs[2 * n + 1]
        x, y, c, others = _place()
        copies = []
        for a, nm in enumerate(names):
            for j, (ox, oy) in enumerate(others):
                copies.append(pltpu.make_async_remote_copy(
                    src_ref=src[nm].at[2 * ox + oy], dst_ref=out[nm].at[j], send_sem=send.at[3 * a + j],
                    recv_sem=recv.at[3 * a + j], device_id=(ox, oy, c), device_id_type=MESH))
        for cpy in copies:
            cpy.start()
        for cpy in copies:
            cpy.wait()

    hbm = pl.BlockSpec(memory_space=pl.ANY)
    outs = tuple(jax.ShapeDtypeStruct((3,) + summed[nm].shape[1:], summed[nm].dtype) for nm in names)
    res = _call(
        body, name="rs_to_chips", out_shape=outs, in_specs=[hbm] * n, out_specs=(hbm,) * n,
        scratch_shapes=[pltpu.SemaphoreType.DMA((3 * n,)), pltpu.SemaphoreType.DMA((3 * n,))],
    )(*[summed[nm] for nm in names])
    return dict(zip(names, res))


def add_chips(name, summed, got, kc_idx, full_shape):
    axis, size = _HALF[name]
    hshape = summed.shape[1:]
    lead, last2 = hshape[:-2], hshape[-2:]
    nlead = len(lead)

    def body(kc_ref, s_ref, g0_ref, g1_ref, g2_ref, o_ref):
        del kc_ref
        o_ref[...] = ((s_ref[...].astype(F32) + g0_ref[...].astype(F32)) + g1_ref[...].astype(F32)) + g2_ref[...].astype(F32)

    blk = (None,) * (nlead + 1) + last2
    oblk = (None,) * nlead + last2

    def got_map(slot):
        return lambda *idx: (slot,) + tuple(idx[:nlead]) + (0, 0)

    def out_map(*idx):
        lead_idx = list(idx[:nlead])
        c = idx[-1][1]
        if axis < nlead:
            lead_idx[axis] = lead_idx[axis] + c * size
            return tuple(lead_idx) + (0, 0)
        return tuple(lead_idx) + (c, 0)

    grid_spec = pltpu.PrefetchScalarGridSpec(
        num_scalar_prefetch=1, grid=lead if nlead else (1,),
        in_specs=[pl.BlockSpec(blk, lambda *idx: (idx[-1][0],) + tuple(idx[:nlead]) + (0, 0)),
                  pl.BlockSpec(blk, got_map(0)), pl.BlockSpec(blk, got_map(1)), pl.BlockSpec(blk, got_map(2))],
        out_specs=pl.BlockSpec(oblk, out_map))
    return _call(
        body, name="add_chips", out_shape=jax.ShapeDtypeStruct(full_shape, F32), grid_spec=grid_spec,
        compiler_params=_cp(*(("parallel",) * max(nlead, 1))),
    )(kc_idx, summed, got, got, got)


def rs_replicate(shards):
    names = list(_GROUPS)
    n = len(names)

    def body(*refs):
        given = dict(zip(names, refs[:n]))
        buf = dict(zip(names, refs[n:2 * n]))
        send, recv = refs[2 * n], refs[2 * n + 1]
        x, y, c, _ = _place()
        copies = []
        for a, nm in enumerate(names):
            copies.append(pltpu.make_async_remote_copy(
                src_ref=_half_of(given[nm], nm, c), dst_ref=_half_of(buf[nm], nm, c), send_sem=send.at[a],
                recv_sem=recv.at[a], device_id=(x, y, 1 - c), device_id_type=MESH))
        for cpy in copies:
            cpy.start()
        for a, nm in enumerate(names):
            other = _half_of(buf[nm], nm, 1 - c)
            pltpu.make_async_remote_copy(src_ref=other, dst_ref=other, send_sem=send.at[a], recv_sem=recv.at[a],
                                         device_id=(x, y, 1 - c), device_id_type=MESH).wait_recv()
        for cpy in copies:
            cpy.wait_send()

    hbm = pl.BlockSpec(memory_space=pl.ANY)
    outs = tuple(jax.ShapeDtypeStruct(shards[nm].shape, F32) for nm in names)
    res = _call(
        body, name="rs_replicate", out_shape=outs, in_specs=[hbm] * n, out_specs=(hbm,) * n,
        input_output_aliases={a: a for a in range(n)},
        scratch_shapes=[pltpu.SemaphoreType.DMA((n,)), pltpu.SemaphoreType.DMA((n,))],
    )(*[shards[nm] for nm in names])
    return dict(zip(names, res))


def allreduce_small(v):
    r = v.shape[0]

    def body(v_ref, o_ref, slots, send, recv):
        x, y, c, _ = _place()
        me = 4 * x + 2 * y + c
        slots[me] = v_ref[...]
        copies = []
        for rel in range(1, 8):
            fx, fy, fc = (rel >> 2) & 1, (rel >> 1) & 1, rel & 1
            peer = (x ^ fx, y ^ fy, c ^ fc)
            copies.append(pltpu.make_async_remote_copy(
                src_ref=v_ref, dst_ref=slots.at[me], send_sem=send.at[rel - 1], recv_sem=recv.at[rel - 1],
                device_id=peer, device_id_type=MESH))
        for cpy in copies:
            cpy.start()
        for rel in range(1, 8):
            fx, fy, fc = (rel >> 2) & 1, (rel >> 1) & 1, rel & 1
            src_id = 4 * (x ^ fx) + 2 * (y ^ fy) + (c ^ fc)
            pltpu.make_async_remote_copy(
                src_ref=v_ref, dst_ref=slots.at[src_id], send_sem=send.at[rel - 1], recv_sem=recv.at[rel - 1],
                device_id=(x, y, c), device_id_type=MESH).wait_recv()
        for cpy in copies:
            cpy.wait_send()
        acc = slots[0]
        for s in range(1, 8):
            acc = acc + slots[s]
        o_ref[...] = acc

    vm = pl.BlockSpec(memory_space=pltpu.VMEM)
    return _call(
        body, name="allreduce_small", out_shape=jax.ShapeDtypeStruct(v.shape, F32), in_specs=[vm], out_specs=vm,
        scratch_shapes=[pltpu.VMEM((8, r, LANES), F32), pltpu.SemaphoreType.DMA((7,)), pltpu.SemaphoreType.DMA((7,))],
    )(v)


def local_step(x, mem, target, wts, small):
    s = x.shape[0]
    ga, gb, gc, gd, ge = wts["A"], wts["B"], wts["C"], wts["D"], wts["E"]
    causal = jnp.tril(jnp.ones((CHUNK, CHUNK), dtype=bool))
    w_s = jnp.where(causal[None], small["sgu_w_s"], 0.0)
    wm = w_s.astype(BF)
    wmt = jnp.swapaxes(w_s, 1, 2).astype(BF)
    bst = small["sgu_b_s"].reshape(SGU_GROUPS, CHUNK, 1)
    ng, nbias = small["sgu_norm_g"], small["sgu_norm_b"]

    a1 = rms_fwd(x, small["ffn1_pre_g"])
    g1, u1, hid1 = ffn_up(a1, ga, 0, 1)
    f1, h1, n1 = mm_res(hid1, gb, 0, x, small["ffn1_post_g"], 0.5, small["mix_pre_g"])
    proj = mm_cb(n1, gd)
    oa = sgu_fwd(proj, wm, ng, nbias, bst)
    ob, tot = sb_fwd(proj)
    merged = merge_norm(oa, ob, small["sgu_out_g"], small["sb_out_g"])
    mo, h2, xn = mm_res(merged, gc, 0, h1, small["mix_post_g"], 1.0, small["xa_pre_g"])
    memn = rms_fwd(mem, small["mem_norm_g"])
    kv = mm_cb(memn, ge)
    xq = mm_res_plain(xn, gc, 1)
    o = xa_fwd(xq, kv)
    cc, h3, a2 = mm_res(o, gc, 2, h2, small["xa_post_g"], 1.0, small["ffn2_pre_g"])
    g2, u2, hid2 = ffn_up(a2, ga, 2, 3)
    f2, h4, _ = mm_res(hid2, gb, 1, h3, small["ffn2_post_g"], 0.5, small["final_norm_g"])

    dga = lax.empty(ga.shape, BF)
    dgb = lax.empty(gb.shape, BF)
    dgc = lax.empty(gc.shape, BF)
    dgd = lax.empty(gd.shape, BF)
    dge = lax.empty(ge.shape, BF)
    sg = {}

    r = norm_bwd(h4, small["final_norm_g"], target=target, f_prev=f2, gp_prev=small["ffn2_post_g"], alpha_prev=0.5)
    loss_tile, dh4, df2 = r["loss"], r["d_h"], r["d_f"]
    sg["final_norm_g"], sg["ffn2_post_g"] = r["d_gn"], r["d_gp"]

    dg2, du2 = ffn_bwd_act(df2, gb, 1, g2, u2)
    dgb = grad_rb(hid2, df2, dgb, 1)
    dga = grad_cb(a2, dg2, dga, 2)
    dga = grad_cb(a2, du2, dga, 3)
    da2 = mm_nt_cb([(dg2, ga, 2), (du2, ga, 3)], D_MODEL, F32)
    r = norm_bwd(h3, small["ffn2_pre_g"], d_a=da2, d_res=dh4, f_prev=cc, gp_prev=small["xa_post_g"], alpha_prev=1.0)
    dh3, dc = r["d_h"], r["d_f"]
    sg["ffn2_pre_g"], sg["xa_post_g"] = r["d_gn"], r["d_gp"]

    d_o = mm_nt_rb(dc, gc, 2, BF)
    dgc = grad_rb(o, dc, dgc, 2)
    dxq, dkv = xa_bwd(xq, kv, d_o)
    dkvb = dkv.astype(BF)
    dge = grad_cb(memn, dkvb, dge)
    dmemn = mm_nt_cb([(dkvb, ge, None)], D_MODEL, F32)
    sg["mem_norm_g"] = norm_bwd(mem, small["mem_norm_g"], d_a=dmemn)["d_gn"]
    dgc = grad_rb(xn, dxq, dgc, 1)
    dxn = mm_nt_rb(dxq, gc, 1, F32)
    r = norm_bwd(h2, small["xa_pre_g"], d_a=dxn, d_res=dh3, f_prev=mo, gp_prev=small["mix_post_g"], alpha_prev=1.0)
    dh2, dmo = r["d_h"], r["d_f"]
    sg["xa_pre_g"], sg["mix_post_g"] = r["d_gn"], r["d_gp"]

    dmerged = mm_nt_rb(dmo, gc, 0, BF)
    dgc = grad_rb(merged, dmo, dgc, 0)
    d_oa, d_ob, sg["sgu_out_g"], sg["sb_out_g"] = merge_norm_bwd(oa, ob, dmerged, small["sgu_out_g"], small["sb_out_g"])
    dp_uv, dws, dbt, sg["sgu_norm_g"], sg["sgu_norm_b"] = sgu_bwd(proj, d_oa, wm, wmt, ng, nbias, bst)
    sg["sgu_w_s"] = dws
    sg["sgu_b_s"] = dbt.reshape(SGU_GROUPS, CHUNK)
    dq, dk, dv = sb_bwd(proj, tot, d_ob)
    dproj = jnp.concatenate([dp_uv, dq.astype(BF), dk.astype(BF), dv.astype(BF)], axis=1)
    dgd = grad_cb(n1, dproj, dgd)
    dn1 = mm_nt_cb([(dproj, gd, None)], D_MODEL, F32)
    r = norm_bwd(h1, small["mix_pre_g"], d_a=dn1, d_res=dh2, f_prev=f1, gp_prev=small["ffn1_post_g"], alpha_prev=0.5)
    dh1, df1 = r["d_h"], r["d_f"]
    sg["mix_pre_g"], sg["ffn1_post_g"] = r["d_gn"], r["d_gp"]

    dg1, du1 = ffn_bwd_act(df1, gb, 0, g1, u1)
    dgb = grad_rb(hid1, df1, dgb, 0)
    dga = grad_cb(a1, dg1, dga, 0)
    dga = grad_cb(a1, du1, dga, 1)
    da1 = mm_nt_cb([(dg1, ga, 0), (du1, ga, 1)], D_MODEL, F32)
    r = norm_bwd(x, small["ffn1_pre_g"], d_a=da1, d_res=dh1)
    grad_x = r["d_h"]
    sg["ffn1_pre_g"] = r["d_gn"]
    return loss_tile, grad_x, {"A": dga, "B": dgb, "C": dgc, "D": dgd, "E": dge}, sg


def mm_res_plain(a, w, which):
    m = a.shape[0]
    kb, n = w.shape[2], w.shape[3]
    tm = _row_tile(m)

    def body(a_ref, w_ref, o_ref):
        acc = None
        for k in range(N_CHIPS):
            t = _dot(a_ref[:, k * kb:(k + 1) * kb], w_ref[k])
            acc = t if acc is None else acc + t
        o_ref[...] = acc.astype(BF)

    return _call(
        body, name="mm_rb", out_shape=jax.ShapeDtypeStruct((m, n), BF), grid=(m // tm,),
        in_specs=[pl.BlockSpec((tm, N_CHIPS * kb), lambda i: (i, 0)),
                  pl.BlockSpec((N_CHIPS, None, kb, n), lambda i: (0, which, 0, 0))],
        out_specs=pl.BlockSpec((tm, n), lambda i: (i, 0)), compiler_params=_cp("parallel"),
    )(a, w)


_BIG = ("ffn1_w_gate", "ffn1_w_up", "ffn1_w_down", "w_in", "w_out", "xa_w_q", "xa_w_kv", "xa_w_o",
        "ffn2_w_gate", "ffn2_w_up", "ffn2_w_down")
_SMALL = ("ffn1_pre_g", "ffn1_post_g", "mix_pre_g", "mix_post_g", "sgu_norm_g", "sgu_norm_b", "sgu_w_s", "sgu_b_s",
          "sgu_out_g", "sb_out_g", "xa_pre_g", "xa_post_g", "mem_norm_g", "ffn2_pre_g", "ffn2_post_g", "final_norm_g")
_WEIGHTS = ("ffn1_pre_g", "ffn1_post_g", "ffn1_w_gate", "ffn1_w_up", "ffn1_w_down", "mix_pre_g", "mix_post_g", "w_in",
            "sgu_norm_g", "sgu_norm_b", "sgu_w_s", "sgu_b_s", "sgu_out_g", "sb_out_g", "w_out", "xa_pre_g", "xa_post_g",
            "mem_norm_g", "xa_w_q", "xa_w_kv", "xa_w_o", "ffn2_pre_g", "ffn2_post_g", "ffn2_w_gate", "ffn2_w_up",
            "ffn2_w_down", "final_norm_g")
_SLOT = {"ffn1_w_gate": ("A", 0), "ffn1_w_up": ("A", 1), "ffn2_w_gate": ("A", 2), "ffn2_w_up": ("A", 3),
         "ffn1_w_down": ("B", 0), "ffn2_w_down": ("B", 1), "w_out": ("C", 0), "xa_w_q": ("C", 1), "xa_w_o": ("C", 2),
         "w_in": ("D", None), "xa_w_kv": ("E", None)}


def _pack_small(vals):
    return jnp.concatenate([vals[nm].reshape(-1, LANES) for nm in _SMALL], axis=0)


def _unpack_small(packed, shapes):
    out, pos = {}, 0
    for nm in _SMALL:
        rows = math.prod(shapes[nm]) // LANES
        out[nm] = packed[pos:pos + rows].reshape(shapes[nm])
        pos += rows
    return out


def kernel(x, mem, ffn1_pre_g, ffn1_post_g, ffn1_w_gate, ffn1_w_up, ffn1_w_down, mix_pre_g, mix_post_g, w_in, sgu_norm_g, sgu_norm_b, sgu_w_s, sgu_b_s, sgu_out_g, sb_out_g, w_out, xa_pre_g, xa_post_g, mem_norm_g, xa_w_q, xa_w_kv, xa_w_o, ffn2_pre_g, ffn2_post_g, ffn2_w_gate, ffn2_w_up, ffn2_w_down, final_norm_g, loss_target, m_ffn1_pre_g, m_ffn1_post_g, m_ffn1_w_gate, m_ffn1_w_up, m_ffn1_w_down, m_mix_pre_g, m_mix_post_g, m_w_in, m_sgu_norm_g, m_sgu_norm_b, m_sgu_w_s, m_sgu_b_s, m_sgu_out_g, m_sb_out_g, m_w_out, m_xa_pre_g, m_xa_post_g, m_mem_norm_g, m_xa_w_q, m_xa_w_kv, m_xa_w_o, m_ffn2_pre_g, m_ffn2_post_g, m_ffn2_w_gate, m_ffn2_w_up, m_ffn2_w_down, m_final_norm_g, v_ffn1_pre_g, v_ffn1_post_g, v_ffn1_w_gate, v_ffn1_w_up, v_ffn1_w_down, v_mix_pre_g, v_mix_post_g, v_w_in, v_sgu_norm_g, v_sgu_norm_b, v_sgu_w_s, v_sgu_b_s, v_sgu_out_g, v_sb_out_g, v_w_out, v_xa_pre_g, v_xa_post_g, v_mem_norm_g, v_xa_w_q, v_xa_w_kv, v_xa_w_o, v_ffn2_pre_g, v_ffn2_post_g, v_ffn2_w_gate, v_ffn2_w_up, v_ffn2_w_down, v_final_norm_g):
    env = dict(locals())
    w = {nm: env[nm] for nm in _WEIGHTS}
    mom = {nm: env["m_" + nm] for nm in _WEIGHTS}
    vel = {nm: env["v_" + nm] for nm in _WEIGHTS}

    loc = {
        "A": jnp.stack([w["ffn1_w_gate"][0], w["ffn1_w_up"][0], w["ffn2_w_gate"][0], w["ffn2_w_up"][0]]).astype(BF),
        "B": jnp.stack([w["ffn1_w_down"][0], w["ffn2_w_down"][0]]).astype(BF),
        "C": jnp.stack([w["w_out"][0], w["xa_w_q"][0], w["xa_w_o"][0]]).astype(BF),
        "D": w["w_in"][0].astype(BF),
        "E": w["xa_w_kv"][0].astype(BF),
    }
    wts = gather_weights(loc)

    small = {nm: w[nm][0] for nm in _SMALL}
    for nm in ("ffn1_pre_g", "ffn1_post_g", "mix_pre_g", "mix_post_g", "sgu_out_g", "sb_out_g", "xa_pre_g", "xa_post_g",
               "mem_norm_g", "ffn2_pre_g", "ffn2_post_g", "final_norm_g"):
        small[nm] = w[nm]
    loss_tile, grad_x, big_g, small_g = local_step(x[0], mem[0], loss_target[0], wts, small)

    small_shapes = {nm: w[nm].shape for nm in _SMALL}
    flat = _pack_small(small_g)
    n_small = flat.shape[0]
    pad = jnp.zeros((-n_small % 8, LANES), F32)
    packed = allreduce_small(jnp.concatenate([flat, pad, loss_tile], axis=0))
    loss = packed[-8, 0]
    g_small = _unpack_small(packed[:n_small], small_shapes)

    c_idx = lax.axis_index("c").astype(jnp.int32).reshape(1)
    kc_idx = jnp.stack([2 * lax.axis_index("x") + lax.axis_index("y"), lax.axis_index("c")]).astype(jnp.int32)
    from_sib = rs_to_sibling(big_g)
    pair_sum = {nm: add_halves(nm, big_g[nm], from_sib[nm], c_idx) for nm in _GROUPS}
    from_chips = rs_to_chips(pair_sum)
    shard = rs_replicate({nm: add_chips(nm, pair_sum[nm], from_chips[nm], kc_idx, loc[nm].shape) for nm in _GROUPS})

    grads, delta, new_m, new_v = {}, {}, {}, {}
    for nm in _BIG:
        grp, idx = _SLOT[nm]
        g2d = shard[grp] if idx is None else shard[grp][idx]
        shape = w[nm].shape
        d, nm_, nv_ = adamw(w[nm][0], g2d, mom[nm][0], vel[nm][0])
        grads[nm], delta[nm], new_m[nm], new_v[nm] = (g2d.reshape(shape), d.reshape(shape), nm_.reshape(shape),
                                                     nv_.reshape(shape))
    d, nm_, nv_ = adamw(_pack_small(w), _pack_small(g_small), _pack_small(mom), _pack_small(vel))
    d, nm_, nv_ = (_unpack_small(t, small_shapes) for t in (d, nm_, nv_))
    for nm in _SMALL:
        grads[nm], delta[nm], new_m[nm], new_v[nm] = g_small[nm], d[nm], nm_[nm], nv_[nm]

    return (loss, grad_x[None], *[grads[nm] for nm in _WEIGHTS], *[delta[nm] for nm in _WEIGHTS],
            *[new_m[nm] for nm in _WEIGHTS], *[new_v[nm] for nm in _WEIGHTS])
```

```python
import functools
import math

import jax
import jax.numpy as jnp
from jax import lax
from jax.experimental import pallas as pl
from jax.experimental.pallas import tpu as pltpu

F32 = jnp.float32
BF = jnp.bfloat16
EPS = 1e-6
D_MODEL = 1024
N_CHIPS = 4
FF_BLOCK = 704
IN_BLOCK = 640
KV_BLOCK = 512
ROW_BLOCK = 256
SGU_GROUPS = 4
CHUNK = 128
SB_HEAD_DIM = 64
SB_SCALE = SB_HEAD_DIM ** -0.5
XA_HEADS = 4
XA_HEAD_DIM = 256
XA_SCALE = XA_HEAD_DIM ** -0.5
LANES = 128
VMEM_LIMIT = 56 * 1024 * 1024
MESH = pl.DeviceIdType.MESH

ADAM_LR = 0.001
ADAM_B1 = 0.9
ADAM_B2 = 0.999
ADAM_EPS = 1e-08
ADAM_WD = 0.01
ADAM_STEP = 10

_GELU_C = math.sqrt(2.0 / math.pi)
_GELU_A = 0.044715


def _cp(*sem):
    return pltpu.CompilerParams(dimension_semantics=sem, vmem_limit_bytes=VMEM_LIMIT)


def _call(body, **kw):
    return pl.pallas_call(body, **kw)


def _call_with_comm(core_body, comm, args, *, name, grid, out_shape, in_specs, out_specs, scratch_shapes, compiler_params):
    if comm is None:
        res = _call(core_body, name=name, grid=grid, out_shape=tuple(out_shape), in_specs=list(in_specs),
                    out_specs=tuple(out_specs), scratch_shapes=list(scratch_shapes), compiler_params=compiler_params)(*args)
        return res, ()
    n_in, n_out, n_scr = len(in_specs), len(out_shape), len(scratch_shapes)
    ni, no = len(comm.ins), len(comm.outs)

    def body(*refs):
        core_in, cin = refs[:n_in], refs[n_in:n_in + ni]
        core_out = refs[n_in + ni:n_in + ni + n_out]
        cout = refs[n_in + ni + n_out:n_in + ni + n_out + no]
        scr = refs[n_in + ni + n_out + no:]
        core_scr, send, recv = scr[:n_scr], scr[n_scr], scr[n_scr + 1]
        ids = [pl.program_id(a) for a in range(len(grid))]
        first = functools.reduce(jnp.logical_and, [i == 0 for i in ids])
        last = functools.reduce(jnp.logical_and, [i == g - 1 for i, g in zip(ids, grid)])

        @pl.when(first)
        def _():
            comm.start(cin, cout, send, recv)

        core_body(*core_in, *core_out, *core_scr)

        @pl.when(last)
        def _():
            comm.finish(cin, cout, send, recv)

    hbm = pl.BlockSpec(memory_space=pl.ANY)
    res = _call(
        body, name=name, grid=grid, out_shape=tuple(out_shape) + tuple(comm.outs),
        in_specs=list(in_specs) + [hbm] * ni, out_specs=tuple(out_specs) + (hbm,) * no,
        scratch_shapes=list(scratch_shapes) + [pltpu.SemaphoreType.DMA((comm.n_sems,))] * 2,
        input_output_aliases={n_in + i: n_out + o for i, o in comm.aliases.items()},
        compiler_params=compiler_params,
    )(*args, *comm.ins)
    return res[:n_out], res[n_out:]


def _dot(a, b):
    return jnp.dot(a, b, preferred_element_type=F32)


def _dot_nt(a, b):
    return lax.dot_general(a, b, (((1,), (1,)), ((), ())), preferred_element_type=F32)


def _dot_tn(a, b):
    return lax.dot_general(a, b, (((0,), (0,)), ((), ())), preferred_element_type=F32)


def _rstd(x):
    return lax.rsqrt(jnp.mean(x * x, axis=-1, keepdims=True) + EPS)


def _rms_bwd(x, g, dy):
    r = _rstd(x)
    xh = x * r
    gd = dy * g
    dx = r * (gd - xh * jnp.mean(gd * xh, axis=-1, keepdims=True))
    dg = jnp.sum(dy * xh, axis=0, keepdims=True)
    return dx, dg


def _gelu(x):
    return 0.5 * x * (1.0 + jnp.tanh(_GELU_C * (x + _GELU_A * (x * x * x))))


def _gelu_grad(x):
    t = jnp.tanh(_GELU_C * (x + _GELU_A * (x * x * x)))
    return 0.5 * (1.0 + t) + 0.5 * x * (1.0 - t * t) * (_GELU_C * (1.0 + 3.0 * _GELU_A * x * x))


def _dot2(x, ones_mat):
    hi = x.astype(BF)
    lo = (x - hi.astype(F32)).astype(BF)
    return _dot(hi, ones_mat) + _dot(lo, ones_mat)


def _row_tile(m, want=512):
    return min(want, m)


def rms_fwd(x, g):
    m, d = x.shape
    tm = _row_tile(m)

    def body(x_ref, g_ref, o_ref):
        xv = x_ref[...]
        o_ref[...] = (xv * _rstd(xv) * g_ref[...]).astype(BF)

    return _call(
        body, name="rms_fwd", out_shape=jax.ShapeDtypeStruct((m, d), BF), grid=(m // tm,),
        in_specs=[pl.BlockSpec((tm, d), lambda i: (i, 0)), pl.BlockSpec((1, d), lambda i: (0, 0))],
        out_specs=pl.BlockSpec((tm, d), lambda i: (i, 0)), compiler_params=_cp("parallel"),
    )(x, g)


def ffn_up(a, ga, ig, iu):
    m, d = a.shape
    tm = _row_tile(m)
    nb = ga.shape[-1]

    def body(a_ref, wg_ref, wu_ref, g_ref, u_ref, h_ref):
        av = a_ref[...]
        g = _dot(av, wg_ref[...])
        u = _dot(av, wu_ref[...])
        g_ref[...] = g.astype(BF)
        u_ref[...] = u.astype(BF)
        h_ref[...] = (g * jax.nn.sigmoid(g) * u).astype(BF)

    blk = jax.ShapeDtypeStruct((N_CHIPS, m, nb), BF)
    ospec = pl.BlockSpec((None, tm, nb), lambda k, i: (k, i, 0))
    return _call(
        body, name="ffn_up", out_shape=(blk, blk, blk), grid=(N_CHIPS, m // tm),
        in_specs=[pl.BlockSpec((tm, d), lambda k, i: (i, 0)),
                  pl.BlockSpec((None, None, d, nb), lambda k, i: (k, ig, 0, 0)),
                  pl.BlockSpec((None, None, d, nb), lambda k, i: (k, iu, 0, 0))],
        out_specs=(ospec, ospec, ospec), compiler_params=_cp("parallel", "parallel"),
    )(a, ga, ga)


def mm_res(lhs, w, which, h, gp, alpha, gn):
    blocked = lhs.ndim == 3
    m = lhs.shape[1] if blocked else lhs.shape[0]
    kb, n = w.shape[2], w.shape[3]
    tm = _row_tile(m)

    def body(l_ref, w_ref, h_ref, gp_ref, gn_ref, f_ref, hn_ref, an_ref):
        acc = None
        for k in range(N_CHIPS):
            lk = l_ref[k] if blocked else l_ref[:, k * kb:(k + 1) * kb]
            t = _dot(lk, w_ref[k])
            acc = t if acc is None else acc + t
        f_ref[...] = acc
        hn = h_ref[...] + alpha * (acc * _rstd(acc) * gp_ref[...])
        hn_ref[...] = hn
        an_ref[...] = (hn * _rstd(hn) * gn_ref[...]).astype(BF)

    lspec = (pl.BlockSpec((N_CHIPS, tm, kb), lambda i: (0, i, 0)) if blocked
             else pl.BlockSpec((tm, N_CHIPS * kb), lambda i: (i, 0)))
    row = pl.BlockSpec((tm, n), lambda i: (i, 0))
    vec = pl.BlockSpec((1, n), lambda i: (0, 0))
    return _call(
        body, name="mm_res", grid=(m // tm,),
        out_shape=(jax.ShapeDtypeStruct((m, n), F32), jax.ShapeDtypeStruct((m, n), F32),
                   jax.ShapeDtypeStruct((m, n), BF)),
        in_specs=[lspec, pl.BlockSpec((N_CHIPS, None, kb, n), lambda i: (0, which, 0, 0)), row, vec, vec],
        out_specs=(row, row, row), compiler_params=_cp("parallel"),
    )(lhs, w, h, gp, gn)


def mm_cb(a, w, out_dtype=BF):
    m, kd = a.shape
    nb = w.shape[-1]
    tm = _row_tile(m)

    def body(a_ref, w_ref, o_ref):
        o_ref[...] = _dot(a_ref[...], w_ref[...]).astype(out_dtype)

    return _call(
        body, name="mm_cb", out_shape=jax.ShapeDtypeStruct((m, N_CHIPS * nb), out_dtype),
        grid=(N_CHIPS, m // tm),
        in_specs=[pl.BlockSpec((tm, kd), lambda k, i: (i, 0)), pl.BlockSpec((None, kd, nb), lambda k, i: (k, 0, 0))],
        out_specs=pl.BlockSpec((tm, nb), lambda k, i: (i, k)), compiler_params=_cp("parallel", "parallel"),
    )(a, w)


def _sgu_core(u_pre, vg_pre, wm_ref, ng_ref, nb_ref, bst_ref, g, c):
    rs = slice(c * CHUNK, (c + 1) * CHUNK)
    cs = slice(g * CHUNK, (g + 1) * CHUNK)
    ug = _gelu(u_pre[rs, cs].astype(F32))
    vgl = _gelu(vg_pre[rs, cs].astype(F32))
    mu = jnp.mean(vgl, axis=-1, keepdims=True)
    cen = vgl - mu
    rstd = lax.rsqrt(jnp.mean(cen * cen, axis=-1, keepdims=True) + EPS)
    xh = cen * rstd
    vn = xh * ng_ref[g:g + 1, :] + nb_ref[g:g + 1, :]
    mixed = _dot(wm_ref[g], vn.astype(BF)) + bst_ref[g]
    return ug, xh, rstd, vn, mixed


def sgu_fwd(proj, wm, ng, nb, bst):
    m = proj.shape[0]
    tm = _row_tile(m)
    wd = SGU_GROUPS * CHUNK

    def body(u_ref, v_ref, wm_ref, ng_ref, nb_ref, bst_ref, o_ref):
        for c in range(tm // CHUNK):
            for g in range(SGU_GROUPS):
                ug, _, _, _, mixed = _sgu_core(u_ref, v_ref, wm_ref, ng_ref, nb_ref, bst_ref, g, c)
                o_ref[c * CHUNK:(c + 1) * CHUNK, g * CHUNK:(g + 1) * CHUNK] = (ug * mixed).astype(BF)

    full = lambda shape: pl.BlockSpec(shape, lambda i: (0,) * len(shape))
    return _call(
        body, name="sgu_fwd", out_shape=jax.ShapeDtypeStruct((m, wd), BF), grid=(m // tm,),
        in_specs=[pl.BlockSpec((tm, wd), lambda i: (i, 0)), pl.BlockSpec((tm, wd), lambda i: (i, 1)),
                  full(wm.shape), full(ng.shape), full(nb.shape), full(bst.shape)],
        out_specs=pl.BlockSpec((tm, wd), lambda i: (i, 0)), compiler_params=_cp("parallel"),
    )(proj, proj, wm, ng, nb, bst)


def _sb_tiles(m):
    tq = min(512, m)
    tk = min(256, m)
    return tq, tk


def _log_sigmoid_pair(z, mask):
    ls = jnp.minimum(z, 0.0) - jnp.log(1.0 + jnp.exp(-jnp.abs(z)))
    l1 = ls - z
    return ls, (l1 if mask is None else jnp.where(mask, l1, 0.0))


def _sb_diag_mask(r0, r1, d, tk):
    rows = r0 + lax.broadcasted_iota(jnp.int32, (r1 - r0, tk), 0)
    cols = d * tk + lax.broadcasted_iota(jnp.int32, (r1 - r0, tk), 1)
    return cols < rows


def _head_masks():
    lane = lax.broadcasted_iota(jnp.int32, (1, LANES), 1)
    return [lane < SB_HEAD_DIM, lane >= SB_HEAD_DIM]


def sb_fwd(proj, comm=None):
    m = proj.shape[0]
    tq, tk = _sb_tiles(m)
    ndiag = tq // tk

    def body(q_ref, k_ref, v_ref, o_ref, tot_ref, qs, acc, car):
        i = pl.program_id(1)
        nfull = i * ndiag
        heads = _head_masks()
        upper = (lax.broadcasted_iota(jnp.int32, (tk, tk), 0) > lax.broadcasted_iota(jnp.int32, (tk, tk), 1)).astype(BF)
        qv = q_ref[...]
        for hd in range(2):
            qs[hd] = jnp.where(heads[hd], qv, jnp.zeros_like(qv)) * SB_SCALE
        acc[...] = jnp.zeros_like(acc)
        car[...] = jnp.zeros_like(car)

        def block(ks, r0, mask):
            rs = slice(r0, tq)
            kb = k_ref[pl.ds(ks, tk), :]
            vb = v_ref[pl.ds(ks, tk), :]
            zs = [_dot_nt(qs[hd, rs, :], kb) for hd in range(2)]
            mid = []
            for hd in range(2):
                ls, l1 = _log_sigmoid_pair(zs[hd], mask)
                mid.append((ls, l1, _dot2(l1, upper)))
            pvs = []
            for hd in range(2):
                ls, l1, cum = mid[hd]
                a = jnp.exp(ls + (cum + car[hd, rs, :]))
                if mask is not None:
                    a = jnp.where(mask, a, 0.0)
                pvs.append(_dot(a.astype(BF), vb))
            for hd in range(2):
                acc[hd, rs, :] += pvs[hd]
                car[hd, rs, :] += jnp.sum(mid[hd][1], axis=-1, keepdims=True)

        for d in reversed(range(ndiag)):
            block(pl.multiple_of((nfull + d) * tk, tk), d * tk, _sb_diag_mask(d * tk, tq, d, tk))

        def step(jj, carry):
            block(pl.multiple_of((nfull - 1 - jj) * tk, tk), 0, None)
            return carry

        lax.fori_loop(0, nfull, step, 0)
        o_ref[...] = jnp.where(heads[0], acc[0], acc[1]).astype(BF)
        tot_ref[...] = jnp.where(heads[0], car[0], car[1])

    qb = 2 * 512 // LANES
    return _call_with_comm(
        body, comm, (proj, proj, proj), name="sb_fwd", grid=(4, m // tq),
        out_shape=(jax.ShapeDtypeStruct((m, 512), BF), jax.ShapeDtypeStruct((m, 512), F32)),
        in_specs=[pl.BlockSpec((tq, LANES), lambda p, i: (i, qb + p)),
                  pl.BlockSpec((m, LANES), lambda p, i: (0, qb + 4 + p)),
                  pl.BlockSpec((m, LANES), lambda p, i: (0, qb + 8 + p))],
        out_specs=(pl.BlockSpec((tq, LANES), lambda p, i: (i, p)), pl.BlockSpec((tq, LANES), lambda p, i: (i, p))),
        scratch_shapes=[pltpu.VMEM((2, tq, LANES), BF), pltpu.VMEM((2, tq, LANES), F32), pltpu.VMEM((2, tq, 1), F32)],
        compiler_params=_cp("arbitrary", "arbitrary"),
    )


def merge_norm(oa, ob, ga, gb):
    m, w = oa.shape
    tm = _row_tile(m)

    def body(a_ref, b_ref, ga_ref, gb_ref, o_ref):
        av = a_ref[...].astype(F32)
        bv = b_ref[...].astype(F32)
        o_ref[:, :w] = (av * _rstd(av) * ga_ref[...]).astype(BF)
        o_ref[:, w:] = (bv * _rstd(bv) * gb_ref[...]).astype(BF)

    row = pl.BlockSpec((tm, w), lambda i: (i, 0))
    vec = pl.BlockSpec((1, w), lambda i: (0, 0))
    return _call(
        body, name="merge_norm", out_shape=jax.ShapeDtypeStruct((m, 2 * w), BF), grid=(m // tm,),
        in_specs=[row, row, vec, vec], out_specs=pl.BlockSpec((tm, 2 * w), lambda i: (i, 0)),
        compiler_params=_cp("parallel"),
    )(oa, ob, ga, gb)


def _xa_probs(qh, kh):
    logits = _dot_nt(qh, kh) * XA_SCALE
    e = jnp.exp(logits - jnp.max(logits, axis=-1, keepdims=True))
    return e / jnp.sum(e, axis=-1, keepdims=True)


def xa_fwd(xq, kv):
    m, d = xq.shape
    mm = kv.shape[0]
    tm = _row_tile(m)

    def body(q_ref, kv_ref, o_ref):
        for hd in range(XA_HEADS):
            cs = slice(hd * XA_HEAD_DIM, (hd + 1) * XA_HEAD_DIM)
            p = _xa_probs(q_ref[:, cs], kv_ref[:, cs])
            vh = kv_ref[:, d + hd * XA_HEAD_DIM:d + (hd + 1) * XA_HEAD_DIM]
            o_ref[:, cs] = _dot(p.astype(BF), vh).astype(BF)

    return _call(
        body, name="xa_fwd", out_shape=jax.ShapeDtypeStruct((m, d), BF), grid=(m // tm,),
        in_specs=[pl.BlockSpec((tm, d), lambda i: (i, 0)), pl.BlockSpec((mm, 2 * d), lambda i: (0, 0))],
        out_specs=pl.BlockSpec((tm, d), lambda i: (i, 0)), compiler_params=_cp("parallel"),
    )(xq, kv)


def norm_bwd(h, gn, d_a=None, d_res=None, target=None, f_prev=None, gp_prev=None, alpha_prev=1.0):
    m, d = h.shape
    tm = _row_tile(m)
    has_loss = target is not None
    has_res = d_res is not None
    has_prev = f_prev is not None

    def body(*refs):
        refs = list(refs)
        h_ref, gn_ref = refs[0], refs[1]
        pos = 2
        da_ref = dres_ref = t_ref = f_ref = gp_ref = None
        if has_loss:
            t_ref = refs[pos]; pos += 1
        else:
            da_ref = refs[pos]; pos += 1
        if has_res:
            dres_ref = refs[pos]; pos += 1
        if has_prev:
            f_ref, gp_ref = refs[pos], refs[pos + 1]; pos += 2
        dh_ref, dgn_ref = refs[pos], refs[pos + 1]; pos += 2
        df_ref = dgp_ref = loss_ref = None
        if has_prev:
            df_ref, dgp_ref = refs[pos], refs[pos + 1]; pos += 2
        if has_loss:
            loss_ref = refs[pos]

        first = pl.program_id(0) == 0
        hv = h_ref[...]
        gn = gn_ref[...]
        if has_loss:
            err = hv * _rstd(hv) * gn - t_ref[...]
            da = err * (1.0 / d)
            part = 0.5 * jnp.sum(jnp.sum(err * err, axis=-1, keepdims=True) * (1.0 / d))

            @pl.when(first)
            def _():
                loss_ref[...] = jnp.zeros_like(loss_ref)

            loss_ref[...] += part
        else:
            da = da_ref[...].astype(F32)
        dx, dgn = _rms_bwd(hv, gn, da)
        dh = dx + dres_ref[...] if has_res else dx
        dh_ref[...] = dh

        @pl.when(first)
        def _():
            dgn_ref[...] = jnp.zeros_like(dgn_ref)

        dgn_ref[...] += dgn
        if has_prev:
            dfv, dgp = _rms_bwd(f_ref[...], gp_ref[...], dh)
            df_ref[...] = (alpha_prev * dfv).astype(BF)

            @pl.when(first)
            def _():
                dgp_ref[...] = jnp.zeros_like(dgp_ref)

            dgp_ref[...] += alpha_prev * dgp

    row = pl.BlockSpec((tm, d), lambda i: (i, 0))
    vec = pl.BlockSpec((1, d), lambda i: (0, 0))
    ins, in_specs = [h, gn], [row, vec]
    ins.append(target if has_loss else d_a); in_specs.append(row)
    if has_res:
        ins.append(d_res); in_specs.append(row)
    if has_prev:
        ins += [f_prev, gp_prev]; in_specs += [row, vec]
    outs = [jax.ShapeDtypeStruct((m, d), F32), jax.ShapeDtypeStruct((1, d), F32)]
    out_specs = [row, vec]
    names = ["d_h", "d_gn"]
    if has_prev:
        outs += [jax.ShapeDtypeStruct((m, d), BF), jax.ShapeDtypeStruct((1, d), F32)]
        out_specs += [row, vec]
        names += ["d_f", "d_gp"]
    if has_loss:
        outs.append(jax.ShapeDtypeStruct((8, LANES), F32))
        out_specs.append(pl.BlockSpec((8, LANES), lambda i: (0, 0)))
        names.append("loss")
    res = _call(
        body, name="norm_bwd", out_shape=tuple(outs), grid=(m // tm,), in_specs=in_specs,
        out_specs=tuple(out_specs), compiler_params=_cp("arbitrary"),
    )(*ins)
    return dict(zip(names, res))


def ffn_bwd_act(df, gb, which, g, u):
    m, d = df.shape
    nb = g.shape[-1]
    tm = _row_tile(m)

    def body(df_ref, w_ref, g_ref, u_ref, dg_ref, du_ref):
        dh = _dot_nt(df_ref[...], w_ref[...])
        gv = g_ref[...].astype(F32)
        uv = u_ref[...].astype(F32)
        s = jax.nn.sigmoid(gv)
        dg_ref[...] = (dh * uv * (s * (1.0 + gv * (1.0 - s)))).astype(BF)
        du_ref[...] = (dh * gv * s).astype(BF)

    blk = jax.ShapeDtypeStruct((N_CHIPS, m, nb), BF)
    aspec = pl.BlockSpec((None, tm, nb), lambda k, i: (k, i, 0))
    return _call(
        body, name="ffn_bwd_act", out_shape=(blk, blk), grid=(N_CHIPS, m // tm),
        in_specs=[pl.BlockSpec((tm, d), lambda k, i: (i, 0)),
                  pl.BlockSpec((None, None, nb, d), lambda k, i: (k, which, 0, 0)), aspec, aspec],
        out_specs=(aspec, aspec), compiler_params=_cp("parallel", "parallel"),
    )(df, gb, g, u)


def mm_tn(a, b, dest, a_spec, b_spec, o_spec, acc_shape, msteps):
    def body(a_ref, b_ref, dest_ref, o_ref, acc):
        del dest_ref
        ms = pl.program_id(1)

        @pl.when(ms == 0)
        def _():
            acc[...] = jnp.zeros_like(acc)

        acc[...] += _dot_tn(a_ref[...], b_ref[...])

        @pl.when(ms == msteps - 1)
        def _():
            o_ref[...] = acc[...].astype(o_ref.dtype)

    return _call(
        body, name="mm_tn", out_shape=jax.ShapeDtypeStruct(dest.shape, dest.dtype), grid=(N_CHIPS, msteps),
        in_specs=[a_spec, b_spec, pl.BlockSpec(memory_space=pl.ANY)], out_specs=o_spec,
        scratch_shapes=[pltpu.VMEM(acc_shape, F32)], input_output_aliases={2: 0},
        compiler_params=_cp("parallel", "arbitrary"),
    )(a, b, dest)


def _act_spec(arr, tm, nb):
    if arr.ndim == 3:
        return pl.BlockSpec((None, tm, nb), lambda k, ms: (k, ms, 0))
    return pl.BlockSpec((tm, nb), lambda k, ms: (ms, k))


def grad_cb(a, dout, dest, which=None):
    m, kd = a.shape
    nb = dest.shape[-1]
    tm = _row_tile(m)
    if which is None:
        o_spec = pl.BlockSpec((None, kd, nb), lambda k, ms: (k, 0, 0))
    else:
        o_spec = pl.BlockSpec((None, None, kd, nb), lambda k, ms: (k, which, 0, 0))
    return mm_tn(a, dout, dest, pl.BlockSpec((tm, kd), lambda k, ms: (ms, 0)), _act_spec(dout, tm, nb), o_spec,
                 (kd, nb), m // tm)


def grad_rb(a, dout, dest, which):
    m, n = dout.shape
    kb = dest.shape[-2]
    tm = _row_tile(m)
    o_spec = pl.BlockSpec((None, None, kb, n), lambda k, ms: (k, which, 0, 0))
    return mm_tn(a, dout, dest, _act_spec(a, tm, kb), pl.BlockSpec((tm, n), lambda k, ms: (ms, 0)), o_spec,
                 (kb, n), m // tm)


def mm_nt_cb(pairs, n, out_dtype):
    d0 = pairs[0][0]
    m = d0.shape[1] if d0.ndim == 3 else d0.shape[0]
    tm = _row_tile(m)
    npair = len(pairs)

    def body(*refs):
        o_ref, acc = refs[2 * npair], refs[2 * npair + 1]
        k = pl.program_id(1)

        @pl.when(k == 0)
        def _():
            acc[...] = jnp.zeros_like(acc)

        for p in range(npair):
            acc[...] += _dot_nt(refs[2 * p][...], refs[2 * p + 1][...])

        @pl.when(k == N_CHIPS - 1)
        def _():
            o_ref[...] = acc[...].astype(out_dtype)

    ins, in_specs = [], []
    for dout, w, which in pairs:
        nb = w.shape[-1]
        if dout.ndim == 3:
            in_specs.append(pl.BlockSpec((None, tm, nb), lambda i, k: (k, i, 0)))
        else:
            in_specs.append(pl.BlockSpec((tm, nb), lambda i, k: (i, k)))
        if w.ndim == 4:
            in_specs.append(pl.BlockSpec((None, None, n, nb), lambda i, k, which=which: (k, which, 0, 0)))
        else:
            in_specs.append(pl.BlockSpec((None, n, nb), lambda i, k: (k, 0, 0)))
        ins += [dout, w]
    return _call(
        body, name="mm_nt_cb", out_shape=jax.ShapeDtypeStruct((m, n), out_dtype), grid=(m // tm, N_CHIPS),
        in_specs=in_specs, out_specs=pl.BlockSpec((tm, n), lambda i, k: (i, 0)),
        scratch_shapes=[pltpu.VMEM((tm, n), F32)], compiler_params=_cp("parallel", "arbitrary"),
    )(*ins)


def mm_nt_rb(dout, w, which, out_dtype):
    m, n = dout.shape
    kb = w.shape[2]
    tm = _row_tile(m)

    def body(d_ref, w_ref, o_ref):
        o_ref[...] = _dot_nt(d_ref[...], w_ref[...]).astype(out_dtype)

    return _call(
        body, name="mm_nt_rb", out_shape=jax.ShapeDtypeStruct((m, N_CHIPS * kb), out_dtype), grid=(N_CHIPS, m // tm),
        in_specs=[pl.BlockSpec((tm, n), lambda k, i: (i, 0)),
                  pl.BlockSpec((None, None, kb, n), lambda k, i: (k, which, 0, 0))],
        out_specs=pl.BlockSpec((tm, kb), lambda k, i: (i, k)), compiler_params=_cp("parallel", "parallel"),
    )(dout, w)


def merge_norm_bwd(oa, ob, dmerged, ga, gb):
    m, w = oa.shape
    tm = _row_tile(m)

    def body(a_ref, b_ref, dm_ref, ga_ref, gb_ref, da_ref, db_ref, dga_ref, dgb_ref):
        @pl.when(pl.program_id(0) == 0)
        def _():
            dga_ref[...] = jnp.zeros_like(dga_ref)
            dgb_ref[...] = jnp.zeros_like(dgb_ref)

        da, dga = _rms_bwd(a_ref[...].astype(F32), ga_ref[...], dm_ref[:, :w].astype(F32))
        db, dgb = _rms_bwd(b_ref[...].astype(F32), gb_ref[...], dm_ref[:, w:].astype(F32))
        da_ref[...] = da
        db_ref[...] = db
        dga_ref[...] += dga
        dgb_ref[...] += dgb

    row = pl.BlockSpec((tm, w), lambda i: (i, 0))
    vec = pl.BlockSpec((1, w), lambda i: (0, 0))
    return _call(
        body, name="merge_norm_bwd", grid=(m // tm,),
        out_shape=(jax.ShapeDtypeStruct((m, w), F32), jax.ShapeDtypeStruct((m, w), F32),
                   jax.ShapeDtypeStruct((1, w), F32), jax.ShapeDtypeStruct((1, w), F32)),
        in_specs=[row, row, pl.BlockSpec((tm, 2 * w), lambda i: (i, 0)), vec, vec],
        out_specs=(row, row, vec, vec), compiler_params=_cp("arbitrary"),
    )(oa, ob, dmerged, ga, gb)


def sgu_bwd(proj, d_oa, wm, wmt, ng, nb, bst):
    m = proj.shape[0]
    tm = _row_tile(m)
    wd = SGU_GROUPS * CHUNK

    def body(u_ref, v_ref, do_ref, wm_ref, wmt_ref, ng_ref, nb_ref, bst_ref,
             dp_ref, dw_ref, dbt_ref, dng_ref, dnb_ref):
        @pl.when(pl.program_id(0) == 0)
        def _():
            dw_ref[...] = jnp.zeros_like(dw_ref)
            dbt_ref[...] = jnp.zeros_like(dbt_ref)
            dng_ref[...] = jnp.zeros_like(dng_ref)
            dnb_ref[...] = jnp.zeros_like(dnb_ref)

        causal = lax.broadcasted_iota(jnp.int32, (CHUNK, CHUNK), 0) >= lax.broadcasted_iota(jnp.int32, (CHUNK, CHUNK), 1)
        for c in range(tm // CHUNK):
            rs = slice(c * CHUNK, (c + 1) * CHUNK)
            for g in range(SGU_GROUPS):
                cs = slice(g * CHUNK, (g + 1) * CHUNK)
                ug, xh, rstd, vn, mixed = _sgu_core(u_ref, v_ref, wm_ref, ng_ref, nb_ref, bst_ref, g, c)
                do = do_ref[rs, cs]
                dug = do * mixed
                dmix = do * ug
                dmb = dmix.astype(BF)
                dbt_ref[g] += jnp.sum(dmix, axis=-1, keepdims=True)
                dw_ref[g] += jnp.where(causal, _dot_nt(dmb, vn.astype(BF)), 0.0)
                dvn = _dot(wmt_ref[g], dmb)
                dng_ref[g:g + 1, :] += jnp.sum(dvn * xh, axis=0, keepdims=True)
                dnb_ref[g:g + 1, :] += jnp.sum(dvn, axis=0, keepdims=True)
                dxh = dvn * ng_ref[g:g + 1, :]
                dvg = rstd * (dxh - jnp.mean(dxh, axis=-1, keepdims=True)
                              - xh * jnp.mean(dxh * xh, axis=-1, keepdims=True))
                dp_ref[rs, cs] = (dug * _gelu_grad(u_ref[rs, cs].astype(F32))).astype(BF)
                dp_ref[rs, wd + g * CHUNK:wd + (g + 1) * CHUNK] = (dvg * _gelu_grad(v_ref[rs, cs].astype(F32))).astype(BF)

    full = lambda shape: pl.BlockSpec(shape, lambda i: (0,) * len(shape))
    return _call(
        body, name="sgu_bwd", grid=(m // tm,),
        out_shape=(jax.ShapeDtypeStruct((m, 2 * wd), BF), jax.ShapeDtypeStruct(wm.shape, F32),
                   jax.ShapeDtypeStruct(bst.shape, F32), jax.ShapeDtypeStruct(ng.shape, F32),
                   jax.ShapeDtypeStruct(nb.shape, F32)),
        in_specs=[pl.BlockSpec((tm, wd), lambda i: (i, 0)), pl.BlockSpec((tm, wd), lambda i: (i, 1)),
                  pl.BlockSpec((tm, wd), lambda i: (i, 0)),
                  full(wm.shape), full(wmt.shape), full(ng.shape), full(nb.shape), full(bst.shape)],
        out_specs=(pl.BlockSpec((tm, 2 * wd), lambda i: (i, 0)), full(wm.shape), full(bst.shape), full(ng.shape),
                   full(nb.shape)),
        compiler_params=_cp("arbitrary"),
    )(proj, proj, d_oa, wm, wmt, ng, nb, bst)


def sb_bwd(proj, tot, d_ob, comm=None):
    m = proj.shape[0]
    tq, tk = _sb_tiles(m)

    ndiag = tq // tk

    def body(q_ref, k_ref, v_ref, tot_ref, do_ref, dq_ref, dk_ref, dv_ref, qs, dos, tots, dqa, cl1, cg):
        i = pl.program_id(1)

        @pl.when(i == 0)
        def _():
            dk_ref[...] = jnp.zeros_like(dk_ref)
            dv_ref[...] = jnp.zeros_like(dv_ref)

        nfull = i * ndiag
        heads = _head_masks()
        r_io = lax.broadcasted_iota(jnp.int32, (tk, tk), 0)
        c_io = lax.broadcasted_iota(jnp.int32, (tk, tk), 1)
        incl = (r_io <= c_io).astype(BF)
        excl = (r_io < c_io).astype(BF)
        qv = q_ref[...]
        dov = do_ref[...].astype(BF)
        totv = tot_ref[...]
        for hd in range(2):
            qs[hd] = jnp.where(heads[hd], qv, jnp.zeros_like(qv)) * SB_SCALE
            dos[hd] = jnp.where(heads[hd], dov, jnp.zeros_like(dov))
            tots[hd] = jnp.max(jnp.where(heads[hd], totv, -jnp.inf), axis=-1, keepdims=True)
        dqa[...] = jnp.zeros_like(dqa)
        cl1[...] = jnp.zeros_like(cl1)
        cg[...] = jnp.zeros_like(cg)

        def block(ks, r0, mask):
            rs = slice(r0, tq)
            kb = k_ref[pl.ds(ks, tk), :]
            vb = v_ref[pl.ds(ks, tk), :]
            qc = [qs[hd, rs, :] for hd in range(2)]
            doc = [dos[hd, rs, :] for hd in range(2)]
            zs = [_dot_nt(qc[hd], kb) for hd in range(2)]
            das = [_dot_nt(doc[hd], vb) for hd in range(2)]
            s1 = []
            for hd in range(2):
                ls, l1 = _log_sigmoid_pair(zs[hd], mask)
                s1.append((ls, l1, _dot2(l1, incl)))
            s2 = []
            for hd in range(2):
                ls, l1, pre = s1[hd]
                a = jnp.exp(ls + (tots[hd, rs, :] - (pre + cl1[hd, rs, :])))
                if mask is not None:
                    a = jnp.where(mask, a, 0.0)
                gmat = a * das[hd]
                s2.append((a, gmat, _dot2(gmat, excl)))
            dk_sum = dv_sum = None
            for hd in range(2):
                a, gmat, pref = s2[hd]
                sg = jnp.exp(s1[hd][0])
                dz = gmat * (1.0 - sg) - (pref + cg[hd, rs, :]) * sg
                if mask is not None:
                    dz = jnp.where(mask, dz, 0.0)
                dz = dz.astype(BF)
                dqa[hd, rs, :] += _dot(dz, kb)
                dk_t = _dot_tn(dz, qc[hd])
                dv_t = _dot_tn(a.astype(BF), doc[hd])
                dk_sum = dk_t if dk_sum is None else dk_sum + dk_t
                dv_sum = dv_t if dv_sum is None else dv_sum + dv_t
            for hd in range(2):
                cl1[hd, rs, :] += jnp.sum(s1[hd][1], axis=-1, keepdims=True)
                cg[hd, rs, :] += jnp.sum(s2[hd][1], axis=-1, keepdims=True)
            dk_ref[pl.ds(ks, tk), :] += dk_sum
            dv_ref[pl.ds(ks, tk), :] += dv_sum

        def step(j, carry):
            block(pl.multiple_of(j * tk, tk), 0, None)
            return carry

        lax.fori_loop(0, nfull, step, 0)
        for d in range(ndiag):
            block(pl.multiple_of((nfull + d) * tk, tk), d * tk, _sb_diag_mask(d * tk, tq, d, tk))
        dq_ref[...] = jnp.where(heads[0], dqa[0], dqa[1]) * SB_SCALE

    qb = 2 * 512 // LANES
    tile = pl.BlockSpec((tq, LANES), lambda p, i: (i, p))
    seq = pl.BlockSpec((m, LANES), lambda p, i: (0, p))
    out = jax.ShapeDtypeStruct((m, 512), F32)
    return _call_with_comm(
        body, comm, (proj, proj, proj, tot, d_ob), name="sb_bwd", grid=(4, m // tq), out_shape=(out, out, out),
        in_specs=[pl.BlockSpec((tq, LANES), lambda p, i: (i, qb + p)),
                  pl.BlockSpec((m, LANES), lambda p, i: (0, qb + 4 + p)),
                  pl.BlockSpec((m, LANES), lambda p, i: (0, qb + 8 + p)), tile, tile],
        out_specs=(tile, seq, seq),
        scratch_shapes=[pltpu.VMEM((2, tq, LANES), BF), pltpu.VMEM((2, tq, LANES), BF), pltpu.VMEM((2, tq, 1), F32),
                        pltpu.VMEM((2, tq, LANES), F32), pltpu.VMEM((2, tq, 1), F32), pltpu.VMEM((2, tq, 1), F32)],
        compiler_params=_cp("arbitrary", "arbitrary"),
    )


def xa_bwd(xq, kv, d_o):
    m, d = xq.shape
    mm = kv.shape[0]
    tm = _row_tile(m)

    def body(q_ref, kv_ref, do_ref, dq_ref, dkv_ref):
        @pl.when(pl.program_id(0) == 0)
        def _():
            dkv_ref[...] = jnp.zeros_like(dkv_ref)

        for hd in range(XA_HEADS):
            cs = slice(hd * XA_HEAD_DIM, (hd + 1) * XA_HEAD_DIM)
            vs = slice(d + hd * XA_HEAD_DIM, d + (hd + 1) * XA_HEAD_DIM)
            qh = q_ref[:, cs]
            kh = kv_ref[:, cs]
            doh = do_ref[:, cs]
            p = _xa_probs(qh, kh)
            dp = _dot_nt(doh, kv_ref[:, vs])
            ds = (p * (dp - jnp.sum(p * dp, axis=-1, keepdims=True))).astype(BF)
            dq_ref[:, cs] = (_dot(ds, kh) * XA_SCALE).astype(BF)
            dkv_ref[:, cs] += _dot_tn(ds, qh) * XA_SCALE
            dkv_ref[:, vs] += _dot_tn(p.astype(BF), doh)

    row = pl.BlockSpec((tm, d), lambda i: (i, 0))
    whole = pl.BlockSpec((mm, 2 * d), lambda i: (0, 0))
    return _call(
        body, name="xa_bwd", grid=(m // tm,),
        out_shape=(jax.ShapeDtypeStruct((m, d), BF), jax.ShapeDtypeStruct((mm, 2 * d), F32)),
        in_specs=[row, whole, row], out_specs=(row, whole), compiler_params=_cp("arbitrary"),
    )(xq, kv, d_o)


def adamw(w, g, mom, vel, g_index=None):
    r, c = w.shape
    tr = r
    for cand in (512, 256, 128, 64, 32, 16, 8):
        if r % cand == 0:
            tr = cand
            break

    def body(w_ref, g_ref, m_ref, v_ref, go_ref, d_ref, nm_ref, nv_ref):
        gv = g_ref[...]
        mn = ADAM_B1 * m_ref[...] + (1.0 - ADAM_B1) * gv
        vn = ADAM_B2 * v_ref[...] + (1.0 - ADAM_B2) * (gv * gv)
        m_hat = mn / (1.0 - ADAM_B1 ** ADAM_STEP)
        v_hat = vn / (1.0 - ADAM_B2 ** ADAM_STEP)
        go_ref[...] = gv
        d_ref[...] = -ADAM_LR * (m_hat / (jnp.sqrt(v_hat) + ADAM_EPS) + ADAM_WD * w_ref[...])
        nm_ref[...] = mn
        nv_ref[...] = vn

    spec = pl.BlockSpec((tr, c), lambda i: (i, 0))
    gspec = spec if g_index is None else pl.BlockSpec((None, tr, c), lambda i: (g_index, i, 0))
    out = jax.ShapeDtypeStruct((r, c), F32)
    return _call(
        body, name="adamw", out_shape=(out, out, out, out), grid=(r // tr,), in_specs=[spec, gspec, spec, spec],
        out_specs=(spec, spec, spec, spec), compiler_params=_cp("parallel"),
    )(w, g, mom, vel)


def _place():
    x, y, c = lax.axis_index("x"), lax.axis_index("y"), lax.axis_index("c")
    others = [(1 - x, y), (x, 1 - y), (1 - x, 1 - y)]
    return x, y, c, others


_HALF = {"A1": (1, 512), "A2": (1, 512), "B1": (0, 352), "B2": (0, 352), "C": (1, 128), "D": (0, 512), "E": (0, 512)}
SET_EARLY = ("A1", "B1", "D")
SET_LATE = ("A2", "B2", "C", "E")
_GROUPS = SET_EARLY + SET_LATE
_HBM = pl.BlockSpec(memory_space=pl.ANY)


def _half_of(ref, name, hc, lead=0):
    axis, size = _HALF[name]
    idx = [slice(None)] * (lead + axis) + [pl.ds(hc * size, size)]
    return ref.at[tuple(idx)]


def _gather_slots(loc, names):
    me = 2 * lax.axis_index("x") + lax.axis_index("y")
    init = []
    for nm in names:
        full = lax.empty((N_CHIPS,) + loc[nm].shape, loc[nm].dtype)
        init.append(lax.dynamic_update_slice(full, loc[nm][None], (me,) + (0,) * loc[nm].ndim))
    return init


def _gather_ici(names, src, out, send, recv, sends=True, arrivals=True):
    x, y, c, others = _place()
    me = 2 * x + y
    out_sends, out_arrivals = [], []
    for a, nm in enumerate(names):
        for j, (ox, oy) in enumerate(others):
            sems = dict(send_sem=send.at[3 * a + j], recv_sem=recv.at[3 * a + j], device_id_type=MESH)
            if sends:
                out_sends.append(pltpu.make_async_remote_copy(
                    src_ref=_half_of(src[a], nm, c), dst_ref=_half_of(out[a].at[me], nm, c), device_id=(ox, oy, c),
                    **sems))
            if arrivals:
                landed = _half_of(out[a].at[2 * ox + oy], nm, c)
                out_arrivals.append(pltpu.make_async_remote_copy(src_ref=landed, dst_ref=landed, device_id=(x, y, c),
                                                                 **sems))
    return out_sends, out_arrivals


def _gather_d2d(names, given, out, send, recv):
    x, y, c, others = _place()
    sends, arrivals = [], []
    for a, nm in enumerate(names):
        for j, (ox, oy) in enumerate(others):
            sems = dict(send_sem=send.at[3 * a + j], recv_sem=recv.at[3 * a + j], device_id_type=MESH)
            sends.append(pltpu.make_async_remote_copy(
                src_ref=_half_of(given[a].at[2 * ox + oy], nm, c), dst_ref=_half_of(out[a].at[2 * ox + oy], nm, c),
                device_id=(x, y, 1 - c), **sems))
            landed = _half_of(out[a].at[2 * ox + oy], nm, 1 - c)
            arrivals.append(pltpu.make_async_remote_copy(src_ref=landed, dst_ref=landed, device_id=(x, y, c), **sems))
    return sends, arrivals


def gather_weights(loc, names):
    n = len(names)

    def body(*refs):
        src, given, out = refs[:n], refs[n:2 * n], refs[2 * n:3 * n]
        send1, recv1, send2, recv2 = refs[3 * n:3 * n + 4]
        first, landed = _gather_ici(names, src, out, send1, recv1)
        del given
        passed, arrivals = _gather_d2d(names, out, out, send2, recv2)
        for f in first:
            f.start()
        for l, p in zip(landed, passed):
            l.wait_recv()
            p.start()
        for a in arrivals:
            a.wait_recv()
        for f in first + passed:
            f.wait_send()

    init = _gather_slots(loc, names)
    res = _call(
        body, name="gather_weights", out_shape=tuple(jax.ShapeDtypeStruct(t.shape, t.dtype) for t in init),
        in_specs=[_HBM] * (2 * n), out_specs=(_HBM,) * n, input_output_aliases={n + a: a for a in range(n)},
        scratch_shapes=[pltpu.SemaphoreType.DMA((3 * n,))] * 4,
    )(*[loc[nm] for nm in names], *init)
    return dict(zip(names, res))


def gather_forward(bufs, names):
    n = len(names)

    def body(*refs):
        given, out = refs[:n], refs[n:2 * n]
        passed, arrivals = _gather_d2d(names, given, out, refs[2 * n], refs[2 * n + 1])
        for p in passed:
            p.start()
        for a in arrivals:
            a.wait_recv()
        for p in passed:
            p.wait_send()

    res = _call(
        body, name="gather_forward", out_shape=tuple(jax.ShapeDtypeStruct(t.shape, t.dtype) for t in bufs),
        in_specs=[_HBM] * n, out_specs=(_HBM,) * n, input_output_aliases={a: a for a in range(n)},
        scratch_shapes=[pltpu.SemaphoreType.DMA((3 * n,))] * 2,
    )(*bufs)
    return dict(zip(names, res))


class FusedComm:
    def __init__(self, ins, outs, aliases, n_sems, start, finish):
        self.ins, self.outs, self.aliases, self.n_sems, self.start, self.finish = ins, outs, aliases, n_sems, start, finish


def gather_comm(loc, names):
    n = len(names)

    def start(ins, outs, send, recv):
        for f in _gather_ici(names, ins[:n], outs, send, recv, arrivals=False)[0]:
            f.start()

    def finish(ins, outs, send, recv):
        first, landed = _gather_ici(names, ins[:n], outs, send, recv)
        for l in landed:
            l.wait_recv()
        for f in first:
            f.wait_send()

    init = _gather_slots(loc, names)
    return FusedComm([loc[nm] for nm in names] + init, [jax.ShapeDtypeStruct(t.shape, t.dtype) for t in init],
                     {n + a: a for a in range(n)}, 3 * n, start, finish)


def _half_shape(name, shape):
    axis, size = _HALF[name]
    s = list(shape)
    s[axis] = size
    return tuple(s)


def rs_to_sibling(grads, names):
    n = len(names)

    def body(*refs):
        src = dict(zip(names, refs[:n]))
        out = dict(zip(names, refs[n:2 * n]))
        send, recv = refs[2 * n], refs[2 * n + 1]
        x, y, c, _ = _place()
        copies = []
        for a, nm in enumerate(names):
            copies.append(pltpu.make_async_remote_copy(
                src_ref=_half_of(src[nm], nm, 1 - c, lead=1), dst_ref=out[nm], send_sem=send.at[a], recv_sem=recv.at[a],
                device_id=(x, y, 1 - c), device_id_type=MESH))
        for cpy in copies:
            cpy.start()
        for cpy in copies:
            cpy.wait()

    hbm = pl.BlockSpec(memory_space=pl.ANY)
    outs = tuple(jax.ShapeDtypeStruct((N_CHIPS,) + _half_shape(nm, grads[nm].shape[1:]), grads[nm].dtype)
                 for nm in names)
    res = _call(
        body, name="rs_to_sibling", out_shape=outs, in_specs=[hbm] * n, out_specs=(hbm,) * n,
        scratch_shapes=[pltpu.SemaphoreType.DMA((n,)), pltpu.SemaphoreType.DMA((n,))],
    )(*[grads[nm] for nm in names])
    return dict(zip(names, res))


def _tile2(shape):
    lead = shape[:-2]
    return lead, shape[-2:]


def add_halves(name, mine, got, c_idx):
    axis, size = _HALF[name]
    hshape = got.shape
    lead, last2 = hshape[:-2], hshape[-2:]
    nlead = len(lead)
    haxis = 1 + axis

    def body(c_ref, m_ref, g_ref, o_ref):
        del c_ref
        o_ref[...] = (m_ref[...].astype(F32) + g_ref[...].astype(F32)).astype(o_ref.dtype)

    blk = (None,) * nlead + last2

    def got_map(*idx):
        return tuple(idx[:nlead]) + (0, 0)

    def mine_map(*idx):
        lead_idx = list(idx[:nlead])
        c = idx[nlead][0]
        if haxis < nlead:
            lead_idx[haxis] = lead_idx[haxis] + c * size
            return tuple(lead_idx) + (0, 0)
        return tuple(lead_idx) + (c, 0)

    grid_spec = pltpu.PrefetchScalarGridSpec(
        num_scalar_prefetch=1, grid=lead,
        in_specs=[pl.BlockSpec(blk, mine_map), pl.BlockSpec(blk, got_map)],
        out_specs=pl.BlockSpec(blk, got_map))
    return _call(
        body, name="add_halves", out_shape=jax.ShapeDtypeStruct(hshape, got.dtype), grid_spec=grid_spec,
        compiler_params=_cp(*(("parallel",) * nlead)),
    )(c_idx, mine, got)


def _rs_ici(n, src, out, send, recv):
    x, y, c, others = _place()
    copies = []
    for a in range(n):
        for j, (ox, oy) in enumerate(others):
            copies.append(pltpu.make_async_remote_copy(
                src_ref=src[a].at[2 * ox + oy], dst_ref=out[a].at[j], send_sem=send.at[3 * a + j],
                recv_sem=recv.at[3 * a + j], device_id=(ox, oy, c), device_id_type=MESH))
    return copies


def _rs_out_shapes(summed, names):
    return [jax.ShapeDtypeStruct((3,) + summed[nm].shape[1:], summed[nm].dtype) for nm in names]


def rs_to_chips(summed, names):
    n = len(names)

    def body(*refs):
        copies = _rs_ici(n, refs[:n], refs[n:2 * n], refs[2 * n], refs[2 * n + 1])
        for cpy in copies:
            cpy.start()
        for cpy in copies:
            cpy.wait()

    res = _call(
        body, name="rs_to_chips", out_shape=tuple(_rs_out_shapes(summed, names)), in_specs=[_HBM] * n,
        out_specs=(_HBM,) * n, scratch_shapes=[pltpu.SemaphoreType.DMA((3 * n,))] * 2,
    )(*[summed[nm] for nm in names])
    return dict(zip(names, res))


def rs_comm(summed, names):
    n = len(names)

    def start(ins, outs, send, recv):
        for cpy in _rs_ici(n, ins, outs, send, recv):
            cpy.start()

    def finish(ins, outs, send, recv):
        for cpy in _rs_ici(n, ins, outs, send, recv):
            cpy.wait()

    return FusedComm([summed[nm] for nm in names], _rs_out_shapes(summed, names), {}, 3 * n, start, finish)


def add_chips(name, summed, got, kc_idx, full_shape):
    axis, size = _HALF[name]
    hshape = summed.shape[1:]
    lead, last2 = hshape[:-2], hshape[-2:]
    nlead = len(lead)

    def body(kc_ref, s_ref, g0_ref, g1_ref, g2_ref, o_ref):
        del kc_ref
        o_ref[...] = ((s_ref[...].astype(F32) + g0_ref[...].astype(F32)) + g1_ref[...].astype(F32)) + g2_ref[...].astype(F32)

    blk = (None,) * (nlead + 1) + last2
    oblk = (None,) * nlead + last2

    def got_map(slot):
        return lambda *idx: (slot,) + tuple(idx[:nlead]) + (0, 0)

    def out_map(*idx):
        lead_idx = list(idx[:nlead])
        c = idx[-1][1]
        if axis < nlead:
            lead_idx[axis] = lead_idx[axis] + c * size
            return tuple(lead_idx) + (0, 0)
        return tuple(lead_idx) + (c, 0)

    grid_spec = pltpu.PrefetchScalarGridSpec(
        num_scalar_prefetch=1, grid=lead if nlead else (1,),
        in_specs=[pl.BlockSpec(blk, lambda *idx: (idx[-1][0],) + tuple(idx[:nlead]) + (0, 0)),
                  pl.BlockSpec(blk, got_map(0)), pl.BlockSpec(blk, got_map(1)), pl.BlockSpec(blk, got_map(2))],
        out_specs=pl.BlockSpec(oblk, out_map))
    return _call(
        body, name="add_chips", out_shape=jax.ShapeDtypeStruct(full_shape, F32), grid_spec=grid_spec,
        compiler_params=_cp(*(("parallel",) * max(nlead, 1))),
    )(kc_idx, summed, got, got, got)


def rs_replicate(shards, names):
    n = len(names)

    def body(*refs):
        given = dict(zip(names, refs[:n]))
        buf = dict(zip(names, refs[n:2 * n]))
        send, recv = refs[2 * n], refs[2 * n + 1]
        x, y, c, _ = _place()
        copies = []
        for a, nm in enumerate(names):
            copies.append(pltpu.make_async_remote_copy(
                src_ref=_half_of(given[nm], nm, c), dst_ref=_half_of(buf[nm], nm, c), send_sem=send.at[a],
                recv_sem=recv.at[a], device_id=(x, y, 1 - c), device_id_type=MESH))
        for cpy in copies:
            cpy.start()
        for a, nm in enumerate(names):
            other = _half_of(buf[nm], nm, 1 - c)
            pltpu.make_async_remote_copy(src_ref=other, dst_ref=other, send_sem=send.at[a], recv_sem=recv.at[a],
                                         device_id=(x, y, 1 - c), device_id_type=MESH).wait_recv()
        for cpy in copies:
            cpy.wait_send()

    hbm = pl.BlockSpec(memory_space=pl.ANY)
    outs = tuple(jax.ShapeDtypeStruct(shards[nm].shape, F32) for nm in names)
    res = _call(
        body, name="rs_replicate", out_shape=outs, in_specs=[hbm] * n, out_specs=(hbm,) * n,
        input_output_aliases={a: a for a in range(n)},
        scratch_shapes=[pltpu.SemaphoreType.DMA((n,)), pltpu.SemaphoreType.DMA((n,))],
    )(*[shards[nm] for nm in names])
    return dict(zip(names, res))


def allreduce_small(v):
    r = v.shape[0]

    def body(v_ref, o_ref, slots, send, recv):
        x, y, c, _ = _place()
        me = 4 * x + 2 * y + c
        slots[me] = v_ref[...]
        copies = []
        for rel in range(1, 8):
            fx, fy, fc = (rel >> 2) & 1, (rel >> 1) & 1, rel & 1
            peer = (x ^ fx, y ^ fy, c ^ fc)
            copies.append(pltpu.make_async_remote_copy(
                src_ref=v_ref, dst_ref=slots.at[me], send_sem=send.at[rel - 1], recv_sem=recv.at[rel - 1],
                device_id=peer, device_id_type=MESH))
        for cpy in copies:
            cpy.start()
        for rel in range(1, 8):
            fx, fy, fc = (rel >> 2) & 1, (rel >> 1) & 1, rel & 1
            src_id = 4 * (x ^ fx) + 2 * (y ^ fy) + (c ^ fc)
            pltpu.make_async_remote_copy(
                src_ref=v_ref, dst_ref=slots.at[src_id], send_sem=send.at[rel - 1], recv_sem=recv.at[rel - 1],
                device_id=(x, y, c), device_id_type=MESH).wait_recv()
        for cpy in copies:
            cpy.wait_send()
        acc = slots[0]
        for s in range(1, 8):
            acc = acc + slots[s]
        o_ref[...] = acc

    vm = pl.BlockSpec(memory_space=pltpu.VMEM)
    return _call(
        body, name="allreduce_small", out_shape=jax.ShapeDtypeStruct(v.shape, F32), in_specs=[vm], out_specs=vm,
        scratch_shapes=[pltpu.VMEM((8, r, LANES), F32), pltpu.SemaphoreType.DMA((7,)), pltpu.SemaphoreType.DMA((7,))],
    )(v)


def local_step(x, mem, target, wts, small, fwd_sb, bwd_sb):
    ga1, gb1, gd = wts["A1"], wts["B1"][:, None], wts["D"]
    causal = jnp.tril(jnp.ones((CHUNK, CHUNK), dtype=bool))
    w_s = jnp.where(causal[None], small["sgu_w_s"], 0.0)
    wm = w_s.astype(BF)
    wmt = jnp.swapaxes(w_s, 1, 2).astype(BF)
    bst = small["sgu_b_s"].reshape(SGU_GROUPS, CHUNK, 1)
    ng, nbias = small["sgu_norm_g"], small["sgu_norm_b"]

    a1 = rms_fwd(x, small["ffn1_pre_g"])
    g1, u1, hid1 = ffn_up(a1, ga1, 0, 1)
    f1, h1, n1 = mm_res(hid1, gb1, 0, x, small["ffn1_post_g"], 0.5, small["mix_pre_g"])
    proj = mm_cb(n1, gd)
    oa = sgu_fwd(proj, wm, ng, nbias, bst)
    ob, tot, late = fwd_sb(proj)
    ga2, gb2, gc, ge = late["A2"], late["B2"][:, None], late["C"], late["E"]
    merged = merge_norm(oa, ob, small["sgu_out_g"], small["sb_out_g"])
    mo, h2, xn = mm_res(merged, gc, 0, h1, small["mix_post_g"], 1.0, small["xa_pre_g"])
    memn = rms_fwd(mem, small["mem_norm_g"])
    kv = mm_cb(memn, ge)
    xq = mm_res_plain(xn, gc, 1)
    o = xa_fwd(xq, kv)
    cc, h3, a2 = mm_res(o, gc, 2, h2, small["xa_post_g"], 1.0, small["ffn2_pre_g"])
    g2, u2, hid2 = ffn_up(a2, ga2, 0, 1)
    f2, h4, _ = mm_res(hid2, gb2, 0, h3, small["ffn2_post_g"], 0.5, small["final_norm_g"])

    dga1, dga2 = lax.empty(ga1.shape, BF), lax.empty(ga2.shape, BF)
    dgb1, dgb2 = lax.empty(gb1.shape, BF), lax.empty(gb2.shape, BF)
    dgc = lax.empty(gc.shape, BF)
    dgd = lax.empty(gd.shape, BF)
    dge = lax.empty(ge.shape, BF)
    sg = {}

    r = norm_bwd(h4, small["final_norm_g"], target=target, f_prev=f2, gp_prev=small["ffn2_post_g"], alpha_prev=0.5)
    loss_tile, dh4, df2 = r["loss"], r["d_h"], r["d_f"]
    sg["final_norm_g"], sg["ffn2_post_g"] = r["d_gn"], r["d_gp"]

    dg2, du2 = ffn_bwd_act(df2, gb2, 0, g2, u2)
    dgb2 = grad_rb(hid2, df2, dgb2, 0)
    dga2 = grad_cb(a2, dg2, dga2, 0)
    dga2 = grad_cb(a2, du2, dga2, 1)
    da2 = mm_nt_cb([(dg2, ga2, 0), (du2, ga2, 1)], D_MODEL, F32)
    r = norm_bwd(h3, small["ffn2_pre_g"], d_a=da2, d_res=dh4, f_prev=cc, gp_prev=small["xa_post_g"], alpha_prev=1.0)
    dh3, dc = r["d_h"], r["d_f"]
    sg["ffn2_pre_g"], sg["xa_post_g"] = r["d_gn"], r["d_gp"]

    d_o = mm_nt_rb(dc, gc, 2, BF)
    dgc = grad_rb(o, dc, dgc, 2)
    dxq, dkv = xa_bwd(xq, kv, d_o)
    dkvb = dkv.astype(BF)
    dge = grad_cb(memn, dkvb, dge)
    dmemn = mm_nt_cb([(dkvb, ge, None)], D_MODEL, F32)
    sg["mem_norm_g"] = norm_bwd(mem, small["mem_norm_g"], d_a=dmemn)["d_gn"]
    dgc = grad_rb(xn, dxq, dgc, 1)
    dxn = mm_nt_rb(dxq, gc, 1, F32)
    r = norm_bwd(h2, small["xa_pre_g"], d_a=dxn, d_res=dh3, f_prev=mo, gp_prev=small["mix_post_g"], alpha_prev=1.0)
    dh2, dmo = r["d_h"], r["d_f"]
    sg["xa_pre_g"], sg["mix_post_g"] = r["d_gn"], r["d_gp"]

    dmerged = mm_nt_rb(dmo, gc, 0, BF)
    dgc = grad_rb(merged, dmo, dgc, 0)
    d_oa, d_ob, sg["sgu_out_g"], sg["sb_out_g"] = merge_norm_bwd(oa, ob, dmerged, small["sgu_out_g"], small["sb_out_g"])
    dp_uv, dws, dbt, sg["sgu_norm_g"], sg["sgu_norm_b"] = sgu_bwd(proj, d_oa, wm, wmt, ng, nbias, bst)
    sg["sgu_w_s"] = dws
    sg["sgu_b_s"] = dbt.reshape(SGU_GROUPS, CHUNK)
    late_grads = {"A2": dga2, "B2": dgb2.reshape(late["B2"].shape), "C": dgc, "E": dge}
    dq, dk, dv, state = bwd_sb(proj, tot, d_ob, late_grads)
    dproj = jnp.concatenate([dp_uv, dq.astype(BF), dk.astype(BF), dv.astype(BF)], axis=1)
    dgd = grad_cb(n1, dproj, dgd)
    dn1 = mm_nt_cb([(dproj, gd, None)], D_MODEL, F32)
    r = norm_bwd(h1, small["mix_pre_g"], d_a=dn1, d_res=dh2, f_prev=f1, gp_prev=small["ffn1_post_g"], alpha_prev=0.5)
    dh1, df1 = r["d_h"], r["d_f"]
    sg["mix_pre_g"], sg["ffn1_post_g"] = r["d_gn"], r["d_gp"]

    dg1, du1 = ffn_bwd_act(df1, gb1, 0, g1, u1)
    dgb1 = grad_rb(hid1, df1, dgb1, 0)
    dga1 = grad_cb(a1, dg1, dga1, 0)
    dga1 = grad_cb(a1, du1, dga1, 1)
    da1 = mm_nt_cb([(dg1, ga1, 0), (du1, ga1, 1)], D_MODEL, F32)
    r = norm_bwd(x, small["ffn1_pre_g"], d_a=da1, d_res=dh1)
    grad_x = r["d_h"]
    sg["ffn1_pre_g"] = r["d_gn"]
    return loss_tile, grad_x, {"A1": dga1, "B1": dgb1.reshape(wts["B1"].shape), "D": dgd}, sg, state


def mm_res_plain(a, w, which):
    m = a.shape[0]
    kb, n = w.shape[2], w.shape[3]
    tm = _row_tile(m)

    def body(a_ref, w_ref, o_ref):
        acc = None
        for k in range(N_CHIPS):
            t = _dot(a_ref[:, k * kb:(k + 1) * kb], w_ref[k])
            acc = t if acc is None else acc + t
        o_ref[...] = acc.astype(BF)

    return _call(
        body, name="mm_rb", out_shape=jax.ShapeDtypeStruct((m, n), BF), grid=(m // tm,),
        in_specs=[pl.BlockSpec((tm, N_CHIPS * kb), lambda i: (i, 0)),
                  pl.BlockSpec((N_CHIPS, None, kb, n), lambda i: (0, which, 0, 0))],
        out_specs=pl.BlockSpec((tm, n), lambda i: (i, 0)), compiler_params=_cp("parallel"),
    )(a, w)


_BIG = ("ffn1_w_gate", "ffn1_w_up", "ffn1_w_down", "w_in", "w_out", "xa_w_q", "xa_w_kv", "xa_w_o",
        "ffn2_w_gate", "ffn2_w_up", "ffn2_w_down")
_SMALL = ("ffn1_pre_g", "ffn1_post_g", "mix_pre_g", "mix_post_g", "sgu_norm_g", "sgu_norm_b", "sgu_w_s", "sgu_b_s",
          "sgu_out_g", "sb_out_g", "xa_pre_g", "xa_post_g", "mem_norm_g", "ffn2_pre_g", "ffn2_post_g", "final_norm_g")
_WEIGHTS = ("ffn1_pre_g", "ffn1_post_g", "ffn1_w_gate", "ffn1_w_up", "ffn1_w_down", "mix_pre_g", "mix_post_g", "w_in",
            "sgu_norm_g", "sgu_norm_b", "sgu_w_s", "sgu_b_s", "sgu_out_g", "sb_out_g", "w_out", "xa_pre_g", "xa_post_g",
            "mem_norm_g", "xa_w_q", "xa_w_kv", "xa_w_o", "ffn2_pre_g", "ffn2_post_g", "ffn2_w_gate", "ffn2_w_up",
            "ffn2_w_down", "final_norm_g")
_SLOT = {"ffn1_w_gate": ("A1", 0), "ffn1_w_up": ("A1", 1), "ffn2_w_gate": ("A2", 0), "ffn2_w_up": ("A2", 1),
         "ffn1_w_down": ("B1", None), "ffn2_w_down": ("B2", None), "w_out": ("C", 0), "xa_w_q": ("C", 1),
         "xa_w_o": ("C", 2), "w_in": ("D", None), "xa_w_kv": ("E", None)}


def _pack_small(vals):
    return jnp.concatenate([vals[nm].reshape(-1, LANES) for nm in _SMALL], axis=0)


def _unpack_small(packed, shapes):
    out, pos = {}, 0
    for nm in _SMALL:
        rows = math.prod(shapes[nm]) // LANES
        out[nm] = packed[pos:pos + rows].reshape(shapes[nm])
        pos += rows
    return out


def kernel(x, mem, ffn1_pre_g, ffn1_post_g, ffn1_w_gate, ffn1_w_up, ffn1_w_down, mix_pre_g, mix_post_g, w_in, sgu_norm_g, sgu_norm_b, sgu_w_s, sgu_b_s, sgu_out_g, sb_out_g, w_out, xa_pre_g, xa_post_g, mem_norm_g, xa_w_q, xa_w_kv, xa_w_o, ffn2_pre_g, ffn2_post_g, ffn2_w_gate, ffn2_w_up, ffn2_w_down, final_norm_g, loss_target, m_ffn1_pre_g, m_ffn1_post_g, m_ffn1_w_gate, m_ffn1_w_up, m_ffn1_w_down, m_mix_pre_g, m_mix_post_g, m_w_in, m_sgu_norm_g, m_sgu_norm_b, m_sgu_w_s, m_sgu_b_s, m_sgu_out_g, m_sb_out_g, m_w_out, m_xa_pre_g, m_xa_post_g, m_mem_norm_g, m_xa_w_q, m_xa_w_kv, m_xa_w_o, m_ffn2_pre_g, m_ffn2_post_g, m_ffn2_w_gate, m_ffn2_w_up, m_ffn2_w_down, m_final_norm_g, v_ffn1_pre_g, v_ffn1_post_g, v_ffn1_w_gate, v_ffn1_w_up, v_ffn1_w_down, v_mix_pre_g, v_mix_post_g, v_w_in, v_sgu_norm_g, v_sgu_norm_b, v_sgu_w_s, v_sgu_b_s, v_sgu_out_g, v_sb_out_g, v_w_out, v_xa_pre_g, v_xa_post_g, v_mem_norm_g, v_xa_w_q, v_xa_w_kv, v_xa_w_o, v_ffn2_pre_g, v_ffn2_post_g, v_ffn2_w_gate, v_ffn2_w_up, v_ffn2_w_down, v_final_norm_g):
    env = dict(locals())
    w = {nm: env[nm] for nm in _WEIGHTS}
    mom = {nm: env["m_" + nm] for nm in _WEIGHTS}
    vel = {nm: env["v_" + nm] for nm in _WEIGHTS}

    loc = {
        "A1": jnp.stack([w["ffn1_w_gate"][0], w["ffn1_w_up"][0]]).astype(BF),
        "A2": jnp.stack([w["ffn2_w_gate"][0], w["ffn2_w_up"][0]]).astype(BF),
        "B1": w["ffn1_w_down"][0].astype(BF),
        "B2": w["ffn2_w_down"][0].astype(BF),
        "C": jnp.stack([w["w_out"][0], w["xa_w_q"][0], w["xa_w_o"][0]]).astype(BF),
        "D": w["w_in"][0].astype(BF),
        "E": w["xa_w_kv"][0].astype(BF),
    }
    wts = gather_weights(loc, SET_EARLY)
    c_idx = lax.axis_index("c").astype(jnp.int32).reshape(1)
    kc_idx = jnp.stack([2 * lax.axis_index("x") + lax.axis_index("y"), lax.axis_index("c")]).astype(jnp.int32)

    def fwd_sb(proj):
        (ob, tot), bufs = sb_fwd(proj, comm=gather_comm(loc, SET_LATE))
        return ob, tot, gather_forward(bufs, SET_LATE)

    def reduce_to_pairs(grads, names):
        from_sib = rs_to_sibling(grads, names)
        return {nm: add_halves(nm, grads[nm], from_sib[nm], c_idx) for nm in names}

    def bwd_sb(proj, tot, d_ob, late_grads):
        pairs = reduce_to_pairs(late_grads, SET_LATE)
        (dq, dk, dv), got = sb_bwd(proj, tot, d_ob, comm=rs_comm(pairs, SET_LATE))
        return dq, dk, dv, (pairs, dict(zip(SET_LATE, got)))

    small = {nm: w[nm][0] for nm in _SMALL}
    for nm in ("ffn1_pre_g", "ffn1_post_g", "mix_pre_g", "mix_post_g", "sgu_out_g", "sb_out_g", "xa_pre_g", "xa_post_g",
               "mem_norm_g", "ffn2_pre_g", "ffn2_post_g", "final_norm_g"):
        small[nm] = w[nm]
    loss_tile, grad_x, early_g, small_g, (pair_sum, from_chips) = local_step(
        x[0], mem[0], loss_target[0], wts, small, fwd_sb, bwd_sb)

    small_shapes = {nm: w[nm].shape for nm in _SMALL}
    flat = _pack_small(small_g)
    n_small = flat.shape[0]
    pad = jnp.zeros((-n_small % 8, LANES), F32)
    packed = allreduce_small(jnp.concatenate([flat, pad, loss_tile], axis=0))
    loss = packed[-8, 0]
    g_small = _unpack_small(packed[:n_small], small_shapes)

    early_pairs = reduce_to_pairs(early_g, SET_EARLY)
    pair_sum = {**pair_sum, **early_pairs}
    from_chips = {**from_chips, **rs_to_chips(early_pairs, SET_EARLY)}
    shard = rs_replicate({nm: add_chips(nm, pair_sum[nm], from_chips[nm], kc_idx, loc[nm].shape) for nm in _GROUPS},
                         _GROUPS)

    grads, delta, new_m, new_v = {}, {}, {}, {}
    for nm in _BIG:
        grp, idx = _SLOT[nm]
        shape = w[nm].shape
        res = adamw(w[nm][0], shard[grp], mom[nm][0], vel[nm][0], g_index=idx)
        grads[nm], delta[nm], new_m[nm], new_v[nm] = (t.reshape(shape) for t in res)
    _, d, nm_, nv_ = adamw(_pack_small(w), _pack_small(g_small), _pack_small(mom), _pack_small(vel))
    d, nm_, nv_ = (_unpack_small(t, small_shapes) for t in (d, nm_, nv_))
    for nm in _SMALL:
        grads[nm], delta[nm], new_m[nm], new_v[nm] = g_small[nm], d[nm], nm_[nm], nv_[nm]

    return (loss, grad_x[None], *[grads[nm] for nm in _WEIGHTS], *[delta[nm] for nm in _WEIGHTS],
            *[new_m[nm] for nm in _WEIGHTS], *[new_v[nm] for nm in _WEIGHTS])
```

```python
import functools
import math

import jax
import jax.numpy as jnp
from jax import lax
from jax.experimental import pallas as pl
from jax.experimental.pallas import tpu as pltpu

F32 = jnp.float32
BF = jnp.bfloat16
EPS = 1e-6
D_MODEL = 1024
N_CHIPS = 4
FF_BLOCK = 704
IN_BLOCK = 640
KV_BLOCK = 512
ROW_BLOCK = 256
SGU_GROUPS = 4
CHUNK = 128
SB_HEAD_DIM = 64
SB_SCALE = SB_HEAD_DIM ** -0.5
XA_HEADS = 4
XA_HEAD_DIM = 256
XA_SCALE = XA_HEAD_DIM ** -0.5
LANES = 128
VMEM_LIMIT = 56 * 1024 * 1024
MESH = pl.DeviceIdType.MESH

ADAM_LR = 0.001
ADAM_B1 = 0.9
ADAM_B2 = 0.999
ADAM_EPS = 1e-08
ADAM_WD = 0.01
ADAM_STEP = 10

_GELU_C = math.sqrt(2.0 / math.pi)
_GELU_A = 0.044715


def _cp(*sem):
    return pltpu.CompilerParams(dimension_semantics=sem, vmem_limit_bytes=VMEM_LIMIT)


def _call(body, **kw):
    return pl.pallas_call(body, **kw)


def _call_with_comm(core_body, comm, args, *, name, grid, out_shape, in_specs, out_specs, scratch_shapes, compiler_params):
    if comm is None:
        res = _call(core_body, name=name, grid=grid, out_shape=tuple(out_shape), in_specs=list(in_specs),
                    out_specs=tuple(out_specs), scratch_shapes=list(scratch_shapes), compiler_params=compiler_params)(*args)
        return res, ()
    n_in, n_out, n_scr = len(in_specs), len(out_shape), len(scratch_shapes)
    ni, no = len(comm.ins), len(comm.outs)

    def body(*refs):
        core_in, cin = refs[:n_in], refs[n_in:n_in + ni]
        core_out = refs[n_in + ni:n_in + ni + n_out]
        cout = refs[n_in + ni + n_out:n_in + ni + n_out + no]
        scr = refs[n_in + ni + n_out + no:]
        core_scr, send, recv = scr[:n_scr], scr[n_scr], scr[n_scr + 1]
        ids = [pl.program_id(a) for a in range(len(grid))]
        first = functools.reduce(jnp.logical_and, [i == 0 for i in ids])
        last = functools.reduce(jnp.logical_and, [i == g - 1 for i, g in zip(ids, grid)])

        @pl.when(first)
        def _():
            comm.start(cin, cout, send, recv)

        core_body(*core_in, *core_out, *core_scr)

        @pl.when(last)
        def _():
            comm.finish(cin, cout, send, recv)

    hbm = pl.BlockSpec(memory_space=pl.ANY)
    res = _call(
        body, name=name, grid=grid, out_shape=tuple(out_shape) + tuple(comm.outs),
        in_specs=list(in_specs) + [hbm] * ni, out_specs=tuple(out_specs) + (hbm,) * no,
        scratch_shapes=list(scratch_shapes) + [pltpu.SemaphoreType.DMA((comm.n_sems,))] * 2,
        input_output_aliases={n_in + i: n_out + o for i, o in comm.aliases.items()},
        compiler_params=compiler_params,
    )(*args, *comm.ins)
    return res[:n_out], res[n_out:]


def _dot(a, b):
    return jnp.dot(a, b, preferred_element_type=F32)


def _dot_nt(a, b):
    return lax.dot_general(a, b, (((1,), (1,)), ((), ())), preferred_element_type=F32)


def _dot_tn(a, b):
    return lax.dot_general(a, b, (((0,), (0,)), ((), ())), preferred_element_type=F32)


def _rstd(x):
    return lax.rsqrt(jnp.mean(x * x, axis=-1, keepdims=True) + EPS)


def _rms_bwd(x, g, dy):
    r = _rstd(x)
    xh = x * r
    gd = dy * g
    dx = r * (gd - xh * jnp.mean(gd * xh, axis=-1, keepdims=True))
    dg = jnp.sum(dy * xh, axis=0, keepdims=True)
    return dx, dg


def _gelu(x):
    return 0.5 * x * (1.0 + jnp.tanh(_GELU_C * (x + _GELU_A * (x * x * x))))


def _gelu_grad(x):
    t = jnp.tanh(_GELU_C * (x + _GELU_A * (x * x * x)))
    return 0.5 * (1.0 + t) + 0.5 * x * (1.0 - t * t) * (_GELU_C * (1.0 + 3.0 * _GELU_A * x * x))


def _dot2(x, ones_mat):
    hi = x.astype(BF)
    lo = (x - hi.astype(F32)).astype(BF)
    return _dot(hi, ones_mat) + _dot(lo, ones_mat)


GRAD_ROWS = 2048


def _row_tile(m, want=512):
    return min(want, m)


def _row_parts(tm, nparts=4):
    step = tm // nparts
    return [slice(p * step, (p + 1) * step) for p in range(nparts)]


def rms_fwd(x, g):
    m, d = x.shape
    tm = _row_tile(m)

    def body(x_ref, g_ref, o_ref):
        xv = x_ref[...]
        o_ref[...] = (xv * _rstd(xv) * g_ref[...]).astype(BF)

    return _call(
        body, name="rms_fwd", out_shape=jax.ShapeDtypeStruct((m, d), BF), grid=(m // tm,),
        in_specs=[pl.BlockSpec((tm, d), lambda i: (i, 0)), pl.BlockSpec((1, d), lambda i: (0, 0))],
        out_specs=pl.BlockSpec((tm, d), lambda i: (i, 0)), compiler_params=_cp("parallel"),
    )(x, g)


def ffn_up(a, ga, ig, iu, comm=None):
    m, d = a.shape
    tm = _row_tile(m)
    nb = ga.shape[-1]

    def body(a_ref, wg_ref, wu_ref, g_ref, u_ref, h_ref):
        av = a_ref[...]
        g = _dot(av, wg_ref[...])
        u = _dot(av, wu_ref[...])
        g_ref[...] = g.astype(BF)
        u_ref[...] = u.astype(BF)
        h_ref[...] = (g * jax.nn.sigmoid(g) * u).astype(BF)

    blk = jax.ShapeDtypeStruct((N_CHIPS, m, nb), BF)
    ospec = pl.BlockSpec((None, tm, nb), lambda k, i: (k, i, 0))
    return _call_with_comm(
        body, comm, (a, ga, ga), name="ffn_up", out_shape=(blk, blk, blk), grid=(N_CHIPS, m // tm),
        in_specs=[pl.BlockSpec((tm, d), lambda k, i: (i, 0)),
                  pl.BlockSpec((None, None, d, nb), lambda k, i: (k, ig, 0, 0)),
                  pl.BlockSpec((None, None, d, nb), lambda k, i: (k, iu, 0, 0))],
        out_specs=(ospec, ospec, ospec), scratch_shapes=[],
        compiler_params=_cp("arbitrary", "arbitrary") if comm is not None else _cp("parallel", "parallel"),
    )


def mm_res(lhs, w, which, h, gp, alpha, gn):
    blocked = lhs.ndim == 3
    m = lhs.shape[1] if blocked else lhs.shape[0]
    kb, n = w.shape[2], w.shape[3]
    tm = _row_tile(m)

    def body(l_ref, w_ref, h_ref, gp_ref, gn_ref, f_ref, hn_ref, an_ref):
        parts = _row_parts(tm)
        accs = []
        for rs in parts:
            acc = None
            for k in range(N_CHIPS):
                lk = l_ref[k, rs, :] if blocked else l_ref[rs, k * kb:(k + 1) * kb]
                t = _dot(lk, w_ref[k])
                acc = t if acc is None else acc + t
            accs.append(acc)
        for rs, acc in zip(parts, accs):
            f_ref[rs, :] = acc
            hn = h_ref[rs, :] + alpha * (acc * _rstd(acc) * gp_ref[...])
            hn_ref[rs, :] = hn
            an_ref[rs, :] = (hn * _rstd(hn) * gn_ref[...]).astype(BF)

    lspec = (pl.BlockSpec((N_CHIPS, tm, kb), lambda i: (0, i, 0)) if blocked
             else pl.BlockSpec((tm, N_CHIPS * kb), lambda i: (i, 0)))
    row = pl.BlockSpec((tm, n), lambda i: (i, 0))
    vec = pl.BlockSpec((1, n), lambda i: (0, 0))
    return _call(
        body, name="mm_res", grid=(m // tm,),
        out_shape=(jax.ShapeDtypeStruct((m, n), F32), jax.ShapeDtypeStruct((m, n), F32),
                   jax.ShapeDtypeStruct((m, n), BF)),
        in_specs=[lspec, pl.BlockSpec((N_CHIPS, None, kb, n), lambda i: (0, which, 0, 0)), row, vec, vec],
        out_specs=(row, row, row), compiler_params=_cp("parallel"),
    )(lhs, w, h, gp, gn)


def mm_cb(a, w, out_dtype=BF):
    m, kd = a.shape
    nb = w.shape[-1]
    tm = _row_tile(m)

    def body(a_ref, w_ref, o_ref):
        o_ref[...] = _dot(a_ref[...], w_ref[...]).astype(out_dtype)

    return _call(
        body, name="mm_cb", out_shape=jax.ShapeDtypeStruct((m, N_CHIPS * nb), out_dtype),
        grid=(N_CHIPS, m // tm),
        in_specs=[pl.BlockSpec((tm, kd), lambda k, i: (i, 0)), pl.BlockSpec((None, kd, nb), lambda k, i: (k, 0, 0))],
        out_specs=pl.BlockSpec((tm, nb), lambda k, i: (i, k)), compiler_params=_cp("parallel", "parallel"),
    )(a, w)


def _sgu_core(u_pre, vg_pre, wm_ref, ng_ref, nb_ref, bst_ref, g, c):
    rs = slice(c * CHUNK, (c + 1) * CHUNK)
    cs = slice(g * CHUNK, (g + 1) * CHUNK)
    ug = _gelu(u_pre[rs, cs].astype(F32))
    vgl = _gelu(vg_pre[rs, cs].astype(F32))
    mu = jnp.mean(vgl, axis=-1, keepdims=True)
    cen = vgl - mu
    rstd = lax.rsqrt(jnp.mean(cen * cen, axis=-1, keepdims=True) + EPS)
    xh = cen * rstd
    vn = xh * ng_ref[g:g + 1, :] + nb_ref[g:g + 1, :]
    mixed = _dot(wm_ref[g], vn.astype(BF)) + bst_ref[g]
    return ug, xh, rstd, vn, mixed


def sgu_fwd(proj, wm, ng, nb, bst):
    m = proj.shape[0]
    tm = _row_tile(m)
    wd = SGU_GROUPS * CHUNK

    def body(u_ref, v_ref, wm_ref, ng_ref, nb_ref, bst_ref, o_ref):
        for c in range(tm // CHUNK):
            for g in range(SGU_GROUPS):
                ug, _, _, _, mixed = _sgu_core(u_ref, v_ref, wm_ref, ng_ref, nb_ref, bst_ref, g, c)
                o_ref[c * CHUNK:(c + 1) * CHUNK, g * CHUNK:(g + 1) * CHUNK] = (ug * mixed).astype(BF)

    full = lambda shape: pl.BlockSpec(shape, lambda i: (0,) * len(shape))
    return _call(
        body, name="sgu_fwd", out_shape=jax.ShapeDtypeStruct((m, wd), BF), grid=(m // tm,),
        in_specs=[pl.BlockSpec((tm, wd), lambda i: (i, 0)), pl.BlockSpec((tm, wd), lambda i: (i, 1)),
                  full(wm.shape), full(ng.shape), full(nb.shape), full(bst.shape)],
        out_specs=pl.BlockSpec((tm, wd), lambda i: (i, 0)), compiler_params=_cp("parallel"),
    )(proj, proj, wm, ng, nb, bst)


def _sb_tiles(m):
    tq = min(512, m)
    tk = min(256, m)
    return tq, tk


def _log_sigmoid_pair(z, mask):
    ls = jnp.minimum(z, 0.0) - jnp.log(1.0 + jnp.exp(-jnp.abs(z)))
    l1 = ls - z
    return ls, (l1 if mask is None else jnp.where(mask, l1, 0.0))


def _sb_diag_mask(r0, r1, d, tk):
    rows = r0 + lax.broadcasted_iota(jnp.int32, (r1 - r0, tk), 0)
    cols = d * tk + lax.broadcasted_iota(jnp.int32, (r1 - r0, tk), 1)
    return cols < rows


SB_ROW_PARTS = 2


def _sb_chains(r0, r1):
    part = (r1 - r0) // SB_ROW_PARTS
    return [(hd, r0 + p * part, r0 + (p + 1) * part) for p in range(SB_ROW_PARTS) for hd in range(2)]


def _head_masks():
    lane = lax.broadcasted_iota(jnp.int32, (1, LANES), 1)
    return [lane < SB_HEAD_DIM, lane >= SB_HEAD_DIM]


def sb_fwd(proj, comm=None):
    m = proj.shape[0]
    tq, tk = _sb_tiles(m)
    ndiag = tq // tk

    def body(q_ref, k_ref, v_ref, o_ref, tot_ref, qs, acc, car):
        i = pl.program_id(1)
        nfull = i * ndiag
        heads = _head_masks()
        upper = (lax.broadcasted_iota(jnp.int32, (tk, tk), 0) > lax.broadcasted_iota(jnp.int32, (tk, tk), 1)).astype(BF)
        qv = q_ref[...]
        for hd in range(2):
            qs[hd] = jnp.where(heads[hd], qv, jnp.zeros_like(qv)) * SB_SCALE
        acc[...] = jnp.zeros_like(acc)
        car[...] = jnp.zeros_like(car)

        def block(ks, r0, diag):
            kb = k_ref[pl.ds(ks, tk), :]
            vb = v_ref[pl.ds(ks, tk), :]
            chains = _sb_chains(r0, tq)
            masks = [None if diag is None else _sb_diag_mask(ra, rb, diag, tk) for _, ra, rb in chains]
            zs = [_dot_nt(qs[hd, ra:rb, :], kb) for hd, ra, rb in chains]
            mid = []
            for z, mask in zip(zs, masks):
                ls, l1 = _log_sigmoid_pair(z, mask)
                mid.append((ls, l1, _dot2(l1, upper)))
            pvs = []
            for (hd, ra, rb), (ls, l1, cum), mask in zip(chains, mid, masks):
                a = jnp.exp(ls + (cum + car[hd, ra:rb, :]))
                if mask is not None:
                    a = jnp.where(mask, a, 0.0)
                pvs.append(_dot(a.astype(BF), vb))
            for (hd, ra, rb), (ls, l1, cum), pv in zip(chains, mid, pvs):
                acc[hd, ra:rb, :] += pv
                car[hd, ra:rb, :] += jnp.sum(l1, axis=-1, keepdims=True)

        for d in reversed(range(ndiag)):
            block(pl.multiple_of((nfull + d) * tk, tk), d * tk, d)

        def step(jj, carry):
            block(pl.multiple_of((nfull - 1 - jj) * tk, tk), 0, None)
            return carry

        lax.fori_loop(0, nfull, step, 0)
        o_ref[...] = jnp.where(heads[0], acc[0], acc[1]).astype(BF)
        tot_ref[...] = jnp.where(heads[0], car[0], car[1])

    qb = 2 * 512 // LANES
    return _call_with_comm(
        body, comm, (proj, proj, proj), name="sb_fwd", grid=(4, m // tq),
        out_shape=(jax.ShapeDtypeStruct((m, 512), BF), jax.ShapeDtypeStruct((m, 512), F32)),
        in_specs=[pl.BlockSpec((tq, LANES), lambda p, i: (i, qb + p)),
                  pl.BlockSpec((m, LANES), lambda p, i: (0, qb + 4 + p)),
                  pl.BlockSpec((m, LANES), lambda p, i: (0, qb + 8 + p))],
        out_specs=(pl.BlockSpec((tq, LANES), lambda p, i: (i, p)), pl.BlockSpec((tq, LANES), lambda p, i: (i, p))),
        scratch_shapes=[pltpu.VMEM((2, tq, LANES), BF), pltpu.VMEM((2, tq, LANES), F32), pltpu.VMEM((2, tq, 1), F32)],
        compiler_params=_cp("arbitrary", "arbitrary"),
    )


def merge_norm(oa, ob, ga, gb):
    m, w = oa.shape
    tm = _row_tile(m)

    def body(a_ref, b_ref, ga_ref, gb_ref, o_ref):
        av = a_ref[...].astype(F32)
        bv = b_ref[...].astype(F32)
        o_ref[:, :w] = (av * _rstd(av) * ga_ref[...]).astype(BF)
        o_ref[:, w:] = (bv * _rstd(bv) * gb_ref[...]).astype(BF)

    row = pl.BlockSpec((tm, w), lambda i: (i, 0))
    vec = pl.BlockSpec((1, w), lambda i: (0, 0))
    return _call(
        body, name="merge_norm", out_shape=jax.ShapeDtypeStruct((m, 2 * w), BF), grid=(m // tm,),
        in_specs=[row, row, vec, vec], out_specs=pl.BlockSpec((tm, 2 * w), lambda i: (i, 0)),
        compiler_params=_cp("parallel"),
    )(oa, ob, ga, gb)


def _xa_probs(qh, kh):
    logits = _dot_nt(qh, kh) * XA_SCALE
    e = jnp.exp(logits - jnp.max(logits, axis=-1, keepdims=True))
    return e / jnp.sum(e, axis=-1, keepdims=True)


def xa_fwd(xq, kv):
    m, d = xq.shape
    mm = kv.shape[0]
    tm = _row_tile(m)

    def body(q_ref, kv_ref, o_ref):
        for hd in range(XA_HEADS):
            cs = slice(hd * XA_HEAD_DIM, (hd + 1) * XA_HEAD_DIM)
            p = _xa_probs(q_ref[:, cs], kv_ref[:, cs])
            vh = kv_ref[:, d + hd * XA_HEAD_DIM:d + (hd + 1) * XA_HEAD_DIM]
            o_ref[:, cs] = _dot(p.astype(BF), vh).astype(BF)

    return _call(
        body, name="xa_fwd", out_shape=jax.ShapeDtypeStruct((m, d), BF), grid=(m // tm,),
        in_specs=[pl.BlockSpec((tm, d), lambda i: (i, 0)), pl.BlockSpec((mm, 2 * d), lambda i: (0, 0))],
        out_specs=pl.BlockSpec((tm, d), lambda i: (i, 0)), compiler_params=_cp("parallel"),
    )(xq, kv)


def norm_bwd(h, gn, d_a=None, d_res=None, target=None, f_prev=None, gp_prev=None, alpha_prev=1.0):
    m, d = h.shape
    tm = _row_tile(m)
    has_loss = target is not None
    has_res = d_res is not None
    has_prev = f_prev is not None

    def body(*refs):
        refs = list(refs)
        h_ref, gn_ref = refs[0], refs[1]
        pos = 2
        da_ref = dres_ref = t_ref = f_ref = gp_ref = None
        if has_loss:
            t_ref = refs[pos]; pos += 1
        else:
            da_ref = refs[pos]; pos += 1
        if has_res:
            dres_ref = refs[pos]; pos += 1
        if has_prev:
            f_ref, gp_ref = refs[pos], refs[pos + 1]; pos += 2
        dh_ref, dgn_ref = refs[pos], refs[pos + 1]; pos += 2
        df_ref = dgp_ref = loss_ref = None
        if has_prev:
            df_ref, dgp_ref = refs[pos], refs[pos + 1]; pos += 2
        if has_loss:
            loss_ref = refs[pos]

        first = pl.program_id(0) == 0
        hv = h_ref[...]
        gn = gn_ref[...]
        if has_loss:
            err = hv * _rstd(hv) * gn - t_ref[...]
            da = err * (1.0 / d)
            part = 0.5 * jnp.sum(jnp.sum(err * err, axis=-1, keepdims=True) * (1.0 / d))

            @pl.when(first)
            def _():
                loss_ref[...] = jnp.zeros_like(loss_ref)

            loss_ref[...] += part
        else:
            da = da_ref[...].astype(F32)
        dx, dgn = _rms_bwd(hv, gn, da)
        dh = dx + dres_ref[...] if has_res else dx
        dh_ref[...] = dh

        @pl.when(first)
        def _():
            dgn_ref[...] = jnp.zeros_like(dgn_ref)

        dgn_ref[...] += dgn
        if has_prev:
            dfv, dgp = _rms_bwd(f_ref[...], gp_ref[...], dh)
            df_ref[...] = (alpha_prev * dfv).astype(BF)

            @pl.when(first)
            def _():
                dgp_ref[...] = jnp.zeros_like(dgp_ref)

            dgp_ref[...] += alpha_prev * dgp

    row = pl.BlockSpec((tm, d), lambda i: (i, 0))
    vec = pl.BlockSpec((1, d), lambda i: (0, 0))
    ins, in_specs = [h, gn], [row, vec]
    ins.append(target if has_loss else d_a); in_specs.append(row)
    if has_res:
        ins.append(d_res); in_specs.append(row)
    if has_prev:
        ins += [f_prev, gp_prev]; in_specs += [row, vec]
    outs = [jax.ShapeDtypeStruct((m, d), F32), jax.ShapeDtypeStruct((1, d), F32)]
    out_specs = [row, vec]
    names = ["d_h", "d_gn"]
    if has_prev:
        outs += [jax.ShapeDtypeStruct((m, d), BF), jax.ShapeDtypeStruct((1, d), F32)]
        out_specs += [row, vec]
        names += ["d_f", "d_gp"]
    if has_loss:
        outs.append(jax.ShapeDtypeStruct((8, LANES), F32))
        out_specs.append(pl.BlockSpec((8, LANES), lambda i: (0, 0)))
        names.append("loss")
    res = _call(
        body, name="norm_bwd", out_shape=tuple(outs), grid=(m // tm,), in_specs=in_specs,
        out_specs=tuple(out_specs), compiler_params=_cp("arbitrary"),
    )(*ins)
    return dict(zip(names, res))


def ffn_bwd_act(df, gb, which, g, u):
    m, d = df.shape
    nb = g.shape[-1]
    tm = _row_tile(m, 1024)

    def body(df_ref, w_ref, g_ref, u_ref, dg_ref, du_ref):
        parts = _row_parts(tm, 2)
        dhs = [_dot_nt(df_ref[rs, :], w_ref[...]) for rs in parts]
        for rs, dh in zip(parts, dhs):
            gv = g_ref[rs, :].astype(F32)
            uv = u_ref[rs, :].astype(F32)
            s = jax.nn.sigmoid(gv)
            dg_ref[rs, :] = (dh * uv * (s * (1.0 + gv * (1.0 - s)))).astype(BF)
            du_ref[rs, :] = (dh * gv * s).astype(BF)

    blk = jax.ShapeDtypeStruct((N_CHIPS, m, nb), BF)
    aspec = pl.BlockSpec((None, tm, nb), lambda k, i: (k, i, 0))
    return _call(
        body, name="ffn_bwd_act", out_shape=(blk, blk), grid=(N_CHIPS, m // tm),
        in_specs=[pl.BlockSpec((tm, d), lambda k, i: (i, 0)),
                  pl.BlockSpec((None, None, nb, d), lambda k, i: (k, which, 0, 0)), aspec, aspec],
        out_specs=(aspec, aspec), compiler_params=_cp("parallel", "parallel"),
    )(df, gb, g, u)


def mm_tn(a, b, dest, a_spec, b_spec, o_spec, acc_shape, msteps):
    def body(a_ref, b_ref, dest_ref, o_ref, acc):
        del dest_ref
        ms = pl.program_id(1)

        @pl.when(ms == 0)
        def _():
            acc[...] = jnp.zeros_like(acc)

        acc[...] += _dot_tn(a_ref[...], b_ref[...])

        @pl.when(ms == msteps - 1)
        def _():
            o_ref[...] = acc[...].astype(o_ref.dtype)

    return _call(
        body, name="mm_tn", out_shape=jax.ShapeDtypeStruct(dest.shape, dest.dtype), grid=(N_CHIPS, msteps),
        in_specs=[a_spec, b_spec, pl.BlockSpec(memory_space=pl.ANY)], out_specs=o_spec,
        scratch_shapes=[pltpu.VMEM(acc_shape, F32)], input_output_aliases={2: 0},
        compiler_params=_cp("parallel", "arbitrary"),
    )(a, b, dest)


def _act_spec(arr, tm, nb):
    if arr.ndim == 3:
        return pl.BlockSpec((None, tm, nb), lambda k, ms: (k, ms, 0))
    return pl.BlockSpec((tm, nb), lambda k, ms: (ms, k))


def grad_cb(a, dout, dest, which=None):
    m, kd = a.shape
    nb = dest.shape[-1]
    tm = _row_tile(m, GRAD_ROWS)
    if which is None:
        o_spec = pl.BlockSpec((None, kd, nb), lambda k, ms: (k, 0, 0))
    else:
        o_spec = pl.BlockSpec((None, None, kd, nb), lambda k, ms: (k, which, 0, 0))
    return mm_tn(a, dout, dest, pl.BlockSpec((tm, kd), lambda k, ms: (ms, 0)), _act_spec(dout, tm, nb), o_spec,
                 (kd, nb), m // tm)


def grad_rb(a, dout, dest, which):
    m, n = dout.shape
    kb = dest.shape[-2]
    tm = _row_tile(m, GRAD_ROWS)
    o_spec = pl.BlockSpec((None, None, kb, n), lambda k, ms: (k, which, 0, 0))
    return mm_tn(a, dout, dest, _act_spec(a, tm, kb), pl.BlockSpec((tm, n), lambda k, ms: (ms, 0)), o_spec,
                 (kb, n), m // tm)


def mm_nt_cb(pairs, n, out_dtype, comm=None):
    d0 = pairs[0][0]
    m = d0.shape[1] if d0.ndim == 3 else d0.shape[0]
    tm = _row_tile(m)
    npair = len(pairs)

    def body(*refs):
        o_ref, acc = refs[2 * npair], refs[2 * npair + 1]
        k = pl.program_id(1)

        @pl.when(k == 0)
        def _():
            acc[...] = jnp.zeros_like(acc)

        for p in range(npair):
            acc[...] += _dot_nt(refs[2 * p][...], refs[2 * p + 1][...])

        @pl.when(k == N_CHIPS - 1)
        def _():
            o_ref[...] = acc[...].astype(out_dtype)

    ins, in_specs = [], []
    for dout, w, which in pairs:
        nb = w.shape[-1]
        if dout.ndim == 3:
            in_specs.append(pl.BlockSpec((None, tm, nb), lambda i, k: (k, i, 0)))
        else:
            in_specs.append(pl.BlockSpec((tm, nb), lambda i, k: (i, k)))
        if w.ndim == 4:
            in_specs.append(pl.BlockSpec((None, None, n, nb), lambda i, k, which=which: (k, which, 0, 0)))
        else:
            in_specs.append(pl.BlockSpec((None, n, nb), lambda i, k: (k, 0, 0)))
        ins += [dout, w]
    (out,), extra = _call_with_comm(
        body, comm, tuple(ins), name="mm_nt_cb", out_shape=(jax.ShapeDtypeStruct((m, n), out_dtype),),
        grid=(m // tm, N_CHIPS), in_specs=in_specs, out_specs=(pl.BlockSpec((tm, n), lambda i, k: (i, 0)),),
        scratch_shapes=[pltpu.VMEM((tm, n), F32)],
        compiler_params=_cp("arbitrary", "arbitrary") if comm is not None else _cp("parallel", "arbitrary"),
    )
    return out if comm is None else (out, extra)


def mm_nt_rb(dout, w, which, out_dtype):
    m, n = dout.shape
    kb = w.shape[2]
    tm = _row_tile(m)

    def body(d_ref, w_ref, o_ref):
        o_ref[...] = _dot_nt(d_ref[...], w_ref[...].reshape(N_CHIPS * kb, n)).astype(out_dtype)

    return _call(
        body, name="mm_nt_rb", out_shape=jax.ShapeDtypeStruct((m, N_CHIPS * kb), out_dtype), grid=(m // tm,),
        in_specs=[pl.BlockSpec((tm, n), lambda i: (i, 0)),
                  pl.BlockSpec((N_CHIPS, None, kb, n), lambda i: (0, which, 0, 0))],
        out_specs=pl.BlockSpec((tm, N_CHIPS * kb), lambda i: (i, 0)), compiler_params=_cp("parallel"),
    )(dout, w)


def merge_norm_bwd(oa, ob, dmerged, ga, gb):
    m, w = oa.shape
    tm = _row_tile(m)

    def body(a_ref, b_ref, dm_ref, ga_ref, gb_ref, da_ref, db_ref, dga_ref, dgb_ref):
        @pl.when(pl.program_id(0) == 0)
        def _():
            dga_ref[...] = jnp.zeros_like(dga_ref)
            dgb_ref[...] = jnp.zeros_like(dgb_ref)

        da, dga = _rms_bwd(a_ref[...].astype(F32), ga_ref[...], dm_ref[:, :w].astype(F32))
        db, dgb = _rms_bwd(b_ref[...].astype(F32), gb_ref[...], dm_ref[:, w:].astype(F32))
        da_ref[...] = da
        db_ref[...] = db
        dga_ref[...] += dga
        dgb_ref[...] += dgb

    row = pl.BlockSpec((tm, w), lambda i: (i, 0))
    vec = pl.BlockSpec((1, w), lambda i: (0, 0))
    return _call(
        body, name="merge_norm_bwd", grid=(m // tm,),
        out_shape=(jax.ShapeDtypeStruct((m, w), F32), jax.ShapeDtypeStruct((m, w), F32),
                   jax.ShapeDtypeStruct((1, w), F32), jax.ShapeDtypeStruct((1, w), F32)),
        in_specs=[row, row, pl.BlockSpec((tm, 2 * w), lambda i: (i, 0)), vec, vec],
        out_specs=(row, row, vec, vec), compiler_params=_cp("arbitrary"),
    )(oa, ob, dmerged, ga, gb)


def sgu_bwd(proj, d_oa, wm, wmt, ng, nb, bst):
    m = proj.shape[0]
    tm = _row_tile(m)
    wd = SGU_GROUPS * CHUNK

    def body(u_ref, v_ref, do_ref, wm_ref, wmt_ref, ng_ref, nb_ref, bst_ref,
             dp_ref, dw_ref, dbt_ref, dng_ref, dnb_ref):
        @pl.when(pl.program_id(0) == 0)
        def _():
            dw_ref[...] = jnp.zeros_like(dw_ref)
            dbt_ref[...] = jnp.zeros_like(dbt_ref)
            dng_ref[...] = jnp.zeros_like(dng_ref)
            dnb_ref[...] = jnp.zeros_like(dnb_ref)

        causal = lax.broadcasted_iota(jnp.int32, (CHUNK, CHUNK), 0) >= lax.broadcasted_iota(jnp.int32, (CHUNK, CHUNK), 1)
        for c in range(tm // CHUNK):
            rs = slice(c * CHUNK, (c + 1) * CHUNK)
            for g in range(SGU_GROUPS):
                cs = slice(g * CHUNK, (g + 1) * CHUNK)
                ug, xh, rstd, vn, mixed = _sgu_core(u_ref, v_ref, wm_ref, ng_ref, nb_ref, bst_ref, g, c)
                do = do_ref[rs, cs]
                dug = do * mixed
                dmix = do * ug
                dmb = dmix.astype(BF)
                dbt_ref[g] += jnp.sum(dmix, axis=-1, keepdims=True)
                dw_ref[g] += jnp.where(causal, _dot_nt(dmb, vn.astype(BF)), 0.0)
                dvn = _dot(wmt_ref[g], dmb)
                dng_ref[g:g + 1, :] += jnp.sum(dvn * xh, axis=0, keepdims=True)
                dnb_ref[g:g + 1, :] += jnp.sum(dvn, axis=0, keepdims=True)
                dxh = dvn * ng_ref[g:g + 1, :]
                dvg = rstd * (dxh - jnp.mean(dxh, axis=-1, keepdims=True)
                              - xh * jnp.mean(dxh * xh, axis=-1, keepdims=True))
                dp_ref[rs, cs] = (dug * _gelu_grad(u_ref[rs, cs].astype(F32))).astype(BF)
                dp_ref[rs, wd + g * CHUNK:wd + (g + 1) * CHUNK] = (dvg * _gelu_grad(v_ref[rs, cs].astype(F32))).astype(BF)

    full = lambda shape: pl.BlockSpec(shape, lambda i: (0,) * len(shape))
    return _call(
        body, name="sgu_bwd", grid=(m // tm,),
        out_shape=(jax.ShapeDtypeStruct((m, 2 * wd), BF), jax.ShapeDtypeStruct(wm.shape, F32),
                   jax.ShapeDtypeStruct(bst.shape, F32), jax.ShapeDtypeStruct(ng.shape, F32),
                   jax.ShapeDtypeStruct(nb.shape, F32)),
        in_specs=[pl.BlockSpec((tm, wd), lambda i: (i, 0)), pl.BlockSpec((tm, wd), lambda i: (i, 1)),
                  pl.BlockSpec((tm, wd), lambda i: (i, 0)),
                  full(wm.shape), full(wmt.shape), full(ng.shape), full(nb.shape), full(bst.shape)],
        out_specs=(pl.BlockSpec((tm, 2 * wd), lambda i: (i, 0)), full(wm.shape), full(bst.shape), full(ng.shape),
                   full(nb.shape)),
        compiler_params=_cp("arbitrary"),
    )(proj, proj, d_oa, wm, wmt, ng, nb, bst)


def sb_bwd(proj, tot, d_ob, comm=None):
    m = proj.shape[0]
    tq, tk = _sb_tiles(m)

    ndiag = tq // tk

    def body(q_ref, k_ref, v_ref, tot_ref, do_ref, dq_ref, dk_ref, dv_ref, qs, dos, tots, dqa, cl1, cg):
        i = pl.program_id(1)

        @pl.when(i == 0)
        def _():
            dk_ref[...] = jnp.zeros_like(dk_ref)
            dv_ref[...] = jnp.zeros_like(dv_ref)

        nfull = i * ndiag
        heads = _head_masks()
        r_io = lax.broadcasted_iota(jnp.int32, (tk, tk), 0)
        c_io = lax.broadcasted_iota(jnp.int32, (tk, tk), 1)
        incl = (r_io <= c_io).astype(BF)
        excl = (r_io < c_io).astype(BF)
        qv = q_ref[...]
        dov = do_ref[...].astype(BF)
        totv = tot_ref[...]
        for hd in range(2):
            qs[hd] = jnp.where(heads[hd], qv, jnp.zeros_like(qv)) * SB_SCALE
            dos[hd] = jnp.where(heads[hd], dov, jnp.zeros_like(dov))
            tots[hd] = jnp.max(jnp.where(heads[hd], totv, -jnp.inf), axis=-1, keepdims=True)
        dqa[...] = jnp.zeros_like(dqa)
        cl1[...] = jnp.zeros_like(cl1)
        cg[...] = jnp.zeros_like(cg)

        def block(ks, r0, diag):
            kb = k_ref[pl.ds(ks, tk), :]
            vb = v_ref[pl.ds(ks, tk), :]
            chains = _sb_chains(r0, tq)
            masks = [None if diag is None else _sb_diag_mask(ra, rb, diag, tk) for _, ra, rb in chains]
            qc = [qs[hd, ra:rb, :] for hd, ra, rb in chains]
            doc = [dos[hd, ra:rb, :] for hd, ra, rb in chains]
            zs = [_dot_nt(q, kb) for q in qc]
            das = [_dot_nt(do, vb) for do in doc]
            s1 = []
            for z, mask in zip(zs, masks):
                ls, l1 = _log_sigmoid_pair(z, mask)
                s1.append((ls, l1, _dot2(l1, incl)))
            s2 = []
            for (hd, ra, rb), (ls, l1, pre), da, mask in zip(chains, s1, das, masks):
                a = jnp.exp(ls + (tots[hd, ra:rb, :] - (pre + cl1[hd, ra:rb, :])))
                if mask is not None:
                    a = jnp.where(mask, a, 0.0)
                gmat = a * da
                s2.append((a, gmat, _dot2(gmat, excl)))
            dk_sum = dv_sum = None
            for n, (hd, ra, rb) in enumerate(chains):
                a, gmat, pref = s2[n]
                sg = jnp.exp(s1[n][0])
                dz = gmat * (1.0 - sg) - (pref + cg[hd, ra:rb, :]) * sg
                if masks[n] is not None:
                    dz = jnp.where(masks[n], dz, 0.0)
                dz = dz.astype(BF)
                dqa[hd, ra:rb, :] += _dot(dz, kb)
                dk_t = _dot_tn(dz, qc[n])
                dv_t = _dot_tn(a.astype(BF), doc[n])
                dk_sum = dk_t if dk_sum is None else dk_sum + dk_t
                dv_sum = dv_t if dv_sum is None else dv_sum + dv_t
            for n, (hd, ra, rb) in enumerate(chains):
                cl1[hd, ra:rb, :] += jnp.sum(s1[n][1], axis=-1, keepdims=True)
                cg[hd, ra:rb, :] += jnp.sum(s2[n][1], axis=-1, keepdims=True)
            dk_ref[pl.ds(ks, tk), :] += dk_sum
            dv_ref[pl.ds(ks, tk), :] += dv_sum

        def step(j, carry):
            block(pl.multiple_of(j * tk, tk), 0, None)
            return carry

        lax.fori_loop(0, nfull, step, 0)
        for d in range(ndiag):
            block(pl.multiple_of((nfull + d) * tk, tk), d * tk, d)
        dq_ref[...] = jnp.where(heads[0], dqa[0], dqa[1]) * SB_SCALE

    qb = 2 * 512 // LANES
    tile = pl.BlockSpec((tq, LANES), lambda p, i: (i, p))
    seq = pl.BlockSpec((m, LANES), lambda p, i: (0, p))
    out = jax.ShapeDtypeStruct((m, 512), F32)
    return _call_with_comm(
        body, comm, (proj, proj, proj, tot, d_ob), name="sb_bwd", grid=(4, m // tq), out_shape=(out, out, out),
        in_specs=[pl.BlockSpec((tq, LANES), lambda p, i: (i, qb + p)),
                  pl.BlockSpec((m, LANES), lambda p, i: (0, qb + 4 + p)),
                  pl.BlockSpec((m, LANES), lambda p, i: (0, qb + 8 + p)), tile, tile],
        out_specs=(tile, seq, seq),
        scratch_shapes=[pltpu.VMEM((2, tq, LANES), BF), pltpu.VMEM((2, tq, LANES), BF), pltpu.VMEM((2, tq, 1), F32),
                        pltpu.VMEM((2, tq, LANES), F32), pltpu.VMEM((2, tq, 1), F32), pltpu.VMEM((2, tq, 1), F32)],
        compiler_params=_cp("arbitrary", "arbitrary"),
    )


def xa_bwd(xq, kv, d_o):
    m, d = xq.shape
    mm = kv.shape[0]
    tm = _row_tile(m)

    def body(q_ref, kv_ref, do_ref, dq_ref, dkv_ref):
        @pl.when(pl.program_id(0) == 0)
        def _():
            dkv_ref[...] = jnp.zeros_like(dkv_ref)

        for hd in range(XA_HEADS):
            cs = slice(hd * XA_HEAD_DIM, (hd + 1) * XA_HEAD_DIM)
            vs = slice(d + hd * XA_HEAD_DIM, d + (hd + 1) * XA_HEAD_DIM)
            qh = q_ref[:, cs]
            kh = kv_ref[:, cs]
            doh = do_ref[:, cs]
            p = _xa_probs(qh, kh)
            dp = _dot_nt(doh, kv_ref[:, vs])
            ds = (p * (dp - jnp.sum(p * dp, axis=-1, keepdims=True))).astype(BF)
            dq_ref[:, cs] = (_dot(ds, kh) * XA_SCALE).astype(BF)
            dkv_ref[:, cs] += _dot_tn(ds, qh) * XA_SCALE
            dkv_ref[:, vs] += _dot_tn(p.astype(BF), doh)

    row = pl.BlockSpec((tm, d), lambda i: (i, 0))
    whole = pl.BlockSpec((mm, 2 * d), lambda i: (0, 0))
    return _call(
        body, name="xa_bwd", grid=(m // tm,),
        out_shape=(jax.ShapeDtypeStruct((m, d), BF), jax.ShapeDtypeStruct((mm, 2 * d), F32)),
        in_specs=[row, whole, row], out_specs=(row, whole), compiler_params=_cp("arbitrary"),
    )(xq, kv, d_o)


def adamw(w, g, mom, vel, g_index=None):
    r, c = w.shape
    tr = r
    for cand in (512, 256, 128, 64, 32, 16, 8):
        if r % cand == 0:
            tr = cand
            break

    def body(w_ref, g_ref, m_ref, v_ref, go_ref, d_ref, nm_ref, nv_ref):
        gv = g_ref[...]
        mn = ADAM_B1 * m_ref[...] + (1.0 - ADAM_B1) * gv
        vn = ADAM_B2 * v_ref[...] + (1.0 - ADAM_B2) * (gv * gv)
        m_hat = mn / (1.0 - ADAM_B1 ** ADAM_STEP)
        v_hat = vn / (1.0 - ADAM_B2 ** ADAM_STEP)
        go_ref[...] = gv
        d_ref[...] = -ADAM_LR * (m_hat / (jnp.sqrt(v_hat) + ADAM_EPS) + ADAM_WD * w_ref[...])
        nm_ref[...] = mn
        nv_ref[...] = vn

    spec = pl.BlockSpec((tr, c), lambda i: (i, 0))
    gspec = spec if g_index is None else pl.BlockSpec((None, tr, c), lambda i: (g_index, i, 0))
    out = jax.ShapeDtypeStruct((r, c), F32)
    return _call(
        body, name="adamw", out_shape=(out, out, out, out), grid=(r // tr,), in_specs=[spec, gspec, spec, spec],
        out_specs=(spec, spec, spec, spec), compiler_params=_cp("parallel"),
    )(w, g, mom, vel)


def _place():
    x, y, c = lax.axis_index("x"), lax.axis_index("y"), lax.axis_index("c")
    others = [(1 - x, y), (x, 1 - y), (1 - x, 1 - y)]
    return x, y, c, others


_HALF = {"A1": (1, 512), "A2": (1, 512), "B1": (0, 352), "B2": (0, 352), "C": (1, 128), "D": (0, 512), "E": (0, 512)}
SET_EARLY = ("A1", "B1", "D")
SET_LATE = ("A2", "B2", "C", "E")
_GROUPS = SET_EARLY + SET_LATE
_HBM = pl.BlockSpec(memory_space=pl.ANY)


def _half_of(ref, name, hc, lead=0):
    axis, size = _HALF[name]
    idx = [slice(None)] * (lead + axis) + [pl.ds(hc * size, size)]
    return ref.at[tuple(idx)]


def _gather_slots(loc, names):
    me = 2 * lax.axis_index("x") + lax.axis_index("y")
    init = []
    for nm in names:
        full = lax.empty((N_CHIPS,) + loc[nm].shape, loc[nm].dtype)
        init.append(lax.dynamic_update_slice(full, loc[nm][None], (me,) + (0,) * loc[nm].ndim))
    return init


def _gather_ici(names, src, out, send, recv, sends=True, arrivals=True):
    x, y, c, others = _place()
    me = 2 * x + y
    out_sends, out_arrivals = [], []
    for a, nm in enumerate(names):
        for j, (ox, oy) in enumerate(others):
            sems = dict(send_sem=send.at[3 * a + j], recv_sem=recv.at[3 * a + j], device_id_type=MESH)
            if sends:
                out_sends.append(pltpu.make_async_remote_copy(
                    src_ref=_half_of(src[a], nm, c), dst_ref=_half_of(out[a].at[me], nm, c), device_id=(ox, oy, c),
                    **sems))
            if arrivals:
                landed = _half_of(out[a].at[2 * ox + oy], nm, c)
                out_arrivals.append(pltpu.make_async_remote_copy(src_ref=landed, dst_ref=landed, device_id=(x, y, c),
                                                                 **sems))
    return out_sends, out_arrivals


def _gather_d2d(names, given, out, send, recv):
    x, y, c, others = _place()
    sends, arrivals = [], []
    for a, nm in enumerate(names):
        for j, (ox, oy) in enumerate(others):
            sems = dict(send_sem=send.at[3 * a + j], recv_sem=recv.at[3 * a + j], device_id_type=MESH)
            sends.append(pltpu.make_async_remote_copy(
                src_ref=_half_of(given[a].at[2 * ox + oy], nm, c), dst_ref=_half_of(out[a].at[2 * ox + oy], nm, c),
                device_id=(x, y, 1 - c), **sems))
            landed = _half_of(out[a].at[2 * ox + oy], nm, 1 - c)
            arrivals.append(pltpu.make_async_remote_copy(src_ref=landed, dst_ref=landed, device_id=(x, y, c), **sems))
    return sends, arrivals


def gather_weights(loc, names):
    n = len(names)

    def body(*refs):
        src, given, out = refs[:n], refs[n:2 * n], refs[2 * n:3 * n]
        send1, recv1, send2, recv2 = refs[3 * n:3 * n + 4]
        first, landed = _gather_ici(names, src, out, send1, recv1)
        del given
        passed, arrivals = _gather_d2d(names, out, out, send2, recv2)
        for f in first:
            f.start()
        for l, p in zip(landed, passed):
            l.wait_recv()
            p.start()
        for a in arrivals:
            a.wait_recv()
        for f in first + passed:
            f.wait_send()

    init = _gather_slots(loc, names)
    res = _call(
        body, name="gather_weights", out_shape=tuple(jax.ShapeDtypeStruct(t.shape, t.dtype) for t in init),
        in_specs=[_HBM] * (2 * n), out_specs=(_HBM,) * n, input_output_aliases={n + a: a for a in range(n)},
        scratch_shapes=[pltpu.SemaphoreType.DMA((3 * n,))] * 4,
    )(*[loc[nm] for nm in names], *init)
    return dict(zip(names, res))


def gather_forward(bufs, names):
    n = len(names)

    def body(*refs):
        given, out = refs[:n], refs[n:2 * n]
        passed, arrivals = _gather_d2d(names, given, out, refs[2 * n], refs[2 * n + 1])
        for p in passed:
            p.start()
        for a in arrivals:
            a.wait_recv()
        for p in passed:
            p.wait_send()

    res = _call(
        body, name="gather_forward", out_shape=tuple(jax.ShapeDtypeStruct(t.shape, t.dtype) for t in bufs),
        in_specs=[_HBM] * n, out_specs=(_HBM,) * n, input_output_aliases={a: a for a in range(n)},
        scratch_shapes=[pltpu.SemaphoreType.DMA((3 * n,))] * 2,
    )(*bufs)
    return dict(zip(names, res))


class FusedComm:
    def __init__(self, ins, outs, aliases, n_sems, start, finish):
        self.ins, self.outs, self.aliases, self.n_sems, self.start, self.finish = ins, outs, aliases, n_sems, start, finish


def gather_comm(loc, names):
    n = len(names)

    def start(ins, outs, send, recv):
        for f in _gather_ici(names, ins[:n], outs, send, recv, arrivals=False)[0]:
            f.start()

    def finish(ins, outs, send, recv):
        first, landed = _gather_ici(names, ins[:n], outs, send, recv)
        for l in landed:
            l.wait_recv()
        for f in first:
            f.wait_send()

    init = _gather_slots(loc, names)
    return FusedComm([loc[nm] for nm in names] + init, [jax.ShapeDtypeStruct(t.shape, t.dtype) for t in init],
                     {n + a: a for a in range(n)}, 3 * n, start, finish)


def _half_shape(name, shape):
    axis, size = _HALF[name]
    s = list(shape)
    s[axis] = size
    return tuple(s)


def rs_to_sibling(grads, names):
    n = len(names)

    def body(*refs):
        src = dict(zip(names, refs[:n]))
        out = dict(zip(names, refs[n:2 * n]))
        send, recv = refs[2 * n], refs[2 * n + 1]
        x, y, c, _ = _place()
        copies = []
        for a, nm in enumerate(names):
            copies.append(pltpu.make_async_remote_copy(
                src_ref=_half_of(src[nm], nm, 1 - c, lead=1), dst_ref=out[nm], send_sem=send.at[a], recv_sem=recv.at[a],
                device_id=(x, y, 1 - c), device_id_type=MESH))
        for cpy in copies:
            cpy.start()
        for cpy in copies:
            cpy.wait()

    hbm = pl.BlockSpec(memory_space=pl.ANY)
    outs = tuple(jax.ShapeDtypeStruct((N_CHIPS,) + _half_shape(nm, grads[nm].shape[1:]), grads[nm].dtype)
                 for nm in names)
    res = _call(
        body, name="rs_to_sibling", out_shape=outs, in_specs=[hbm] * n, out_specs=(hbm,) * n,
        scratch_shapes=[pltpu.SemaphoreType.DMA((n,)), pltpu.SemaphoreType.DMA((n,))],
    )(*[grads[nm] for nm in names])
    return dict(zip(names, res))


def _tile2(shape):
    lead = shape[:-2]
    return lead, shape[-2:]


def add_halves(name, mine, got, c_idx):
    axis, size = _HALF[name]
    hshape = got.shape
    lead, last2 = hshape[:-2], hshape[-2:]
    nlead = len(lead)
    haxis = 1 + axis

    def body(c_ref, m_ref, g_ref, o_ref):
        del c_ref
        o_ref[...] = (m_ref[...].astype(F32) + g_ref[...].astype(F32)).astype(o_ref.dtype)

    blk = (None,) * nlead + last2

    def got_map(*idx):
        return tuple(idx[:nlead]) + (0, 0)

    def mine_map(*idx):
        lead_idx = list(idx[:nlead])
        c = idx[nlead][0]
        if haxis < nlead:
            lead_idx[haxis] = lead_idx[haxis] + c * size
            return tuple(lead_idx) + (0, 0)
        return tuple(lead_idx) + (c, 0)

    grid_spec = pltpu.PrefetchScalarGridSpec(
        num_scalar_prefetch=1, grid=lead,
        in_specs=[pl.BlockSpec(blk, mine_map), pl.BlockSpec(blk, got_map)],
        out_specs=pl.BlockSpec(blk, got_map))
    return _call(
        body, name="add_halves", out_shape=jax.ShapeDtypeStruct(hshape, got.dtype), grid_spec=grid_spec,
        compiler_params=_cp(*(("parallel",) * nlead)),
    )(c_idx, mine, got)


def _rs_ici(n, src, out, send, recv):
    x, y, c, others = _place()
    copies = []
    for a in range(n):
        for j, (ox, oy) in enumerate(others):
            copies.append(pltpu.make_async_remote_copy(
                src_ref=src[a].at[2 * ox + oy], dst_ref=out[a].at[j], send_sem=send.at[3 * a + j],
                recv_sem=recv.at[3 * a + j], device_id=(ox, oy, c), device_id_type=MESH))
    return copies


def _rs_out_shapes(summed, names):
    return [jax.ShapeDtypeStruct((3,) + summed[nm].shape[1:], summed[nm].dtype) for nm in names]


def rs_comm(summed, names):
    n = len(names)

    def start(ins, outs, send, recv):
        for cpy in _rs_ici(n, ins, outs, send, recv):
            cpy.start()

    def finish(ins, outs, send, recv):
        for cpy in _rs_ici(n, ins, outs, send, recv):
            cpy.wait()

    return FusedComm([summed[nm] for nm in names], _rs_out_shapes(summed, names), {}, 3 * n, start, finish)


def add_chips(name, summed, got, kc_idx, full_shape):
    axis, size = _HALF[name]
    hshape = summed.shape[1:]
    lead, last2 = hshape[:-2], hshape[-2:]
    nlead = len(lead)

    def body(kc_ref, s_ref, g0_ref, g1_ref, g2_ref, o_ref):
        del kc_ref
        o_ref[...] = ((s_ref[...].astype(F32) + g0_ref[...].astype(F32)) + g1_ref[...].astype(F32)) + g2_ref[...].astype(F32)

    blk = (None,) * (nlead + 1) + last2
    oblk = (None,) * nlead + last2

    def got_map(slot):
        return lambda *idx: (slot,) + tuple(idx[:nlead]) + (0, 0)

    def out_map(*idx):
        lead_idx = list(idx[:nlead])
        c = idx[-1][1]
        if axis < nlead:
            lead_idx[axis] = lead_idx[axis] + c * size
            return tuple(lead_idx) + (0, 0)
        return tuple(lead_idx) + (c, 0)

    grid_spec = pltpu.PrefetchScalarGridSpec(
        num_scalar_prefetch=1, grid=lead if nlead else (1,),
        in_specs=[pl.BlockSpec(blk, lambda *idx: (idx[-1][0],) + tuple(idx[:nlead]) + (0, 0)),
                  pl.BlockSpec(blk, got_map(0)), pl.BlockSpec(blk, got_map(1)), pl.BlockSpec(blk, got_map(2))],
        out_specs=pl.BlockSpec(oblk, out_map))
    return _call(
        body, name="add_chips", out_shape=jax.ShapeDtypeStruct(full_shape, F32), grid_spec=grid_spec,
        compiler_params=_cp(*(("parallel",) * max(nlead, 1))),
    )(kc_idx, summed, got, got, got)


def rs_replicate(shards, names):
    n = len(names)

    def body(*refs):
        given = dict(zip(names, refs[:n]))
        buf = dict(zip(names, refs[n:2 * n]))
        send, recv = refs[2 * n], refs[2 * n + 1]
        x, y, c, _ = _place()
        copies = []
        for a, nm in enumerate(names):
            copies.append(pltpu.make_async_remote_copy(
                src_ref=_half_of(given[nm], nm, c), dst_ref=_half_of(buf[nm], nm, c), send_sem=send.at[a],
                recv_sem=recv.at[a], device_id=(x, y, 1 - c), device_id_type=MESH))
        for cpy in copies:
            cpy.start()
        for a, nm in enumerate(names):
            other = _half_of(buf[nm], nm, 1 - c)
            pltpu.make_async_remote_copy(src_ref=other, dst_ref=other, send_sem=send.at[a], recv_sem=recv.at[a],
                                         device_id=(x, y, 1 - c), device_id_type=MESH).wait_recv()
        for cpy in copies:
            cpy.wait_send()

    hbm = pl.BlockSpec(memory_space=pl.ANY)
    outs = tuple(jax.ShapeDtypeStruct(shards[nm].shape, F32) for nm in names)
    res = _call(
        body, name="rs_replicate", out_shape=outs, in_specs=[hbm] * n, out_specs=(hbm,) * n,
        input_output_aliases={a: a for a in range(n)},
        scratch_shapes=[pltpu.SemaphoreType.DMA((n,)), pltpu.SemaphoreType.DMA((n,))],
    )(*[shards[nm] for nm in names])
    return dict(zip(names, res))


def allreduce_small(v):
    r = v.shape[0]

    def body(v_ref, o_ref, slots, send, recv):
        x, y, c, _ = _place()
        me = 4 * x + 2 * y + c
        slots[me] = v_ref[...]
        copies = []
        for rel in range(1, 8):
            fx, fy, fc = (rel >> 2) & 1, (rel >> 1) & 1, rel & 1
            peer = (x ^ fx, y ^ fy, c ^ fc)
            copies.append(pltpu.make_async_remote_copy(
                src_ref=v_ref, dst_ref=slots.at[me], send_sem=send.at[rel - 1], recv_sem=recv.at[rel - 1],
                device_id=peer, device_id_type=MESH))
        for cpy in copies:
            cpy.start()
        for rel in range(1, 8):
            fx, fy, fc = (rel >> 2) & 1, (rel >> 1) & 1, rel & 1
            src_id = 4 * (x ^ fx) + 2 * (y ^ fy) + (c ^ fc)
            pltpu.make_async_remote_copy(
                src_ref=v_ref, dst_ref=slots.at[src_id], send_sem=send.at[rel - 1], recv_sem=recv.at[rel - 1],
                device_id=(x, y, c), device_id_type=MESH).wait_recv()
        for cpy in copies:
            cpy.wait_send()
        acc = slots[0]
        for s in range(1, 8):
            acc = acc + slots[s]
        o_ref[...] = acc

    vm = pl.BlockSpec(memory_space=pltpu.VMEM)
    return _call(
        body, name="allreduce_small", out_shape=jax.ShapeDtypeStruct(v.shape, F32), in_specs=[vm], out_specs=vm,
        scratch_shapes=[pltpu.VMEM((8, r, LANES), F32), pltpu.SemaphoreType.DMA((7,)), pltpu.SemaphoreType.DMA((7,))],
    )(v)


def local_step(x, mem, target, ga1, small, ffn1_up, fwd_sb, bwd_sb, ffn1_da):
    causal = jnp.tril(jnp.ones((CHUNK, CHUNK), dtype=bool))
    w_s = jnp.where(causal[None], small["sgu_w_s"], 0.0)
    wm = w_s.astype(BF)
    wmt = jnp.swapaxes(w_s, 1, 2).astype(BF)
    bst = small["sgu_b_s"].reshape(SGU_GROUPS, CHUNK, 1)
    ng, nbias = small["sgu_norm_g"], small["sgu_norm_b"]

    a1 = rms_fwd(x, small["ffn1_pre_g"])
    g1, u1, hid1, rest = ffn1_up(a1)
    gb1, gd = rest["B1"][:, None], rest["D"]
    f1, h1, n1 = mm_res(hid1, gb1, 0, x, small["ffn1_post_g"], 0.5, small["mix_pre_g"])
    proj = mm_cb(n1, gd)
    oa = sgu_fwd(proj, wm, ng, nbias, bst)
    ob, tot, late = fwd_sb(proj)
    ga2, gb2, gc, ge = late["A2"], late["B2"][:, None], late["C"], late["E"]
    merged = merge_norm(oa, ob, small["sgu_out_g"], small["sb_out_g"])
    mo, h2, xn = mm_res(merged, gc, 0, h1, small["mix_post_g"], 1.0, small["xa_pre_g"])
    memn = rms_fwd(mem, small["mem_norm_g"])
    kv = mm_cb(memn, ge)
    xq = mm_res_plain(xn, gc, 1)
    o = xa_fwd(xq, kv)
    cc, h3, a2 = mm_res(o, gc, 2, h2, small["xa_post_g"], 1.0, small["ffn2_pre_g"])
    (g2, u2, hid2), _ = ffn_up(a2, ga2, 0, 1)
    f2, h4, _ = mm_res(hid2, gb2, 0, h3, small["ffn2_post_g"], 0.5, small["final_norm_g"])

    dga1, dga2 = lax.empty(ga1.shape, BF), lax.empty(ga2.shape, BF)
    dgb1, dgb2 = lax.empty(gb1.shape, BF), lax.empty(gb2.shape, BF)
    dgc = lax.empty(gc.shape, BF)
    dgd = lax.empty(gd.shape, BF)
    dge = lax.empty(ge.shape, BF)
    sg = {}

    r = norm_bwd(h4, small["final_norm_g"], target=target, f_prev=f2, gp_prev=small["ffn2_post_g"], alpha_prev=0.5)
    loss_tile, dh4, df2 = r["loss"], r["d_h"], r["d_f"]
    sg["final_norm_g"], sg["ffn2_post_g"] = r["d_gn"], r["d_gp"]

    dg2, du2 = ffn_bwd_act(df2, gb2, 0, g2, u2)
    dgb2 = grad_rb(hid2, df2, dgb2, 0)
    dga2 = grad_cb(a2, dg2, dga2, 0)
    dga2 = grad_cb(a2, du2, dga2, 1)
    da2 = mm_nt_cb([(dg2, ga2, 0), (du2, ga2, 1)], D_MODEL, F32)
    r = norm_bwd(h3, small["ffn2_pre_g"], d_a=da2, d_res=dh4, f_prev=cc, gp_prev=small["xa_post_g"], alpha_prev=1.0)
    dh3, dc = r["d_h"], r["d_f"]
    sg["ffn2_pre_g"], sg["xa_post_g"] = r["d_gn"], r["d_gp"]

    d_o = mm_nt_rb(dc, gc, 2, BF)
    dgc = grad_rb(o, dc, dgc, 2)
    dxq, dkv = xa_bwd(xq, kv, d_o)
    dkvb = dkv.astype(BF)
    dge = grad_cb(memn, dkvb, dge)
    dmemn = mm_nt_cb([(dkvb, ge, None)], D_MODEL, F32)
    sg["mem_norm_g"] = norm_bwd(mem, small["mem_norm_g"], d_a=dmemn)["d_gn"]
    dgc = grad_rb(xn, dxq, dgc, 1)
    dxn = mm_nt_rb(dxq, gc, 1, F32)
    r = norm_bwd(h2, small["xa_pre_g"], d_a=dxn, d_res=dh3, f_prev=mo, gp_prev=small["mix_post_g"], alpha_prev=1.0)
    dh2, dmo = r["d_h"], r["d_f"]
    sg["xa_pre_g"], sg["mix_post_g"] = r["d_gn"], r["d_gp"]

    dmerged = mm_nt_rb(dmo, gc, 0, BF)
    dgc = grad_rb(merged, dmo, dgc, 0)
    d_oa, d_ob, sg["sgu_out_g"], sg["sb_out_g"] = merge_norm_bwd(oa, ob, dmerged, small["sgu_out_g"], small["sb_out_g"])
    dp_uv, dws, dbt, sg["sgu_norm_g"], sg["sgu_norm_b"] = sgu_bwd(proj, d_oa, wm, wmt, ng, nbias, bst)
    sg["sgu_w_s"] = dws
    sg["sgu_b_s"] = dbt.reshape(SGU_GROUPS, CHUNK)
    late_grads = {"A2": dga2, "B2": dgb2.reshape(late["B2"].shape), "C": dgc, "E": dge}
    dq, dk, dv, state = bwd_sb(proj, tot, d_ob, late_grads)
    dproj = jnp.concatenate([dp_uv, dq.astype(BF), dk.astype(BF), dv.astype(BF)], axis=1)
    dgd = grad_cb(n1, dproj, dgd)
    dn1 = mm_nt_cb([(dproj, gd, None)], D_MODEL, F32)
    r = norm_bwd(h1, small["mix_pre_g"], d_a=dn1, d_res=dh2, f_prev=f1, gp_prev=small["ffn1_post_g"], alpha_prev=0.5)
    dh1, df1 = r["d_h"], r["d_f"]
    sg["mix_pre_g"], sg["ffn1_post_g"] = r["d_gn"], r["d_gp"]

    dg1, du1 = ffn_bwd_act(df1, gb1, 0, g1, u1)
    dgb1 = grad_rb(hid1, df1, dgb1, 0)
    dga1 = grad_cb(a1, dg1, dga1, 0)
    dga1 = grad_cb(a1, du1, dga1, 1)
    early_grads = {"A1": dga1, "B1": dgb1.reshape(rest["B1"].shape), "D": dgd}
    da1, state1 = ffn1_da([(dg1, ga1, 0), (du1, ga1, 1)], early_grads)
    r = norm_bwd(x, small["ffn1_pre_g"], d_a=da1, d_res=dh1)
    grad_x = r["d_h"]
    sg["ffn1_pre_g"] = r["d_gn"]
    return loss_tile, grad_x, sg, state, state1


def mm_res_plain(a, w, which):
    m = a.shape[0]
    kb, n = w.shape[2], w.shape[3]
    tm = _row_tile(m)

    def body(a_ref, w_ref, o_ref):
        acc = None
        for k in range(N_CHIPS):
            t = _dot(a_ref[:, k * kb:(k + 1) * kb], w_ref[k])
            acc = t if acc is None else acc + t
        o_ref[...] = acc.astype(BF)

    return _call(
        body, name="mm_rb", out_shape=jax.ShapeDtypeStruct((m, n), BF), grid=(m // tm,),
        in_specs=[pl.BlockSpec((tm, N_CHIPS * kb), lambda i: (i, 0)),
                  pl.BlockSpec((N_CHIPS, None, kb, n), lambda i: (0, which, 0, 0))],
        out_specs=pl.BlockSpec((tm, n), lambda i: (i, 0)), compiler_params=_cp("parallel"),
    )(a, w)


_BIG = ("ffn1_w_gate", "ffn1_w_up", "ffn1_w_down", "w_in", "w_out", "xa_w_q", "xa_w_kv", "xa_w_o",
        "ffn2_w_gate", "ffn2_w_up", "ffn2_w_down")
_SMALL = ("ffn1_pre_g", "ffn1_post_g", "mix_pre_g", "mix_post_g", "sgu_norm_g", "sgu_norm_b", "sgu_w_s", "sgu_b_s",
          "sgu_out_g", "sb_out_g", "xa_pre_g", "xa_post_g", "mem_norm_g", "ffn2_pre_g", "ffn2_post_g", "final_norm_g")
_WEIGHTS = ("ffn1_pre_g", "ffn1_post_g", "ffn1_w_gate", "ffn1_w_up", "ffn1_w_down", "mix_pre_g", "mix_post_g", "w_in",
            "sgu_norm_g", "sgu_norm_b", "sgu_w_s", "sgu_b_s", "sgu_out_g", "sb_out_g", "w_out", "xa_pre_g", "xa_post_g",
            "mem_norm_g", "xa_w_q", "xa_w_kv", "xa_w_o", "ffn2_pre_g", "ffn2_post_g", "ffn2_w_gate", "ffn2_w_up",
            "ffn2_w_down", "final_norm_g")
_SLOT = {"ffn1_w_gate": ("A1", 0), "ffn1_w_up": ("A1", 1), "ffn2_w_gate": ("A2", 0), "ffn2_w_up": ("A2", 1),
         "ffn1_w_down": ("B1", None), "ffn2_w_down": ("B2", None), "w_out": ("C", 0), "xa_w_q": ("C", 1),
         "xa_w_o": ("C", 2), "w_in": ("D", None), "xa_w_kv": ("E", None)}


def _pack_small(vals):
    return jnp.concatenate([vals[nm].reshape(-1, LANES) for nm in _SMALL], axis=0)


def _unpack_small(packed, shapes):
    out, pos = {}, 0
    for nm in _SMALL:
        rows = math.prod(shapes[nm]) // LANES
        out[nm] = packed[pos:pos + rows].reshape(shapes[nm])
        pos += rows
    return out


def kernel(x, mem, ffn1_pre_g, ffn1_post_g, ffn1_w_gate, ffn1_w_up, ffn1_w_down, mix_pre_g, mix_post_g, w_in, sgu_norm_g, sgu_norm_b, sgu_w_s, sgu_b_s, sgu_out_g, sb_out_g, w_out, xa_pre_g, xa_post_g, mem_norm_g, xa_w_q, xa_w_kv, xa_w_o, ffn2_pre_g, ffn2_post_g, ffn2_w_gate, ffn2_w_up, ffn2_w_down, final_norm_g, loss_target, m_ffn1_pre_g, m_ffn1_post_g, m_ffn1_w_gate, m_ffn1_w_up, m_ffn1_w_down, m_mix_pre_g, m_mix_post_g, m_w_in, m_sgu_norm_g, m_sgu_norm_b, m_sgu_w_s, m_sgu_b_s, m_sgu_out_g, m_sb_out_g, m_w_out, m_xa_pre_g, m_xa_post_g, m_mem_norm_g, m_xa_w_q, m_xa_w_kv, m_xa_w_o, m_ffn2_pre_g, m_ffn2_post_g, m_ffn2_w_gate, m_ffn2_w_up, m_ffn2_w_down, m_final_norm_g, v_ffn1_pre_g, v_ffn1_post_g, v_ffn1_w_gate, v_ffn1_w_up, v_ffn1_w_down, v_mix_pre_g, v_mix_post_g, v_w_in, v_sgu_norm_g, v_sgu_norm_b, v_sgu_w_s, v_sgu_b_s, v_sgu_out_g, v_sb_out_g, v_w_out, v_xa_pre_g, v_xa_post_g, v_mem_norm_g, v_xa_w_q, v_xa_w_kv, v_xa_w_o, v_ffn2_pre_g, v_ffn2_post_g, v_ffn2_w_gate, v_ffn2_w_up, v_ffn2_w_down, v_final_norm_g):
    env = dict(locals())
    w = {nm: env[nm] for nm in _WEIGHTS}
    mom = {nm: env["m_" + nm] for nm in _WEIGHTS}
    vel = {nm: env["v_" + nm] for nm in _WEIGHTS}

    loc = {
        "A1": jnp.stack([w["ffn1_w_gate"][0], w["ffn1_w_up"][0]]).astype(BF),
        "A2": jnp.stack([w["ffn2_w_gate"][0], w["ffn2_w_up"][0]]).astype(BF),
        "B1": w["ffn1_w_down"][0].astype(BF),
        "B2": w["ffn2_w_down"][0].astype(BF),
        "C": jnp.stack([w["w_out"][0], w["xa_w_q"][0], w["xa_w_o"][0]]).astype(BF),
        "D": w["w_in"][0].astype(BF),
        "E": w["xa_w_kv"][0].astype(BF),
    }
    ga1 = gather_weights(loc, ("A1",))["A1"]
    c_idx = lax.axis_index("c").astype(jnp.int32).reshape(1)
    kc_idx = jnp.stack([2 * lax.axis_index("x") + lax.axis_index("y"), lax.axis_index("c")]).astype(jnp.int32)
    early_rest = tuple(nm for nm in SET_EARLY if nm != "A1")

    def ffn1_up(a1):
        (g1, u1, hid1), bufs = ffn_up(a1, ga1, 0, 1, comm=gather_comm(loc, early_rest))
        return g1, u1, hid1, gather_forward(bufs, early_rest)

    def fwd_sb(proj):
        (ob, tot), bufs = sb_fwd(proj, comm=gather_comm(loc, SET_LATE))
        return ob, tot, gather_forward(bufs, SET_LATE)

    def reduce_to_pairs(grads, names):
        from_sib = rs_to_sibling(grads, names)
        return {nm: add_halves(nm, grads[nm], from_sib[nm], c_idx) for nm in names}

    def bwd_sb(proj, tot, d_ob, late_grads):
        pairs = reduce_to_pairs(late_grads, SET_LATE)
        (dq, dk, dv), got = sb_bwd(proj, tot, d_ob, comm=rs_comm(pairs, SET_LATE))
        return dq, dk, dv, (pairs, dict(zip(SET_LATE, got)))

    def ffn1_da(mm_pairs, early_grads):
        pairs = reduce_to_pairs(early_grads, SET_EARLY)
        da1, got = mm_nt_cb(mm_pairs, D_MODEL, F32, comm=rs_comm(pairs, SET_EARLY))
        return da1, (pairs, dict(zip(SET_EARLY, got)))

    small = {nm: w[nm][0] for nm in _SMALL}
    for nm in ("ffn1_pre_g", "ffn1_post_g", "mix_pre_g", "mix_post_g", "sgu_out_g", "sb_out_g", "xa_pre_g", "xa_post_g",
               "mem_norm_g", "ffn2_pre_g", "ffn2_post_g", "final_norm_g"):
        small[nm] = w[nm]
    loss_tile, grad_x, small_g, (late_pairs, late_got), (early_pairs, early_got) = local_step(
        x[0], mem[0], loss_target[0], ga1, small, ffn1_up, fwd_sb, bwd_sb, ffn1_da)
    pair_sum = {**late_pairs, **early_pairs}
    from_chips = {**late_got, **early_got}

    small_shapes = {nm: w[nm].shape for nm in _SMALL}
    flat = _pack_small(small_g)
    n_small = flat.shape[0]
    pad = jnp.zeros((-n_small % 8, LANES), F32)
    packed = allreduce_small(jnp.concatenate([flat, pad, loss_tile], axis=0))
    loss = packed[-8, 0]
    g_small = _unpack_small(packed[:n_small], small_shapes)

    shard = rs_replicate({nm: add_chips(nm, pair_sum[nm], from_chips[nm], kc_idx, loc[nm].shape) for nm in _GROUPS},
                         _GROUPS)

    grads, delta, new_m, new_v = {}, {}, {}, {}
    for nm in _BIG:
        grp, idx = _SLOT[nm]
        shape = w[nm].shape
        res = adamw(w[nm][0], shard[grp], mom[nm][0], vel[nm][0], g_index=idx)
        grads[nm], delta[nm], new_m[nm], new_v[nm] = (t.reshape(shape) for t in res)
    _, d, nm_, nv_ = adamw(_pack_small(w), _pack_small(g_small), _pack_small(mom), _pack_small(vel))
    d, nm_, nv_ = (_unpack_small(t, small_shapes) for t in (d, nm_, nv_))
    for nm in _SMALL:
        grads[nm], delta[nm], new_m[nm], new_v[nm] = g_small[nm], d[nm], nm_[nm], nv_[nm]

    return (loss, grad_x[None], *[grads[nm] for nm in _WEIGHTS], *[delta[nm] for nm in _WEIGHTS],
            *[new_m[nm] for nm in _WEIGHTS], *[new_v[nm] for nm in _WEIGHTS])
```

```python
import functools
import math

import jax
import jax.numpy as jnp
from jax import lax
from jax.experimental import pallas as pl
from jax.experimental.pallas import tpu as pltpu

F32 = jnp.float32
BF = jnp.bfloat16
EPS = 1e-6
D_MODEL = 1024
N_CHIPS = 4
FF_BLOCK = 704
IN_BLOCK = 640
KV_BLOCK = 512
ROW_BLOCK = 256
SGU_GROUPS = 4
CHUNK = 128
SB_HEAD_DIM = 64
SB_SCALE = SB_HEAD_DIM ** -0.5
XA_HEADS = 4
XA_HEAD_DIM = 256
XA_SCALE = XA_HEAD_DIM ** -0.5
LANES = 128
VMEM_LIMIT = 56 * 1024 * 1024
MESH = pl.DeviceIdType.MESH

ADAM_LR = 0.001
ADAM_B1 = 0.9
ADAM_B2 = 0.999
ADAM_EPS = 1e-08
ADAM_WD = 0.01
ADAM_STEP = 10

_GELU_C = math.sqrt(2.0 / math.pi)
_GELU_A = 0.044715


def _cp(*sem):
    return pltpu.CompilerParams(dimension_semantics=sem, vmem_limit_bytes=VMEM_LIMIT)


def _call(body, **kw):
    return pl.pallas_call(body, **kw)


def _call_with_comm(core_body, comm, args, *, name, grid, out_shape, in_specs, out_specs, scratch_shapes, compiler_params):
    if comm is None:
        res = _call(core_body, name=name, grid=grid, out_shape=tuple(out_shape), in_specs=list(in_specs),
                    out_specs=tuple(out_specs), scratch_shapes=list(scratch_shapes), compiler_params=compiler_params)(*args)
        return res, ()
    n_in, n_out, n_scr = len(in_specs), len(out_shape), len(scratch_shapes)
    ni, no = len(comm.ins), len(comm.outs)

    def body(*refs):
        core_in, cin = refs[:n_in], refs[n_in:n_in + ni]
        core_out = refs[n_in + ni:n_in + ni + n_out]
        cout = refs[n_in + ni + n_out:n_in + ni + n_out + no]
        scr = refs[n_in + ni + n_out + no:]
        core_scr, send, recv = scr[:n_scr], scr[n_scr], scr[n_scr + 1]
        ids = [pl.program_id(a) for a in range(len(grid))]
        first = functools.reduce(jnp.logical_and, [i == 0 for i in ids])
        last = functools.reduce(jnp.logical_and, [i == g - 1 for i, g in zip(ids, grid)])

        @pl.when(first)
        def _():
            comm.start(cin, cout, send, recv)

        core_body(*core_in, *core_out, *core_scr)

        @pl.when(last)
        def _():
            comm.finish(cin, cout, send, recv)

    hbm = pl.BlockSpec(memory_space=pl.ANY)
    res = _call(
        body, name=name, grid=grid, out_shape=tuple(out_shape) + tuple(comm.outs),
        in_specs=list(in_specs) + [hbm] * ni, out_specs=tuple(out_specs) + (hbm,) * no,
        scratch_shapes=list(scratch_shapes) + [pltpu.SemaphoreType.DMA((comm.n_sems,))] * 2,
        input_output_aliases={n_in + i: n_out + o for i, o in comm.aliases.items()},
        compiler_params=compiler_params,
    )(*args, *comm.ins)
    return res[:n_out], res[n_out:]


def _dot(a, b):
    return jnp.dot(a, b, preferred_element_type=F32)


def _dot_nt(a, b):
    return lax.dot_general(a, b, (((1,), (1,)), ((), ())), preferred_element_type=F32)


def _dot_tn(a, b):
    return lax.dot_general(a, b, (((0,), (0,)), ((), ())), preferred_element_type=F32)


def _rstd(x):
    return lax.rsqrt(jnp.mean(x * x, axis=-1, keepdims=True) + EPS)


def _rms_bwd(x, g, dy):
    r = _rstd(x)
    xh = x * r
    gd = dy * g
    dx = r * (gd - xh * jnp.mean(gd * xh, axis=-1, keepdims=True))
    dg = jnp.sum(dy * xh, axis=0, keepdims=True)
    return dx, dg


def _gelu(x):
    return 0.5 * x * (1.0 + jnp.tanh(_GELU_C * (x + _GELU_A * (x * x * x))))


def _gelu_grad(x):
    t = jnp.tanh(_GELU_C * (x + _GELU_A * (x * x * x)))
    return 0.5 * (1.0 + t) + 0.5 * x * (1.0 - t * t) * (_GELU_C * (1.0 + 3.0 * _GELU_A * x * x))


def _dot2(x, ones_mat):
    hi = x.astype(BF)
    lo = (x - hi.astype(F32)).astype(BF)
    return _dot(hi, ones_mat) + _dot(lo, ones_mat)


GRAD_ROWS = 2048


def _row_tile(m, want=512):
    return min(want, m)


def _row_parts(tm, nparts=4):
    step = tm // nparts
    return [slice(p * step, (p + 1) * step) for p in range(nparts)]


def rms_fwd(x, g):
    m, d = x.shape
    tm = _row_tile(m)

    def body(x_ref, g_ref, o_ref):
        xv = x_ref[...]
        o_ref[...] = (xv * _rstd(xv) * g_ref[...]).astype(BF)

    return _call(
        body, name="rms_fwd", out_shape=jax.ShapeDtypeStruct((m, d), BF), grid=(m // tm,),
        in_specs=[pl.BlockSpec((tm, d), lambda i: (i, 0)), pl.BlockSpec((1, d), lambda i: (0, 0))],
        out_specs=pl.BlockSpec((tm, d), lambda i: (i, 0)), compiler_params=_cp("parallel"),
    )(x, g)


def ffn_up(a, ga, ig, iu, comm=None):
    m, d = a.shape
    tm = _row_tile(m)
    nb = ga.shape[-1]

    def body(a_ref, wg_ref, wu_ref, g_ref, u_ref, h_ref):
        av = a_ref[...]
        g = _dot(av, wg_ref[...])
        u = _dot(av, wu_ref[...])
        g_ref[...] = g.astype(BF)
        u_ref[...] = u.astype(BF)
        h_ref[...] = (g * jax.nn.sigmoid(g) * u).astype(BF)

    blk = jax.ShapeDtypeStruct((N_CHIPS, m, nb), BF)
    ospec = pl.BlockSpec((None, tm, nb), lambda k, i: (k, i, 0))
    return _call_with_comm(
        body, comm, (a, ga, ga), name="ffn_up", out_shape=(blk, blk, blk), grid=(N_CHIPS, m // tm),
        in_specs=[pl.BlockSpec((tm, d), lambda k, i: (i, 0)),
                  pl.BlockSpec((None, None, d, nb), lambda k, i: (k, ig, 0, 0)),
                  pl.BlockSpec((None, None, d, nb), lambda k, i: (k, iu, 0, 0))],
        out_specs=(ospec, ospec, ospec), scratch_shapes=[],
        compiler_params=_cp("arbitrary", "arbitrary") if comm is not None else _cp("parallel", "parallel"),
    )


def mm_res(lhs, w, which, h, gp, alpha, gn):
    blocked = lhs.ndim == 3
    m = lhs.shape[1] if blocked else lhs.shape[0]
    kb, n = w.shape[2], w.shape[3]
    tm = _row_tile(m)

    def body(l_ref, w_ref, h_ref, gp_ref, gn_ref, f_ref, hn_ref, an_ref):
        parts = _row_parts(tm)
        accs = []
        for rs in parts:
            acc = None
            for k in range(N_CHIPS):
                lk = l_ref[k, rs, :] if blocked else l_ref[rs, k * kb:(k + 1) * kb]
                t = _dot(lk, w_ref[k])
                acc = t if acc is None else acc + t
            accs.append(acc)
        for rs, acc in zip(parts, accs):
            f_ref[rs, :] = acc
            hn = h_ref[rs, :] + alpha * (acc * _rstd(acc) * gp_ref[...])
            hn_ref[rs, :] = hn
            an_ref[rs, :] = (hn * _rstd(hn) * gn_ref[...]).astype(BF)

    lspec = (pl.BlockSpec((N_CHIPS, tm, kb), lambda i: (0, i, 0)) if blocked
             else pl.BlockSpec((tm, N_CHIPS * kb), lambda i: (i, 0)))
    row = pl.BlockSpec((tm, n), lambda i: (i, 0))
    vec = pl.BlockSpec((1, n), lambda i: (0, 0))
    return _call(
        body, name="mm_res", grid=(m // tm,),
        out_shape=(jax.ShapeDtypeStruct((m, n), F32), jax.ShapeDtypeStruct((m, n), F32),
                   jax.ShapeDtypeStruct((m, n), BF)),
        in_specs=[lspec, pl.BlockSpec((N_CHIPS, None, kb, n), lambda i: (0, which, 0, 0)), row, vec, vec],
        out_specs=(row, row, row), compiler_params=_cp("parallel"),
    )(lhs, w, h, gp, gn)


def mm_cb(a, w, out_dtype=BF):
    m, kd = a.shape
    nb = w.shape[-1]
    tm = _row_tile(m)

    def body(a_ref, w_ref, o_ref):
        o_ref[...] = _dot(a_ref[...], w_ref[...]).astype(out_dtype)

    return _call(
        body, name="mm_cb", out_shape=jax.ShapeDtypeStruct((m, N_CHIPS * nb), out_dtype),
        grid=(N_CHIPS, m // tm),
        in_specs=[pl.BlockSpec((tm, kd), lambda k, i: (i, 0)), pl.BlockSpec((None, kd, nb), lambda k, i: (k, 0, 0))],
        out_specs=pl.BlockSpec((tm, nb), lambda k, i: (i, k)), compiler_params=_cp("parallel", "parallel"),
    )(a, w)


def _sgu_core(u_pre, vg_pre, wm_ref, ng_ref, nb_ref, bst_ref, g, c):
    rs = slice(c * CHUNK, (c + 1) * CHUNK)
    cs = slice(g * CHUNK, (g + 1) * CHUNK)
    ug = _gelu(u_pre[rs, cs].astype(F32))
    vgl = _gelu(vg_pre[rs, cs].astype(F32))
    mu = jnp.mean(vgl, axis=-1, keepdims=True)
    cen = vgl - mu
    rstd = lax.rsqrt(jnp.mean(cen * cen, axis=-1, keepdims=True) + EPS)
    xh = cen * rstd
    vn = xh * ng_ref[g:g + 1, :] + nb_ref[g:g + 1, :]
    mixed = _dot(wm_ref[g], vn.astype(BF)) + bst_ref[g]
    return ug, xh, rstd, vn, mixed


def sgu_fwd(proj, wm, ng, nb, bst):
    m = proj.shape[0]
    tm = _row_tile(m)
    wd = SGU_GROUPS * CHUNK

    def body(u_ref, v_ref, wm_ref, ng_ref, nb_ref, bst_ref, o_ref):
        for c in range(tm // CHUNK):
            for g in range(SGU_GROUPS):
                ug, _, _, _, mixed = _sgu_core(u_ref, v_ref, wm_ref, ng_ref, nb_ref, bst_ref, g, c)
                o_ref[c * CHUNK:(c + 1) * CHUNK, g * CHUNK:(g + 1) * CHUNK] = (ug * mixed).astype(BF)

    full = lambda shape: pl.BlockSpec(shape, lambda i: (0,) * len(shape))
    return _call(
        body, name="sgu_fwd", out_shape=jax.ShapeDtypeStruct((m, wd), BF), grid=(m // tm,),
        in_specs=[pl.BlockSpec((tm, wd), lambda i: (i, 0)), pl.BlockSpec((tm, wd), lambda i: (i, 1)),
                  full(wm.shape), full(ng.shape), full(nb.shape), full(bst.shape)],
        out_specs=pl.BlockSpec((tm, wd), lambda i: (i, 0)), compiler_params=_cp("parallel"),
    )(proj, proj, wm, ng, nb, bst)


def _sb_tiles(m):
    tq = min(512, m)
    tk = min(256, m)
    return tq, tk


def _log_sigmoid_pair(z, mask):
    ls = jnp.minimum(z, 0.0) - jnp.log(1.0 + jnp.exp(-jnp.abs(z)))
    l1 = ls - z
    return ls, (l1 if mask is None else jnp.where(mask, l1, 0.0))


def _sb_diag_mask(r0, r1, d, tk):
    rows = r0 + lax.broadcasted_iota(jnp.int32, (r1 - r0, tk), 0)
    cols = d * tk + lax.broadcasted_iota(jnp.int32, (r1 - r0, tk), 1)
    return cols < rows


SB_ROW_PARTS = 2


def _sb_chains(r0, r1):
    part = (r1 - r0) // SB_ROW_PARTS
    return [(hd, r0 + p * part, r0 + (p + 1) * part) for p in range(SB_ROW_PARTS) for hd in range(2)]


def _head_masks():
    lane = lax.broadcasted_iota(jnp.int32, (1, LANES), 1)
    return [lane < SB_HEAD_DIM, lane >= SB_HEAD_DIM]


def sb_fwd(proj, comm=None):
    m = proj.shape[0]
    tq, tk = _sb_tiles(m)
    ndiag = tq // tk

    def body(q_ref, k_ref, v_ref, o_ref, tot_ref, qs, acc, car):
        i = pl.program_id(1)
        nfull = i * ndiag
        heads = _head_masks()
        upper = (lax.broadcasted_iota(jnp.int32, (tk, tk), 0) > lax.broadcasted_iota(jnp.int32, (tk, tk), 1)).astype(BF)
        qv = q_ref[...]
        for hd in range(2):
            qs[hd] = jnp.where(heads[hd], qv, jnp.zeros_like(qv)) * SB_SCALE
        acc[...] = jnp.zeros_like(acc)
        car[...] = jnp.zeros_like(car)

        def block(ks, r0, diag):
            kb = k_ref[pl.ds(ks, tk), :]
            vb = v_ref[pl.ds(ks, tk), :]
            chains = _sb_chains(r0, tq)
            masks = [None if diag is None else _sb_diag_mask(ra, rb, diag, tk) for _, ra, rb in chains]
            zs = [_dot_nt(qs[hd, ra:rb, :], kb) for hd, ra, rb in chains]
            mid = []
            for z, mask in zip(zs, masks):
                ls, l1 = _log_sigmoid_pair(z, mask)
                mid.append((ls, l1, _dot2(l1, upper)))
            pvs = []
            for (hd, ra, rb), (ls, l1, cum), mask in zip(chains, mid, masks):
                a = jnp.exp(ls + (cum + car[hd, ra:rb, :]))
                if mask is not None:
                    a = jnp.where(mask, a, 0.0)
                pvs.append(_dot(a.astype(BF), vb))
            for (hd, ra, rb), (ls, l1, cum), pv in zip(chains, mid, pvs):
                acc[hd, ra:rb, :] += pv
                car[hd, ra:rb, :] += jnp.sum(l1, axis=-1, keepdims=True)

        for d in reversed(range(ndiag)):
            block(pl.multiple_of((nfull + d) * tk, tk), d * tk, d)

        def step(jj, carry):
            block(pl.multiple_of((nfull - 1 - jj) * tk, tk), 0, None)
            return carry

        lax.fori_loop(0, nfull, step, 0)
        o_ref[...] = jnp.where(heads[0], acc[0], acc[1]).astype(BF)
        tot_ref[...] = jnp.where(heads[0], car[0], car[1])

    qb = 2 * 512 // LANES
    return _call_with_comm(
        body, comm, (proj, proj, proj), name="sb_fwd", grid=(4, m // tq),
        out_shape=(jax.ShapeDtypeStruct((m, 512), BF), jax.ShapeDtypeStruct((m, 512), F32)),
        in_specs=[pl.BlockSpec((tq, LANES), lambda p, i: (i, qb + p)),
                  pl.BlockSpec((m, LANES), lambda p, i: (0, qb + 4 + p)),
                  pl.BlockSpec((m, LANES), lambda p, i: (0, qb + 8 + p))],
        out_specs=(pl.BlockSpec((tq, LANES), lambda p, i: (i, p)), pl.BlockSpec((tq, LANES), lambda p, i: (i, p))),
        scratch_shapes=[pltpu.VMEM((2, tq, LANES), BF), pltpu.VMEM((2, tq, LANES), F32), pltpu.VMEM((2, tq, 1), F32)],
        compiler_params=_cp("arbitrary", "arbitrary"),
    )


def merge_norm(oa, ob, ga, gb):
    m, w = oa.shape
    tm = _row_tile(m)

    def body(a_ref, b_ref, ga_ref, gb_ref, o_ref):
        av = a_ref[...].astype(F32)
        bv = b_ref[...].astype(F32)
        o_ref[:, :w] = (av * _rstd(av) * ga_ref[...]).astype(BF)
        o_ref[:, w:] = (bv * _rstd(bv) * gb_ref[...]).astype(BF)

    row = pl.BlockSpec((tm, w), lambda i: (i, 0))
    vec = pl.BlockSpec((1, w), lambda i: (0, 0))
    return _call(
        body, name="merge_norm", out_shape=jax.ShapeDtypeStruct((m, 2 * w), BF), grid=(m // tm,),
        in_specs=[row, row, vec, vec], out_specs=pl.BlockSpec((tm, 2 * w), lambda i: (i, 0)),
        compiler_params=_cp("parallel"),
    )(oa, ob, ga, gb)


def _xa_probs(qh, kh):
    logits = _dot_nt(qh, kh) * XA_SCALE
    e = jnp.exp(logits - jnp.max(logits, axis=-1, keepdims=True))
    return e / jnp.sum(e, axis=-1, keepdims=True)


def xa_fwd(xq, kv):
    m, d = xq.shape
    mm = kv.shape[0]
    tm = _row_tile(m)

    def body(q_ref, kv_ref, o_ref):
        for hd in range(XA_HEADS):
            cs = slice(hd * XA_HEAD_DIM, (hd + 1) * XA_HEAD_DIM)
            p = _xa_probs(q_ref[:, cs], kv_ref[:, cs])
            vh = kv_ref[:, d + hd * XA_HEAD_DIM:d + (hd + 1) * XA_HEAD_DIM]
            o_ref[:, cs] = _dot(p.astype(BF), vh).astype(BF)

    return _call(
        body, name="xa_fwd", out_shape=jax.ShapeDtypeStruct((m, d), BF), grid=(m // tm,),
        in_specs=[pl.BlockSpec((tm, d), lambda i: (i, 0)), pl.BlockSpec((mm, 2 * d), lambda i: (0, 0))],
        out_specs=pl.BlockSpec((tm, d), lambda i: (i, 0)), compiler_params=_cp("parallel"),
    )(xq, kv)


def norm_bwd(h, gn, d_a=None, d_res=None, target=None, f_prev=None, gp_prev=None, alpha_prev=1.0):
    m, d = h.shape
    tm = _row_tile(m)
    has_loss = target is not None
    has_res = d_res is not None
    has_prev = f_prev is not None

    def body(*refs):
        refs = list(refs)
        h_ref, gn_ref = refs[0], refs[1]
        pos = 2
        da_ref = dres_ref = t_ref = f_ref = gp_ref = None
        if has_loss:
            t_ref = refs[pos]; pos += 1
        else:
            da_ref = refs[pos]; pos += 1
        if has_res:
            dres_ref = refs[pos]; pos += 1
        if has_prev:
            f_ref, gp_ref = refs[pos], refs[pos + 1]; pos += 2
        dh_ref, dgn_ref = refs[pos], refs[pos + 1]; pos += 2
        df_ref = dgp_ref = loss_ref = None
        if has_prev:
            df_ref, dgp_ref = refs[pos], refs[pos + 1]; pos += 2
        if has_loss:
            loss_ref = refs[pos]

        first = pl.program_id(0) == 0
        hv = h_ref[...]
        gn = gn_ref[...]
        if has_loss:
            err = hv * _rstd(hv) * gn - t_ref[...]
            da = err * (1.0 / d)
            part = 0.5 * jnp.sum(jnp.sum(err * err, axis=-1, keepdims=True) * (1.0 / d))

            @pl.when(first)
            def _():
                loss_ref[...] = jnp.zeros_like(loss_ref)

            loss_ref[...] += part
        else:
            da = da_ref[...].astype(F32)
        dx, dgn = _rms_bwd(hv, gn, da)
        dh = dx + dres_ref[...] if has_res else dx
        dh_ref[...] = dh

        @pl.when(first)
        def _():
            dgn_ref[...] = jnp.zeros_like(dgn_ref)

        dgn_ref[...] += dgn
        if has_prev:
            dfv, dgp = _rms_bwd(f_ref[...], gp_ref[...], dh)
            df_ref[...] = (alpha_prev * dfv).astype(BF)

            @pl.when(first)
            def _():
                dgp_ref[...] = jnp.zeros_like(dgp_ref)

            dgp_ref[...] += alpha_prev * dgp

    row = pl.BlockSpec((tm, d), lambda i: (i, 0))
    vec = pl.BlockSpec((1, d), lambda i: (0, 0))
    ins, in_specs = [h, gn], [row, vec]
    ins.append(target if has_loss else d_a); in_specs.append(row)
    if has_res:
        ins.append(d_res); in_specs.append(row)
    if has_prev:
        ins += [f_prev, gp_prev]; in_specs += [row, vec]
    outs = [jax.ShapeDtypeStruct((m, d), F32), jax.ShapeDtypeStruct((1, d), F32)]
    out_specs = [row, vec]
    names = ["d_h", "d_gn"]
    if has_prev:
        outs += [jax.ShapeDtypeStruct((m, d), BF), jax.ShapeDtypeStruct((1, d), F32)]
        out_specs += [row, vec]
        names += ["d_f", "d_gp"]
    if has_loss:
        outs.append(jax.ShapeDtypeStruct((8, LANES), F32))
        out_specs.append(pl.BlockSpec((8, LANES), lambda i: (0, 0)))
        names.append("loss")
    res = _call(
        body, name="norm_bwd", out_shape=tuple(outs), grid=(m // tm,), in_specs=in_specs,
        out_specs=tuple(out_specs), compiler_params=_cp("arbitrary"),
    )(*ins)
    return dict(zip(names, res))


def ffn_bwd_act(df, gb, which, g, u):
    m, d = df.shape
    nb = g.shape[-1]
    tm = _row_tile(m, 1024)

    def body(df_ref, w_ref, g_ref, u_ref, dg_ref, du_ref):
        parts = _row_parts(tm, 2)
        dhs = [_dot_nt(df_ref[rs, :], w_ref[...]) for rs in parts]
        for rs, dh in zip(parts, dhs):
            gv = g_ref[rs, :].astype(F32)
            uv = u_ref[rs, :].astype(F32)
            s = jax.nn.sigmoid(gv)
            dg_ref[rs, :] = (dh * uv * (s * (1.0 + gv * (1.0 - s)))).astype(BF)
            du_ref[rs, :] = (dh * gv * s).astype(BF)

    blk = jax.ShapeDtypeStruct((N_CHIPS, m, nb), BF)
    aspec = pl.BlockSpec((None, tm, nb), lambda k, i: (k, i, 0))
    return _call(
        body, name="ffn_bwd_act", out_shape=(blk, blk), grid=(N_CHIPS, m // tm),
        in_specs=[pl.BlockSpec((tm, d), lambda k, i: (i, 0)),
                  pl.BlockSpec((None, None, nb, d), lambda k, i: (k, which, 0, 0)), aspec, aspec],
        out_specs=(aspec, aspec), compiler_params=_cp("parallel", "parallel"),
    )(df, gb, g, u)


def mm_tn(a, b, dest, a_spec, b_spec, o_spec, acc_shape, msteps):
    def body(a_ref, b_ref, dest_ref, o_ref, acc):
        del dest_ref
        ms = pl.program_id(1)

        @pl.when(ms == 0)
        def _():
            acc[...] = jnp.zeros_like(acc)

        acc[...] += _dot_tn(a_ref[...], b_ref[...])

        @pl.when(ms == msteps - 1)
        def _():
            o_ref[...] = acc[...].astype(o_ref.dtype)

    return _call(
        body, name="mm_tn", out_shape=jax.ShapeDtypeStruct(dest.shape, dest.dtype), grid=(N_CHIPS, msteps),
        in_specs=[a_spec, b_spec, pl.BlockSpec(memory_space=pl.ANY)], out_specs=o_spec,
        scratch_shapes=[pltpu.VMEM(acc_shape, F32)], input_output_aliases={2: 0},
        compiler_params=_cp("parallel", "arbitrary"),
    )(a, b, dest)


def _act_spec(arr, tm, nb):
    if arr.ndim == 3:
        return pl.BlockSpec((None, tm, nb), lambda k, ms: (k, ms, 0))
    return pl.BlockSpec((tm, nb), lambda k, ms: (ms, k))


def grad_cb(a, dout, dest, which=None):
    m, kd = a.shape
    nb = dest.shape[-1]
    tm = _row_tile(m, GRAD_ROWS)
    if which is None:
        o_spec = pl.BlockSpec((None, kd, nb), lambda k, ms: (k, 0, 0))
    else:
        o_spec = pl.BlockSpec((None, None, kd, nb), lambda k, ms: (k, which, 0, 0))
    return mm_tn(a, dout, dest, pl.BlockSpec((tm, kd), lambda k, ms: (ms, 0)), _act_spec(dout, tm, nb), o_spec,
                 (kd, nb), m // tm)


def grad_rb(a, dout, dest, which):
    m, n = dout.shape
    kb = dest.shape[-2]
    tm = _row_tile(m, GRAD_ROWS)
    o_spec = pl.BlockSpec((None, None, kb, n), lambda k, ms: (k, which, 0, 0))
    return mm_tn(a, dout, dest, _act_spec(a, tm, kb), pl.BlockSpec((tm, n), lambda k, ms: (ms, 0)), o_spec,
                 (kb, n), m // tm)


def mm_nt_cb(pairs, n, out_dtype, comm=None):
    d0 = pairs[0][0]
    m = d0.shape[1] if d0.ndim == 3 else d0.shape[0]
    tm = _row_tile(m, 1024)
    npair = len(pairs)

    def body(*refs):
        o_ref, acc = refs[2 * npair], refs[2 * npair + 1]
        k = pl.program_id(1)

        @pl.when(k == 0)
        def _():
            acc[...] = jnp.zeros_like(acc)

        for p in range(npair):
            acc[...] += _dot_nt(refs[2 * p][...], refs[2 * p + 1][...])

        @pl.when(k == N_CHIPS - 1)
        def _():
            o_ref[...] = acc[...].astype(out_dtype)

    ins, in_specs = [], []
    for dout, w, which in pairs:
        nb = w.shape[-1]
        if dout.ndim == 3:
            in_specs.append(pl.BlockSpec((None, tm, nb), lambda i, k: (k, i, 0)))
        else:
            in_specs.append(pl.BlockSpec((tm, nb), lambda i, k: (i, k)))
        if w.ndim == 4:
            in_specs.append(pl.BlockSpec((None, None, n, nb), lambda i, k, which=which: (k, which, 0, 0)))
        else:
            in_specs.append(pl.BlockSpec((None, n, nb), lambda i, k: (k, 0, 0)))
        ins += [dout, w]
    (out,), extra = _call_with_comm(
        body, comm, tuple(ins), name="mm_nt_cb", out_shape=(jax.ShapeDtypeStruct((m, n), out_dtype),),
        grid=(m // tm, N_CHIPS), in_specs=in_specs, out_specs=(pl.BlockSpec((tm, n), lambda i, k: (i, 0)),),
        scratch_shapes=[pltpu.VMEM((tm, n), F32)],
        compiler_params=_cp("arbitrary", "arbitrary") if comm is not None else _cp("parallel", "arbitrary"),
    )
    return out if comm is None else (out, extra)


def mm_nt_rb(dout, w, which, out_dtype):
    m, n = dout.shape
    kb = w.shape[2]
    tm = _row_tile(m)

    def body(d_ref, w_ref, o_ref):
        o_ref[...] = _dot_nt(d_ref[...], w_ref[...].reshape(N_CHIPS * kb, n)).astype(out_dtype)

    return _call(
        body, name="mm_nt_rb", out_shape=jax.ShapeDtypeStruct((m, N_CHIPS * kb), out_dtype), grid=(m // tm,),
        in_specs=[pl.BlockSpec((tm, n), lambda i: (i, 0)),
                  pl.BlockSpec((N_CHIPS, None, kb, n), lambda i: (0, which, 0, 0))],
        out_specs=pl.BlockSpec((tm, N_CHIPS * kb), lambda i: (i, 0)), compiler_params=_cp("parallel"),
    )(dout, w)


def merge_norm_bwd(oa, ob, dmerged, ga, gb):
    m, w = oa.shape
    tm = _row_tile(m)

    def body(a_ref, b_ref, dm_ref, ga_ref, gb_ref, da_ref, db_ref, dga_ref, dgb_ref):
        @pl.when(pl.program_id(0) == 0)
        def _():
            dga_ref[...] = jnp.zeros_like(dga_ref)
            dgb_ref[...] = jnp.zeros_like(dgb_ref)

        da, dga = _rms_bwd(a_ref[...].astype(F32), ga_ref[...], dm_ref[:, :w].astype(F32))
        db, dgb = _rms_bwd(b_ref[...].astype(F32), gb_ref[...], dm_ref[:, w:].astype(F32))
        da_ref[...] = da
        db_ref[...] = db
        dga_ref[...] += dga
        dgb_ref[...] += dgb

    row = pl.BlockSpec((tm, w), lambda i: (i, 0))
    vec = pl.BlockSpec((1, w), lambda i: (0, 0))
    return _call(
        body, name="merge_norm_bwd", grid=(m // tm,),
        out_shape=(jax.ShapeDtypeStruct((m, w), F32), jax.ShapeDtypeStruct((m, w), F32),
                   jax.ShapeDtypeStruct((1, w), F32), jax.ShapeDtypeStruct((1, w), F32)),
        in_specs=[row, row, pl.BlockSpec((tm, 2 * w), lambda i: (i, 0)), vec, vec],
        out_specs=(row, row, vec, vec), compiler_params=_cp("arbitrary"),
    )(oa, ob, dmerged, ga, gb)


def sgu_bwd(proj, d_oa, wm, wmt, ng, nb, bst):
    m = proj.shape[0]
    tm = _row_tile(m)
    wd = SGU_GROUPS * CHUNK

    def body(u_ref, v_ref, do_ref, wm_ref, wmt_ref, ng_ref, nb_ref, bst_ref,
             dp_ref, dw_ref, dbt_ref, dng_ref, dnb_ref):
        @pl.when(pl.program_id(0) == 0)
        def _():
            dw_ref[...] = jnp.zeros_like(dw_ref)
            dbt_ref[...] = jnp.zeros_like(dbt_ref)
            dng_ref[...] = jnp.zeros_like(dng_ref)
            dnb_ref[...] = jnp.zeros_like(dnb_ref)

        causal = lax.broadcasted_iota(jnp.int32, (CHUNK, CHUNK), 0) >= lax.broadcasted_iota(jnp.int32, (CHUNK, CHUNK), 1)
        for c in range(tm // CHUNK):
            rs = slice(c * CHUNK, (c + 1) * CHUNK)
            for g in range(SGU_GROUPS):
                cs = slice(g * CHUNK, (g + 1) * CHUNK)
                ug, xh, rstd, vn, mixed = _sgu_core(u_ref, v_ref, wm_ref, ng_ref, nb_ref, bst_ref, g, c)
                do = do_ref[rs, cs]
                dug = do * mixed
                dmix = do * ug
                dmb = dmix.astype(BF)
                dbt_ref[g] += jnp.sum(dmix, axis=-1, keepdims=True)
                dw_ref[g] += jnp.where(causal, _dot_nt(dmb, vn.astype(BF)), 0.0)
                dvn = _dot(wmt_ref[g], dmb)
                dng_ref[g:g + 1, :] += jnp.sum(dvn * xh, axis=0, keepdims=True)
                dnb_ref[g:g + 1, :] += jnp.sum(dvn, axis=0, keepdims=True)
                dxh = dvn * ng_ref[g:g + 1, :]
                dvg = rstd * (dxh - jnp.mean(dxh, axis=-1, keepdims=True)
                              - xh * jnp.mean(dxh * xh, axis=-1, keepdims=True))
                dp_ref[rs, cs] = (dug * _gelu_grad(u_ref[rs, cs].astype(F32))).astype(BF)
                dp_ref[rs, wd + g * CHUNK:wd + (g + 1) * CHUNK] = (dvg * _gelu_grad(v_ref[rs, cs].astype(F32))).astype(BF)

    full = lambda shape: pl.BlockSpec(shape, lambda i: (0,) * len(shape))
    return _call(
        body, name="sgu_bwd", grid=(m // tm,),
        out_shape=(jax.ShapeDtypeStruct((m, 2 * wd), BF), jax.ShapeDtypeStruct(wm.shape, F32),
                   jax.ShapeDtypeStruct(bst.shape, F32), jax.ShapeDtypeStruct(ng.shape, F32),
                   jax.ShapeDtypeStruct(nb.shape, F32)),
        in_specs=[pl.BlockSpec((tm, wd), lambda i: (i, 0)), pl.BlockSpec((tm, wd), lambda i: (i, 1)),
                  pl.BlockSpec((tm, wd), lambda i: (i, 0)),
                  full(wm.shape), full(wmt.shape), full(ng.shape), full(nb.shape), full(bst.shape)],
        out_specs=(pl.BlockSpec((tm, 2 * wd), lambda i: (i, 0)), full(wm.shape), full(bst.shape), full(ng.shape),
                   full(nb.shape)),
        compiler_params=_cp("arbitrary"),
    )(proj, proj, d_oa, wm, wmt, ng, nb, bst)


def sb_bwd(proj, tot, d_ob, comm=None):
    m = proj.shape[0]
    tq, tk = _sb_tiles(m)

    ndiag = tq // tk

    def body(q_ref, k_ref, v_ref, tot_ref, do_ref, dq_ref, dk_ref, dv_ref, qs, dos, tots, dqa, cl1, cg):
        i = pl.program_id(1)

        @pl.when(i == 0)
        def _():
            dk_ref[...] = jnp.zeros_like(dk_ref)
            dv_ref[...] = jnp.zeros_like(dv_ref)

        nfull = i * ndiag
        heads = _head_masks()
        r_io = lax.broadcasted_iota(jnp.int32, (tk, tk), 0)
        c_io = lax.broadcasted_iota(jnp.int32, (tk, tk), 1)
        incl = (r_io <= c_io).astype(BF)
        excl = (r_io < c_io).astype(BF)
        qv = q_ref[...]
        dov = do_ref[...].astype(BF)
        totv = tot_ref[...]
        for hd in range(2):
            qs[hd] = jnp.where(heads[hd], qv, jnp.zeros_like(qv)) * SB_SCALE
            dos[hd] = jnp.where(heads[hd], dov, jnp.zeros_like(dov))
            tots[hd] = jnp.max(jnp.where(heads[hd], totv, -jnp.inf), axis=-1, keepdims=True)
        dqa[...] = jnp.zeros_like(dqa)
        cl1[...] = jnp.zeros_like(cl1)
        cg[...] = jnp.zeros_like(cg)

        def block(ks, r0, diag):
            kb = k_ref[pl.ds(ks, tk), :]
            vb = v_ref[pl.ds(ks, tk), :]
            chains = _sb_chains(r0, tq)
            masks = [None if diag is None else _sb_diag_mask(ra, rb, diag, tk) for _, ra, rb in chains]
            qc = [qs[hd, ra:rb, :] for hd, ra, rb in chains]
            doc = [dos[hd, ra:rb, :] for hd, ra, rb in chains]
            zs = [_dot_nt(q, kb) for q in qc]
            das = [_dot_nt(do, vb) for do in doc]
            s1 = []
            for z, mask in zip(zs, masks):
                ls, l1 = _log_sigmoid_pair(z, mask)
                s1.append((ls, l1, _dot2(l1, incl)))
            s2 = []
            for (hd, ra, rb), (ls, l1, pre), da, mask in zip(chains, s1, das, masks):
                a = jnp.exp(ls + (tots[hd, ra:rb, :] - (pre + cl1[hd, ra:rb, :])))
                if mask is not None:
                    a = jnp.where(mask, a, 0.0)
                gmat = a * da
                s2.append((a, gmat, _dot2(gmat, excl)))
            dk_sum = dv_sum = None
            for n, (hd, ra, rb) in enumerate(chains):
                a, gmat, pref = s2[n]
                sg = jnp.exp(s1[n][0])
                dz = gmat * (1.0 - sg) - (pref + cg[hd, ra:rb, :]) * sg
                if masks[n] is not None:
                    dz = jnp.where(masks[n], dz, 0.0)
                dz = dz.astype(BF)
                dqa[hd, ra:rb, :] += _dot(dz, kb)
                dk_t = _dot_tn(dz, qc[n])
                dv_t = _dot_tn(a.astype(BF), doc[n])
                dk_sum = dk_t if dk_sum is None else dk_sum + dk_t
                dv_sum = dv_t if dv_sum is None else dv_sum + dv_t
            for n, (hd, ra, rb) in enumerate(chains):
                cl1[hd, ra:rb, :] += jnp.sum(s1[n][1], axis=-1, keepdims=True)
                cg[hd, ra:rb, :] += jnp.sum(s2[n][1], axis=-1, keepdims=True)
            dk_ref[pl.ds(ks, tk), :] += dk_sum
            dv_ref[pl.ds(ks, tk), :] += dv_sum

        def step(j, carry):
            block(pl.multiple_of(j * tk, tk), 0, None)
            return carry

        lax.fori_loop(0, nfull, step, 0)
        for d in range(ndiag):
            block(pl.multiple_of((nfull + d) * tk, tk), d * tk, d)
        dq_ref[...] = jnp.where(heads[0], dqa[0], dqa[1]) * SB_SCALE

    qb = 2 * 512 // LANES
    tile = pl.BlockSpec((tq, LANES), lambda p, i: (i, p))
    seq = pl.BlockSpec((m, LANES), lambda p, i: (0, p))
    out = jax.ShapeDtypeStruct((m, 512), F32)
    return _call_with_comm(
        body, comm, (proj, proj, proj, tot, d_ob), name="sb_bwd", grid=(4, m // tq), out_shape=(out, out, out),
        in_specs=[pl.BlockSpec((tq, LANES), lambda p, i: (i, qb + p)),
                  pl.BlockSpec((m, LANES), lambda p, i: (0, qb + 4 + p)),
                  pl.BlockSpec((m, LANES), lambda p, i: (0, qb + 8 + p)), tile, tile],
        out_specs=(tile, seq, seq),
        scratch_shapes=[pltpu.VMEM((2, tq, LANES), BF), pltpu.VMEM((2, tq, LANES), BF), pltpu.VMEM((2, tq, 1), F32),
                        pltpu.VMEM((2, tq, LANES), F32), pltpu.VMEM((2, tq, 1), F32), pltpu.VMEM((2, tq, 1), F32)],
        compiler_params=_cp("arbitrary", "arbitrary"),
    )


def xa_bwd(xq, kv, d_o):
    m, d = xq.shape
    mm = kv.shape[0]
    tm = _row_tile(m)

    def body(q_ref, kv_ref, do_ref, dq_ref, dkv_ref):
        @pl.when(pl.program_id(0) == 0)
        def _():
            dkv_ref[...] = jnp.zeros_like(dkv_ref)

        for hd in range(XA_HEADS):
            cs = slice(hd * XA_HEAD_DIM, (hd + 1) * XA_HEAD_DIM)
            vs = slice(d + hd * XA_HEAD_DIM, d + (hd + 1) * XA_HEAD_DIM)
            qh = q_ref[:, cs]
            kh = kv_ref[:, cs]
            doh = do_ref[:, cs]
            p = _xa_probs(qh, kh)
            dp = _dot_nt(doh, kv_ref[:, vs])
            ds = (p * (dp - jnp.sum(p * dp, axis=-1, keepdims=True))).astype(BF)
            dq_ref[:, cs] = (_dot(ds, kh) * XA_SCALE).astype(BF)
            dkv_ref[:, cs] += _dot_tn(ds, qh) * XA_SCALE
            dkv_ref[:, vs] += _dot_tn(p.astype(BF), doh)

    row = pl.BlockSpec((tm, d), lambda i: (i, 0))
    whole = pl.BlockSpec((mm, 2 * d), lambda i: (0, 0))
    return _call(
        body, name="xa_bwd", grid=(m // tm,),
        out_shape=(jax.ShapeDtypeStruct((m, d), BF), jax.ShapeDtypeStruct((mm, 2 * d), F32)),
        in_specs=[row, whole, row], out_specs=(row, whole), compiler_params=_cp("arbitrary"),
    )(xq, kv, d_o)


def adamw(w, g, mom, vel, g_index=None):
    r, c = w.shape
    tr = r
    for cand in (512, 256, 128, 64, 32, 16, 8):
        if r % cand == 0:
            tr = cand
            break

    def body(w_ref, g_ref, m_ref, v_ref, go_ref, d_ref, nm_ref, nv_ref):
        gv = g_ref[...]
        mn = ADAM_B1 * m_ref[...] + (1.0 - ADAM_B1) * gv
        vn = ADAM_B2 * v_ref[...] + (1.0 - ADAM_B2) * (gv * gv)
        m_hat = mn / (1.0 - ADAM_B1 ** ADAM_STEP)
        v_hat = vn / (1.0 - ADAM_B2 ** ADAM_STEP)
        go_ref[...] = gv
        d_ref[...] = -ADAM_LR * (m_hat / (jnp.sqrt(v_hat) + ADAM_EPS) + ADAM_WD * w_ref[...])
        nm_ref[...] = mn
        nv_ref[...] = vn

    spec = pl.BlockSpec((tr, c), lambda i: (i, 0))
    gspec = spec if g_index is None else pl.BlockSpec((None, tr, c), lambda i: (g_index, i, 0))
    out = jax.ShapeDtypeStruct((r, c), F32)
    return _call(
        body, name="adamw", out_shape=(out, out, out, out), grid=(r // tr,), in_specs=[spec, gspec, spec, spec],
        out_specs=(spec, spec, spec, spec), compiler_params=_cp("parallel"),
    )(w, g, mom, vel)


def _place():
    x, y, c = lax.axis_index("x"), lax.axis_index("y"), lax.axis_index("c")
    others = [(1 - x, y), (x, 1 - y), (1 - x, 1 - y)]
    return x, y, c, others


_HALF = {"A1": (1, 512), "A2": (1, 512), "B1": (0, 352), "B2": (0, 352), "C": (1, 128), "D": (0, 512), "E": (0, 512),
         "A1T": (1, 352), "A2T": (1, 352)}
SET_EARLY = ("A1", "B1", "D")
SET_LATE = ("A2", "B2", "C", "E")
GRAD_EARLY = ("A1T", "B1", "D")
GRAD_LATE = ("A2T", "B2", "C", "E")
_HBM = pl.BlockSpec(memory_space=pl.ANY)


def _half_of(ref, name, hc, lead=0):
    axis, size = _HALF[name]
    idx = [slice(None)] * (lead + axis) + [pl.ds(hc * size, size)]
    return ref.at[tuple(idx)]


def _gather_slots(loc, names):
    me = 2 * lax.axis_index("x") + lax.axis_index("y")
    init = []
    for nm in names:
        full = lax.empty((N_CHIPS,) + loc[nm].shape, loc[nm].dtype)
        init.append(lax.dynamic_update_slice(full, loc[nm][None], (me,) + (0,) * loc[nm].ndim))
    return init


def _gather_ici(names, src, out, send, recv, sends=True, arrivals=True):
    x, y, c, others = _place()
    me = 2 * x + y
    out_sends, out_arrivals = [], []
    for a, nm in enumerate(names):
        for j, (ox, oy) in enumerate(others):
            sems = dict(send_sem=send.at[3 * a + j], recv_sem=recv.at[3 * a + j], device_id_type=MESH)
            if sends:
                out_sends.append(pltpu.make_async_remote_copy(
                    src_ref=_half_of(src[a], nm, c), dst_ref=_half_of(out[a].at[me], nm, c), device_id=(ox, oy, c),
                    **sems))
            if arrivals:
                landed = _half_of(out[a].at[2 * ox + oy], nm, c)
                out_arrivals.append(pltpu.make_async_remote_copy(src_ref=landed, dst_ref=landed, device_id=(x, y, c),
                                                                 **sems))
    return out_sends, out_arrivals


def _gather_d2d(names, given, out, send, recv):
    x, y, c, others = _place()
    sends, arrivals = [], []
    for a, nm in enumerate(names):
        for j, (ox, oy) in enumerate(others):
            sems = dict(send_sem=send.at[3 * a + j], recv_sem=recv.at[3 * a + j], device_id_type=MESH)
            sends.append(pltpu.make_async_remote_copy(
                src_ref=_half_of(given[a].at[2 * ox + oy], nm, c), dst_ref=_half_of(out[a].at[2 * ox + oy], nm, c),
                device_id=(x, y, 1 - c), **sems))
            landed = _half_of(out[a].at[2 * ox + oy], nm, 1 - c)
            arrivals.append(pltpu.make_async_remote_copy(src_ref=landed, dst_ref=landed, device_id=(x, y, c), **sems))
    return sends, arrivals


def gather_weights(loc, names):
    n = len(names)

    def body(*refs):
        src, given, out = refs[:n], refs[n:2 * n], refs[2 * n:3 * n]
        send1, recv1, send2, recv2 = refs[3 * n:3 * n + 4]
        first, landed = _gather_ici(names, src, out, send1, recv1)
        del given
        passed, arrivals = _gather_d2d(names, out, out, send2, recv2)
        for f in first:
            f.start()
        for l, p in zip(landed, passed):
            l.wait_recv()
            p.start()
        for a in arrivals:
            a.wait_recv()
        for f in first + passed:
            f.wait_send()

    init = _gather_slots(loc, names)
    res = _call(
        body, name="gather_weights", out_shape=tuple(jax.ShapeDtypeStruct(t.shape, t.dtype) for t in init),
        in_specs=[_HBM] * (2 * n), out_specs=(_HBM,) * n, input_output_aliases={n + a: a for a in range(n)},
        scratch_shapes=[pltpu.SemaphoreType.DMA((3 * n,))] * 4,
    )(*[loc[nm] for nm in names], *init)
    return dict(zip(names, res))


def gather_forward(bufs, names):
    n = len(names)

    def body(*refs):
        given, out = refs[:n], refs[n:2 * n]
        passed, arrivals = _gather_d2d(names, given, out, refs[2 * n], refs[2 * n + 1])
        for p in passed:
            p.start()
        for a in arrivals:
            a.wait_recv()
        for p in passed:
            p.wait_send()

    res = _call(
        body, name="gather_forward", out_shape=tuple(jax.ShapeDtypeStruct(t.shape, t.dtype) for t in bufs),
        in_specs=[_HBM] * n, out_specs=(_HBM,) * n, input_output_aliases={a: a for a in range(n)},
        scratch_shapes=[pltpu.SemaphoreType.DMA((3 * n,))] * 2,
    )(*bufs)
    return dict(zip(names, res))


class FusedComm:
    def __init__(self, ins, outs, aliases, n_sems, start, finish):
        self.ins, self.outs, self.aliases, self.n_sems, self.start, self.finish = ins, outs, aliases, n_sems, start, finish


def gather_comm(loc, names):
    n = len(names)

    def start(ins, outs, send, recv):
        for f in _gather_ici(names, ins[:n], outs, send, recv, arrivals=False)[0]:
            f.start()

    def finish(ins, outs, send, recv):
        first, landed = _gather_ici(names, ins[:n], outs, send, recv)
        for l in landed:
            l.wait_recv()
        for f in first:
            f.wait_send()

    init = _gather_slots(loc, names)
    return FusedComm([loc[nm] for nm in names] + init, [jax.ShapeDtypeStruct(t.shape, t.dtype) for t in init],
                     {n + a: a for a in range(n)}, 3 * n, start, finish)


def _half_shape(name, shape):
    axis, size = _HALF[name]
    s = list(shape)
    s[axis] = size
    return tuple(s)


def rs_to_sibling(grads, names):
    n = len(names)

    def body(*refs):
        src = dict(zip(names, refs[:n]))
        out = dict(zip(names, refs[n:2 * n]))
        send, recv = refs[2 * n], refs[2 * n + 1]
        x, y, c, _ = _place()
        copies = []
        for a, nm in enumerate(names):
            copies.append(pltpu.make_async_remote_copy(
                src_ref=_half_of(src[nm], nm, 1 - c, lead=1), dst_ref=out[nm], send_sem=send.at[a], recv_sem=recv.at[a],
                device_id=(x, y, 1 - c), device_id_type=MESH))
        for cpy in copies:
            cpy.start()
        for cpy in copies:
            cpy.wait()

    hbm = pl.BlockSpec(memory_space=pl.ANY)
    outs = tuple(jax.ShapeDtypeStruct((N_CHIPS,) + _half_shape(nm, grads[nm].shape[1:]), grads[nm].dtype)
                 for nm in names)
    res = _call(
        body, name="rs_to_sibling", out_shape=outs, in_specs=[hbm] * n, out_specs=(hbm,) * n,
        scratch_shapes=[pltpu.SemaphoreType.DMA((n,)), pltpu.SemaphoreType.DMA((n,))],
    )(*[grads[nm] for nm in names])
    return dict(zip(names, res))


def _tile2(shape):
    lead = shape[:-2]
    return lead, shape[-2:]


def add_halves(name, mine, got, c_idx):
    axis, size = _HALF[name]
    hshape = got.shape
    lead, last2 = hshape[:-2], hshape[-2:]
    nlead = len(lead)
    haxis = 1 + axis

    def body(c_ref, m_ref, g_ref, o_ref):
        del c_ref
        o_ref[...] = (m_ref[...].astype(F32) + g_ref[...].astype(F32)).astype(o_ref.dtype)

    blk = (None,) * nlead + last2

    def got_map(*idx):
        return tuple(idx[:nlead]) + (0, 0)

    def mine_map(*idx):
        lead_idx = list(idx[:nlead])
        c = idx[nlead][0]
        if haxis < nlead:
            lead_idx[haxis] = lead_idx[haxis] + c * size
            return tuple(lead_idx) + (0, 0)
        return tuple(lead_idx) + (c, 0)

    grid_spec = pltpu.PrefetchScalarGridSpec(
        num_scalar_prefetch=1, grid=lead,
        in_specs=[pl.BlockSpec(blk, mine_map), pl.BlockSpec(blk, got_map)],
        out_specs=pl.BlockSpec(blk, got_map))
    return _call(
        body, name="add_halves", out_shape=jax.ShapeDtypeStruct(hshape, got.dtype), grid_spec=grid_spec,
        compiler_params=_cp(*(("parallel",) * nlead)),
    )(c_idx, mine, got)


def _rs_ici(n, src, out, send, recv):
    x, y, c, others = _place()
    copies = []
    for a in range(n):
        for j, (ox, oy) in enumerate(others):
            copies.append(pltpu.make_async_remote_copy(
                src_ref=src[a].at[2 * ox + oy], dst_ref=out[a].at[j], send_sem=send.at[3 * a + j],
                recv_sem=recv.at[3 * a + j], device_id=(ox, oy, c), device_id_type=MESH))
    return copies


def _rs_out_shapes(summed, names):
    return [jax.ShapeDtypeStruct((3,) + summed[nm].shape[1:], summed[nm].dtype) for nm in names]


def rs_comm(summed, names):
    n = len(names)

    def start(ins, outs, send, recv):
        for cpy in _rs_ici(n, ins, outs, send, recv):
            cpy.start()

    def finish(ins, outs, send, recv):
        for cpy in _rs_ici(n, ins, outs, send, recv):
            cpy.wait()

    return FusedComm([summed[nm] for nm in names], _rs_out_shapes(summed, names), {}, 3 * n, start, finish)


def add_chips(name, summed, got, kc_idx, full_shape):
    axis, size = _HALF[name]
    hshape = summed.shape[1:]
    lead, last2 = hshape[:-2], hshape[-2:]
    nlead = len(lead)

    def body(kc_ref, s_ref, g0_ref, g1_ref, g2_ref, o_ref):
        del kc_ref
        o_ref[...] = ((s_ref[...].astype(F32) + g0_ref[...].astype(F32)) + g1_ref[...].astype(F32)) + g2_ref[...].astype(F32)

    blk = (None,) * (nlead + 1) + last2
    oblk = (None,) * nlead + last2

    def got_map(slot):
        return lambda *idx: (slot,) + tuple(idx[:nlead]) + (0, 0)

    def out_map(*idx):
        lead_idx = list(idx[:nlead])
        c = idx[-1][1]
        if axis < nlead:
            lead_idx[axis] = lead_idx[axis] + c * size
            return tuple(lead_idx) + (0, 0)
        return tuple(lead_idx) + (c, 0)

    grid_spec = pltpu.PrefetchScalarGridSpec(
        num_scalar_prefetch=1, grid=lead if nlead else (1,),
        in_specs=[pl.BlockSpec(blk, lambda *idx: (idx[-1][0],) + tuple(idx[:nlead]) + (0, 0)),
                  pl.BlockSpec(blk, got_map(0)), pl.BlockSpec(blk, got_map(1)), pl.BlockSpec(blk, got_map(2))],
        out_specs=pl.BlockSpec(oblk, out_map))
    return _call(
        body, name="add_chips", out_shape=jax.ShapeDtypeStruct(full_shape, F32), grid_spec=grid_spec,
        compiler_params=_cp(*(("parallel",) * max(nlead, 1))),
    )(kc_idx, summed, got, got, got)


def rs_replicate(shards, names):
    n = len(names)

    def body(*refs):
        given = dict(zip(names, refs[:n]))
        buf = dict(zip(names, refs[n:2 * n]))
        send, recv = refs[2 * n], refs[2 * n + 1]
        x, y, c, _ = _place()
        copies = []
        for a, nm in enumerate(names):
            copies.append(pltpu.make_async_remote_copy(
                src_ref=_half_of(given[nm], nm, c), dst_ref=_half_of(buf[nm], nm, c), send_sem=send.at[a],
                recv_sem=recv.at[a], device_id=(x, y, 1 - c), device_id_type=MESH))
        for cpy in copies:
            cpy.start()
        for a, nm in enumerate(names):
            other = _half_of(buf[nm], nm, 1 - c)
            pltpu.make_async_remote_copy(src_ref=other, dst_ref=other, send_sem=send.at[a], recv_sem=recv.at[a],
                                         device_id=(x, y, 1 - c), device_id_type=MESH).wait_recv()
        for cpy in copies:
            cpy.wait_send()

    hbm = pl.BlockSpec(memory_space=pl.ANY)
    outs = tuple(jax.ShapeDtypeStruct(shards[nm].shape, F32) for nm in names)
    res = _call(
        body, name="rs_replicate", out_shape=outs, in_specs=[hbm] * n, out_specs=(hbm,) * n,
        input_output_aliases={a: a for a in range(n)},
        scratch_shapes=[pltpu.SemaphoreType.DMA((n,)), pltpu.SemaphoreType.DMA((n,))],
    )(*[shards[nm] for nm in names])
    return dict(zip(names, res))


def allreduce_small(v):
    r = v.shape[0]

    def body(v_ref, o_ref, slots, send, recv):
        x, y, c, _ = _place()
        me = 4 * x + 2 * y + c
        slots[me] = v_ref[...]
        copies = []
        for rel in range(1, 8):
            fx, fy, fc = (rel >> 2) & 1, (rel >> 1) & 1, rel & 1
            peer = (x ^ fx, y ^ fy, c ^ fc)
            copies.append(pltpu.make_async_remote_copy(
                src_ref=v_ref, dst_ref=slots.at[me], send_sem=send.at[rel - 1], recv_sem=recv.at[rel - 1],
                device_id=peer, device_id_type=MESH))
        for cpy in copies:
            cpy.start()
        for rel in range(1, 8):
            fx, fy, fc = (rel >> 2) & 1, (rel >> 1) & 1, rel & 1
            src_id = 4 * (x ^ fx) + 2 * (y ^ fy) + (c ^ fc)
            pltpu.make_async_remote_copy(
                src_ref=v_ref, dst_ref=slots.at[src_id], send_sem=send.at[rel - 1], recv_sem=recv.at[rel - 1],
                device_id=(x, y, c), device_id_type=MESH).wait_recv()
        for cpy in copies:
            cpy.wait_send()
        acc = slots[0]
        for s in range(1, 8):
            acc = acc + slots[s]
        o_ref[...] = acc

    vm = pl.BlockSpec(memory_space=pltpu.VMEM)
    return _call(
        body, name="allreduce_small", out_shape=jax.ShapeDtypeStruct(v.shape, F32), in_specs=[vm], out_specs=vm,
        scratch_shapes=[pltpu.VMEM((8, r, LANES), F32), pltpu.SemaphoreType.DMA((7,)), pltpu.SemaphoreType.DMA((7,))],
    )(v)


def local_step(x, mem, target, ga1, small, ffn1_up, fwd_sb, bwd_sb, ffn1_da):
    causal = jnp.tril(jnp.ones((CHUNK, CHUNK), dtype=bool))
    w_s = jnp.where(causal[None], small["sgu_w_s"], 0.0)
    wm = w_s.astype(BF)
    wmt = jnp.swapaxes(w_s, 1, 2).astype(BF)
    bst = small["sgu_b_s"].reshape(SGU_GROUPS, CHUNK, 1)
    ng, nbias = small["sgu_norm_g"], small["sgu_norm_b"]

    a1 = rms_fwd(x, small["ffn1_pre_g"])
    g1, u1, hid1, rest = ffn1_up(a1)
    gb1, gd = rest["B1"][:, None], rest["D"]
    f1, h1, n1 = mm_res(hid1, gb1, 0, x, small["ffn1_post_g"], 0.5, small["mix_pre_g"])
    proj = mm_cb(n1, gd)
    oa = sgu_fwd(proj, wm, ng, nbias, bst)
    ob, tot, late = fwd_sb(proj)
    ga2, gb2, gc, ge = late["A2"], late["B2"][:, None], late["C"], late["E"]
    merged = merge_norm(oa, ob, small["sgu_out_g"], small["sb_out_g"])
    mo, h2, xn = mm_res(merged, gc, 0, h1, small["mix_post_g"], 1.0, small["xa_pre_g"])
    memn = rms_fwd(mem, small["mem_norm_g"])
    kv = mm_cb(memn, ge)
    xq = mm_res_plain(xn, gc, 1)
    o = xa_fwd(xq, kv)
    cc, h3, a2 = mm_res(o, gc, 2, h2, small["xa_post_g"], 1.0, small["ffn2_pre_g"])
    (g2, u2, hid2), _ = ffn_up(a2, ga2, 0, 1)
    f2, h4, _ = mm_res(hid2, gb2, 0, h3, small["ffn2_post_g"], 0.5, small["final_norm_g"])

    gate_up_t = (N_CHIPS, 2, ga1.shape[3], ga1.shape[2])
    dga1, dga2 = lax.empty(gate_up_t, BF), lax.empty(gate_up_t, BF)
    dgb1, dgb2 = lax.empty(gb1.shape, BF), lax.empty(gb2.shape, BF)
    dgc = lax.empty(gc.shape, BF)
    dgd = lax.empty(gd.shape, BF)
    dge = lax.empty(ge.shape, BF)
    sg = {}

    r = norm_bwd(h4, small["final_norm_g"], target=target, f_prev=f2, gp_prev=small["ffn2_post_g"], alpha_prev=0.5)
    loss_tile, dh4, df2 = r["loss"], r["d_h"], r["d_f"]
    sg["final_norm_g"], sg["ffn2_post_g"] = r["d_gn"], r["d_gp"]

    dg2, du2 = ffn_bwd_act(df2, gb2, 0, g2, u2)
    dgb2 = grad_rb(hid2, df2, dgb2, 0)
    dga2 = grad_rb(dg2, a2, dga2, 0)
    dga2 = grad_rb(du2, a2, dga2, 1)
    da2 = mm_nt_cb([(dg2, ga2, 0), (du2, ga2, 1)], D_MODEL, F32)
    r = norm_bwd(h3, small["ffn2_pre_g"], d_a=da2, d_res=dh4, f_prev=cc, gp_prev=small["xa_post_g"], alpha_prev=1.0)
    dh3, dc = r["d_h"], r["d_f"]
    sg["ffn2_pre_g"], sg["xa_post_g"] = r["d_gn"], r["d_gp"]

    d_o = mm_nt_rb(dc, gc, 2, BF)
    dgc = grad_rb(o, dc, dgc, 2)
    dxq, dkv = xa_bwd(xq, kv, d_o)
    dkvb = dkv.astype(BF)
    dge = grad_cb(memn, dkvb, dge)
    dmemn = mm_nt_cb([(dkvb, ge, None)], D_MODEL, F32)
    sg["mem_norm_g"] = norm_bwd(mem, small["mem_norm_g"], d_a=dmemn)["d_gn"]
    dgc = grad_rb(xn, dxq, dgc, 1)
    dxn = mm_nt_rb(dxq, gc, 1, F32)
    r = norm_bwd(h2, small["xa_pre_g"], d_a=dxn, d_res=dh3, f_prev=mo, gp_prev=small["mix_post_g"], alpha_prev=1.0)
    dh2, dmo = r["d_h"], r["d_f"]
    sg["xa_pre_g"], sg["mix_post_g"] = r["d_gn"], r["d_gp"]

    dmerged = mm_nt_rb(dmo, gc, 0, BF)
    dgc = grad_rb(merged, dmo, dgc, 0)
    d_oa, d_ob, sg["sgu_out_g"], sg["sb_out_g"] = merge_norm_bwd(oa, ob, dmerged, small["sgu_out_g"], small["sb_out_g"])
    dp_uv, dws, dbt, sg["sgu_norm_g"], sg["sgu_norm_b"] = sgu_bwd(proj, d_oa, wm, wmt, ng, nbias, bst)
    sg["sgu_w_s"] = dws
    sg["sgu_b_s"] = dbt.reshape(SGU_GROUPS, CHUNK)
    late_grads = {"A2T": dga2, "B2": dgb2.reshape(late["B2"].shape), "C": dgc, "E": dge}
    dq, dk, dv, state = bwd_sb(proj, tot, d_ob, late_grads)
    dproj = jnp.concatenate([dp_uv, dq.astype(BF), dk.astype(BF), dv.astype(BF)], axis=1)
    dgd = grad_cb(n1, dproj, dgd)
    dn1 = mm_nt_cb([(dproj, gd, None)], D_MODEL, F32)
    r = norm_bwd(h1, small["mix_pre_g"], d_a=dn1, d_res=dh2, f_prev=f1, gp_prev=small["ffn1_post_g"], alpha_prev=0.5)
    dh1, df1 = r["d_h"], r["d_f"]
    sg["mix_pre_g"], sg["ffn1_post_g"] = r["d_gn"], r["d_gp"]

    dg1, du1 = ffn_bwd_act(df1, gb1, 0, g1, u1)
    dgb1 = grad_rb(hid1, df1, dgb1, 0)
    dga1 = grad_rb(dg1, a1, dga1, 0)
    dga1 = grad_rb(du1, a1, dga1, 1)
    early_grads = {"A1T": dga1, "B1": dgb1.reshape(rest["B1"].shape), "D": dgd}
    da1, state1 = ffn1_da([(dg1, ga1, 0), (du1, ga1, 1)], early_grads)
    r = norm_bwd(x, small["ffn1_pre_g"], d_a=da1, d_res=dh1)
    grad_x = r["d_h"]
    sg["ffn1_pre_g"] = r["d_gn"]
    return loss_tile, grad_x, sg, state, state1


def mm_res_plain(a, w, which):
    m = a.shape[0]
    kb, n = w.shape[2], w.shape[3]
    tm = _row_tile(m)

    def body(a_ref, w_ref, o_ref):
        acc = None
        for k in range(N_CHIPS):
            t = _dot(a_ref[:, k * kb:(k + 1) * kb], w_ref[k])
            acc = t if acc is None else acc + t
        o_ref[...] = acc.astype(BF)

    return _call(
        body, name="mm_rb", out_shape=jax.ShapeDtypeStruct((m, n), BF), grid=(m // tm,),
        in_specs=[pl.BlockSpec((tm, N_CHIPS * kb), lambda i: (i, 0)),
                  pl.BlockSpec((N_CHIPS, None, kb, n), lambda i: (0, which, 0, 0))],
        out_specs=pl.BlockSpec((tm, n), lambda i: (i, 0)), compiler_params=_cp("parallel"),
    )(a, w)


_BIG = ("ffn1_w_gate", "ffn1_w_up", "ffn1_w_down", "w_in", "w_out", "xa_w_q", "xa_w_kv", "xa_w_o",
        "ffn2_w_gate", "ffn2_w_up", "ffn2_w_down")
_SMALL = ("ffn1_pre_g", "ffn1_post_g", "mix_pre_g", "mix_post_g", "sgu_norm_g", "sgu_norm_b", "sgu_w_s", "sgu_b_s",
          "sgu_out_g", "sb_out_g", "xa_pre_g", "xa_post_g", "mem_norm_g", "ffn2_pre_g", "ffn2_post_g", "final_norm_g")
_WEIGHTS = ("ffn1_pre_g", "ffn1_post_g", "ffn1_w_gate", "ffn1_w_up", "ffn1_w_down", "mix_pre_g", "mix_post_g", "w_in",
            "sgu_norm_g", "sgu_norm_b", "sgu_w_s", "sgu_b_s", "sgu_out_g", "sb_out_g", "w_out", "xa_pre_g", "xa_post_g",
            "mem_norm_g", "xa_w_q", "xa_w_kv", "xa_w_o", "ffn2_pre_g", "ffn2_post_g", "ffn2_w_gate", "ffn2_w_up",
            "ffn2_w_down", "final_norm_g")
_SLOT = {"ffn1_w_gate": ("A1T", 0, True), "ffn1_w_up": ("A1T", 1, True), "ffn2_w_gate": ("A2T", 0, True),
         "ffn2_w_up": ("A2T", 1, True), "ffn1_w_down": ("B1", None, False), "ffn2_w_down": ("B2", None, False),
         "w_out": ("C", 0, False), "xa_w_q": ("C", 1, False), "xa_w_o": ("C", 2, False), "w_in": ("D", None, False),
         "xa_w_kv": ("E", None, False)}


def _pack_small(vals):
    return jnp.concatenate([vals[nm].reshape(-1, LANES) for nm in _SMALL], axis=0)


def _unpack_small(packed, shapes):
    out, pos = {}, 0
    for nm in _SMALL:
        rows = math.prod(shapes[nm]) // LANES
        out[nm] = packed[pos:pos + rows].reshape(shapes[nm])
        pos += rows
    return out


def kernel(x, mem, ffn1_pre_g, ffn1_post_g, ffn1_w_gate, ffn1_w_up, ffn1_w_down, mix_pre_g, mix_post_g, w_in, sgu_norm_g, sgu_norm_b, sgu_w_s, sgu_b_s, sgu_out_g, sb_out_g, w_out, xa_pre_g, xa_post_g, mem_norm_g, xa_w_q, xa_w_kv, xa_w_o, ffn2_pre_g, ffn2_post_g, ffn2_w_gate, ffn2_w_up, ffn2_w_down, final_norm_g, loss_target, m_ffn1_pre_g, m_ffn1_post_g, m_ffn1_w_gate, m_ffn1_w_up, m_ffn1_w_down, m_mix_pre_g, m_mix_post_g, m_w_in, m_sgu_norm_g, m_sgu_norm_b, m_sgu_w_s, m_sgu_b_s, m_sgu_out_g, m_sb_out_g, m_w_out, m_xa_pre_g, m_xa_post_g, m_mem_norm_g, m_xa_w_q, m_xa_w_kv, m_xa_w_o, m_ffn2_pre_g, m_ffn2_post_g, m_ffn2_w_gate, m_ffn2_w_up, m_ffn2_w_down, m_final_norm_g, v_ffn1_pre_g, v_ffn1_post_g, v_ffn1_w_gate, v_ffn1_w_up, v_ffn1_w_down, v_mix_pre_g, v_mix_post_g, v_w_in, v_sgu_norm_g, v_sgu_norm_b, v_sgu_w_s, v_sgu_b_s, v_sgu_out_g, v_sb_out_g, v_w_out, v_xa_pre_g, v_xa_post_g, v_mem_norm_g, v_xa_w_q, v_xa_w_kv, v_xa_w_o, v_ffn2_pre_g, v_ffn2_post_g, v_ffn2_w_gate, v_ffn2_w_up, v_ffn2_w_down, v_final_norm_g):
    env = dict(locals())
    w = {nm: env[nm] for nm in _WEIGHTS}
    mom = {nm: env["m_" + nm] for nm in _WEIGHTS}
    vel = {nm: env["v_" + nm] for nm in _WEIGHTS}

    loc = {
        "A1": jnp.stack([w["ffn1_w_gate"][0], w["ffn1_w_up"][0]]).astype(BF),
        "A2": jnp.stack([w["ffn2_w_gate"][0], w["ffn2_w_up"][0]]).astype(BF),
        "B1": w["ffn1_w_down"][0].astype(BF),
        "B2": w["ffn2_w_down"][0].astype(BF),
        "C": jnp.stack([w["w_out"][0], w["xa_w_q"][0], w["xa_w_o"][0]]).astype(BF),
        "D": w["w_in"][0].astype(BF),
        "E": w["xa_w_kv"][0].astype(BF),
    }
    ga1 = gather_weights(loc, ("A1",))["A1"]
    c_idx = lax.axis_index("c").astype(jnp.int32).reshape(1)
    kc_idx = jnp.stack([2 * lax.axis_index("x") + lax.axis_index("y"), lax.axis_index("c")]).astype(jnp.int32)
    early_rest = tuple(nm for nm in SET_EARLY if nm != "A1")

    def ffn1_up(a1):
        (g1, u1, hid1), bufs = ffn_up(a1, ga1, 0, 1, comm=gather_comm(loc, early_rest))
        return g1, u1, hid1, gather_forward(bufs, early_rest)

    def fwd_sb(proj):
        (ob, tot), bufs = sb_fwd(proj, comm=gather_comm(loc, SET_LATE))
        return ob, tot, gather_forward(bufs, SET_LATE)

    def reduce_to_pairs(grads, names):
        from_sib = rs_to_sibling(grads, names)
        return {nm: add_halves(nm, grads[nm], from_sib[nm], c_idx) for nm in names}

    def bwd_sb(proj, tot, d_ob, late_grads):
        pairs = reduce_to_pairs(late_grads, GRAD_LATE)
        (dq, dk, dv), got = sb_bwd(proj, tot, d_ob, comm=rs_comm(pairs, GRAD_LATE))
        return dq, dk, dv, (pairs, dict(zip(GRAD_LATE, got)))

    def ffn1_da(mm_pairs, early_grads):
        pairs = reduce_to_pairs(early_grads, GRAD_EARLY)
        da1, got = mm_nt_cb(mm_pairs, D_MODEL, F32, comm=rs_comm(pairs, GRAD_EARLY))
        return da1, (pairs, dict(zip(GRAD_EARLY, got)))

    small = {nm: w[nm][0] for nm in _SMALL}
    for nm in ("ffn1_pre_g", "ffn1_post_g", "mix_pre_g", "mix_post_g", "sgu_out_g", "sb_out_g", "xa_pre_g", "xa_post_g",
               "mem_norm_g", "ffn2_pre_g", "ffn2_post_g", "final_norm_g"):
        small[nm] = w[nm]
    loss_tile, grad_x, small_g, (late_pairs, late_got), (early_pairs, early_got) = local_step(
        x[0], mem[0], loss_target[0], ga1, small, ffn1_up, fwd_sb, bwd_sb, ffn1_da)
    pair_sum = {**late_pairs, **early_pairs}
    from_chips = {**late_got, **early_got}

    small_shapes = {nm: w[nm].shape for nm in _SMALL}
    flat = _pack_small(small_g)
    n_small = flat.shape[0]
    pad = jnp.zeros((-n_small % 8, LANES), F32)
    packed = allreduce_small(jnp.concatenate([flat, pad, loss_tile], axis=0))
    loss = packed[-8, 0]
    g_small = _unpack_small(packed[:n_small], small_shapes)

    grad_groups = GRAD_EARLY + GRAD_LATE
    shard_shape = {nm: pair_sum[nm].shape[1:] for nm in grad_groups}
    for nm in grad_groups:
        axis, size = _HALF[nm]
        shard_shape[nm] = shard_shape[nm][:axis] + (2 * size,) + shard_shape[nm][axis + 1:]
    shard = rs_replicate({nm: add_chips(nm, pair_sum[nm], from_chips[nm], kc_idx, shard_shape[nm])
                          for nm in grad_groups}, grad_groups)

    grads, delta, new_m, new_v = {}, {}, {}, {}
    for nm in _BIG:
        grp, idx, transposed = _SLOT[nm]
        shape = w[nm].shape
        if transposed:
            res = adamw(w[nm][0].T, shard[grp], mom[nm][0].T, vel[nm][0].T, g_index=idx)
            res = [t.T for t in res]
        else:
            res = adamw(w[nm][0], shard[grp], mom[nm][0], vel[nm][0], g_index=idx)
        grads[nm], delta[nm], new_m[nm], new_v[nm] = (t.reshape(shape) for t in res)
    _, d, nm_, nv_ = adamw(_pack_small(w), _pack_small(g_small), _pack_small(mom), _pack_small(vel))
    d, nm_, nv_ = (_unpack_small(t, small_shapes) for t in (d, nm_, nv_))
    for nm in _SMALL:
        grads[nm], delta[nm], new_m[nm], new_v[nm] = g_small[nm], d[nm], nm_[nm], nv_[nm]

    return (loss, grad_x[None], *[grads[nm] for nm in _WEIGHTS], *[delta[nm] for nm in _WEIGHTS],
            *[new_m[nm] for nm in _WEIGHTS], *[new_v[nm] for nm in _WEIGHTS])
```

```python
import functools
import math

import jax
import jax.numpy as jnp
from jax import lax
from jax.experimental import pallas as pl
from jax.experimental.pallas import tpu as pltpu

F32 = jnp.float32
BF = jnp.bfloat16
EPS = 1e-6
D_MODEL = 1024
N_CHIPS = 4
FF_BLOCK = 704
IN_BLOCK = 640
KV_BLOCK = 512
ROW_BLOCK = 256
SGU_GROUPS = 4
CHUNK = 128
SB_HEAD_DIM = 64
SB_SCALE = SB_HEAD_DIM ** -0.5
XA_HEADS = 4
XA_HEAD_DIM = 256
XA_SCALE = XA_HEAD_DIM ** -0.5
LANES = 128
VMEM_LIMIT = 56 * 1024 * 1024
MESH = pl.DeviceIdType.MESH

ADAM_LR = 0.001
ADAM_B1 = 0.9
ADAM_B2 = 0.999
ADAM_EPS = 1e-08
ADAM_WD = 0.01
ADAM_STEP = 10

_GELU_C = math.sqrt(2.0 / math.pi)
_GELU_A = 0.044715


def _cp(*sem):
    return pltpu.CompilerParams(dimension_semantics=sem, vmem_limit_bytes=VMEM_LIMIT)


def _call(body, **kw):
    return pl.pallas_call(body, **kw)


def _call_with_comm(core_body, comm, args, *, name, grid, out_shape, in_specs, out_specs, scratch_shapes, compiler_params,
                    core_aliases=None):
    core_aliases = dict(core_aliases or {})
    if comm is None:
        res = _call(core_body, name=name, grid=grid, out_shape=tuple(out_shape), in_specs=list(in_specs),
                    out_specs=tuple(out_specs), scratch_shapes=list(scratch_shapes), input_output_aliases=core_aliases,
                    compiler_params=compiler_params)(*args)
        return res, ()
    n_in, n_out, n_scr = len(in_specs), len(out_shape), len(scratch_shapes)
    ni, no = len(comm.ins), len(comm.outs)

    def body(*refs):
        core_in, cin = refs[:n_in], refs[n_in:n_in + ni]
        core_out = refs[n_in + ni:n_in + ni + n_out]
        cout = refs[n_in + ni + n_out:n_in + ni + n_out + no]
        scr = refs[n_in + ni + n_out + no:]
        core_scr, send, recv = scr[:n_scr], scr[n_scr], scr[n_scr + 1]
        ids = [pl.program_id(a) for a in range(len(grid))]
        first = functools.reduce(jnp.logical_and, [i == 0 for i in ids])
        last = functools.reduce(jnp.logical_and, [i == g - 1 for i, g in zip(ids, grid)])

        @pl.when(first)
        def _():
            comm.start(cin, cout, send, recv)

        core_body(*core_in, *core_out, *core_scr)

        @pl.when(last)
        def _():
            comm.finish(cin, cout, send, recv)

    hbm = pl.BlockSpec(memory_space=pl.ANY)
    res = _call(
        body, name=name, grid=grid, out_shape=tuple(out_shape) + tuple(comm.outs),
        in_specs=list(in_specs) + [hbm] * ni, out_specs=tuple(out_specs) + (hbm,) * no,
        scratch_shapes=list(scratch_shapes) + [pltpu.SemaphoreType.DMA((comm.n_sems,))] * 2,
        input_output_aliases={**core_aliases, **{n_in + i: n_out + o for i, o in comm.aliases.items()}},
        compiler_params=compiler_params,
    )(*args, *comm.ins)
    return res[:n_out], res[n_out:]


def _dot(a, b):
    return jnp.dot(a, b, preferred_element_type=F32)


def _dot_nt(a, b):
    return lax.dot_general(a, b, (((1,), (1,)), ((), ())), preferred_element_type=F32)


def _dot_tn(a, b):
    return lax.dot_general(a, b, (((0,), (0,)), ((), ())), preferred_element_type=F32)


def _rstd(x):
    return lax.rsqrt(jnp.mean(x * x, axis=-1, keepdims=True) + EPS)


def _rms_bwd(x, g, dy):
    r = _rstd(x)
    xh = x * r
    gd = dy * g
    dx = r * (gd - xh * jnp.mean(gd * xh, axis=-1, keepdims=True))
    dg = jnp.sum(dy * xh, axis=0, keepdims=True)
    return dx, dg


def _gelu(x):
    return 0.5 * x * (1.0 + jnp.tanh(_GELU_C * (x + _GELU_A * (x * x * x))))


def _gelu_grad(x):
    t = jnp.tanh(_GELU_C * (x + _GELU_A * (x * x * x)))
    return 0.5 * (1.0 + t) + 0.5 * x * (1.0 - t * t) * (_GELU_C * (1.0 + 3.0 * _GELU_A * x * x))


def _dot2(x, ones_mat):
    hi = x.astype(BF)
    lo = (x - hi.astype(F32)).astype(BF)
    return _dot(hi, ones_mat) + _dot(lo, ones_mat)


GRAD_ROWS = 2048


def _row_tile(m, want=512):
    return min(want, m)


def _row_parts(tm, nparts=4):
    step = tm // nparts
    return [slice(p * step, (p + 1) * step) for p in range(nparts)]


def rms_fwd(x, g):
    m, d = x.shape
    tm = _row_tile(m)

    def body(x_ref, g_ref, o_ref):
        xv = x_ref[...]
        o_ref[...] = (xv * _rstd(xv) * g_ref[...]).astype(BF)

    return _call(
        body, name="rms_fwd", out_shape=jax.ShapeDtypeStruct((m, d), BF), grid=(m // tm,),
        in_specs=[pl.BlockSpec((tm, d), lambda i: (i, 0)), pl.BlockSpec((1, d), lambda i: (0, 0))],
        out_specs=pl.BlockSpec((tm, d), lambda i: (i, 0)), compiler_params=_cp("parallel"),
    )(x, g)


def ffn_up(a, ga, ig, iu, comm=None):
    m, d = a.shape
    tm = _row_tile(m)
    nb = ga.shape[-1]

    def body(a_ref, wg_ref, wu_ref, g_ref, u_ref, h_ref):
        av = a_ref[...]
        g = _dot(av, wg_ref[...])
        u = _dot(av, wu_ref[...])
        g_ref[...] = g.astype(BF)
        u_ref[...] = u.astype(BF)
        h_ref[...] = (g * jax.nn.sigmoid(g) * u).astype(BF)

    blk = jax.ShapeDtypeStruct((N_CHIPS, m, nb), BF)
    ospec = pl.BlockSpec((None, tm, nb), lambda k, i: (k, i, 0))
    return _call_with_comm(
        body, comm, (a, ga, ga), name="ffn_up", out_shape=(blk, blk, blk), grid=(N_CHIPS, m // tm),
        in_specs=[pl.BlockSpec((tm, d), lambda k, i: (i, 0)),
                  pl.BlockSpec((None, None, d, nb), lambda k, i: (k, ig, 0, 0)),
                  pl.BlockSpec((None, None, d, nb), lambda k, i: (k, iu, 0, 0))],
        out_specs=(ospec, ospec, ospec), scratch_shapes=[],
        compiler_params=_cp("arbitrary", "arbitrary") if comm is not None else _cp("parallel", "parallel"),
    )


def mm_res(lhs, w, which, h, gp, alpha, gn):
    blocked = lhs.ndim == 3
    m = lhs.shape[1] if blocked else lhs.shape[0]
    kb, n = w.shape[2], w.shape[3]
    tm = _row_tile(m)

    def body(l_ref, w_ref, h_ref, gp_ref, gn_ref, f_ref, hn_ref, an_ref):
        parts = _row_parts(tm)
        accs = []
        for rs in parts:
            acc = None
            for k in range(N_CHIPS):
                lk = l_ref[k, rs, :] if blocked else l_ref[rs, k * kb:(k + 1) * kb]
                t = _dot(lk, w_ref[k])
                acc = t if acc is None else acc + t
            accs.append(acc)
        for rs, acc in zip(parts, accs):
            f_ref[rs, :] = acc
            hn = h_ref[rs, :] + alpha * (acc * _rstd(acc) * gp_ref[...])
            hn_ref[rs, :] = hn
            an_ref[rs, :] = (hn * _rstd(hn) * gn_ref[...]).astype(BF)

    lspec = (pl.BlockSpec((N_CHIPS, tm, kb), lambda i: (0, i, 0)) if blocked
             else pl.BlockSpec((tm, N_CHIPS * kb), lambda i: (i, 0)))
    row = pl.BlockSpec((tm, n), lambda i: (i, 0))
    vec = pl.BlockSpec((1, n), lambda i: (0, 0))
    return _call(
        body, name="mm_res", grid=(m // tm,),
        out_shape=(jax.ShapeDtypeStruct((m, n), F32), jax.ShapeDtypeStruct((m, n), F32),
                   jax.ShapeDtypeStruct((m, n), BF)),
        in_specs=[lspec, pl.BlockSpec((N_CHIPS, None, kb, n), lambda i: (0, which, 0, 0)), row, vec, vec],
        out_specs=(row, row, row), compiler_params=_cp("parallel"),
    )(lhs, w, h, gp, gn)


def mm_cb(a, w, out_dtype=BF):
    m, kd = a.shape
    nb = w.shape[-1]
    tm = _row_tile(m)

    def body(a_ref, w_ref, o_ref):
        o_ref[...] = _dot(a_ref[...], w_ref[...]).astype(out_dtype)

    return _call(
        body, name="mm_cb", out_shape=jax.ShapeDtypeStruct((m, N_CHIPS * nb), out_dtype),
        grid=(N_CHIPS, m // tm),
        in_specs=[pl.BlockSpec((tm, kd), lambda k, i: (i, 0)), pl.BlockSpec((None, kd, nb), lambda k, i: (k, 0, 0))],
        out_specs=pl.BlockSpec((tm, nb), lambda k, i: (i, k)), compiler_params=_cp("parallel", "parallel"),
    )(a, w)


def _sgu_core(u_pre, vg_pre, wm_ref, ng_ref, nb_ref, bst_ref, g, c):
    rs = slice(c * CHUNK, (c + 1) * CHUNK)
    cs = slice(g * CHUNK, (g + 1) * CHUNK)
    ug = _gelu(u_pre[rs, cs].astype(F32))
    vgl = _gelu(vg_pre[rs, cs].astype(F32))
    mu = jnp.mean(vgl, axis=-1, keepdims=True)
    cen = vgl - mu
    rstd = lax.rsqrt(jnp.mean(cen * cen, axis=-1, keepdims=True) + EPS)
    xh = cen * rstd
    vn = xh * ng_ref[g:g + 1, :] + nb_ref[g:g + 1, :]
    mixed = _dot(wm_ref[g], vn.astype(BF)) + bst_ref[g]
    return ug, xh, rstd, vn, mixed


def sgu_fwd(proj, wm, ng, nb, bst):
    m = proj.shape[0]
    tm = _row_tile(m)
    wd = SGU_GROUPS * CHUNK

    def body(u_ref, v_ref, wm_ref, ng_ref, nb_ref, bst_ref, o_ref):
        for c in range(tm // CHUNK):
            for g in range(SGU_GROUPS):
                ug, _, _, _, mixed = _sgu_core(u_ref, v_ref, wm_ref, ng_ref, nb_ref, bst_ref, g, c)
                o_ref[c * CHUNK:(c + 1) * CHUNK, g * CHUNK:(g + 1) * CHUNK] = (ug * mixed).astype(BF)

    full = lambda shape: pl.BlockSpec(shape, lambda i: (0,) * len(shape))
    return _call(
        body, name="sgu_fwd", out_shape=jax.ShapeDtypeStruct((m, wd), BF), grid=(m // tm,),
        in_specs=[pl.BlockSpec((tm, wd), lambda i: (i, 0)), pl.BlockSpec((tm, wd), lambda i: (i, 1)),
                  full(wm.shape), full(ng.shape), full(nb.shape), full(bst.shape)],
        out_specs=pl.BlockSpec((tm, wd), lambda i: (i, 0)), compiler_params=_cp("parallel"),
    )(proj, proj, wm, ng, nb, bst)


def _sb_tiles(m):
    tq = min(512, m)
    tk = min(256, m)
    return tq, tk


def _log_sigmoid_pair(z, mask):
    ls = jnp.minimum(z, 0.0) - jnp.log(1.0 + jnp.exp(-jnp.abs(z)))
    l1 = ls - z
    return ls, (l1 if mask is None else jnp.where(mask, l1, 0.0))


def _sb_diag_mask(r0, r1, d, tk):
    rows = r0 + lax.broadcasted_iota(jnp.int32, (r1 - r0, tk), 0)
    cols = d * tk + lax.broadcasted_iota(jnp.int32, (r1 - r0, tk), 1)
    return cols < rows


SB_ROW_PARTS = 2


def _sb_chains(r0, r1):
    part = (r1 - r0) // SB_ROW_PARTS
    return [(hd, r0 + p * part, r0 + (p + 1) * part) for p in range(SB_ROW_PARTS) for hd in range(2)]


def _head_masks():
    lane = lax.broadcasted_iota(jnp.int32, (1, LANES), 1)
    return [lane < SB_HEAD_DIM, lane >= SB_HEAD_DIM]


def sb_fwd(proj, comm=None):
    m = proj.shape[0]
    tq, tk = _sb_tiles(m)
    ndiag = tq // tk

    def body(q_ref, k_ref, v_ref, o_ref, tot_ref, qs, acc, car):
        i = pl.program_id(1)
        nfull = i * ndiag
        heads = _head_masks()
        upper = (lax.broadcasted_iota(jnp.int32, (tk, tk), 0) > lax.broadcasted_iota(jnp.int32, (tk, tk), 1)).astype(BF)
        qv = q_ref[...]
        for hd in range(2):
            qs[hd] = jnp.where(heads[hd], qv, jnp.zeros_like(qv)) * SB_SCALE
        acc[...] = jnp.zeros_like(acc)
        car[...] = jnp.zeros_like(car)

        def block(ks, r0, diag):
            kb = k_ref[pl.ds(ks, tk), :]
            vb = v_ref[pl.ds(ks, tk), :]
            chains = _sb_chains(r0, tq)
            masks = [None if diag is None else _sb_diag_mask(ra, rb, diag, tk) for _, ra, rb in chains]
            zs = [_dot_nt(qs[hd, ra:rb, :], kb) for hd, ra, rb in chains]
            mid = []
            for z, mask in zip(zs, masks):
                ls, l1 = _log_sigmoid_pair(z, mask)
                mid.append((ls, l1, _dot2(l1, upper)))
            pvs = []
            for (hd, ra, rb), (ls, l1, cum), mask in zip(chains, mid, masks):
                a = jnp.exp(ls + (cum + car[hd, ra:rb, :]))
                if mask is not None:
                    a = jnp.where(mask, a, 0.0)
                pvs.append(_dot(a.astype(BF), vb))
            for (hd, ra, rb), (ls, l1, cum), pv in zip(chains, mid, pvs):
                acc[hd, ra:rb, :] += pv
                car[hd, ra:rb, :] += jnp.sum(l1, axis=-1, keepdims=True)

        for d in reversed(range(ndiag)):
            block(pl.multiple_of((nfull + d) * tk, tk), d * tk, d)

        def step(jj, carry):
            block(pl.multiple_of((nfull - 1 - jj) * tk, tk), 0, None)
            return carry

        lax.fori_loop(0, nfull, step, 0)
        o_ref[...] = jnp.where(heads[0], acc[0], acc[1]).astype(BF)
        tot_ref[...] = jnp.where(heads[0], car[0], car[1])

    qb = 2 * 512 // LANES
    return _call_with_comm(
        body, comm, (proj, proj, proj), name="sb_fwd", grid=(4, m // tq),
        out_shape=(jax.ShapeDtypeStruct((m, 512), BF), jax.ShapeDtypeStruct((m, 512), F32)),
        in_specs=[pl.BlockSpec((tq, LANES), lambda p, i: (i, qb + p)),
                  pl.BlockSpec((m, LANES), lambda p, i: (0, qb + 4 + p)),
                  pl.BlockSpec((m, LANES), lambda p, i: (0, qb + 8 + p))],
        out_specs=(pl.BlockSpec((tq, LANES), lambda p, i: (i, p)), pl.BlockSpec((tq, LANES), lambda p, i: (i, p))),
        scratch_shapes=[pltpu.VMEM((2, tq, LANES), BF), pltpu.VMEM((2, tq, LANES), F32), pltpu.VMEM((2, tq, 1), F32)],
        compiler_params=_cp("arbitrary", "arbitrary"),
    )


def merge_norm(oa, ob, ga, gb):
    m, w = oa.shape
    tm = _row_tile(m)

    def body(a_ref, b_ref, ga_ref, gb_ref, o_ref):
        av = a_ref[...].astype(F32)
        bv = b_ref[...].astype(F32)
        o_ref[:, :w] = (av * _rstd(av) * ga_ref[...]).astype(BF)
        o_ref[:, w:] = (bv * _rstd(bv) * gb_ref[...]).astype(BF)

    row = pl.BlockSpec((tm, w), lambda i: (i, 0))
    vec = pl.BlockSpec((1, w), lambda i: (0, 0))
    return _call(
        body, name="merge_norm", out_shape=jax.ShapeDtypeStruct((m, 2 * w), BF), grid=(m // tm,),
        in_specs=[row, row, vec, vec], out_specs=pl.BlockSpec((tm, 2 * w), lambda i: (i, 0)),
        compiler_params=_cp("parallel"),
    )(oa, ob, ga, gb)


def _xa_probs(qh, kh):
    logits = _dot_nt(qh, kh) * XA_SCALE
    e = jnp.exp(logits - jnp.max(logits, axis=-1, keepdims=True))
    return e / jnp.sum(e, axis=-1, keepdims=True)


def xa_fwd(xq, kv):
    m, d = xq.shape
    mm = kv.shape[0]
    tm = _row_tile(m)

    def body(q_ref, kv_ref, o_ref):
        for hd in range(XA_HEADS):
            cs = slice(hd * XA_HEAD_DIM, (hd + 1) * XA_HEAD_DIM)
            p = _xa_probs(q_ref[:, cs], kv_ref[:, cs])
            vh = kv_ref[:, d + hd * XA_HEAD_DIM:d + (hd + 1) * XA_HEAD_DIM]
            o_ref[:, cs] = _dot(p.astype(BF), vh).astype(BF)

    return _call(
        body, name="xa_fwd", out_shape=jax.ShapeDtypeStruct((m, d), BF), grid=(m // tm,),
        in_specs=[pl.BlockSpec((tm, d), lambda i: (i, 0)), pl.BlockSpec((mm, 2 * d), lambda i: (0, 0))],
        out_specs=pl.BlockSpec((tm, d), lambda i: (i, 0)), compiler_params=_cp("parallel"),
    )(xq, kv)


def norm_bwd(h, gn, d_a=None, d_res=None, target=None, f_prev=None, gp_prev=None, alpha_prev=1.0):
    m, d = h.shape
    tm = _row_tile(m)
    has_loss = target is not None
    has_res = d_res is not None
    has_prev = f_prev is not None

    def body(*refs):
        refs = list(refs)
        h_ref, gn_ref = refs[0], refs[1]
        pos = 2
        da_ref = dres_ref = t_ref = f_ref = gp_ref = None
        if has_loss:
            t_ref = refs[pos]; pos += 1
        else:
            da_ref = refs[pos]; pos += 1
        if has_res:
            dres_ref = refs[pos]; pos += 1
        if has_prev:
            f_ref, gp_ref = refs[pos], refs[pos + 1]; pos += 2
        dh_ref, dgn_ref = refs[pos], refs[pos + 1]; pos += 2
        df_ref = dgp_ref = loss_ref = None
        if has_prev:
            df_ref, dgp_ref = refs[pos], refs[pos + 1]; pos += 2
        if has_loss:
            loss_ref = refs[pos]

        first = pl.program_id(0) == 0
        hv = h_ref[...]
        gn = gn_ref[...]
        if has_loss:
            err = hv * _rstd(hv) * gn - t_ref[...]
            da = err * (1.0 / d)
            part = 0.5 * jnp.sum(jnp.sum(err * err, axis=-1, keepdims=True) * (1.0 / d))

            @pl.when(first)
            def _():
                loss_ref[...] = jnp.zeros_like(loss_ref)

            loss_ref[...] += part
        else:
            da = da_ref[...].astype(F32)
        dx, dgn = _rms_bwd(hv, gn, da)
        dh = dx + dres_ref[...] if has_res else dx
        dh_ref[...] = dh

        @pl.when(first)
        def _():
            dgn_ref[...] = jnp.zeros_like(dgn_ref)

        dgn_ref[...] += dgn
        if has_prev:
            dfv, dgp = _rms_bwd(f_ref[...], gp_ref[...], dh)
            df_ref[...] = (alpha_prev * dfv).astype(BF)

            @pl.when(first)
            def _():
                dgp_ref[...] = jnp.zeros_like(dgp_ref)

            dgp_ref[...] += alpha_prev * dgp

    row = pl.BlockSpec((tm, d), lambda i: (i, 0))
    vec = pl.BlockSpec((1, d), lambda i: (0, 0))
    ins, in_specs = [h, gn], [row, vec]
    ins.append(target if has_loss else d_a); in_specs.append(row)
    if has_res:
        ins.append(d_res); in_specs.append(row)
    if has_prev:
        ins += [f_prev, gp_prev]; in_specs += [row, vec]
    outs = [jax.ShapeDtypeStruct((m, d), F32), jax.ShapeDtypeStruct((1, d), F32)]
    out_specs = [row, vec]
    names = ["d_h", "d_gn"]
    if has_prev:
        outs += [jax.ShapeDtypeStruct((m, d), BF), jax.ShapeDtypeStruct((1, d), F32)]
        out_specs += [row, vec]
        names += ["d_f", "d_gp"]
    if has_loss:
        outs.append(jax.ShapeDtypeStruct((8, LANES), F32))
        out_specs.append(pl.BlockSpec((8, LANES), lambda i: (0, 0)))
        names.append("loss")
    res = _call(
        body, name="norm_bwd", out_shape=tuple(outs), grid=(m // tm,), in_specs=in_specs,
        out_specs=tuple(out_specs), compiler_params=_cp("arbitrary"),
    )(*ins)
    return dict(zip(names, res))


def ffn_bwd_act(df, gb, which, g, u):
    m, d = df.shape
    nb = g.shape[-1]
    tm = _row_tile(m, 1024)

    def body(df_ref, w_ref, g_ref, u_ref, dg_ref, du_ref):
        parts = _row_parts(tm, 2)
        dhs = [_dot_nt(df_ref[rs, :], w_ref[...]) for rs in parts]
        for rs, dh in zip(parts, dhs):
            gv = g_ref[rs, :].astype(F32)
            uv = u_ref[rs, :].astype(F32)
            s = jax.nn.sigmoid(gv)
            dg_ref[rs, :] = (dh * uv * (s * (1.0 + gv * (1.0 - s)))).astype(BF)
            du_ref[rs, :] = (dh * gv * s).astype(BF)

    blk = jax.ShapeDtypeStruct((N_CHIPS, m, nb), BF)
    aspec = pl.BlockSpec((None, tm, nb), lambda k, i: (k, i, 0))
    return _call(
        body, name="ffn_bwd_act", out_shape=(blk, blk), grid=(N_CHIPS, m // tm),
        in_specs=[pl.BlockSpec((tm, d), lambda k, i: (i, 0)),
                  pl.BlockSpec((None, None, nb, d), lambda k, i: (k, which, 0, 0)), aspec, aspec],
        out_specs=(aspec, aspec), compiler_params=_cp("parallel", "parallel"),
    )(df, gb, g, u)


def mm_tn(a, b, dest, a_spec, b_spec, o_spec, acc_shape, msteps, comm=None):
    def body(a_ref, b_ref, dest_ref, o_ref, acc):
        del dest_ref
        ms = pl.program_id(1)

        @pl.when(ms == 0)
        def _():
            acc[...] = jnp.zeros_like(acc)

        acc[...] += _dot_tn(a_ref[...], b_ref[...])

        @pl.when(ms == msteps - 1)
        def _():
            o_ref[...] = acc[...].astype(o_ref.dtype)

    (out,), extra = _call_with_comm(
        body, comm, (a, b, dest), name="mm_tn", out_shape=(jax.ShapeDtypeStruct(dest.shape, dest.dtype),),
        grid=(N_CHIPS, msteps), in_specs=[a_spec, b_spec, pl.BlockSpec(memory_space=pl.ANY)], out_specs=(o_spec,),
        scratch_shapes=[pltpu.VMEM(acc_shape, F32)], core_aliases={2: 0},
        compiler_params=_cp("arbitrary", "arbitrary") if comm is not None else _cp("parallel", "arbitrary"),
    )
    return out if comm is None else (out, extra)


def _act_spec(arr, tm, nb):
    if arr.ndim == 3:
        return pl.BlockSpec((None, tm, nb), lambda k, ms: (k, ms, 0))
    return pl.BlockSpec((tm, nb), lambda k, ms: (ms, k))


def grad_cb(a, dout, dest, which=None):
    m, kd = a.shape
    nb = dest.shape[-1]
    tm = _row_tile(m, GRAD_ROWS)
    if which is None:
        o_spec = pl.BlockSpec((None, kd, nb), lambda k, ms: (k, 0, 0))
    else:
        o_spec = pl.BlockSpec((None, None, kd, nb), lambda k, ms: (k, which, 0, 0))
    return mm_tn(a, dout, dest, pl.BlockSpec((tm, kd), lambda k, ms: (ms, 0)), _act_spec(dout, tm, nb), o_spec,
                 (kd, nb), m // tm)


def grad_rb(a, dout, dest, which, comm=None):
    m, n = dout.shape
    kb = dest.shape[-2]
    tm = _row_tile(m, GRAD_ROWS)
    o_spec = pl.BlockSpec((None, None, kb, n), lambda k, ms: (k, which, 0, 0))
    return mm_tn(a, dout, dest, _act_spec(a, tm, kb), pl.BlockSpec((tm, n), lambda k, ms: (ms, 0)), o_spec,
                 (kb, n), m // tm, comm=comm)


def mm_nt_cb(pairs, n, out_dtype, comm=None):
    d0 = pairs[0][0]
    m = d0.shape[1] if d0.ndim == 3 else d0.shape[0]
    tm = _row_tile(m, 1024)
    npair = len(pairs)

    def body(*refs):
        o_ref, acc = refs[2 * npair], refs[2 * npair + 1]
        k = pl.program_id(1)

        @pl.when(k == 0)
        def _():
            acc[...] = jnp.zeros_like(acc)

        for p in range(npair):
            acc[...] += _dot_nt(refs[2 * p][...], refs[2 * p + 1][...])

        @pl.when(k == N_CHIPS - 1)
        def _():
            o_ref[...] = acc[...].astype(out_dtype)

    ins, in_specs = [], []
    for dout, w, which in pairs:
        nb = w.shape[-1]
        if dout.ndim == 3:
            in_specs.append(pl.BlockSpec((None, tm, nb), lambda i, k: (k, i, 0)))
        else:
            in_specs.append(pl.BlockSpec((tm, nb), lambda i, k: (i, k)))
        if w.ndim == 4:
            in_specs.append(pl.BlockSpec((None, None, n, nb), lambda i, k, which=which: (k, which, 0, 0)))
        else:
            in_specs.append(pl.BlockSpec((None, n, nb), lambda i, k: (k, 0, 0)))
        ins += [dout, w]
    (out,), extra = _call_with_comm(
        body, comm, tuple(ins), name="mm_nt_cb", out_shape=(jax.ShapeDtypeStruct((m, n), out_dtype),),
        grid=(m // tm, N_CHIPS), in_specs=in_specs, out_specs=(pl.BlockSpec((tm, n), lambda i, k: (i, 0)),),
        scratch_shapes=[pltpu.VMEM((tm, n), F32)],
        compiler_params=_cp("arbitrary", "arbitrary") if comm is not None else _cp("parallel", "arbitrary"),
    )
    return out if comm is None else (out, extra)


def mm_nt_rb(dout, w, which, out_dtype):
    m, n = dout.shape
    kb = w.shape[2]
    tm = _row_tile(m)

    def body(d_ref, w_ref, o_ref):
        o_ref[...] = _dot_nt(d_ref[...], w_ref[...].reshape(N_CHIPS * kb, n)).astype(out_dtype)

    return _call(
        body, name="mm_nt_rb", out_shape=jax.ShapeDtypeStruct((m, N_CHIPS * kb), out_dtype), grid=(m // tm,),
        in_specs=[pl.BlockSpec((tm, n), lambda i: (i, 0)),
                  pl.BlockSpec((N_CHIPS, None, kb, n), lambda i: (0, which, 0, 0))],
        out_specs=pl.BlockSpec((tm, N_CHIPS * kb), lambda i: (i, 0)), compiler_params=_cp("parallel"),
    )(dout, w)


def merge_norm_bwd(oa, ob, dmerged, ga, gb):
    m, w = oa.shape
    tm = _row_tile(m)

    def body(a_ref, b_ref, dm_ref, ga_ref, gb_ref, da_ref, db_ref, dga_ref, dgb_ref):
        @pl.when(pl.program_id(0) == 0)
        def _():
            dga_ref[...] = jnp.zeros_like(dga_ref)
            dgb_ref[...] = jnp.zeros_like(dgb_ref)

        da, dga = _rms_bwd(a_ref[...].astype(F32), ga_ref[...], dm_ref[:, :w].astype(F32))
        db, dgb = _rms_bwd(b_ref[...].astype(F32), gb_ref[...], dm_ref[:, w:].astype(F32))
        da_ref[...] = da
        db_ref[...] = db
        dga_ref[...] += dga
        dgb_ref[...] += dgb

    row = pl.BlockSpec((tm, w), lambda i: (i, 0))
    vec = pl.BlockSpec((1, w), lambda i: (0, 0))
    return _call(
        body, name="merge_norm_bwd", grid=(m // tm,),
        out_shape=(jax.ShapeDtypeStruct((m, w), F32), jax.ShapeDtypeStruct((m, w), F32),
                   jax.ShapeDtypeStruct((1, w), F32), jax.ShapeDtypeStruct((1, w), F32)),
        in_specs=[row, row, pl.BlockSpec((tm, 2 * w), lambda i: (i, 0)), vec, vec],
        out_specs=(row, row, vec, vec), compiler_params=_cp("arbitrary"),
    )(oa, ob, dmerged, ga, gb)


def sgu_bwd(proj, d_oa, wm, wmt, ng, nb, bst):
    m = proj.shape[0]
    tm = _row_tile(m)
    wd = SGU_GROUPS * CHUNK

    def body(u_ref, v_ref, do_ref, wm_ref, wmt_ref, ng_ref, nb_ref, bst_ref,
             dp_ref, dw_ref, dbt_ref, dng_ref, dnb_ref):
        @pl.when(pl.program_id(0) == 0)
        def _():
            dw_ref[...] = jnp.zeros_like(dw_ref)
            dbt_ref[...] = jnp.zeros_like(dbt_ref)
            dng_ref[...] = jnp.zeros_like(dng_ref)
            dnb_ref[...] = jnp.zeros_like(dnb_ref)

        causal = lax.broadcasted_iota(jnp.int32, (CHUNK, CHUNK), 0) >= lax.broadcasted_iota(jnp.int32, (CHUNK, CHUNK), 1)
        for c in range(tm // CHUNK):
            rs = slice(c * CHUNK, (c + 1) * CHUNK)
            for g in range(SGU_GROUPS):
                cs = slice(g * CHUNK, (g + 1) * CHUNK)
                ug, xh, rstd, vn, mixed = _sgu_core(u_ref, v_ref, wm_ref, ng_ref, nb_ref, bst_ref, g, c)
                do = do_ref[rs, cs]
                dug = do * mixed
                dmix = do * ug
                dmb = dmix.astype(BF)
                dbt_ref[g] += jnp.sum(dmix, axis=-1, keepdims=True)
                dw_ref[g] += jnp.where(causal, _dot_nt(dmb, vn.astype(BF)), 0.0)
                dvn = _dot(wmt_ref[g], dmb)
                dng_ref[g:g + 1, :] += jnp.sum(dvn * xh, axis=0, keepdims=True)
                dnb_ref[g:g + 1, :] += jnp.sum(dvn, axis=0, keepdims=True)
                dxh = dvn * ng_ref[g:g + 1, :]
                dvg = rstd * (dxh - jnp.mean(dxh, axis=-1, keepdims=True)
                              - xh * jnp.mean(dxh * xh, axis=-1, keepdims=True))
                dp_ref[rs, cs] = (dug * _gelu_grad(u_ref[rs, cs].astype(F32))).astype(BF)
                dp_ref[rs, wd + g * CHUNK:wd + (g + 1) * CHUNK] = (dvg * _gelu_grad(v_ref[rs, cs].astype(F32))).astype(BF)

    full = lambda shape: pl.BlockSpec(shape, lambda i: (0,) * len(shape))
    return _call(
        body, name="sgu_bwd", grid=(m // tm,),
        out_shape=(jax.ShapeDtypeStruct((m, 2 * wd), BF), jax.ShapeDtypeStruct(wm.shape, F32),
                   jax.ShapeDtypeStruct(bst.shape, F32), jax.ShapeDtypeStruct(ng.shape, F32),
                   jax.ShapeDtypeStruct(nb.shape, F32)),
        in_specs=[pl.BlockSpec((tm, wd), lambda i: (i, 0)), pl.BlockSpec((tm, wd), lambda i: (i, 1)),
                  pl.BlockSpec((tm, wd), lambda i: (i, 0)),
                  full(wm.shape), full(wmt.shape), full(ng.shape), full(nb.shape), full(bst.shape)],
        out_specs=(pl.BlockSpec((tm, 2 * wd), lambda i: (i, 0)), full(wm.shape), full(bst.shape), full(ng.shape),
                   full(nb.shape)),
        compiler_params=_cp("arbitrary"),
    )(proj, proj, d_oa, wm, wmt, ng, nb, bst)


def sb_bwd(proj, tot, d_ob, comm=None):
    m = proj.shape[0]
    tq, tk = _sb_tiles(m)

    ndiag = tq // tk

    def body(q_ref, k_ref, v_ref, tot_ref, do_ref, dq_ref, dk_ref, dv_ref, qs, dos, tots, dqa, cl1, cg):
        i = pl.program_id(1)

        @pl.when(i == 0)
        def _():
            dk_ref[...] = jnp.zeros_like(dk_ref)
            dv_ref[...] = jnp.zeros_like(dv_ref)

        nfull = i * ndiag
        heads = _head_masks()
        r_io = lax.broadcasted_iota(jnp.int32, (tk, tk), 0)
        c_io = lax.broadcasted_iota(jnp.int32, (tk, tk), 1)
        incl = (r_io <= c_io).astype(BF)
        excl = (r_io < c_io).astype(BF)
        qv = q_ref[...]
        dov = do_ref[...].astype(BF)
        totv = tot_ref[...]
        for hd in range(2):
            qs[hd] = jnp.where(heads[hd], qv, jnp.zeros_like(qv)) * SB_SCALE
            dos[hd] = jnp.where(heads[hd], dov, jnp.zeros_like(dov))
            tots[hd] = jnp.max(jnp.where(heads[hd], totv, -jnp.inf), axis=-1, keepdims=True)
        dqa[...] = jnp.zeros_like(dqa)
        cl1[...] = jnp.zeros_like(cl1)
        cg[...] = jnp.zeros_like(cg)

        def block(ks, r0, diag):
            kb = k_ref[pl.ds(ks, tk), :]
            vb = v_ref[pl.ds(ks, tk), :]
            chains = _sb_chains(r0, tq)
            masks = [None if diag is None else _sb_diag_mask(ra, rb, diag, tk) for _, ra, rb in chains]
            qc = [qs[hd, ra:rb, :] for hd, ra, rb in chains]
            doc = [dos[hd, ra:rb, :] for hd, ra, rb in chains]
            zs = [_dot_nt(q, kb) for q in qc]
            das = [_dot_nt(do, vb) for do in doc]
            s1 = []
            for z, mask in zip(zs, masks):
                ls, l1 = _log_sigmoid_pair(z, mask)
                s1.append((ls, l1, _dot2(l1, incl)))
            s2 = []
            for (hd, ra, rb), (ls, l1, pre), da, mask in zip(chains, s1, das, masks):
                a = jnp.exp(ls + (tots[hd, ra:rb, :] - (pre + cl1[hd, ra:rb, :])))
                if mask is not None:
                    a = jnp.where(mask, a, 0.0)
                gmat = a * da
                s2.append((a, gmat, _dot2(gmat, excl)))
            dk_sum = dv_sum = None
            for n, (hd, ra, rb) in enumerate(chains):
                a, gmat, pref = s2[n]
                sg = jnp.exp(s1[n][0])
                dz = gmat * (1.0 - sg) - (pref + cg[hd, ra:rb, :]) * sg
                if masks[n] is not None:
                    dz = jnp.where(masks[n], dz, 0.0)
                dz = dz.astype(BF)
                dqa[hd, ra:rb, :] += _dot(dz, kb)
                dk_t = _dot_tn(dz, qc[n])
                dv_t = _dot_tn(a.astype(BF), doc[n])
                dk_sum = dk_t if dk_sum is None else dk_sum + dk_t
                dv_sum = dv_t if dv_sum is None else dv_sum + dv_t
            for n, (hd, ra, rb) in enumerate(chains):
                cl1[hd, ra:rb, :] += jnp.sum(s1[n][1], axis=-1, keepdims=True)
                cg[hd, ra:rb, :] += jnp.sum(s2[n][1], axis=-1, keepdims=True)
            dk_ref[pl.ds(ks, tk), :] += dk_sum
            dv_ref[pl.ds(ks, tk), :] += dv_sum

        def step(j, carry):
            block(pl.multiple_of(j * tk, tk), 0, None)
            return carry

        lax.fori_loop(0, nfull, step, 0)
        for d in range(ndiag):
            block(pl.multiple_of((nfull + d) * tk, tk), d * tk, d)
        dq_ref[...] = jnp.where(heads[0], dqa[0], dqa[1]) * SB_SCALE

    qb = 2 * 512 // LANES
    tile = pl.BlockSpec((tq, LANES), lambda p, i: (i, p))
    seq = pl.BlockSpec((m, LANES), lambda p, i: (0, p))
    out = jax.ShapeDtypeStruct((m, 512), F32)
    return _call_with_comm(
        body, comm, (proj, proj, proj, tot, d_ob), name="sb_bwd", grid=(4, m // tq), out_shape=(out, out, out),
        in_specs=[pl.BlockSpec((tq, LANES), lambda p, i: (i, qb + p)),
                  pl.BlockSpec((m, LANES), lambda p, i: (0, qb + 4 + p)),
                  pl.BlockSpec((m, LANES), lambda p, i: (0, qb + 8 + p)), tile, tile],
        out_specs=(tile, seq, seq),
        scratch_shapes=[pltpu.VMEM((2, tq, LANES), BF), pltpu.VMEM((2, tq, LANES), BF), pltpu.VMEM((2, tq, 1), F32),
                        pltpu.VMEM((2, tq, LANES), F32), pltpu.VMEM((2, tq, 1), F32), pltpu.VMEM((2, tq, 1), F32)],
        compiler_params=_cp("arbitrary", "arbitrary"),
    )


def xa_bwd(xq, kv, d_o):
    m, d = xq.shape
    mm = kv.shape[0]
    tm = _row_tile(m)

    def body(q_ref, kv_ref, do_ref, dq_ref, dkv_ref):
        @pl.when(pl.program_id(0) == 0)
        def _():
            dkv_ref[...] = jnp.zeros_like(dkv_ref)

        for hd in range(XA_HEADS):
            cs = slice(hd * XA_HEAD_DIM, (hd + 1) * XA_HEAD_DIM)
            vs = slice(d + hd * XA_HEAD_DIM, d + (hd + 1) * XA_HEAD_DIM)
            qh = q_ref[:, cs]
            kh = kv_ref[:, cs]
            doh = do_ref[:, cs]
            p = _xa_probs(qh, kh)
            dp = _dot_nt(doh, kv_ref[:, vs])
            ds = (p * (dp - jnp.sum(p * dp, axis=-1, keepdims=True))).astype(BF)
            dq_ref[:, cs] = (_dot(ds, kh) * XA_SCALE).astype(BF)
            dkv_ref[:, cs] += _dot_tn(ds, qh) * XA_SCALE
            dkv_ref[:, vs] += _dot_tn(p.astype(BF), doh)

    row = pl.BlockSpec((tm, d), lambda i: (i, 0))
    whole = pl.BlockSpec((mm, 2 * d), lambda i: (0, 0))
    return _call(
        body, name="xa_bwd", grid=(m // tm,),
        out_shape=(jax.ShapeDtypeStruct((m, d), BF), jax.ShapeDtypeStruct((mm, 2 * d), F32)),
        in_specs=[row, whole, row], out_specs=(row, whole), compiler_params=_cp("arbitrary"),
    )(xq, kv, d_o)


def adamw(w, g, mom, vel, g_index=None):
    r, c = w.shape
    tr = r
    for cand in range(512, 7, -8):
        if r % cand == 0:
            tr = cand
            break

    def body(w_ref, g_ref, m_ref, v_ref, go_ref, d_ref, nm_ref, nv_ref):
        gv = g_ref[...]
        mn = ADAM_B1 * m_ref[...] + (1.0 - ADAM_B1) * gv
        vn = ADAM_B2 * v_ref[...] + (1.0 - ADAM_B2) * (gv * gv)
        m_hat = mn / (1.0 - ADAM_B1 ** ADAM_STEP)
        v_hat = vn / (1.0 - ADAM_B2 ** ADAM_STEP)
        go_ref[...] = gv
        d_ref[...] = -ADAM_LR * (m_hat / (jnp.sqrt(v_hat) + ADAM_EPS) + ADAM_WD * w_ref[...])
        nm_ref[...] = mn
        nv_ref[...] = vn

    spec = pl.BlockSpec((tr, c), lambda i: (i, 0))
    gspec = spec if g_index is None else pl.BlockSpec((None, tr, c), lambda i: (g_index, i, 0))
    out = jax.ShapeDtypeStruct((r, c), F32)
    return _call(
        body, name="adamw", out_shape=(out, out, out, out), grid=(r // tr,), in_specs=[spec, gspec, spec, spec],
        out_specs=(spec, spec, spec, spec), compiler_params=_cp("parallel"),
    )(w, g, mom, vel)


def _place():
    x, y, c = lax.axis_index("x"), lax.axis_index("y"), lax.axis_index("c")
    others = [(1 - x, y), (x, 1 - y), (1 - x, 1 - y)]
    return x, y, c, others


_HALF = {"A1": (1, 512), "A2": (1, 512), "B1": (0, 352), "B2": (0, 352), "C": (1, 128), "D": (0, 512), "E": (0, 512),
         "A1T": (1, 352), "A2T": (1, 352)}
SET_EARLY = ("A1", "B1", "D")
SET_LATE = ("A2", "B2", "C", "E")
GRAD_EARLY = ("A1T", "B1", "D")
GRAD_LATE = ("A2T", "B2", "C", "E")
_HBM = pl.BlockSpec(memory_space=pl.ANY)


def _half_of(ref, name, hc, lead=0):
    axis, size = _HALF[name]
    idx = [slice(None)] * (lead + axis) + [pl.ds(hc * size, size)]
    return ref.at[tuple(idx)]


def _gather_slots(loc, names):
    me = 2 * lax.axis_index("x") + lax.axis_index("y")
    init = []
    for nm in names:
        full = lax.empty((N_CHIPS,) + loc[nm].shape, loc[nm].dtype)
        init.append(lax.dynamic_update_slice(full, loc[nm][None], (me,) + (0,) * loc[nm].ndim))
    return init


def _gather_ici(names, src, out, send, recv, sends=True, arrivals=True):
    x, y, c, others = _place()
    me = 2 * x + y
    out_sends, out_arrivals = [], []
    for a, nm in enumerate(names):
        for j, (ox, oy) in enumerate(others):
            sems = dict(send_sem=send.at[3 * a + j], recv_sem=recv.at[3 * a + j], device_id_type=MESH)
            if sends:
                out_sends.append(pltpu.make_async_remote_copy(
                    src_ref=_half_of(src[a], nm, c), dst_ref=_half_of(out[a].at[me], nm, c), device_id=(ox, oy, c),
                    **sems))
            if arrivals:
                landed = _half_of(out[a].at[2 * ox + oy], nm, c)
                out_arrivals.append(pltpu.make_async_remote_copy(src_ref=landed, dst_ref=landed, device_id=(x, y, c),
                                                                 **sems))
    return out_sends, out_arrivals


def _gather_d2d(names, given, out, send, recv):
    x, y, c, others = _place()
    sends, arrivals = [], []
    for a, nm in enumerate(names):
        for j, (ox, oy) in enumerate(others):
            sems = dict(send_sem=send.at[3 * a + j], recv_sem=recv.at[3 * a + j], device_id_type=MESH)
            sends.append(pltpu.make_async_remote_copy(
                src_ref=_half_of(given[a].at[2 * ox + oy], nm, c), dst_ref=_half_of(out[a].at[2 * ox + oy], nm, c),
                device_id=(x, y, 1 - c), **sems))
            landed = _half_of(out[a].at[2 * ox + oy], nm, 1 - c)
            arrivals.append(pltpu.make_async_remote_copy(src_ref=landed, dst_ref=landed, device_id=(x, y, c), **sems))
    return sends, arrivals


def gather_weights(loc, names):
    n = len(names)

    def body(*refs):
        src, given, out = refs[:n], refs[n:2 * n], refs[2 * n:3 * n]
        send1, recv1, send2, recv2 = refs[3 * n:3 * n + 4]
        first, landed = _gather_ici(names, src, out, send1, recv1)
        del given
        passed, arrivals = _gather_d2d(names, out, out, send2, recv2)
        for f in first:
            f.start()
        for l, p in zip(landed, passed):
            l.wait_recv()
            p.start()
        for a in arrivals:
            a.wait_recv()
        for f in first + passed:
            f.wait_send()

    init = _gather_slots(loc, names)
    res = _call(
        body, name="gather_weights", out_shape=tuple(jax.ShapeDtypeStruct(t.shape, t.dtype) for t in init),
        in_specs=[_HBM] * (2 * n), out_specs=(_HBM,) * n, input_output_aliases={n + a: a for a in range(n)},
        scratch_shapes=[pltpu.SemaphoreType.DMA((3 * n,))] * 4,
    )(*[loc[nm] for nm in names], *init)
    return dict(zip(names, res))


def gather_forward(bufs, names):
    n = len(names)

    def body(*refs):
        given, out = refs[:n], refs[n:2 * n]
        passed, arrivals = _gather_d2d(names, given, out, refs[2 * n], refs[2 * n + 1])
        for p in passed:
            p.start()
        for a in arrivals:
            a.wait_recv()
        for p in passed:
            p.wait_send()

    res = _call(
        body, name="gather_forward", out_shape=tuple(jax.ShapeDtypeStruct(t.shape, t.dtype) for t in bufs),
        in_specs=[_HBM] * n, out_specs=(_HBM,) * n, input_output_aliases={a: a for a in range(n)},
        scratch_shapes=[pltpu.SemaphoreType.DMA((3 * n,))] * 2,
    )(*bufs)
    return dict(zip(names, res))


class FusedComm:
    def __init__(self, ins, outs, aliases, n_sems, start, finish):
        self.ins, self.outs, self.aliases, self.n_sems, self.start, self.finish = ins, outs, aliases, n_sems, start, finish


def gather_comm(loc, names):
    n = len(names)

    def start(ins, outs, send, recv):
        for f in _gather_ici(names, ins[:n], outs, send, recv, arrivals=False)[0]:
            f.start()

    def finish(ins, outs, send, recv):
        first, landed = _gather_ici(names, ins[:n], outs, send, recv)
        for l in landed:
            l.wait_recv()
        for f in first:
            f.wait_send()

    init = _gather_slots(loc, names)
    return FusedComm([loc[nm] for nm in names] + init, [jax.ShapeDtypeStruct(t.shape, t.dtype) for t in init],
                     {n + a: a for a in range(n)}, 3 * n, start, finish)


def _half_shape(name, shape):
    axis, size = _HALF[name]
    s = list(shape)
    s[axis] = size
    return tuple(s)


def rs_to_sibling(grads, names):
    n = len(names)

    def body(*refs):
        src = dict(zip(names, refs[:n]))
        out = dict(zip(names, refs[n:2 * n]))
        send, recv = refs[2 * n], refs[2 * n + 1]
        x, y, c, _ = _place()
        copies = []
        for a, nm in enumerate(names):
            copies.append(pltpu.make_async_remote_copy(
                src_ref=_half_of(src[nm], nm, 1 - c, lead=1), dst_ref=out[nm], send_sem=send.at[a], recv_sem=recv.at[a],
                device_id=(x, y, 1 - c), device_id_type=MESH))
        for cpy in copies:
            cpy.start()
        for cpy in copies:
            cpy.wait()

    hbm = pl.BlockSpec(memory_space=pl.ANY)
    outs = tuple(jax.ShapeDtypeStruct((N_CHIPS,) + _half_shape(nm, grads[nm].shape[1:]), grads[nm].dtype)
                 for nm in names)
    res = _call(
        body, name="rs_to_sibling", out_shape=outs, in_specs=[hbm] * n, out_specs=(hbm,) * n,
        scratch_shapes=[pltpu.SemaphoreType.DMA((n,)), pltpu.SemaphoreType.DMA((n,))],
    )(*[grads[nm] for nm in names])
    return dict(zip(names, res))


def _tile2(shape):
    lead = shape[:-2]
    return lead, shape[-2:]


def add_halves(name, mine, got, c_idx):
    axis, size = _HALF[name]
    hshape = got.shape
    lead, last2 = hshape[:-2], hshape[-2:]
    nlead = len(lead)
    haxis = 1 + axis

    def body(c_ref, m_ref, g_ref, o_ref):
        del c_ref
        o_ref[...] = (m_ref[...].astype(F32) + g_ref[...].astype(F32)).astype(o_ref.dtype)

    blk = (None,) * nlead + last2

    def got_map(*idx):
        return tuple(idx[:nlead]) + (0, 0)

    def mine_map(*idx):
        lead_idx = list(idx[:nlead])
        c = idx[nlead][0]
        if haxis < nlead:
            lead_idx[haxis] = lead_idx[haxis] + c * size
            return tuple(lead_idx) + (0, 0)
        return tuple(lead_idx) + (c, 0)

    grid_spec = pltpu.PrefetchScalarGridSpec(
        num_scalar_prefetch=1, grid=lead,
        in_specs=[pl.BlockSpec(blk, mine_map), pl.BlockSpec(blk, got_map)],
        out_specs=pl.BlockSpec(blk, got_map))
    return _call(
        body, name="add_halves", out_shape=jax.ShapeDtypeStruct(hshape, got.dtype), grid_spec=grid_spec,
        compiler_params=_cp(*(("parallel",) * nlead)),
    )(c_idx, mine, got)


def _rs_ici(n, src, out, send, recv):
    x, y, c, others = _place()
    copies = []
    for a in range(n):
        for j, (ox, oy) in enumerate(others):
            copies.append(pltpu.make_async_remote_copy(
                src_ref=src[a].at[2 * ox + oy], dst_ref=out[a].at[j], send_sem=send.at[3 * a + j],
                recv_sem=recv.at[3 * a + j], device_id=(ox, oy, c), device_id_type=MESH))
    return copies


def _rs_out_shapes(summed, names):
    return [jax.ShapeDtypeStruct((3,) + summed[nm].shape[1:], summed[nm].dtype) for nm in names]


def rs_comm(summed, names):
    n = len(names)

    def start(ins, outs, send, recv):
        for cpy in _rs_ici(n, ins, outs, send, recv):
            cpy.start()

    def finish(ins, outs, send, recv):
        for cpy in _rs_ici(n, ins, outs, send, recv):
            cpy.wait()

    return FusedComm([summed[nm] for nm in names], _rs_out_shapes(summed, names), {}, 3 * n, start, finish)


def add_chips(name, summed, got, kc_idx, full_shape):
    axis, size = _HALF[name]
    hshape = summed.shape[1:]
    lead, last2 = hshape[:-2], hshape[-2:]
    nlead = len(lead)

    def body(kc_ref, s_ref, g0_ref, g1_ref, g2_ref, o_ref):
        del kc_ref
        o_ref[...] = ((s_ref[...].astype(F32) + g0_ref[...].astype(F32)) + g1_ref[...].astype(F32)) + g2_ref[...].astype(F32)

    blk = (None,) * (nlead + 1) + last2
    oblk = (None,) * nlead + last2

    def got_map(slot):
        return lambda *idx: (slot,) + tuple(idx[:nlead]) + (0, 0)

    def out_map(*idx):
        lead_idx = list(idx[:nlead])
        c = idx[-1][1]
        if axis < nlead:
            lead_idx[axis] = lead_idx[axis] + c * size
            return tuple(lead_idx) + (0, 0)
        return tuple(lead_idx) + (c, 0)

    grid_spec = pltpu.PrefetchScalarGridSpec(
        num_scalar_prefetch=1, grid=lead if nlead else (1,),
        in_specs=[pl.BlockSpec(blk, lambda *idx: (idx[-1][0],) + tuple(idx[:nlead]) + (0, 0)),
                  pl.BlockSpec(blk, got_map(0)), pl.BlockSpec(blk, got_map(1)), pl.BlockSpec(blk, got_map(2))],
        out_specs=pl.BlockSpec(oblk, out_map))
    return _call(
        body, name="add_chips", out_shape=jax.ShapeDtypeStruct(full_shape, F32), grid_spec=grid_spec,
        compiler_params=_cp(*(("parallel",) * max(nlead, 1))),
    )(kc_idx, summed, got, got, got)


def rs_replicate(shards, names):
    n = len(names)

    def body(*refs):
        given = dict(zip(names, refs[:n]))
        buf = dict(zip(names, refs[n:2 * n]))
        send, recv = refs[2 * n], refs[2 * n + 1]
        x, y, c, _ = _place()
        copies = []
        for a, nm in enumerate(names):
            copies.append(pltpu.make_async_remote_copy(
                src_ref=_half_of(given[nm], nm, c), dst_ref=_half_of(buf[nm], nm, c), send_sem=send.at[a],
                recv_sem=recv.at[a], device_id=(x, y, 1 - c), device_id_type=MESH))
        for cpy in copies:
            cpy.start()
        for a, nm in enumerate(names):
            other = _half_of(buf[nm], nm, 1 - c)
            pltpu.make_async_remote_copy(src_ref=other, dst_ref=other, send_sem=send.at[a], recv_sem=recv.at[a],
                                         device_id=(x, y, 1 - c), device_id_type=MESH).wait_recv()
        for cpy in copies:
            cpy.wait_send()

    hbm = pl.BlockSpec(memory_space=pl.ANY)
    outs = tuple(jax.ShapeDtypeStruct(shards[nm].shape, F32) for nm in names)
    res = _call(
        body, name="rs_replicate", out_shape=outs, in_specs=[hbm] * n, out_specs=(hbm,) * n,
        input_output_aliases={a: a for a in range(n)},
        scratch_shapes=[pltpu.SemaphoreType.DMA((n,)), pltpu.SemaphoreType.DMA((n,))],
    )(*[shards[nm] for nm in names])
    return dict(zip(names, res))


def allreduce_small(v):
    r = v.shape[0]

    def body(v_ref, o_ref, slots, send, recv):
        x, y, c, _ = _place()
        me = 4 * x + 2 * y + c
        slots[me] = v_ref[...]
        copies = []
        for rel in range(1, 8):
            fx, fy, fc = (rel >> 2) & 1, (rel >> 1) & 1, rel & 1
            peer = (x ^ fx, y ^ fy, c ^ fc)
            copies.append(pltpu.make_async_remote_copy(
                src_ref=v_ref, dst_ref=slots.at[me], send_sem=send.at[rel - 1], recv_sem=recv.at[rel - 1],
                device_id=peer, device_id_type=MESH))
        for cpy in copies:
            cpy.start()
        for rel in range(1, 8):
            fx, fy, fc = (rel >> 2) & 1, (rel >> 1) & 1, rel & 1
            src_id = 4 * (x ^ fx) + 2 * (y ^ fy) + (c ^ fc)
            pltpu.make_async_remote_copy(
                src_ref=v_ref, dst_ref=slots.at[src_id], send_sem=send.at[rel - 1], recv_sem=recv.at[rel - 1],
                device_id=(x, y, c), device_id_type=MESH).wait_recv()
        for cpy in copies:
            cpy.wait_send()
        acc = slots[0]
        for s in range(1, 8):
            acc = acc + slots[s]
        o_ref[...] = acc

    vm = pl.BlockSpec(memory_space=pltpu.VMEM)
    return _call(
        body, name="allreduce_small", out_shape=jax.ShapeDtypeStruct(v.shape, F32), in_specs=[vm], out_specs=vm,
        scratch_shapes=[pltpu.VMEM((8, r, LANES), F32), pltpu.SemaphoreType.DMA((7,)), pltpu.SemaphoreType.DMA((7,))],
    )(v)


def local_step(x, mem, target, ga1, small, ffn1_up, fwd_sb, bwd_sb, ffn1_mid, ffn1_da):
    causal = jnp.tril(jnp.ones((CHUNK, CHUNK), dtype=bool))
    w_s = jnp.where(causal[None], small["sgu_w_s"], 0.0)
    wm = w_s.astype(BF)
    wmt = jnp.swapaxes(w_s, 1, 2).astype(BF)
    bst = small["sgu_b_s"].reshape(SGU_GROUPS, CHUNK, 1)
    ng, nbias = small["sgu_norm_g"], small["sgu_norm_b"]

    a1 = rms_fwd(x, small["ffn1_pre_g"])
    g1, u1, hid1, rest = ffn1_up(a1)
    gb1, gd = rest["B1"][:, None], rest["D"]
    f1, h1, n1 = mm_res(hid1, gb1, 0, x, small["ffn1_post_g"], 0.5, small["mix_pre_g"])
    proj = mm_cb(n1, gd)
    oa = sgu_fwd(proj, wm, ng, nbias, bst)
    ob, tot, late = fwd_sb(proj)
    ga2, gb2, gc, ge = late["A2"], late["B2"][:, None], late["C"], late["E"]
    merged = merge_norm(oa, ob, small["sgu_out_g"], small["sb_out_g"])
    mo, h2, xn = mm_res(merged, gc, 0, h1, small["mix_post_g"], 1.0, small["xa_pre_g"])
    memn = rms_fwd(mem, small["mem_norm_g"])
    kv = mm_cb(memn, ge)
    xq = mm_res_plain(xn, gc, 1)
    o = xa_fwd(xq, kv)
    cc, h3, a2 = mm_res(o, gc, 2, h2, small["xa_post_g"], 1.0, small["ffn2_pre_g"])
    (g2, u2, hid2), _ = ffn_up(a2, ga2, 0, 1)
    f2, h4, _ = mm_res(hid2, gb2, 0, h3, small["ffn2_post_g"], 0.5, small["final_norm_g"])

    gate_up_t = (N_CHIPS, 2, ga1.shape[3], ga1.shape[2])
    dga1, dga2 = lax.empty(gate_up_t, BF), lax.empty(gate_up_t, BF)
    dgb1, dgb2 = lax.empty(gb1.shape, BF), lax.empty(gb2.shape, BF)
    dgc = lax.empty(gc.shape, BF)
    dgd = lax.empty(gd.shape, BF)
    dge = lax.empty(ge.shape, BF)
    sg = {}

    r = norm_bwd(h4, small["final_norm_g"], target=target, f_prev=f2, gp_prev=small["ffn2_post_g"], alpha_prev=0.5)
    loss_tile, dh4, df2 = r["loss"], r["d_h"], r["d_f"]
    sg["final_norm_g"], sg["ffn2_post_g"] = r["d_gn"], r["d_gp"]

    dg2, du2 = ffn_bwd_act(df2, gb2, 0, g2, u2)
    dgb2 = grad_rb(hid2, df2, dgb2, 0)
    dga2 = grad_rb(dg2, a2, dga2, 0)
    dga2 = grad_rb(du2, a2, dga2, 1)
    da2 = mm_nt_cb([(dg2, ga2, 0), (du2, ga2, 1)], D_MODEL, F32)
    r = norm_bwd(h3, small["ffn2_pre_g"], d_a=da2, d_res=dh4, f_prev=cc, gp_prev=small["xa_post_g"], alpha_prev=1.0)
    dh3, dc = r["d_h"], r["d_f"]
    sg["ffn2_pre_g"], sg["xa_post_g"] = r["d_gn"], r["d_gp"]

    d_o = mm_nt_rb(dc, gc, 2, BF)
    dgc = grad_rb(o, dc, dgc, 2)
    dxq, dkv = xa_bwd(xq, kv, d_o)
    dkvb = dkv.astype(BF)
    dge = grad_cb(memn, dkvb, dge)
    dmemn = mm_nt_cb([(dkvb, ge, None)], D_MODEL, F32)
    sg["mem_norm_g"] = norm_bwd(mem, small["mem_norm_g"], d_a=dmemn)["d_gn"]
    dgc = grad_rb(xn, dxq, dgc, 1)
    dxn = mm_nt_rb(dxq, gc, 1, F32)
    r = norm_bwd(h2, small["xa_pre_g"], d_a=dxn, d_res=dh3, f_prev=mo, gp_prev=small["mix_post_g"], alpha_prev=1.0)
    dh2, dmo = r["d_h"], r["d_f"]
    sg["xa_pre_g"], sg["mix_post_g"] = r["d_gn"], r["d_gp"]

    dmerged = mm_nt_rb(dmo, gc, 0, BF)
    dgc = grad_rb(merged, dmo, dgc, 0)
    d_oa, d_ob, sg["sgu_out_g"], sg["sb_out_g"] = merge_norm_bwd(oa, ob, dmerged, small["sgu_out_g"], small["sb_out_g"])
    dp_uv, dws, dbt, sg["sgu_norm_g"], sg["sgu_norm_b"] = sgu_bwd(proj, d_oa, wm, wmt, ng, nbias, bst)
    sg["sgu_w_s"] = dws
    sg["sgu_b_s"] = dbt.reshape(SGU_GROUPS, CHUNK)
    late_grads = {"A2T": dga2, "B2": dgb2.reshape(late["B2"].shape), "C": dgc, "E": dge}
    dq, dk, dv, state = bwd_sb(proj, tot, d_ob, late_grads)
    dproj = jnp.concatenate([dp_uv, dq.astype(BF), dk.astype(BF), dv.astype(BF)], axis=1)
    dgd = grad_cb(n1, dproj, dgd)
    dn1 = mm_nt_cb([(dproj, gd, None)], D_MODEL, F32)
    r = norm_bwd(h1, small["mix_pre_g"], d_a=dn1, d_res=dh2, f_prev=f1, gp_prev=small["ffn1_post_g"], alpha_prev=0.5)
    dh1, df1 = r["d_h"], r["d_f"]
    sg["mix_pre_g"], sg["ffn1_post_g"] = r["d_gn"], r["d_gp"]

    dg1, du1 = ffn_bwd_act(df1, gb1, 0, g1, u1)
    dgb1 = grad_rb(hid1, df1, dgb1, 0)
    mid_state, comm_b1, comm_d = ffn1_mid({"B1": dgb1.reshape(rest["B1"].shape), "D": dgd})
    res = grad_rb(dg1, a1, dga1, 0, comm=comm_b1)
    dga1, got_b1 = res if comm_b1 is not None else (res, ())
    res = grad_rb(du1, a1, dga1, 1, comm=comm_d)
    dga1, got_d = res if comm_d is not None else (res, ())
    da1, da_state = ffn1_da([(dg1, ga1, 0), (du1, ga1, 1)], {"A1T": dga1})
    state1 = (mid_state, got_b1, got_d, da_state)
    r = norm_bwd(x, small["ffn1_pre_g"], d_a=da1, d_res=dh1)
    grad_x = r["d_h"]
    sg["ffn1_pre_g"] = r["d_gn"]
    return loss_tile, grad_x, sg, state, state1


def mm_res_plain(a, w, which):
    m = a.shape[0]
    kb, n = w.shape[2], w.shape[3]
    tm = _row_tile(m)

    def body(a_ref, w_ref, o_ref):
        acc = None
        for k in range(N_CHIPS):
            t = _dot(a_ref[:, k * kb:(k + 1) * kb], w_ref[k])
            acc = t if acc is None else acc + t
        o_ref[...] = acc.astype(BF)

    return _call(
        body, name="mm_rb", out_shape=jax.ShapeDtypeStruct((m, n), BF), grid=(m // tm,),
        in_specs=[pl.BlockSpec((tm, N_CHIPS * kb), lambda i: (i, 0)),
                  pl.BlockSpec((N_CHIPS, None, kb, n), lambda i: (0, which, 0, 0))],
        out_specs=pl.BlockSpec((tm, n), lambda i: (i, 0)), compiler_params=_cp("parallel"),
    )(a, w)


_BIG = ("ffn1_w_gate", "ffn1_w_up", "ffn1_w_down", "w_in", "w_out", "xa_w_q", "xa_w_kv", "xa_w_o",
        "ffn2_w_gate", "ffn2_w_up", "ffn2_w_down")
_SMALL = ("ffn1_pre_g", "ffn1_post_g", "mix_pre_g", "mix_post_g", "sgu_norm_g", "sgu_norm_b", "sgu_w_s", "sgu_b_s",
          "sgu_out_g", "sb_out_g", "xa_pre_g", "xa_post_g", "mem_norm_g", "ffn2_pre_g", "ffn2_post_g", "final_norm_g")
_WEIGHTS = ("ffn1_pre_g", "ffn1_post_g", "ffn1_w_gate", "ffn1_w_up", "ffn1_w_down", "mix_pre_g", "mix_post_g", "w_in",
            "sgu_norm_g", "sgu_norm_b", "sgu_w_s", "sgu_b_s", "sgu_out_g", "sb_out_g", "w_out", "xa_pre_g", "xa_post_g",
            "mem_norm_g", "xa_w_q", "xa_w_kv", "xa_w_o", "ffn2_pre_g", "ffn2_post_g", "ffn2_w_gate", "ffn2_w_up",
            "ffn2_w_down", "final_norm_g")
_SLOT = {"ffn1_w_gate": ("A1T", 0, True), "ffn1_w_up": ("A1T", 1, True), "ffn2_w_gate": ("A2T", 0, True),
         "ffn2_w_up": ("A2T", 1, True), "ffn1_w_down": ("B1", None, False), "ffn2_w_down": ("B2", None, False),
         "w_out": ("C", 0, False), "xa_w_q": ("C", 1, False), "xa_w_o": ("C", 2, False), "w_in": ("D", None, False),
         "xa_w_kv": ("E", None, False)}


def _pack_small(vals):
    return jnp.concatenate([vals[nm].reshape(-1, LANES) for nm in _SMALL], axis=0)


def _unpack_small(packed, shapes):
    out, pos = {}, 0
    for nm in _SMALL:
        rows = math.prod(shapes[nm]) // LANES
        out[nm] = packed[pos:pos + rows].reshape(shapes[nm])
        pos += rows
    return out


def kernel(x, mem, ffn1_pre_g, ffn1_post_g, ffn1_w_gate, ffn1_w_up, ffn1_w_down, mix_pre_g, mix_post_g, w_in, sgu_norm_g, sgu_norm_b, sgu_w_s, sgu_b_s, sgu_out_g, sb_out_g, w_out, xa_pre_g, xa_post_g, mem_norm_g, xa_w_q, xa_w_kv, xa_w_o, ffn2_pre_g, ffn2_post_g, ffn2_w_gate, ffn2_w_up, ffn2_w_down, final_norm_g, loss_target, m_ffn1_pre_g, m_ffn1_post_g, m_ffn1_w_gate, m_ffn1_w_up, m_ffn1_w_down, m_mix_pre_g, m_mix_post_g, m_w_in, m_sgu_norm_g, m_sgu_norm_b, m_sgu_w_s, m_sgu_b_s, m_sgu_out_g, m_sb_out_g, m_w_out, m_xa_pre_g, m_xa_post_g, m_mem_norm_g, m_xa_w_q, m_xa_w_kv, m_xa_w_o, m_ffn2_pre_g, m_ffn2_post_g, m_ffn2_w_gate, m_ffn2_w_up, m_ffn2_w_down, m_final_norm_g, v_ffn1_pre_g, v_ffn1_post_g, v_ffn1_w_gate, v_ffn1_w_up, v_ffn1_w_down, v_mix_pre_g, v_mix_post_g, v_w_in, v_sgu_norm_g, v_sgu_norm_b, v_sgu_w_s, v_sgu_b_s, v_sgu_out_g, v_sb_out_g, v_w_out, v_xa_pre_g, v_xa_post_g, v_mem_norm_g, v_xa_w_q, v_xa_w_kv, v_xa_w_o, v_ffn2_pre_g, v_ffn2_post_g, v_ffn2_w_gate, v_ffn2_w_up, v_ffn2_w_down, v_final_norm_g):
    env = dict(locals())
    w = {nm: env[nm] for nm in _WEIGHTS}
    mom = {nm: env["m_" + nm] for nm in _WEIGHTS}
    vel = {nm: env["v_" + nm] for nm in _WEIGHTS}

    loc = {
        "A1": jnp.stack([w["ffn1_w_gate"][0], w["ffn1_w_up"][0]]).astype(BF),
        "A2": jnp.stack([w["ffn2_w_gate"][0], w["ffn2_w_up"][0]]).astype(BF),
        "B1": w["ffn1_w_down"][0].astype(BF),
        "B2": w["ffn2_w_down"][0].astype(BF),
        "C": jnp.stack([w["w_out"][0], w["xa_w_q"][0], w["xa_w_o"][0]]).astype(BF),
        "D": w["w_in"][0].astype(BF),
        "E": w["xa_w_kv"][0].astype(BF),
    }
    ga1 = gather_weights(loc, ("A1",))["A1"]
    c_idx = lax.axis_index("c").astype(jnp.int32).reshape(1)
    kc_idx = jnp.stack([2 * lax.axis_index("x") + lax.axis_index("y"), lax.axis_index("c")]).astype(jnp.int32)
    early_rest = tuple(nm for nm in SET_EARLY if nm != "A1")

    def ffn1_up(a1):
        (g1, u1, hid1), bufs = ffn_up(a1, ga1, 0, 1, comm=gather_comm(loc, early_rest))
        return g1, u1, hid1, gather_forward(bufs, early_rest)

    def fwd_sb(proj):
        (ob, tot), bufs = sb_fwd(proj, comm=gather_comm(loc, SET_LATE))
        return ob, tot, gather_forward(bufs, SET_LATE)

    def reduce_to_pairs(grads, names):
        from_sib = rs_to_sibling(grads, names)
        return {nm: add_halves(nm, grads[nm], from_sib[nm], c_idx) for nm in names}

    def bwd_sb(proj, tot, d_ob, late_grads):
        pairs = reduce_to_pairs(late_grads, GRAD_LATE)
        (dq, dk, dv), got = sb_bwd(proj, tot, d_ob, comm=rs_comm(pairs, GRAD_LATE))
        return dq, dk, dv, (pairs, dict(zip(GRAD_LATE, got)))

    def ffn1_mid(bd_grads):
        pairs = reduce_to_pairs(bd_grads, ("B1", "D"))
        return pairs, rs_comm(pairs, ("B1",)), rs_comm(pairs, ("D",))

    def ffn1_da(mm_pairs, gate_up_grads):
        pairs = reduce_to_pairs(gate_up_grads, ("A1T",))
        da1, got = mm_nt_cb(mm_pairs, D_MODEL, F32, comm=rs_comm(pairs, ("A1T",)))
        return da1, (pairs, got)

    small = {nm: w[nm][0] for nm in _SMALL}
    for nm in ("ffn1_pre_g", "ffn1_post_g", "mix_pre_g", "mix_post_g", "sgu_out_g", "sb_out_g", "xa_pre_g", "xa_post_g",
               "mem_norm_g", "ffn2_pre_g", "ffn2_post_g", "final_norm_g"):
        small[nm] = w[nm]
    loss_tile, grad_x, small_g, (late_pairs, late_got), (bd_pairs, got_b1, got_d, (a1_pairs, got_a1)) = local_step(
        x[0], mem[0], loss_target[0], ga1, small, ffn1_up, fwd_sb, bwd_sb, ffn1_mid, ffn1_da)
    pair_sum = {**late_pairs, **bd_pairs, **a1_pairs}
    from_chips = {**late_got, "B1": got_b1[0], "D": got_d[0], "A1T": got_a1[0]}

    small_shapes = {nm: w[nm].shape for nm in _SMALL}
    flat = _pack_small(small_g)
    n_small = flat.shape[0]
    pad = jnp.zeros((-n_small % 8, LANES), F32)
    packed = allreduce_small(jnp.concatenate([flat, pad, loss_tile], axis=0))
    loss = packed[-8, 0]
    g_small = _unpack_small(packed[:n_small], small_shapes)

    grad_groups = GRAD_EARLY + GRAD_LATE
    shard_shape = {nm: pair_sum[nm].shape[1:] for nm in grad_groups}
    for nm in grad_groups:
        axis, size = _HALF[nm]
        shard_shape[nm] = shard_shape[nm][:axis] + (2 * size,) + shard_shape[nm][axis + 1:]
    shard = rs_replicate({nm: add_chips(nm, pair_sum[nm], from_chips[nm], kc_idx, shard_shape[nm])
                          for nm in grad_groups}, grad_groups)

    grads, delta, new_m, new_v = {}, {}, {}, {}
    for nm in _BIG:
        grp, idx, transposed = _SLOT[nm]
        shape = w[nm].shape
        if transposed:
            res = adamw(w[nm][0].T, shard[grp], mom[nm][0].T, vel[nm][0].T, g_index=idx)
            res = [t.T for t in res]
        else:
            res = adamw(w[nm][0], shard[grp], mom[nm][0], vel[nm][0], g_index=idx)
        grads[nm], delta[nm], new_m[nm], new_v[nm] = (t.reshape(shape) for t in res)
    _, d, nm_, nv_ = adamw(_pack_small(w), _pack_small(g_small), _pack_small(mom), _pack_small(vel))
    d, nm_, nv_ = (_unpack_small(t, small_shapes) for t in (d, nm_, nv_))
    for nm in _SMALL:
        grads[nm], delta[nm], new_m[nm], new_v[nm] = g_small[nm], d[nm], nm_[nm], nv_[nm]

    return (loss, grad_x[None], *[grads[nm] for nm in _WEIGHTS], *[delta[nm] for nm in _WEIGHTS],
            *[new_m[nm] for nm in _WEIGHTS], *[new_v[nm] for nm in _WEIGHTS])
```

```python
import functools
import math

import jax
import jax.numpy as jnp
from jax import lax
from jax.experimental import pallas as pl
from jax.experimental.pallas import tpu as pltpu

F32 = jnp.float32
BF = jnp.bfloat16
EPS = 1e-6
D_MODEL = 1024
N_CHIPS = 4
FF_BLOCK = 704
IN_BLOCK = 640
KV_BLOCK = 512
ROW_BLOCK = 256
SGU_GROUPS = 4
CHUNK = 128
SB_HEAD_DIM = 64
SB_SCALE = SB_HEAD_DIM ** -0.5
XA_HEADS = 4
XA_HEAD_DIM = 256
XA_SCALE = XA_HEAD_DIM ** -0.5
LANES = 128
VMEM_LIMIT = 56 * 1024 * 1024
MESH = pl.DeviceIdType.MESH

ADAM_LR = 0.001
ADAM_B1 = 0.9
ADAM_B2 = 0.999
ADAM_EPS = 1e-08
ADAM_WD = 0.01
ADAM_STEP = 10

_GELU_C = math.sqrt(2.0 / math.pi)
_GELU_A = 0.044715


def _cp(*sem):
    return pltpu.CompilerParams(dimension_semantics=sem, vmem_limit_bytes=VMEM_LIMIT)


def _call(body, **kw):
    return pl.pallas_call(body, **kw)


def _call_with_comm(core_body, comm, args, *, name, grid, out_shape, in_specs, out_specs, scratch_shapes, compiler_params,
                    core_aliases=None):
    core_aliases = dict(core_aliases or {})
    if comm is None:
        res = _call(core_body, name=name, grid=grid, out_shape=tuple(out_shape), in_specs=list(in_specs),
                    out_specs=tuple(out_specs), scratch_shapes=list(scratch_shapes), input_output_aliases=core_aliases,
                    compiler_params=compiler_params)(*args)
        return res, ()
    n_in, n_out, n_scr = len(in_specs), len(out_shape), len(scratch_shapes)
    ni, no = len(comm.ins), len(comm.outs)

    def body(*refs):
        core_in, cin = refs[:n_in], refs[n_in:n_in + ni]
        core_out = refs[n_in + ni:n_in + ni + n_out]
        cout = refs[n_in + ni + n_out:n_in + ni + n_out + no]
        scr = refs[n_in + ni + n_out + no:]
        core_scr, send, recv = scr[:n_scr], scr[n_scr], scr[n_scr + 1]
        ids = [pl.program_id(a) for a in range(len(grid))]
        first = functools.reduce(jnp.logical_and, [i == 0 for i in ids])
        last = functools.reduce(jnp.logical_and, [i == g - 1 for i, g in zip(ids, grid)])

        @pl.when(first)
        def _():
            comm.start(cin, cout, send, recv)

        core_body(*core_in, *core_out, *core_scr)

        @pl.when(last)
        def _():
            comm.finish(cin, cout, send, recv)

    hbm = pl.BlockSpec(memory_space=pl.ANY)
    res = _call(
        body, name=name, grid=grid, out_shape=tuple(out_shape) + tuple(comm.outs),
        in_specs=list(in_specs) + [hbm] * ni, out_specs=tuple(out_specs) + (hbm,) * no,
        scratch_shapes=list(scratch_shapes) + [pltpu.SemaphoreType.DMA((comm.n_sems,))] * 2,
        input_output_aliases={**core_aliases, **{n_in + i: n_out + o for i, o in comm.aliases.items()}},
        compiler_params=compiler_params,
    )(*args, *comm.ins)
    return res[:n_out], res[n_out:]


def _dot(a, b):
    return jnp.dot(a, b, preferred_element_type=F32)


def _dot_nt(a, b):
    return lax.dot_general(a, b, (((1,), (1,)), ((), ())), preferred_element_type=F32)


def _dot_tn(a, b):
    return lax.dot_general(a, b, (((0,), (0,)), ((), ())), preferred_element_type=F32)


def _rstd(x):
    return lax.rsqrt(jnp.mean(x * x, axis=-1, keepdims=True) + EPS)


def _rms_bwd(x, g, dy):
    r = _rstd(x)
    xh = x * r
    gd = dy * g
    dx = r * (gd - xh * jnp.mean(gd * xh, axis=-1, keepdims=True))
    dg = jnp.sum(dy * xh, axis=0, keepdims=True)
    return dx, dg


def _gelu(x):
    return 0.5 * x * (1.0 + jnp.tanh(_GELU_C * (x + _GELU_A * (x * x * x))))


def _gelu_grad(x):
    t = jnp.tanh(_GELU_C * (x + _GELU_A * (x * x * x)))
    return 0.5 * (1.0 + t) + 0.5 * x * (1.0 - t * t) * (_GELU_C * (1.0 + 3.0 * _GELU_A * x * x))


def _dot2(x, ones_mat):
    hi = x.astype(BF)
    lo = (x - hi.astype(F32)).astype(BF)
    return _dot(hi, ones_mat) + _dot(lo, ones_mat)


GRAD_ROWS = 2048


def _row_tile(m, want=512):
    return min(want, m)


def _row_parts(tm, nparts=4):
    step = tm // nparts
    return [slice(p * step, (p + 1) * step) for p in range(nparts)]


def rms_fwd(x, g):
    m, d = x.shape
    tm = _row_tile(m)

    def body(x_ref, g_ref, o_ref):
        xv = x_ref[...]
        o_ref[...] = (xv * _rstd(xv) * g_ref[...]).astype(BF)

    return _call(
        body, name="rms_fwd", out_shape=jax.ShapeDtypeStruct((m, d), BF), grid=(m // tm,),
        in_specs=[pl.BlockSpec((tm, d), lambda i: (i, 0)), pl.BlockSpec((1, d), lambda i: (0, 0))],
        out_specs=pl.BlockSpec((tm, d), lambda i: (i, 0)), compiler_params=_cp("parallel"),
    )(x, g)


def ffn_up(a, ga, ig, iu, comm=None):
    m, d = a.shape
    tm = _row_tile(m)
    nb = ga.shape[-1]

    def body(a_ref, wg_ref, wu_ref, g_ref, u_ref, h_ref):
        av = a_ref[...]
        g = _dot(av, wg_ref[...])
        u = _dot(av, wu_ref[...])
        g_ref[...] = g.astype(BF)
        u_ref[...] = u.astype(BF)
        h_ref[...] = (g * jax.nn.sigmoid(g) * u).astype(BF)

    blk = jax.ShapeDtypeStruct((N_CHIPS, m, nb), BF)
    ospec = pl.BlockSpec((None, tm, nb), lambda k, i: (k, i, 0))
    return _call_with_comm(
        body, comm, (a, ga, ga), name="ffn_up", out_shape=(blk, blk, blk), grid=(N_CHIPS, m // tm),
        in_specs=[pl.BlockSpec((tm, d), lambda k, i: (i, 0)),
                  pl.BlockSpec((None, None, d, nb), lambda k, i: (k, ig, 0, 0)),
                  pl.BlockSpec((None, None, d, nb), lambda k, i: (k, iu, 0, 0))],
        out_specs=(ospec, ospec, ospec), scratch_shapes=[],
        compiler_params=_cp("arbitrary", "arbitrary") if comm is not None else _cp("parallel", "parallel"),
    )


def mm_res(lhs, w, which, h, gp, alpha, gn):
    blocked = lhs.ndim == 3
    m = lhs.shape[1] if blocked else lhs.shape[0]
    kb, n = w.shape[2], w.shape[3]
    tm = _row_tile(m)

    def body(l_ref, w_ref, h_ref, gp_ref, gn_ref, f_ref, hn_ref, an_ref):
        parts = _row_parts(tm)
        accs = []
        for rs in parts:
            acc = None
            for k in range(N_CHIPS):
                lk = l_ref[k, rs, :] if blocked else l_ref[rs, k * kb:(k + 1) * kb]
                t = _dot(lk, w_ref[k])
                acc = t if acc is None else acc + t
            accs.append(acc)
        for rs, acc in zip(parts, accs):
            f_ref[rs, :] = acc
            hn = h_ref[rs, :] + alpha * (acc * _rstd(acc) * gp_ref[...])
            hn_ref[rs, :] = hn
            an_ref[rs, :] = (hn * _rstd(hn) * gn_ref[...]).astype(BF)

    lspec = (pl.BlockSpec((N_CHIPS, tm, kb), lambda i: (0, i, 0)) if blocked
             else pl.BlockSpec((tm, N_CHIPS * kb), lambda i: (i, 0)))
    row = pl.BlockSpec((tm, n), lambda i: (i, 0))
    vec = pl.BlockSpec((1, n), lambda i: (0, 0))
    return _call(
        body, name="mm_res", grid=(m // tm,),
        out_shape=(jax.ShapeDtypeStruct((m, n), F32), jax.ShapeDtypeStruct((m, n), F32),
                   jax.ShapeDtypeStruct((m, n), BF)),
        in_specs=[lspec, pl.BlockSpec((N_CHIPS, None, kb, n), lambda i: (0, which, 0, 0)), row, vec, vec],
        out_specs=(row, row, row), compiler_params=_cp("parallel"),
    )(lhs, w, h, gp, gn)


def mm_cb(a, w, out_dtype=BF):
    m, kd = a.shape
    nb = w.shape[-1]
    tm = _row_tile(m)

    def body(a_ref, w_ref, o_ref):
        o_ref[...] = _dot(a_ref[...], w_ref[...]).astype(out_dtype)

    return _call(
        body, name="mm_cb", out_shape=jax.ShapeDtypeStruct((m, N_CHIPS * nb), out_dtype),
        grid=(N_CHIPS, m // tm),
        in_specs=[pl.BlockSpec((tm, kd), lambda k, i: (i, 0)), pl.BlockSpec((None, kd, nb), lambda k, i: (k, 0, 0))],
        out_specs=pl.BlockSpec((tm, nb), lambda k, i: (i, k)), compiler_params=_cp("parallel", "parallel"),
    )(a, w)


def _sgu_core(u_pre, vg_pre, wm_ref, ng_ref, nb_ref, bst_ref, g, c):
    rs = slice(c * CHUNK, (c + 1) * CHUNK)
    cs = slice(g * CHUNK, (g + 1) * CHUNK)
    ug = _gelu(u_pre[rs, cs].astype(F32))
    vgl = _gelu(vg_pre[rs, cs].astype(F32))
    mu = jnp.mean(vgl, axis=-1, keepdims=True)
    cen = vgl - mu
    rstd = lax.rsqrt(jnp.mean(cen * cen, axis=-1, keepdims=True) + EPS)
    xh = cen * rstd
    vn = xh * ng_ref[g:g + 1, :] + nb_ref[g:g + 1, :]
    mixed = _dot(wm_ref[g], vn.astype(BF)) + bst_ref[g]
    return ug, xh, rstd, vn, mixed


def sgu_fwd(proj, wm, ng, nb, bst):
    m = proj.shape[0]
    tm = _row_tile(m)
    wd = SGU_GROUPS * CHUNK

    def body(u_ref, v_ref, wm_ref, ng_ref, nb_ref, bst_ref, o_ref):
        for c in range(tm // CHUNK):
            for g in range(SGU_GROUPS):
                ug, _, _, _, mixed = _sgu_core(u_ref, v_ref, wm_ref, ng_ref, nb_ref, bst_ref, g, c)
                o_ref[c * CHUNK:(c + 1) * CHUNK, g * CHUNK:(g + 1) * CHUNK] = (ug * mixed).astype(BF)

    full = lambda shape: pl.BlockSpec(shape, lambda i: (0,) * len(shape))
    return _call(
        body, name="sgu_fwd", out_shape=jax.ShapeDtypeStruct((m, wd), BF), grid=(m // tm,),
        in_specs=[pl.BlockSpec((tm, wd), lambda i: (i, 0)), pl.BlockSpec((tm, wd), lambda i: (i, 1)),
                  full(wm.shape), full(ng.shape), full(nb.shape), full(bst.shape)],
        out_specs=pl.BlockSpec((tm, wd), lambda i: (i, 0)), compiler_params=_cp("parallel"),
    )(proj, proj, wm, ng, nb, bst)


def _sb_tiles(m):
    tq = min(512, m)
    tk = min(256, m)
    return tq, tk


def _log_sigmoid_pair(z, mask):
    ls = jnp.minimum(z, 0.0) - jnp.log(1.0 + jnp.exp(-jnp.abs(z)))
    l1 = ls - z
    return ls, (l1 if mask is None else jnp.where(mask, l1, 0.0))


def _sb_diag_mask(r0, r1, d, tk):
    rows = r0 + lax.broadcasted_iota(jnp.int32, (r1 - r0, tk), 0)
    cols = d * tk + lax.broadcasted_iota(jnp.int32, (r1 - r0, tk), 1)
    return cols < rows


SB_ROW_PARTS = 2


def _sb_chains(r0, r1):
    part = (r1 - r0) // SB_ROW_PARTS
    return [(hd, r0 + p * part, r0 + (p + 1) * part) for p in range(SB_ROW_PARTS) for hd in range(2)]


def _head_masks():
    lane = lax.broadcasted_iota(jnp.int32, (1, LANES), 1)
    return [lane < SB_HEAD_DIM, lane >= SB_HEAD_DIM]


def sb_fwd(proj, comm=None):
    m = proj.shape[0]
    tq, tk = _sb_tiles(m)
    ndiag = tq // tk

    def body(q_ref, k_ref, v_ref, o_ref, tot_ref, qs, acc, car):
        i = pl.program_id(1)
        nfull = i * ndiag
        heads = _head_masks()
        upper = (lax.broadcasted_iota(jnp.int32, (tk, tk), 0) > lax.broadcasted_iota(jnp.int32, (tk, tk), 1)).astype(BF)
        qv = q_ref[...]
        for hd in range(2):
            qs[hd] = jnp.where(heads[hd], qv, jnp.zeros_like(qv)) * SB_SCALE
        acc[...] = jnp.zeros_like(acc)
        car[...] = jnp.zeros_like(car)

        def block(ks, r0, diag):
            kb = k_ref[pl.ds(ks, tk), :]
            vb = v_ref[pl.ds(ks, tk), :]
            chains = _sb_chains(r0, tq)
            masks = [None if diag is None else _sb_diag_mask(ra, rb, diag, tk) for _, ra, rb in chains]
            zs = [_dot_nt(qs[hd, ra:rb, :], kb) for hd, ra, rb in chains]
            mid = []
            for z, mask in zip(zs, masks):
                ls, l1 = _log_sigmoid_pair(z, mask)
                mid.append((ls, l1, _dot2(l1, upper)))
            pvs = []
            for (hd, ra, rb), (ls, l1, cum), mask in zip(chains, mid, masks):
                a = jnp.exp(ls + (cum + car[hd, ra:rb, :]))
                if mask is not None:
                    a = jnp.where(mask, a, 0.0)
                pvs.append(_dot(a.astype(BF), vb))
            for (hd, ra, rb), (ls, l1, cum), pv in zip(chains, mid, pvs):
                acc[hd, ra:rb, :] += pv
                car[hd, ra:rb, :] += jnp.sum(l1, axis=-1, keepdims=True)

        for d in reversed(range(ndiag)):
            block(pl.multiple_of((nfull + d) * tk, tk), d * tk, d)

        def step(jj, carry):
            block(pl.multiple_of((nfull - 1 - jj) * tk, tk), 0, None)
            return carry

        lax.fori_loop(0, nfull, step, 0)
        o_ref[...] = jnp.where(heads[0], acc[0], acc[1]).astype(BF)
        tot_ref[...] = jnp.where(heads[0], car[0], car[1])

    qb = 2 * 512 // LANES
    return _call_with_comm(
        body, comm, (proj, proj, proj), name="sb_fwd", grid=(4, m // tq),
        out_shape=(jax.ShapeDtypeStruct((m, 512), BF), jax.ShapeDtypeStruct((m, 512), F32)),
        in_specs=[pl.BlockSpec((tq, LANES), lambda p, i: (i, qb + p)),
                  pl.BlockSpec((m, LANES), lambda p, i: (0, qb + 4 + p)),
                  pl.BlockSpec((m, LANES), lambda p, i: (0, qb + 8 + p))],
        out_specs=(pl.BlockSpec((tq, LANES), lambda p, i: (i, p)), pl.BlockSpec((tq, LANES), lambda p, i: (i, p))),
        scratch_shapes=[pltpu.VMEM((2, tq, LANES), BF), pltpu.VMEM((2, tq, LANES), F32), pltpu.VMEM((2, tq, 1), F32)],
        compiler_params=_cp("arbitrary", "arbitrary"),
    )


def merge_norm(oa, ob, ga, gb):
    m, w = oa.shape
    tm = _row_tile(m)

    def body(a_ref, b_ref, ga_ref, gb_ref, o_ref):
        av = a_ref[...].astype(F32)
        bv = b_ref[...].astype(F32)
        o_ref[:, :w] = (av * _rstd(av) * ga_ref[...]).astype(BF)
        o_ref[:, w:] = (bv * _rstd(bv) * gb_ref[...]).astype(BF)

    row = pl.BlockSpec((tm, w), lambda i: (i, 0))
    vec = pl.BlockSpec((1, w), lambda i: (0, 0))
    return _call(
        body, name="merge_norm", out_shape=jax.ShapeDtypeStruct((m, 2 * w), BF), grid=(m // tm,),
        in_specs=[row, row, vec, vec], out_specs=pl.BlockSpec((tm, 2 * w), lambda i: (i, 0)),
        compiler_params=_cp("parallel"),
    )(oa, ob, ga, gb)


def _xa_probs(qh, kh):
    logits = _dot_nt(qh, kh) * XA_SCALE
    e = jnp.exp(logits - jnp.max(logits, axis=-1, keepdims=True))
    return e / jnp.sum(e, axis=-1, keepdims=True)


def xa_fwd(xq, kv):
    m, d = xq.shape
    mm = kv.shape[0]
    tm = _row_tile(m)

    def body(q_ref, kv_ref, o_ref):
        for hd in range(XA_HEADS):
            cs = slice(hd * XA_HEAD_DIM, (hd + 1) * XA_HEAD_DIM)
            p = _xa_probs(q_ref[:, cs], kv_ref[:, cs])
            vh = kv_ref[:, d + hd * XA_HEAD_DIM:d + (hd + 1) * XA_HEAD_DIM]
            o_ref[:, cs] = _dot(p.astype(BF), vh).astype(BF)

    return _call(
        body, name="xa_fwd", out_shape=jax.ShapeDtypeStruct((m, d), BF), grid=(m // tm,),
        in_specs=[pl.BlockSpec((tm, d), lambda i: (i, 0)), pl.BlockSpec((mm, 2 * d), lambda i: (0, 0))],
        out_specs=pl.BlockSpec((tm, d), lambda i: (i, 0)), compiler_params=_cp("parallel"),
    )(xq, kv)


def norm_bwd(h, gn, d_a=None, d_res=None, target=None, f_prev=None, gp_prev=None, alpha_prev=1.0):
    m, d = h.shape
    tm = _row_tile(m)
    has_loss = target is not None
    has_res = d_res is not None
    has_prev = f_prev is not None

    def body(*refs):
        refs = list(refs)
        h_ref, gn_ref = refs[0], refs[1]
        pos = 2
        da_ref = dres_ref = t_ref = f_ref = gp_ref = None
        if has_loss:
            t_ref = refs[pos]; pos += 1
        else:
            da_ref = refs[pos]; pos += 1
        if has_res:
            dres_ref = refs[pos]; pos += 1
        if has_prev:
            f_ref, gp_ref = refs[pos], refs[pos + 1]; pos += 2
        dh_ref, dgn_ref = refs[pos], refs[pos + 1]; pos += 2
        df_ref = dgp_ref = loss_ref = None
        if has_prev:
            df_ref, dgp_ref = refs[pos], refs[pos + 1]; pos += 2
        if has_loss:
            loss_ref = refs[pos]

        first = pl.program_id(0) == 0
        hv = h_ref[...]
        gn = gn_ref[...]
        if has_loss:
            err = hv * _rstd(hv) * gn - t_ref[...]
            da = err * (1.0 / d)
            part = 0.5 * jnp.sum(jnp.sum(err * err, axis=-1, keepdims=True) * (1.0 / d))

            @pl.when(first)
            def _():
                loss_ref[...] = jnp.zeros_like(loss_ref)

            loss_ref[...] += part
        else:
            da = da_ref[...].astype(F32)
        dx, dgn = _rms_bwd(hv, gn, da)
        dh = dx + dres_ref[...] if has_res else dx
        dh_ref[...] = dh

        @pl.when(first)
        def _():
            dgn_ref[...] = jnp.zeros_like(dgn_ref)

        dgn_ref[...] += dgn
        if has_prev:
            dfv, dgp = _rms_bwd(f_ref[...], gp_ref[...], dh)
            df_ref[...] = (alpha_prev * dfv).astype(BF)

            @pl.when(first)
            def _():
                dgp_ref[...] = jnp.zeros_like(dgp_ref)

            dgp_ref[...] += alpha_prev * dgp

    row = pl.BlockSpec((tm, d), lambda i: (i, 0))
    vec = pl.BlockSpec((1, d), lambda i: (0, 0))
    ins, in_specs = [h, gn], [row, vec]
    ins.append(target if has_loss else d_a); in_specs.append(row)
    if has_res:
        ins.append(d_res); in_specs.append(row)
    if has_prev:
        ins += [f_prev, gp_prev]; in_specs += [row, vec]
    outs = [jax.ShapeDtypeStruct((m, d), F32), jax.ShapeDtypeStruct((1, d), F32)]
    out_specs = [row, vec]
    names = ["d_h", "d_gn"]
    if has_prev:
        outs += [jax.ShapeDtypeStruct((m, d), BF), jax.ShapeDtypeStruct((1, d), F32)]
        out_specs += [row, vec]
        names += ["d_f", "d_gp"]
    if has_loss:
        outs.append(jax.ShapeDtypeStruct((8, LANES), F32))
        out_specs.append(pl.BlockSpec((8, LANES), lambda i: (0, 0)))
        names.append("loss")
    res = _call(
        body, name="norm_bwd", out_shape=tuple(outs), grid=(m // tm,), in_specs=in_specs,
        out_specs=tuple(out_specs), compiler_params=_cp("arbitrary"),
    )(*ins)
    return dict(zip(names, res))


def ffn_bwd_act(df, gb, which, g, u):
    m, d = df.shape
    nb = g.shape[-1]
    tm = _row_tile(m, 1024)

    def body(df_ref, w_ref, g_ref, u_ref, dg_ref, du_ref):
        parts = _row_parts(tm, 2)
        dhs = [_dot_nt(df_ref[rs, :], w_ref[...]) for rs in parts]
        for rs, dh in zip(parts, dhs):
            gv = g_ref[rs, :].astype(F32)
            uv = u_ref[rs, :].astype(F32)
            s = jax.nn.sigmoid(gv)
            dg_ref[rs, :] = (dh * uv * (s * (1.0 + gv * (1.0 - s)))).astype(BF)
            du_ref[rs, :] = (dh * gv * s).astype(BF)

    blk = jax.ShapeDtypeStruct((N_CHIPS, m, nb), BF)
    aspec = pl.BlockSpec((None, tm, nb), lambda k, i: (k, i, 0))
    return _call(
        body, name="ffn_bwd_act", out_shape=(blk, blk), grid=(N_CHIPS, m // tm),
        in_specs=[pl.BlockSpec((tm, d), lambda k, i: (i, 0)),
                  pl.BlockSpec((None, None, nb, d), lambda k, i: (k, which, 0, 0)), aspec, aspec],
        out_specs=(aspec, aspec), compiler_params=_cp("parallel", "parallel"),
    )(df, gb, g, u)


def mm_tn(a, b, dest, a_spec, b_spec, o_spec, acc_shape, msteps, comm=None):
    def body(a_ref, b_ref, dest_ref, o_ref, acc):
        del dest_ref
        ms = pl.program_id(1)

        @pl.when(ms == 0)
        def _():
            acc[...] = jnp.zeros_like(acc)

        acc[...] += _dot_tn(a_ref[...], b_ref[...])

        @pl.when(ms == msteps - 1)
        def _():
            o_ref[...] = acc[...].astype(o_ref.dtype)

    (out,), extra = _call_with_comm(
        body, comm, (a, b, dest), name="mm_tn", out_shape=(jax.ShapeDtypeStruct(dest.shape, dest.dtype),),
        grid=(N_CHIPS, msteps), in_specs=[a_spec, b_spec, pl.BlockSpec(memory_space=pl.ANY)], out_specs=(o_spec,),
        scratch_shapes=[pltpu.VMEM(acc_shape, F32)], core_aliases={2: 0},
        compiler_params=_cp("arbitrary", "arbitrary") if comm is not None else _cp("parallel", "arbitrary"),
    )
    return out if comm is None else (out, extra)


def _act_spec(arr, tm, nb):
    if arr.ndim == 3:
        return pl.BlockSpec((None, tm, nb), lambda k, ms: (k, ms, 0))
    return pl.BlockSpec((tm, nb), lambda k, ms: (ms, k))


def grad_cb(a, dout, dest, which=None):
    m, kd = a.shape
    nb = dest.shape[-1]
    tm = _row_tile(m, GRAD_ROWS)
    if which is None:
        o_spec = pl.BlockSpec((None, kd, nb), lambda k, ms: (k, 0, 0))
    else:
        o_spec = pl.BlockSpec((None, None, kd, nb), lambda k, ms: (k, which, 0, 0))
    return mm_tn(a, dout, dest, pl.BlockSpec((tm, kd), lambda k, ms: (ms, 0)), _act_spec(dout, tm, nb), o_spec,
                 (kd, nb), m // tm)


def grad_rb(a, dout, dest, which, comm=None):
    m, n = dout.shape
    kb = dest.shape[-2]
    tm = _row_tile(m, GRAD_ROWS)
    o_spec = pl.BlockSpec((None, None, kb, n), lambda k, ms: (k, which, 0, 0))
    return mm_tn(a, dout, dest, _act_spec(a, tm, kb), pl.BlockSpec((tm, n), lambda k, ms: (ms, 0)), o_spec,
                 (kb, n), m // tm, comm=comm)


def _norm_n_in(norm):
    return 3 + (2 if norm.get("f_prev") is not None else 0)


def _norm_join_operands(norm, tm, row_map, vec_map):
    h = norm["h"]
    m, d = h.shape
    row, vec = pl.BlockSpec((tm, d), row_map), pl.BlockSpec((1, d), vec_map)
    ins, in_specs = [h, norm["gn"], norm["d_res"]], [row, vec, row]
    outs = [jax.ShapeDtypeStruct((m, d), F32), jax.ShapeDtypeStruct((1, d), F32)]
    out_specs, names = [row, vec], ["d_h", "d_gn"]
    if norm.get("f_prev") is not None:
        ins += [norm["f_prev"], norm["gp_prev"]]
        in_specs += [row, vec]
        outs += [jax.ShapeDtypeStruct((m, d), BF), jax.ShapeDtypeStruct((1, d), F32)]
        out_specs += [row, vec]
        names += ["d_f", "d_gp"]
    return ins, in_specs, outs, out_specs, names


def _norm_join(da, in_refs, out_refs, norm, first):
    h_ref, gn_ref, dres_ref = in_refs[:3]
    dh_ref, dgn_ref = out_refs[:2]
    has_prev = norm.get("f_prev") is not None
    alpha = norm.get("alpha_prev", 1.0)

    @pl.when(first)
    def _():
        dgn_ref[...] = jnp.zeros_like(dgn_ref)
        if has_prev:
            out_refs[3][...] = jnp.zeros_like(out_refs[3])

    dx, dgn = _rms_bwd(h_ref[...], gn_ref[...], da)
    dh = dx + dres_ref[...]
    dh_ref[...] = dh
    dgn_ref[...] += dgn
    if has_prev:
        dfv, dgp = _rms_bwd(in_refs[3][...], in_refs[4][...], dh)
        out_refs[2][...] = (alpha * dfv).astype(BF)
        out_refs[3][...] += alpha * dgp


def mm_nt_cb(pairs, n, out_dtype, comm=None, norm=None):
    d0 = pairs[0][0]
    m = d0.shape[1] if d0.ndim == 3 else d0.shape[0]
    tm = _row_tile(m, 1024 if norm is None else 512)
    npair = len(pairs)
    n_norm_in = 0 if norm is None else _norm_n_in(norm)

    def body(*refs):
        outs = refs[2 * npair + n_norm_in:-1]
        acc = refs[-1]
        i, k = pl.program_id(0), pl.program_id(1)

        @pl.when(k == 0)
        def _():
            acc[...] = jnp.zeros_like(acc)

        for p in range(npair):
            acc[...] += _dot_nt(refs[2 * p][...], refs[2 * p + 1][...])

        @pl.when(k == N_CHIPS - 1)
        def _():
            if norm is None:
                outs[0][...] = acc[...].astype(out_dtype)
            else:
                _norm_join(acc[...], refs[2 * npair:2 * npair + n_norm_in], outs, norm, i == 0)

    ins, in_specs = [], []
    for dout, w, which in pairs:
        nb = w.shape[-1]
        if dout.ndim == 3:
            in_specs.append(pl.BlockSpec((None, tm, nb), lambda i, k: (k, i, 0)))
        else:
            in_specs.append(pl.BlockSpec((tm, nb), lambda i, k: (i, k)))
        if w.ndim == 4:
            in_specs.append(pl.BlockSpec((None, None, n, nb), lambda i, k, which=which: (k, which, 0, 0)))
        else:
            in_specs.append(pl.BlockSpec((None, n, nb), lambda i, k: (k, 0, 0)))
        ins += [dout, w]
    if norm is None:
        out_shape = (jax.ShapeDtypeStruct((m, n), out_dtype),)
        out_specs = (pl.BlockSpec((tm, n), lambda i, k: (i, 0)),)
        names = None
    else:
        n_ins, n_specs, out_shape, out_specs, names = _norm_join_operands(norm, tm, lambda i, k: (i, 0), lambda i, k: (0, 0))
        ins += n_ins
        in_specs += n_specs
    sequential = comm is not None or norm is not None
    res, extra = _call_with_comm(
        body, comm, tuple(ins), name="mm_nt_cb", out_shape=tuple(out_shape), grid=(m // tm, N_CHIPS),
        in_specs=in_specs, out_specs=tuple(out_specs), scratch_shapes=[pltpu.VMEM((tm, n), F32)],
        compiler_params=_cp("arbitrary", "arbitrary") if sequential else _cp("parallel", "arbitrary"),
    )
    out = res[0] if norm is None else dict(zip(names, res))
    return out if comm is None else (out, extra)


def mm_nt_rb(dout, w, which, out_dtype, norm=None):
    m, n = dout.shape
    kb = w.shape[2]
    tm = _row_tile(m)

    def body(d_ref, w_ref, *rest):
        da = _dot_nt(d_ref[...], w_ref[...].reshape(N_CHIPS * kb, n))
        if norm is None:
            rest[0][...] = da.astype(out_dtype)
        else:
            _norm_join(da, rest[:_norm_n_in(norm)], rest[_norm_n_in(norm):], norm, pl.program_id(0) == 0)

    ins = [dout, w]
    in_specs = [pl.BlockSpec((tm, n), lambda i: (i, 0)), pl.BlockSpec((N_CHIPS, None, kb, n), lambda i: (0, which, 0, 0))]
    if norm is None:
        return _call(
            body, name="mm_nt_rb", out_shape=jax.ShapeDtypeStruct((m, N_CHIPS * kb), out_dtype), grid=(m // tm,),
            in_specs=in_specs, out_specs=pl.BlockSpec((tm, N_CHIPS * kb), lambda i: (i, 0)),
            compiler_params=_cp("parallel"),
        )(*ins)
    n_ins, n_specs, out_shape, out_specs, names = _norm_join_operands(norm, tm, lambda i: (i, 0), lambda i: (0, 0))
    res = _call(
        body, name="mm_nt_rb", out_shape=tuple(out_shape), grid=(m // tm,), in_specs=in_specs + n_specs,
        out_specs=tuple(out_specs), compiler_params=_cp("arbitrary"),
    )(*ins, *n_ins)
    return dict(zip(names, res))


def merge_norm_bwd(oa, ob, dmerged, ga, gb):
    m, w = oa.shape
    tm = _row_tile(m)

    def body(a_ref, b_ref, dm_ref, ga_ref, gb_ref, da_ref, db_ref, dga_ref, dgb_ref):
        @pl.when(pl.program_id(0) == 0)
        def _():
            dga_ref[...] = jnp.zeros_like(dga_ref)
            dgb_ref[...] = jnp.zeros_like(dgb_ref)

        da, dga = _rms_bwd(a_ref[...].astype(F32), ga_ref[...], dm_ref[:, :w].astype(F32))
        db, dgb = _rms_bwd(b_ref[...].astype(F32), gb_ref[...], dm_ref[:, w:].astype(F32))
        da_ref[...] = da
        db_ref[...] = db
        dga_ref[...] += dga
        dgb_ref[...] += dgb

    row = pl.BlockSpec((tm, w), lambda i: (i, 0))
    vec = pl.BlockSpec((1, w), lambda i: (0, 0))
    return _call(
        body, name="merge_norm_bwd", grid=(m // tm,),
        out_shape=(jax.ShapeDtypeStruct((m, w), F32), jax.ShapeDtypeStruct((m, w), F32),
                   jax.ShapeDtypeStruct((1, w), F32), jax.ShapeDtypeStruct((1, w), F32)),
        in_specs=[row, row, pl.BlockSpec((tm, 2 * w), lambda i: (i, 0)), vec, vec],
        out_specs=(row, row, vec, vec), compiler_params=_cp("arbitrary"),
    )(oa, ob, dmerged, ga, gb)


def sgu_bwd(proj, d_oa, wm, wmt, ng, nb, bst):
    m = proj.shape[0]
    tm = _row_tile(m)
    wd = SGU_GROUPS * CHUNK

    def body(u_ref, v_ref, do_ref, wm_ref, wmt_ref, ng_ref, nb_ref, bst_ref,
             dp_ref, dw_ref, dbt_ref, dng_ref, dnb_ref):
        @pl.when(pl.program_id(0) == 0)
        def _():
            dw_ref[...] = jnp.zeros_like(dw_ref)
            dbt_ref[...] = jnp.zeros_like(dbt_ref)
            dng_ref[...] = jnp.zeros_like(dng_ref)
            dnb_ref[...] = jnp.zeros_like(dnb_ref)

        causal = lax.broadcasted_iota(jnp.int32, (CHUNK, CHUNK), 0) >= lax.broadcasted_iota(jnp.int32, (CHUNK, CHUNK), 1)
        for c in range(tm // CHUNK):
            rs = slice(c * CHUNK, (c + 1) * CHUNK)
            for g in range(SGU_GROUPS):
                cs = slice(g * CHUNK, (g + 1) * CHUNK)
                ug, xh, rstd, vn, mixed = _sgu_core(u_ref, v_ref, wm_ref, ng_ref, nb_ref, bst_ref, g, c)
                do = do_ref[rs, cs]
                dug = do * mixed
                dmix = do * ug
                dmb = dmix.astype(BF)
                dbt_ref[g] += jnp.sum(dmix, axis=-1, keepdims=True)
                dw_ref[g] += jnp.where(causal, _dot_nt(dmb, vn.astype(BF)), 0.0)
                dvn = _dot(wmt_ref[g], dmb)
                dng_ref[g:g + 1, :] += jnp.sum(dvn * xh, axis=0, keepdims=True)
                dnb_ref[g:g + 1, :] += jnp.sum(dvn, axis=0, keepdims=True)
                dxh = dvn * ng_ref[g:g + 1, :]
                dvg = rstd * (dxh - jnp.mean(dxh, axis=-1, keepdims=True)
                              - xh * jnp.mean(dxh * xh, axis=-1, keepdims=True))
                dp_ref[rs, cs] = (dug * _gelu_grad(u_ref[rs, cs].astype(F32))).astype(BF)
                dp_ref[rs, wd + g * CHUNK:wd + (g + 1) * CHUNK] = (dvg * _gelu_grad(v_ref[rs, cs].astype(F32))).astype(BF)

    full = lambda shape: pl.BlockSpec(shape, lambda i: (0,) * len(shape))
    return _call(
        body, name="sgu_bwd", grid=(m // tm,),
        out_shape=(jax.ShapeDtypeStruct((m, 2 * wd), BF), jax.ShapeDtypeStruct(wm.shape, F32),
                   jax.ShapeDtypeStruct(bst.shape, F32), jax.ShapeDtypeStruct(ng.shape, F32),
                   jax.ShapeDtypeStruct(nb.shape, F32)),
        in_specs=[pl.BlockSpec((tm, wd), lambda i: (i, 0)), pl.BlockSpec((tm, wd), lambda i: (i, 1)),
                  pl.BlockSpec((tm, wd), lambda i: (i, 0)),
                  full(wm.shape), full(wmt.shape), full(ng.shape), full(nb.shape), full(bst.shape)],
        out_specs=(pl.BlockSpec((tm, 2 * wd), lambda i: (i, 0)), full(wm.shape), full(bst.shape), full(ng.shape),
                   full(nb.shape)),
        compiler_params=_cp("arbitrary"),
    )(proj, proj, d_oa, wm, wmt, ng, nb, bst)


def sb_bwd(proj, tot, d_ob, comm=None):
    m = proj.shape[0]
    tq, tk = _sb_tiles(m)

    ndiag = tq // tk

    def body(q_ref, k_ref, v_ref, tot_ref, do_ref, dq_ref, dk_ref, dv_ref, qs, dos, tots, dqa, cl1, cg):
        i = pl.program_id(1)

        @pl.when(i == 0)
        def _():
            dk_ref[...] = jnp.zeros_like(dk_ref)
            dv_ref[...] = jnp.zeros_like(dv_ref)

        nfull = i * ndiag
        heads = _head_masks()
        r_io = lax.broadcasted_iota(jnp.int32, (tk, tk), 0)
        c_io = lax.broadcasted_iota(jnp.int32, (tk, tk), 1)
        incl = (r_io <= c_io).astype(BF)
        excl = (r_io < c_io).astype(BF)
        qv = q_ref[...]
        dov = do_ref[...].astype(BF)
        totv = tot_ref[...]
        for hd in range(2):
            qs[hd] = jnp.where(heads[hd], qv, jnp.zeros_like(qv)) * SB_SCALE
            dos[hd] = jnp.where(heads[hd], dov, jnp.zeros_like(dov))
            tots[hd] = jnp.max(jnp.where(heads[hd], totv, -jnp.inf), axis=-1, keepdims=True)
        dqa[...] = jnp.zeros_like(dqa)
        cl1[...] = jnp.zeros_like(cl1)
        cg[...] = jnp.zeros_like(cg)

        def block(ks, r0, diag):
            kb = k_ref[pl.ds(ks, tk), :]
            vb = v_ref[pl.ds(ks, tk), :]
            chains = _sb_chains(r0, tq)
            masks = [None if diag is None else _sb_diag_mask(ra, rb, diag, tk) for _, ra, rb in chains]
            qc = [qs[hd, ra:rb, :] for hd, ra, rb in chains]
            doc = [dos[hd, ra:rb, :] for hd, ra, rb in chains]
            zs = [_dot_nt(q, kb) for q in qc]
            das = [_dot_nt(do, vb) for do in doc]
            s1 = []
            for z, mask in zip(zs, masks):
                ls, l1 = _log_sigmoid_pair(z, mask)
                s1.append((ls, l1, _dot2(l1, incl)))
            s2 = []
            for (hd, ra, rb), (ls, l1, pre), da, mask in zip(chains, s1, das, masks):
                a = jnp.exp(ls + (tots[hd, ra:rb, :] - (pre + cl1[hd, ra:rb, :])))
                if mask is not None:
                    a = jnp.where(mask, a, 0.0)
                gmat = a * da
                s2.append((a, gmat, _dot2(gmat, excl)))
            dk_sum = dv_sum = None
            for n, (hd, ra, rb) in enumerate(chains):
                a, gmat, pref = s2[n]
                sg = jnp.exp(s1[n][0])
                dz = gmat * (1.0 - sg) - (pref + cg[hd, ra:rb, :]) * sg
                if masks[n] is not None:
                    dz = jnp.where(masks[n], dz, 0.0)
                dz = dz.astype(BF)
                dqa[hd, ra:rb, :] += _dot(dz, kb)
                dk_t = _dot_tn(dz, qc[n])
                dv_t = _dot_tn(a.astype(BF), doc[n])
                dk_sum = dk_t if dk_sum is None else dk_sum + dk_t
                dv_sum = dv_t if dv_sum is None else dv_sum + dv_t
            for n, (hd, ra, rb) in enumerate(chains):
                cl1[hd, ra:rb, :] += jnp.sum(s1[n][1], axis=-1, keepdims=True)
                cg[hd, ra:rb, :] += jnp.sum(s2[n][1], axis=-1, keepdims=True)
            dk_ref[pl.ds(ks, tk), :] += dk_sum
            dv_ref[pl.ds(ks, tk), :] += dv_sum

        def step(j, carry):
            block(pl.multiple_of(j * tk, tk), 0, None)
            return carry

        lax.fori_loop(0, nfull, step, 0)
        for d in range(ndiag):
            block(pl.multiple_of((nfull + d) * tk, tk), d * tk, d)
        dq_ref[...] = jnp.where(heads[0], dqa[0], dqa[1]) * SB_SCALE

    qb = 2 * 512 // LANES
    tile = pl.BlockSpec((tq, LANES), lambda p, i: (i, p))
    seq = pl.BlockSpec((m, LANES), lambda p, i: (0, p))
    out = jax.ShapeDtypeStruct((m, 512), F32)
    return _call_with_comm(
        body, comm, (proj, proj, proj, tot, d_ob), name="sb_bwd", grid=(4, m // tq), out_shape=(out, out, out),
        in_specs=[pl.BlockSpec((tq, LANES), lambda p, i: (i, qb + p)),
                  pl.BlockSpec((m, LANES), lambda p, i: (0, qb + 4 + p)),
                  pl.BlockSpec((m, LANES), lambda p, i: (0, qb + 8 + p)), tile, tile],
        out_specs=(tile, seq, seq),
        scratch_shapes=[pltpu.VMEM((2, tq, LANES), BF), pltpu.VMEM((2, tq, LANES), BF), pltpu.VMEM((2, tq, 1), F32),
                        pltpu.VMEM((2, tq, LANES), F32), pltpu.VMEM((2, tq, 1), F32), pltpu.VMEM((2, tq, 1), F32)],
        compiler_params=_cp("arbitrary", "arbitrary"),
    )


def xa_bwd(xq, kv, d_o):
    m, d = xq.shape
    mm = kv.shape[0]
    tm = _row_tile(m)

    def body(q_ref, kv_ref, do_ref, dq_ref, dkv_ref):
        @pl.when(pl.program_id(0) == 0)
        def _():
            dkv_ref[...] = jnp.zeros_like(dkv_ref)

        for hd in range(XA_HEADS):
            cs = slice(hd * XA_HEAD_DIM, (hd + 1) * XA_HEAD_DIM)
            vs = slice(d + hd * XA_HEAD_DIM, d + (hd + 1) * XA_HEAD_DIM)
            qh = q_ref[:, cs]
            kh = kv_ref[:, cs]
            doh = do_ref[:, cs]
            p = _xa_probs(qh, kh)
            dp = _dot_nt(doh, kv_ref[:, vs])
            ds = (p * (dp - jnp.sum(p * dp, axis=-1, keepdims=True))).astype(BF)
            dq_ref[:, cs] = (_dot(ds, kh) * XA_SCALE).astype(BF)
            dkv_ref[:, cs] += _dot_tn(ds, qh) * XA_SCALE
            dkv_ref[:, vs] += _dot_tn(p.astype(BF), doh)

    row = pl.BlockSpec((tm, d), lambda i: (i, 0))
    whole = pl.BlockSpec((mm, 2 * d), lambda i: (0, 0))
    return _call(
        body, name="xa_bwd", grid=(m // tm,),
        out_shape=(jax.ShapeDtypeStruct((m, d), BF), jax.ShapeDtypeStruct((mm, 2 * d), F32)),
        in_specs=[row, whole, row], out_specs=(row, whole), compiler_params=_cp("arbitrary"),
    )(xq, kv, d_o)


def adamw(w, g, mom, vel, g_index=None):
    r, c = w.shape
    tr = r
    for cand in range(512, 7, -8):
        if r % cand == 0:
            tr = cand
            break

    def body(w_ref, g_ref, m_ref, v_ref, go_ref, d_ref, nm_ref, nv_ref):
        gv = g_ref[...]
        mn = ADAM_B1 * m_ref[...] + (1.0 - ADAM_B1) * gv
        vn = ADAM_B2 * v_ref[...] + (1.0 - ADAM_B2) * (gv * gv)
        m_hat = mn / (1.0 - ADAM_B1 ** ADAM_STEP)
        v_hat = vn / (1.0 - ADAM_B2 ** ADAM_STEP)
        go_ref[...] = gv
        d_ref[...] = -ADAM_LR * (m_hat / (jnp.sqrt(v_hat) + ADAM_EPS) + ADAM_WD * w_ref[...])
        nm_ref[...] = mn
        nv_ref[...] = vn

    spec = pl.BlockSpec((tr, c), lambda i: (i, 0))
    gspec = spec if g_index is None else pl.BlockSpec((None, tr, c), lambda i: (g_index, i, 0))
    out = jax.ShapeDtypeStruct((r, c), F32)
    return _call(
        body, name="adamw", out_shape=(out, out, out, out), grid=(r // tr,), in_specs=[spec, gspec, spec, spec],
        out_specs=(spec, spec, spec, spec), compiler_params=_cp("parallel"),
    )(w, g, mom, vel)


def _place():
    x, y, c = lax.axis_index("x"), lax.axis_index("y"), lax.axis_index("c")
    others = [(1 - x, y), (x, 1 - y), (1 - x, 1 - y)]
    return x, y, c, others


_HALF = {"A1": (1, 512), "A2": (1, 512), "B1": (0, 352), "B2": (0, 352), "C": (1, 128), "D": (0, 512), "E": (0, 512),
         "A1T": (1, 352), "A2T": (1, 352)}
SET_EARLY = ("A1", "B1", "D")
SET_LATE = ("A2", "B2", "C", "E")
GRAD_EARLY = ("A1T", "B1", "D")
GRAD_LATE = ("A2T", "B2", "C", "E")
_HBM = pl.BlockSpec(memory_space=pl.ANY)


def _half_of(ref, name, hc, lead=0):
    axis, size = _HALF[name]
    idx = [slice(None)] * (lead + axis) + [pl.ds(hc * size, size)]
    return ref.at[tuple(idx)]


def _gather_slots(loc, names):
    me = 2 * lax.axis_index("x") + lax.axis_index("y")
    init = []
    for nm in names:
        full = lax.empty((N_CHIPS,) + loc[nm].shape, loc[nm].dtype)
        init.append(lax.dynamic_update_slice(full, loc[nm][None], (me,) + (0,) * loc[nm].ndim))
    return init


def _gather_ici(names, src, out, send, recv, sends=True, arrivals=True):
    x, y, c, others = _place()
    me = 2 * x + y
    out_sends, out_arrivals = [], []
    for a, nm in enumerate(names):
        for j, (ox, oy) in enumerate(others):
            sems = dict(send_sem=send.at[3 * a + j], recv_sem=recv.at[3 * a + j], device_id_type=MESH)
            if sends:
                out_sends.append(pltpu.make_async_remote_copy(
                    src_ref=_half_of(src[a], nm, c), dst_ref=_half_of(out[a].at[me], nm, c), device_id=(ox, oy, c),
                    **sems))
            if arrivals:
                landed = _half_of(out[a].at[2 * ox + oy], nm, c)
                out_arrivals.append(pltpu.make_async_remote_copy(src_ref=landed, dst_ref=landed, device_id=(x, y, c),
                                                                 **sems))
    return out_sends, out_arrivals


def _gather_d2d(names, given, out, send, recv):
    x, y, c, others = _place()
    sends, arrivals = [], []
    for a, nm in enumerate(names):
        for j, (ox, oy) in enumerate(others):
            sems = dict(send_sem=send.at[3 * a + j], recv_sem=recv.at[3 * a + j], device_id_type=MESH)
            sends.append(pltpu.make_async_remote_copy(
                src_ref=_half_of(given[a].at[2 * ox + oy], nm, c), dst_ref=_half_of(out[a].at[2 * ox + oy], nm, c),
                device_id=(x, y, 1 - c), **sems))
            landed = _half_of(out[a].at[2 * ox + oy], nm, 1 - c)
            arrivals.append(pltpu.make_async_remote_copy(src_ref=landed, dst_ref=landed, device_id=(x, y, c), **sems))
    return sends, arrivals


def gather_weights(loc, names):
    n = len(names)

    def body(*refs):
        src, given, out = refs[:n], refs[n:2 * n], refs[2 * n:3 * n]
        send1, recv1, send2, recv2 = refs[3 * n:3 * n + 4]
        first, landed = _gather_ici(names, src, out, send1, recv1)
        del given
        passed, arrivals = _gather_d2d(names, out, out, send2, recv2)
        for f in first:
            f.start()
        for l, p in zip(landed, passed):
            l.wait_recv()
            p.start()
        for a in arrivals:
            a.wait_recv()
        for f in first + passed:
            f.wait_send()

    init = _gather_slots(loc, names)
    res = _call(
        body, name="gather_weights", out_shape=tuple(jax.ShapeDtypeStruct(t.shape, t.dtype) for t in init),
        in_specs=[_HBM] * (2 * n), out_specs=(_HBM,) * n, input_output_aliases={n + a: a for a in range(n)},
        scratch_shapes=[pltpu.SemaphoreType.DMA((3 * n,))] * 4,
    )(*[loc[nm] for nm in names], *init)
    return dict(zip(names, res))


def gather_forward(bufs, names):
    n = len(names)

    def body(*refs):
        given, out = refs[:n], refs[n:2 * n]
        passed, arrivals = _gather_d2d(names, given, out, refs[2 * n], refs[2 * n + 1])
        for p in passed:
            p.start()
        for a in arrivals:
            a.wait_recv()
        for p in passed:
            p.wait_send()

    res = _call(
        body, name="gather_forward", out_shape=tuple(jax.ShapeDtypeStruct(t.shape, t.dtype) for t in bufs),
        in_specs=[_HBM] * n, out_specs=(_HBM,) * n, input_output_aliases={a: a for a in range(n)},
        scratch_shapes=[pltpu.SemaphoreType.DMA((3 * n,))] * 2,
    )(*bufs)
    return dict(zip(names, res))


class FusedComm:
    def __init__(self, ins, outs, aliases, n_sems, start, finish):
        self.ins, self.outs, self.aliases, self.n_sems, self.start, self.finish = ins, outs, aliases, n_sems, start, finish


def gather_comm(loc, names):
    n = len(names)

    def start(ins, outs, send, recv):
        for f in _gather_ici(names, ins[:n], outs, send, recv, arrivals=False)[0]:
            f.start()

    def finish(ins, outs, send, recv):
        first, landed = _gather_ici(names, ins[:n], outs, send, recv)
        for l in landed:
            l.wait_recv()
        for f in first:
            f.wait_send()

    init = _gather_slots(loc, names)
    return FusedComm([loc[nm] for nm in names] + init, [jax.ShapeDtypeStruct(t.shape, t.dtype) for t in init],
                     {n + a: a for a in range(n)}, 3 * n, start, finish)


def _half_shape(name, shape):
    axis, size = _HALF[name]
    s = list(shape)
    s[axis] = size
    return tuple(s)


def rs_to_sibling(grads, names):
    n = len(names)

    def body(*refs):
        src = dict(zip(names, refs[:n]))
        out = dict(zip(names, refs[n:2 * n]))
        send, recv = refs[2 * n], refs[2 * n + 1]
        x, y, c, _ = _place()
        copies = []
        for a, nm in enumerate(names):
            copies.append(pltpu.make_async_remote_copy(
                src_ref=_half_of(src[nm], nm, 1 - c, lead=1), dst_ref=out[nm], send_sem=send.at[a], recv_sem=recv.at[a],
                device_id=(x, y, 1 - c), device_id_type=MESH))
        for cpy in copies:
            cpy.start()
        for cpy in copies:
            cpy.wait()

    hbm = pl.BlockSpec(memory_space=pl.ANY)
    outs = tuple(jax.ShapeDtypeStruct((N_CHIPS,) + _half_shape(nm, grads[nm].shape[1:]), grads[nm].dtype)
                 for nm in names)
    res = _call(
        body, name="rs_to_sibling", out_shape=outs, in_specs=[hbm] * n, out_specs=(hbm,) * n,
        scratch_shapes=[pltpu.SemaphoreType.DMA((n,)), pltpu.SemaphoreType.DMA((n,))],
    )(*[grads[nm] for nm in names])
    return dict(zip(names, res))


def _tile2(shape):
    lead = shape[:-2]
    return lead, shape[-2:]


def add_halves(name, mine, got, c_idx):
    axis, size = _HALF[name]
    hshape = got.shape
    lead, last2 = hshape[:-2], hshape[-2:]
    nlead = len(lead)
    haxis = 1 + axis

    def body(c_ref, m_ref, g_ref, o_ref):
        del c_ref
        o_ref[...] = (m_ref[...].astype(F32) + g_ref[...].astype(F32)).astype(o_ref.dtype)

    blk = (None,) * nlead + last2

    def got_map(*idx):
        return tuple(idx[:nlead]) + (0, 0)

    def mine_map(*idx):
        lead_idx = list(idx[:nlead])
        c = idx[nlead][0]
        if haxis < nlead:
            lead_idx[haxis] = lead_idx[haxis] + c * size
            return tuple(lead_idx) + (0, 0)
        return tuple(lead_idx) + (c, 0)

    grid_spec = pltpu.PrefetchScalarGridSpec(
        num_scalar_prefetch=1, grid=lead,
        in_specs=[pl.BlockSpec(blk, mine_map), pl.BlockSpec(blk, got_map)],
        out_specs=pl.BlockSpec(blk, got_map))
    return _call(
        body, name="add_halves", out_shape=jax.ShapeDtypeStruct(hshape, got.dtype), grid_spec=grid_spec,
        compiler_params=_cp(*(("parallel",) * nlead)),
    )(c_idx, mine, got)


def _rs_ici(n, src, out, send, recv):
    x, y, c, others = _place()
    copies = []
    for a in range(n):
        for j, (ox, oy) in enumerate(others):
            copies.append(pltpu.make_async_remote_copy(
                src_ref=src[a].at[2 * ox + oy], dst_ref=out[a].at[j], send_sem=send.at[3 * a + j],
                recv_sem=recv.at[3 * a + j], device_id=(ox, oy, c), device_id_type=MESH))
    return copies


def _rs_out_shapes(summed, names):
    return [jax.ShapeDtypeStruct((3,) + summed[nm].shape[1:], summed[nm].dtype) for nm in names]


def rs_comm(summed, names):
    n = len(names)

    def start(ins, outs, send, recv):
        for cpy in _rs_ici(n, ins, outs, send, recv):
            cpy.start()

    def finish(ins, outs, send, recv):
        for cpy in _rs_ici(n, ins, outs, send, recv):
            cpy.wait()

    return FusedComm([summed[nm] for nm in names], _rs_out_shapes(summed, names), {}, 3 * n, start, finish)


def add_chips(name, summed, got, kc_idx, full_shape):
    axis, size = _HALF[name]
    hshape = summed.shape[1:]
    lead, last2 = hshape[:-2], hshape[-2:]
    nlead = len(lead)

    def body(kc_ref, s_ref, g0_ref, g1_ref, g2_ref, o_ref):
        del kc_ref
        o_ref[...] = ((s_ref[...].astype(F32) + g0_ref[...].astype(F32)) + g1_ref[...].astype(F32)) + g2_ref[...].astype(F32)

    blk = (None,) * (nlead + 1) + last2
    oblk = (None,) * nlead + last2

    def got_map(slot):
        return lambda *idx: (slot,) + tuple(idx[:nlead]) + (0, 0)

    def out_map(*idx):
        lead_idx = list(idx[:nlead])
        c = idx[-1][1]
        if axis < nlead:
            lead_idx[axis] = lead_idx[axis] + c * size
            return tuple(lead_idx) + (0, 0)
        return tuple(lead_idx) + (c, 0)

    grid_spec = pltpu.PrefetchScalarGridSpec(
        num_scalar_prefetch=1, grid=lead if nlead else (1,),
        in_specs=[pl.BlockSpec(blk, lambda *idx: (idx[-1][0],) + tuple(idx[:nlead]) + (0, 0)),
                  pl.BlockSpec(blk, got_map(0)), pl.BlockSpec(blk, got_map(1)), pl.BlockSpec(blk, got_map(2))],
        out_specs=pl.BlockSpec(oblk, out_map))
    return _call(
        body, name="add_chips", out_shape=jax.ShapeDtypeStruct(full_shape, F32), grid_spec=grid_spec,
        compiler_params=_cp(*(("parallel",) * max(nlead, 1))),
    )(kc_idx, summed, got, got, got)


def rs_replicate(shards, names):
    n = len(names)

    def body(*refs):
        given = dict(zip(names, refs[:n]))
        buf = dict(zip(names, refs[n:2 * n]))
        send, recv = refs[2 * n], refs[2 * n + 1]
        x, y, c, _ = _place()
        copies = []
        for a, nm in enumerate(names):
            copies.append(pltpu.make_async_remote_copy(
                src_ref=_half_of(given[nm], nm, c), dst_ref=_half_of(buf[nm], nm, c), send_sem=send.at[a],
                recv_sem=recv.at[a], device_id=(x, y, 1 - c), device_id_type=MESH))
        for cpy in copies:
            cpy.start()
        for a, nm in enumerate(names):
            other = _half_of(buf[nm], nm, 1 - c)
            pltpu.make_async_remote_copy(src_ref=other, dst_ref=other, send_sem=send.at[a], recv_sem=recv.at[a],
                                         device_id=(x, y, 1 - c), device_id_type=MESH).wait_recv()
        for cpy in copies:
            cpy.wait_send()

    hbm = pl.BlockSpec(memory_space=pl.ANY)
    outs = tuple(jax.ShapeDtypeStruct(shards[nm].shape, F32) for nm in names)
    res = _call(
        body, name="rs_replicate", out_shape=outs, in_specs=[hbm] * n, out_specs=(hbm,) * n,
        input_output_aliases={a: a for a in range(n)},
        scratch_shapes=[pltpu.SemaphoreType.DMA((n,)), pltpu.SemaphoreType.DMA((n,))],
    )(*[shards[nm] for nm in names])
    return dict(zip(names, res))


def allreduce_small(v):
    r = v.shape[0]

    def body(v_ref, o_ref, slots, send, recv):
        x, y, c, _ = _place()
        me = 4 * x + 2 * y + c
        slots[me] = v_ref[...]
        copies = []
        for rel in range(1, 8):
            fx, fy, fc = (rel >> 2) & 1, (rel >> 1) & 1, rel & 1
            peer = (x ^ fx, y ^ fy, c ^ fc)
            copies.append(pltpu.make_async_remote_copy(
                src_ref=v_ref, dst_ref=slots.at[me], send_sem=send.at[rel - 1], recv_sem=recv.at[rel - 1],
                device_id=peer, device_id_type=MESH))
        for cpy in copies:
            cpy.start()
        for rel in range(1, 8):
            fx, fy, fc = (rel >> 2) & 1, (rel >> 1) & 1, rel & 1
            src_id = 4 * (x ^ fx) + 2 * (y ^ fy) + (c ^ fc)
            pltpu.make_async_remote_copy(
                src_ref=v_ref, dst_ref=slots.at[src_id], send_sem=send.at[rel - 1], recv_sem=recv.at[rel - 1],
                device_id=(x, y, c), device_id_type=MESH).wait_recv()
        for cpy in copies:
            cpy.wait_send()
        acc = slots[0]
        for s in range(1, 8):
            acc = acc + slots[s]
        o_ref[...] = acc

    vm = pl.BlockSpec(memory_space=pltpu.VMEM)
    return _call(
        body, name="allreduce_small", out_shape=jax.ShapeDtypeStruct(v.shape, F32), in_specs=[vm], out_specs=vm,
        scratch_shapes=[pltpu.VMEM((8, r, LANES), F32), pltpu.SemaphoreType.DMA((7,)), pltpu.SemaphoreType.DMA((7,))],
    )(v)


def local_step(x, mem, target, ga1, small, ffn1_up, fwd_sb, bwd_sb, ffn1_mid, ffn1_da):
    causal = jnp.tril(jnp.ones((CHUNK, CHUNK), dtype=bool))
    w_s = jnp.where(causal[None], small["sgu_w_s"], 0.0)
    wm = w_s.astype(BF)
    wmt = jnp.swapaxes(w_s, 1, 2).astype(BF)
    bst = small["sgu_b_s"].reshape(SGU_GROUPS, CHUNK, 1)
    ng, nbias = small["sgu_norm_g"], small["sgu_norm_b"]

    a1 = rms_fwd(x, small["ffn1_pre_g"])
    g1, u1, hid1, rest = ffn1_up(a1)
    gb1, gd = rest["B1"][:, None], rest["D"]
    f1, h1, n1 = mm_res(hid1, gb1, 0, x, small["ffn1_post_g"], 0.5, small["mix_pre_g"])
    proj = mm_cb(n1, gd)
    oa = sgu_fwd(proj, wm, ng, nbias, bst)
    ob, tot, late = fwd_sb(proj)
    ga2, gb2, gc, ge = late["A2"], late["B2"][:, None], late["C"], late["E"]
    merged = merge_norm(oa, ob, small["sgu_out_g"], small["sb_out_g"])
    mo, h2, xn = mm_res(merged, gc, 0, h1, small["mix_post_g"], 1.0, small["xa_pre_g"])
    memn = rms_fwd(mem, small["mem_norm_g"])
    kv = mm_cb(memn, ge)
    xq = mm_res_plain(xn, gc, 1)
    o = xa_fwd(xq, kv)
    cc, h3, a2 = mm_res(o, gc, 2, h2, small["xa_post_g"], 1.0, small["ffn2_pre_g"])
    (g2, u2, hid2), _ = ffn_up(a2, ga2, 0, 1)
    f2, h4, _ = mm_res(hid2, gb2, 0, h3, small["ffn2_post_g"], 0.5, small["final_norm_g"])

    gate_up_t = (N_CHIPS, 2, ga1.shape[3], ga1.shape[2])
    dga1, dga2 = lax.empty(gate_up_t, BF), lax.empty(gate_up_t, BF)
    dgb1, dgb2 = lax.empty(gb1.shape, BF), lax.empty(gb2.shape, BF)
    dgc = lax.empty(gc.shape, BF)
    dgd = lax.empty(gd.shape, BF)
    dge = lax.empty(ge.shape, BF)
    sg = {}

    r = norm_bwd(h4, small["final_norm_g"], target=target, f_prev=f2, gp_prev=small["ffn2_post_g"], alpha_prev=0.5)
    loss_tile, dh4, df2 = r["loss"], r["d_h"], r["d_f"]
    sg["final_norm_g"], sg["ffn2_post_g"] = r["d_gn"], r["d_gp"]

    dg2, du2 = ffn_bwd_act(df2, gb2, 0, g2, u2)
    dgb2 = grad_rb(hid2, df2, dgb2, 0)
    dga2 = grad_rb(dg2, a2, dga2, 0)
    dga2 = grad_rb(du2, a2, dga2, 1)
    r = mm_nt_cb([(dg2, ga2, 0), (du2, ga2, 1)], D_MODEL, F32,
                 norm=dict(h=h3, gn=small["ffn2_pre_g"], d_res=dh4, f_prev=cc, gp_prev=small["xa_post_g"], alpha_prev=1.0))
    dh3, dc = r["d_h"], r["d_f"]
    sg["ffn2_pre_g"], sg["xa_post_g"] = r["d_gn"], r["d_gp"]

    d_o = mm_nt_rb(dc, gc, 2, BF)
    dgc = grad_rb(o, dc, dgc, 2)
    dxq, dkv = xa_bwd(xq, kv, d_o)
    dkvb = dkv.astype(BF)
    dge = grad_cb(memn, dkvb, dge)
    dmemn = mm_nt_cb([(dkvb, ge, None)], D_MODEL, F32)
    sg["mem_norm_g"] = norm_bwd(mem, small["mem_norm_g"], d_a=dmemn)["d_gn"]
    dgc = grad_rb(xn, dxq, dgc, 1)
    r = mm_nt_rb(dxq, gc, 1, F32,
                 norm=dict(h=h2, gn=small["xa_pre_g"], d_res=dh3, f_prev=mo, gp_prev=small["mix_post_g"], alpha_prev=1.0))
    dh2, dmo = r["d_h"], r["d_f"]
    sg["xa_pre_g"], sg["mix_post_g"] = r["d_gn"], r["d_gp"]

    dmerged = mm_nt_rb(dmo, gc, 0, BF)
    dgc = grad_rb(merged, dmo, dgc, 0)
    d_oa, d_ob, sg["sgu_out_g"], sg["sb_out_g"] = merge_norm_bwd(oa, ob, dmerged, small["sgu_out_g"], small["sb_out_g"])
    dp_uv, dws, dbt, sg["sgu_norm_g"], sg["sgu_norm_b"] = sgu_bwd(proj, d_oa, wm, wmt, ng, nbias, bst)
    sg["sgu_w_s"] = dws
    sg["sgu_b_s"] = dbt.reshape(SGU_GROUPS, CHUNK)
    late_grads = {"A2T": dga2, "B2": dgb2.reshape(late["B2"].shape), "C": dgc, "E": dge}
    dq, dk, dv, state = bwd_sb(proj, tot, d_ob, late_grads)
    dproj = jnp.concatenate([dp_uv, dq.astype(BF), dk.astype(BF), dv.astype(BF)], axis=1)
    dgd = grad_cb(n1, dproj, dgd)
    r = mm_nt_cb([(dproj, gd, None)], D_MODEL, F32,
                 norm=dict(h=h1, gn=small["mix_pre_g"], d_res=dh2, f_prev=f1, gp_prev=small["ffn1_post_g"], alpha_prev=0.5))
    dh1, df1 = r["d_h"], r["d_f"]
    sg["mix_pre_g"], sg["ffn1_post_g"] = r["d_gn"], r["d_gp"]

    dg1, du1 = ffn_bwd_act(df1, gb1, 0, g1, u1)
    dgb1 = grad_rb(hid1, df1, dgb1, 0)
    mid_state, comm_b1, comm_d = ffn1_mid({"B1": dgb1.reshape(rest["B1"].shape), "D": dgd})
    res = grad_rb(dg1, a1, dga1, 0, comm=comm_b1)
    dga1, got_b1 = res if comm_b1 is not None else (res, ())
    res = grad_rb(du1, a1, dga1, 1, comm=comm_d)
    dga1, got_d = res if comm_d is not None else (res, ())
    r, da_state = ffn1_da([(dg1, ga1, 0), (du1, ga1, 1)], {"A1T": dga1},
                          dict(h=x, gn=small["ffn1_pre_g"], d_res=dh1))
    state1 = (mid_state, got_b1, got_d, da_state)
    grad_x = r["d_h"]
    sg["ffn1_pre_g"] = r["d_gn"]
    return loss_tile, grad_x, sg, state, state1


def mm_res_plain(a, w, which):
    m = a.shape[0]
    kb, n = w.shape[2], w.shape[3]
    tm = _row_tile(m)

    def body(a_ref, w_ref, o_ref):
        acc = None
        for k in range(N_CHIPS):
            t = _dot(a_ref[:, k * kb:(k + 1) * kb], w_ref[k])
            acc = t if acc is None else acc + t
        o_ref[...] = acc.astype(BF)

    return _call(
        body, name="mm_rb", out_shape=jax.ShapeDtypeStruct((m, n), BF), grid=(m // tm,),
        in_specs=[pl.BlockSpec((tm, N_CHIPS * kb), lambda i: (i, 0)),
                  pl.BlockSpec((N_CHIPS, None, kb, n), lambda i: (0, which, 0, 0))],
        out_specs=pl.BlockSpec((tm, n), lambda i: (i, 0)), compiler_params=_cp("parallel"),
    )(a, w)


_BIG = ("ffn1_w_gate", "ffn1_w_up", "ffn1_w_down", "w_in", "w_out", "xa_w_q", "xa_w_kv", "xa_w_o",
        "ffn2_w_gate", "ffn2_w_up", "ffn2_w_down")
_SMALL = ("ffn1_pre_g", "ffn1_post_g", "mix_pre_g", "mix_post_g", "sgu_norm_g", "sgu_norm_b", "sgu_w_s", "sgu_b_s",
          "sgu_out_g", "sb_out_g", "xa_pre_g", "xa_post_g", "mem_norm_g", "ffn2_pre_g", "ffn2_post_g", "final_norm_g")
_WEIGHTS = ("ffn1_pre_g", "ffn1_post_g", "ffn1_w_gate", "ffn1_w_up", "ffn1_w_down", "mix_pre_g", "mix_post_g", "w_in",
            "sgu_norm_g", "sgu_norm_b", "sgu_w_s", "sgu_b_s", "sgu_out_g", "sb_out_g", "w_out", "xa_pre_g", "xa_post_g",
            "mem_norm_g", "xa_w_q", "xa_w_kv", "xa_w_o", "ffn2_pre_g", "ffn2_post_g", "ffn2_w_gate", "ffn2_w_up",
            "ffn2_w_down", "final_norm_g")
_SLOT = {"ffn1_w_gate": ("A1T", 0, True), "ffn1_w_up": ("A1T", 1, True), "ffn2_w_gate": ("A2T", 0, True),
         "ffn2_w_up": ("A2T", 1, True), "ffn1_w_down": ("B1", None, False), "ffn2_w_down": ("B2", None, False),
         "w_out": ("C", 0, False), "xa_w_q": ("C", 1, False), "xa_w_o": ("C", 2, False), "w_in": ("D", None, False),
         "xa_w_kv": ("E", None, False)}


def _pack_small(vals):
    return jnp.concatenate([vals[nm].reshape(-1, LANES) for nm in _SMALL], axis=0)


def _unpack_small(packed, shapes):
    out, pos = {}, 0
    for nm in _SMALL:
        rows = math.prod(shapes[nm]) // LANES
        out[nm] = packed[pos:pos + rows].reshape(shapes[nm])
        pos += rows
    return out


def kernel(x, mem, ffn1_pre_g, ffn1_post_g, ffn1_w_gate, ffn1_w_up, ffn1_w_down, mix_pre_g, mix_post_g, w_in, sgu_norm_g, sgu_norm_b, sgu_w_s, sgu_b_s, sgu_out_g, sb_out_g, w_out, xa_pre_g, xa_post_g, mem_norm_g, xa_w_q, xa_w_kv, xa_w_o, ffn2_pre_g, ffn2_post_g, ffn2_w_gate, ffn2_w_up, ffn2_w_down, final_norm_g, loss_target, m_ffn1_pre_g, m_ffn1_post_g, m_ffn1_w_gate, m_ffn1_w_up, m_ffn1_w_down, m_mix_pre_g, m_mix_post_g, m_w_in, m_sgu_norm_g, m_sgu_norm_b, m_sgu_w_s, m_sgu_b_s, m_sgu_out_g, m_sb_out_g, m_w_out, m_xa_pre_g, m_xa_post_g, m_mem_norm_g, m_xa_w_q, m_xa_w_kv, m_xa_w_o, m_ffn2_pre_g, m_ffn2_post_g, m_ffn2_w_gate, m_ffn2_w_up, m_ffn2_w_down, m_final_norm_g, v_ffn1_pre_g, v_ffn1_post_g, v_ffn1_w_gate, v_ffn1_w_up, v_ffn1_w_down, v_mix_pre_g, v_mix_post_g, v_w_in, v_sgu_norm_g, v_sgu_norm_b, v_sgu_w_s, v_sgu_b_s, v_sgu_out_g, v_sb_out_g, v_w_out, v_xa_pre_g, v_xa_post_g, v_mem_norm_g, v_xa_w_q, v_xa_w_kv, v_xa_w_o, v_ffn2_pre_g, v_ffn2_post_g, v_ffn2_w_gate, v_ffn2_w_up, v_ffn2_w_down, v_final_norm_g):
    env = dict(locals())
    w = {nm: env[nm] for nm in _WEIGHTS}
    mom = {nm: env["m_" + nm] for nm in _WEIGHTS}
    vel = {nm: env["v_" + nm] for nm in _WEIGHTS}

    loc = {
        "A1": jnp.stack([w["ffn1_w_gate"][0], w["ffn1_w_up"][0]]).astype(BF),
        "A2": jnp.stack([w["ffn2_w_gate"][0], w["ffn2_w_up"][0]]).astype(BF),
        "B1": w["ffn1_w_down"][0].astype(BF),
        "B2": w["ffn2_w_down"][0].astype(BF),
        "C": jnp.stack([w["w_out"][0], w["xa_w_q"][0], w["xa_w_o"][0]]).astype(BF),
        "D": w["w_in"][0].astype(BF),
        "E": w["xa_w_kv"][0].astype(BF),
    }
    ga1 = gather_weights(loc, ("A1",))["A1"]
    c_idx = lax.axis_index("c").astype(jnp.int32).reshape(1)
    kc_idx = jnp.stack([2 * lax.axis_index("x") + lax.axis_index("y"), lax.axis_index("c")]).astype(jnp.int32)
    early_rest = tuple(nm for nm in SET_EARLY if nm != "A1")

    def ffn1_up(a1):
        (g1, u1, hid1), bufs = ffn_up(a1, ga1, 0, 1, comm=gather_comm(loc, early_rest))
        return g1, u1, hid1, gather_forward(bufs, early_rest)

    def fwd_sb(proj):
        (ob, tot), bufs = sb_fwd(proj, comm=gather_comm(loc, SET_LATE))
        return ob, tot, gather_forward(bufs, SET_LATE)

    def reduce_to_pairs(grads, names):
        from_sib = rs_to_sibling(grads, names)
        return {nm: add_halves(nm, grads[nm], from_sib[nm], c_idx) for nm in names}

    def bwd_sb(proj, tot, d_ob, late_grads):
        pairs = reduce_to_pairs(late_grads, GRAD_LATE)
        (dq, dk, dv), got = sb_bwd(proj, tot, d_ob, comm=rs_comm(pairs, GRAD_LATE))
        return dq, dk, dv, (pairs, dict(zip(GRAD_LATE, got)))

    def ffn1_mid(bd_grads):
        pairs = reduce_to_pairs(bd_grads, ("B1", "D"))
        return pairs, rs_comm(pairs, ("B1",)), rs_comm(pairs, ("D",))

    def ffn1_da(mm_pairs, gate_up_grads, norm):
        pairs = reduce_to_pairs(gate_up_grads, ("A1T",))
        res, got = mm_nt_cb(mm_pairs, D_MODEL, F32, comm=rs_comm(pairs, ("A1T",)), norm=norm)
        return res, (pairs, got)

    small = {nm: w[nm][0] for nm in _SMALL}
    for nm in ("ffn1_pre_g", "ffn1_post_g", "mix_pre_g", "mix_post_g", "sgu_out_g", "sb_out_g", "xa_pre_g", "xa_post_g",
               "mem_norm_g", "ffn2_pre_g", "ffn2_post_g", "final_norm_g"):
        small[nm] = w[nm]
    loss_tile, grad_x, small_g, (late_pairs, late_got), (bd_pairs, got_b1, got_d, (a1_pairs, got_a1)) = local_step(
        x[0], mem[0], loss_target[0], ga1, small, ffn1_up, fwd_sb, bwd_sb, ffn1_mid, ffn1_da)
    pair_sum = {**late_pairs, **bd_pairs, **a1_pairs}
    from_chips = {**late_got, "B1": got_b1[0], "D": got_d[0], "A1T": got_a1[0]}

    small_shapes = {nm: w[nm].shape for nm in _SMALL}
    flat = _pack_small(small_g)
    n_small = flat.shape[0]
    pad = jnp.zeros((-n_small % 8, LANES), F32)
    packed = allreduce_small(jnp.concatenate([flat, pad, loss_tile], axis=0))
    loss = packed[-8, 0]
    g_small = _unpack_small(packed[:n_small], small_shapes)

    grad_groups = GRAD_EARLY + GRAD_LATE
    shard_shape = {nm: pair_sum[nm].shape[1:] for nm in grad_groups}
    for nm in grad_groups:
        axis, size = _HALF[nm]
        shard_shape[nm] = shard_shape[nm][:axis] + (2 * size,) + shard_shape[nm][axis + 1:]
    shard = rs_replicate({nm: add_chips(nm, pair_sum[nm], from_chips[nm], kc_idx, shard_shape[nm])
                          for nm in grad_groups}, grad_groups)

    grads, delta, new_m, new_v = {}, {}, {}, {}
    for nm in _BIG:
        grp, idx, transposed = _SLOT[nm]
        shape = w[nm].shape
        if transposed:
            res = adamw(w[nm][0].T, shard[grp], mom[nm][0].T, vel[nm][0].T, g_index=idx)
            res = [t.T for t in res]
        else:
            res = adamw(w[nm][0], shard[grp], mom[nm][0], vel[nm][0], g_index=idx)
        grads[nm], delta[nm], new_m[nm], new_v[nm] = (t.reshape(shape) for t in res)
    _, d, nm_, nv_ = adamw(_pack_small(w), _pack_small(g_small), _pack_small(mom), _pack_small(vel))
    d, nm_, nv_ = (_unpack_small(t, small_shapes) for t in (d, nm_, nv_))
    for nm in _SMALL:
        grads[nm], delta[nm], new_m[nm], new_v[nm] = g_small[nm], d[nm], nm_[nm], nv_[nm]

    return (loss, grad_x[None], *[grads[nm] for nm in _WEIGHTS], *[delta[nm] for nm in _WEIGHTS],
            *[new_m[nm] for nm in _WEIGHTS], *[new_v[nm] for nm in _WEIGHTS])
```

```python
import functools
import math

import jax
import jax.numpy as jnp
from jax import lax
from jax.experimental import pallas as pl
from jax.experimental.pallas import tpu as pltpu

F32 = jnp.float32
BF = jnp.bfloat16
EPS = 1e-6
D_MODEL = 1024
N_CHIPS = 4
FF_BLOCK = 704
IN_BLOCK = 640
KV_BLOCK = 512
ROW_BLOCK = 256
SGU_GROUPS = 4
CHUNK = 128
SB_HEAD_DIM = 64
SB_SCALE = SB_HEAD_DIM ** -0.5
XA_HEADS = 4
XA_HEAD_DIM = 256
XA_SCALE = XA_HEAD_DIM ** -0.5
LANES = 128
VMEM_LIMIT = 56 * 1024 * 1024
MESH = pl.DeviceIdType.MESH

ADAM_LR = 0.001
ADAM_B1 = 0.9
ADAM_B2 = 0.999
ADAM_EPS = 1e-08
ADAM_WD = 0.01
ADAM_STEP = 10

_GELU_C = math.sqrt(2.0 / math.pi)
_GELU_A = 0.044715


def _cp(*sem):
    return pltpu.CompilerParams(dimension_semantics=sem, vmem_limit_bytes=VMEM_LIMIT)


def _call(body, **kw):
    return pl.pallas_call(body, **kw)


def _call_with_comm(core_body, comm, args, *, name, grid, out_shape, in_specs, out_specs, scratch_shapes, compiler_params,
                    core_aliases=None):
    core_aliases = dict(core_aliases or {})
    if comm is None:
        res = _call(core_body, name=name, grid=grid, out_shape=tuple(out_shape), in_specs=list(in_specs),
                    out_specs=tuple(out_specs), scratch_shapes=list(scratch_shapes), input_output_aliases=core_aliases,
                    compiler_params=compiler_params)(*args)
        return res, ()
    n_in, n_out, n_scr = len(in_specs), len(out_shape), len(scratch_shapes)
    ni, no = len(comm.ins), len(comm.outs)

    def body(*refs):
        core_in, cin = refs[:n_in], refs[n_in:n_in + ni]
        core_out = refs[n_in + ni:n_in + ni + n_out]
        cout = refs[n_in + ni + n_out:n_in + ni + n_out + no]
        scr = refs[n_in + ni + n_out + no:]
        core_scr, send, recv = scr[:n_scr], scr[n_scr], scr[n_scr + 1]
        ids = [pl.program_id(a) for a in range(len(grid))]
        first = functools.reduce(jnp.logical_and, [i == 0 for i in ids])
        last = functools.reduce(jnp.logical_and, [i == g - 1 for i, g in zip(ids, grid)])

        @pl.when(first)
        def _():
            comm.start(cin, cout, send, recv)

        core_body(*core_in, *core_out, *core_scr)

        @pl.when(last)
        def _():
            comm.finish(cin, cout, send, recv)

    hbm = pl.BlockSpec(memory_space=pl.ANY)
    res = _call(
        body, name=name, grid=grid, out_shape=tuple(out_shape) + tuple(comm.outs),
        in_specs=list(in_specs) + [hbm] * ni, out_specs=tuple(out_specs) + (hbm,) * no,
        scratch_shapes=list(scratch_shapes) + [pltpu.SemaphoreType.DMA((comm.n_sems,))] * 2,
        input_output_aliases={**core_aliases, **{n_in + i: n_out + o for i, o in comm.aliases.items()}},
        compiler_params=compiler_params,
    )(*args, *comm.ins)
    return res[:n_out], res[n_out:]


def _dot(a, b):
    return jnp.dot(a, b, preferred_element_type=F32)


def _dot_nt(a, b):
    return lax.dot_general(a, b, (((1,), (1,)), ((), ())), preferred_element_type=F32)


def _dot_tn(a, b):
    return lax.dot_general(a, b, (((0,), (0,)), ((), ())), preferred_element_type=F32)


def _rstd(x):
    return lax.rsqrt(jnp.mean(x * x, axis=-1, keepdims=True) + EPS)


def _rms_bwd(x, g, dy):
    r = _rstd(x)
    xh = x * r
    gd = dy * g
    dx = r * (gd - xh * jnp.mean(gd * xh, axis=-1, keepdims=True))
    dg = jnp.sum(dy * xh, axis=0, keepdims=True)
    return dx, dg


def _gelu(x):
    return 0.5 * x * (1.0 + jnp.tanh(_GELU_C * (x + _GELU_A * (x * x * x))))


def _gelu_grad(x):
    t = jnp.tanh(_GELU_C * (x + _GELU_A * (x * x * x)))
    return 0.5 * (1.0 + t) + 0.5 * x * (1.0 - t * t) * (_GELU_C * (1.0 + 3.0 * _GELU_A * x * x))


def _dot2(x, ones_mat):
    hi = x.astype(BF)
    lo = (x - hi.astype(F32)).astype(BF)
    return _dot(hi, ones_mat) + _dot(lo, ones_mat)


GRAD_ROWS = 2048


def _row_tile(m, want=512):
    return min(want, m)


def _row_parts(tm, nparts=4):
    step = tm // nparts
    return [slice(p * step, (p + 1) * step) for p in range(nparts)]


def rms_fwd(x, g):
    m, d = x.shape
    tm = _row_tile(m)

    def body(x_ref, g_ref, o_ref):
        xv = x_ref[...]
        o_ref[...] = (xv * _rstd(xv) * g_ref[...]).astype(BF)

    return _call(
        body, name="rms_fwd", out_shape=jax.ShapeDtypeStruct((m, d), BF), grid=(m // tm,),
        in_specs=[pl.BlockSpec((tm, d), lambda i: (i, 0)), pl.BlockSpec((1, d), lambda i: (0, 0))],
        out_specs=pl.BlockSpec((tm, d), lambda i: (i, 0)), compiler_params=_cp("parallel"),
    )(x, g)


def ffn_up(a, ga, ig, iu, comm=None):
    m, d = a.shape
    tm = _row_tile(m)
    nb = ga.shape[-1]

    def body(a_ref, wg_ref, wu_ref, g_ref, u_ref, h_ref):
        av = a_ref[...]
        g = _dot(av, wg_ref[...])
        u = _dot(av, wu_ref[...])
        g_ref[...] = g.astype(BF)
        u_ref[...] = u.astype(BF)
        h_ref[...] = (g * jax.nn.sigmoid(g) * u).astype(BF)

    blk = jax.ShapeDtypeStruct((N_CHIPS, m, nb), BF)
    ospec = pl.BlockSpec((None, tm, nb), lambda k, i: (k, i, 0))
    return _call_with_comm(
        body, comm, (a, ga, ga), name="ffn_up", out_shape=(blk, blk, blk), grid=(N_CHIPS, m // tm),
        in_specs=[pl.BlockSpec((tm, d), lambda k, i: (i, 0)),
                  pl.BlockSpec((None, None, d, nb), lambda k, i: (k, ig, 0, 0)),
                  pl.BlockSpec((None, None, d, nb), lambda k, i: (k, iu, 0, 0))],
        out_specs=(ospec, ospec, ospec), scratch_shapes=[],
        compiler_params=_cp("arbitrary", "arbitrary") if comm is not None else _cp("parallel", "parallel"),
    )


def mm_res(lhs, w, which, h, gp, alpha, gn):
    blocked = lhs.ndim == 3
    m = lhs.shape[1] if blocked else lhs.shape[0]
    kb, n = w.shape[2], w.shape[3]
    tm = _row_tile(m)

    def body(l_ref, w_ref, h_ref, gp_ref, gn_ref, f_ref, hn_ref, an_ref):
        parts = _row_parts(tm)
        accs = []
        for rs in parts:
            acc = None
            for k in range(N_CHIPS):
                lk = l_ref[k, rs, :] if blocked else l_ref[rs, k * kb:(k + 1) * kb]
                t = _dot(lk, w_ref[k])
                acc = t if acc is None else acc + t
            accs.append(acc)
        for rs, acc in zip(parts, accs):
            f_ref[rs, :] = acc
            hn = h_ref[rs, :] + alpha * (acc * _rstd(acc) * gp_ref[...])
            hn_ref[rs, :] = hn
            an_ref[rs, :] = (hn * _rstd(hn) * gn_ref[...]).astype(BF)

    lspec = (pl.BlockSpec((N_CHIPS, tm, kb), lambda i: (0, i, 0)) if blocked
             else pl.BlockSpec((tm, N_CHIPS * kb), lambda i: (i, 0)))
    row = pl.BlockSpec((tm, n), lambda i: (i, 0))
    vec = pl.BlockSpec((1, n), lambda i: (0, 0))
    return _call(
        body, name="mm_res", grid=(m // tm,),
        out_shape=(jax.ShapeDtypeStruct((m, n), F32), jax.ShapeDtypeStruct((m, n), F32),
                   jax.ShapeDtypeStruct((m, n), BF)),
        in_specs=[lspec, pl.BlockSpec((N_CHIPS, None, kb, n), lambda i: (0, which, 0, 0)), row, vec, vec],
        out_specs=(row, row, row), compiler_params=_cp("parallel"),
    )(lhs, w, h, gp, gn)


def mm_cb(a, w, out_dtype=BF):
    m, kd = a.shape
    nb = w.shape[-1]
    tm = _row_tile(m)

    def body(a_ref, w_ref, o_ref):
        o_ref[...] = _dot(a_ref[...], w_ref[...]).astype(out_dtype)

    return _call(
        body, name="mm_cb", out_shape=jax.ShapeDtypeStruct((m, N_CHIPS * nb), out_dtype),
        grid=(N_CHIPS, m // tm),
        in_specs=[pl.BlockSpec((tm, kd), lambda k, i: (i, 0)), pl.BlockSpec((None, kd, nb), lambda k, i: (k, 0, 0))],
        out_specs=pl.BlockSpec((tm, nb), lambda k, i: (i, k)), compiler_params=_cp("parallel", "parallel"),
    )(a, w)


def _sgu_core(u_pre, vg_pre, wm_ref, ng_ref, nb_ref, bst_ref, g, c):
    rs = slice(c * CHUNK, (c + 1) * CHUNK)
    cs = slice(g * CHUNK, (g + 1) * CHUNK)
    ug = _gelu(u_pre[rs, cs].astype(F32))
    vgl = _gelu(vg_pre[rs, cs].astype(F32))
    mu = jnp.mean(vgl, axis=-1, keepdims=True)
    cen = vgl - mu
    rstd = lax.rsqrt(jnp.mean(cen * cen, axis=-1, keepdims=True) + EPS)
    xh = cen * rstd
    vn = xh * ng_ref[g:g + 1, :] + nb_ref[g:g + 1, :]
    mixed = _dot(wm_ref[g], vn.astype(BF)) + bst_ref[g]
    return ug, xh, rstd, vn, mixed


def sgu_fwd(proj, wm, ng, nb, bst):
    m = proj.shape[0]
    tm = _row_tile(m)
    wd = SGU_GROUPS * CHUNK

    def body(u_ref, v_ref, wm_ref, ng_ref, nb_ref, bst_ref, o_ref):
        for c in range(tm // CHUNK):
            for g in range(SGU_GROUPS):
                ug, _, _, _, mixed = _sgu_core(u_ref, v_ref, wm_ref, ng_ref, nb_ref, bst_ref, g, c)
                o_ref[c * CHUNK:(c + 1) * CHUNK, g * CHUNK:(g + 1) * CHUNK] = (ug * mixed).astype(BF)

    full = lambda shape: pl.BlockSpec(shape, lambda i: (0,) * len(shape))
    return _call(
        body, name="sgu_fwd", out_shape=jax.ShapeDtypeStruct((m, wd), BF), grid=(m // tm,),
        in_specs=[pl.BlockSpec((tm, wd), lambda i: (i, 0)), pl.BlockSpec((tm, wd), lambda i: (i, 1)),
                  full(wm.shape), full(ng.shape), full(nb.shape), full(bst.shape)],
        out_specs=pl.BlockSpec((tm, wd), lambda i: (i, 0)), compiler_params=_cp("parallel"),
    )(proj, proj, wm, ng, nb, bst)


def _sb_tiles(m, tq=512):
    return min(tq, m), min(256, m)


def _log_sigmoid_pair(z, mask):
    ls = jnp.minimum(z, 0.0) - jnp.log(1.0 + jnp.exp(-jnp.abs(z)))
    l1 = ls - z
    return ls, (l1 if mask is None else jnp.where(mask, l1, 0.0))


def _sb_diag_mask(r0, r1, d, tk):
    rows = r0 + lax.broadcasted_iota(jnp.int32, (r1 - r0, tk), 0)
    cols = d * tk + lax.broadcasted_iota(jnp.int32, (r1 - r0, tk), 1)
    return cols < rows


def _sb_chains(r0, r1, rows):
    part = (r1 - r0) // max(1, (r1 - r0) // rows)
    return [(hd, ra, ra + part) for ra in range(r0, r1, part) for hd in range(2)]


def _head_masks():
    lane = lax.broadcasted_iota(jnp.int32, (1, LANES), 1)
    return [lane < SB_HEAD_DIM, lane >= SB_HEAD_DIM]


def sb_fwd(proj, comm=None):
    m = proj.shape[0]
    tq, tk = _sb_tiles(m, 1024)
    ndiag = tq // tk

    def body(q_ref, k_ref, v_ref, o_ref, tot_ref, qs, acc, car):
        i = pl.program_id(1)
        nfull = i * ndiag
        heads = _head_masks()
        upper = (lax.broadcasted_iota(jnp.int32, (tk, tk), 0) > lax.broadcasted_iota(jnp.int32, (tk, tk), 1)).astype(BF)
        qv = q_ref[...]
        for hd in range(2):
            qs[hd] = jnp.where(heads[hd], qv, jnp.zeros_like(qv)) * SB_SCALE
        acc[...] = jnp.zeros_like(acc)
        car[...] = jnp.zeros_like(car)

        def block(ks, r0, diag):
            kb = k_ref[pl.ds(ks, tk), :]
            vb = v_ref[pl.ds(ks, tk), :]
            chains = _sb_chains(r0, tq, tq // 2)
            masks = [None if diag is None else _sb_diag_mask(ra, rb, diag, tk) for _, ra, rb in chains]
            zs = [_dot_nt(qs[hd, ra:rb, :], kb) for hd, ra, rb in chains]
            mid = []
            for z, mask in zip(zs, masks):
                ls, l1 = _log_sigmoid_pair(z, mask)
                mid.append((ls, l1, _dot2(l1, upper)))
            pvs = []
            for (hd, ra, rb), (ls, l1, cum), mask in zip(chains, mid, masks):
                a = jnp.exp(ls + (cum + car[hd, ra:rb, :]))
                if mask is not None:
                    a = jnp.where(mask, a, 0.0)
                pvs.append(_dot(a.astype(BF), vb))
            for (hd, ra, rb), (ls, l1, cum), pv in zip(chains, mid, pvs):
                acc[hd, ra:rb, :] += pv
                car[hd, ra:rb, :] += jnp.sum(l1, axis=-1, keepdims=True)

        for d in reversed(range(ndiag)):
            block(pl.multiple_of((nfull + d) * tk, tk), d * tk, d)

        def step(jj, carry):
            block(pl.multiple_of((nfull - 1 - jj) * tk, tk), 0, None)
            return carry

        lax.fori_loop(0, nfull, step, 0)
        o_ref[...] = jnp.where(heads[0], acc[0], acc[1]).astype(BF)
        tot_ref[...] = jnp.where(heads[0], car[0], car[1])

    qb = 2 * 512 // LANES
    return _call_with_comm(
        body, comm, (proj, proj, proj), name="sb_fwd", grid=(4, m // tq),
        out_shape=(jax.ShapeDtypeStruct((m, 512), BF), jax.ShapeDtypeStruct((m, 512), F32)),
        in_specs=[pl.BlockSpec((tq, LANES), lambda p, i: (i, qb + p)),
                  pl.BlockSpec((m, LANES), lambda p, i: (0, qb + 4 + p)),
                  pl.BlockSpec((m, LANES), lambda p, i: (0, qb + 8 + p))],
        out_specs=(pl.BlockSpec((tq, LANES), lambda p, i: (i, p)), pl.BlockSpec((tq, LANES), lambda p, i: (i, p))),
        scratch_shapes=[pltpu.VMEM((2, tq, LANES), BF), pltpu.VMEM((2, tq, LANES), F32), pltpu.VMEM((2, tq, 1), F32)],
        compiler_params=_cp("arbitrary", "arbitrary"),
    )


def merge_norm(oa, ob, ga, gb):
    m, w = oa.shape
    tm = _row_tile(m)

    def body(a_ref, b_ref, ga_ref, gb_ref, o_ref):
        av = a_ref[...].astype(F32)
        bv = b_ref[...].astype(F32)
        o_ref[:, :w] = (av * _rstd(av) * ga_ref[...]).astype(BF)
        o_ref[:, w:] = (bv * _rstd(bv) * gb_ref[...]).astype(BF)

    row = pl.BlockSpec((tm, w), lambda i: (i, 0))
    vec = pl.BlockSpec((1, w), lambda i: (0, 0))
    return _call(
        body, name="merge_norm", out_shape=jax.ShapeDtypeStruct((m, 2 * w), BF), grid=(m // tm,),
        in_specs=[row, row, vec, vec], out_specs=pl.BlockSpec((tm, 2 * w), lambda i: (i, 0)),
        compiler_params=_cp("parallel"),
    )(oa, ob, ga, gb)


def _xa_probs(qh, kh):
    logits = _dot_nt(qh, kh) * XA_SCALE
    e = jnp.exp(logits - jnp.max(logits, axis=-1, keepdims=True))
    return e / jnp.sum(e, axis=-1, keepdims=True)


def xa_fwd(xq, kv):
    m, d = xq.shape
    mm = kv.shape[0]
    tm = _row_tile(m)

    def body(q_ref, kv_ref, o_ref):
        for hd in range(XA_HEADS):
            cs = slice(hd * XA_HEAD_DIM, (hd + 1) * XA_HEAD_DIM)
            p = _xa_probs(q_ref[:, cs], kv_ref[:, cs])
            vh = kv_ref[:, d + hd * XA_HEAD_DIM:d + (hd + 1) * XA_HEAD_DIM]
            o_ref[:, cs] = _dot(p.astype(BF), vh).astype(BF)

    return _call(
        body, name="xa_fwd", out_shape=jax.ShapeDtypeStruct((m, d), BF), grid=(m // tm,),
        in_specs=[pl.BlockSpec((tm, d), lambda i: (i, 0)), pl.BlockSpec((mm, 2 * d), lambda i: (0, 0))],
        out_specs=pl.BlockSpec((tm, d), lambda i: (i, 0)), compiler_params=_cp("parallel"),
    )(xq, kv)


def norm_bwd(h, gn, d_a=None, d_res=None, target=None, f_prev=None, gp_prev=None, alpha_prev=1.0):
    m, d = h.shape
    tm = _row_tile(m)
    has_loss = target is not None
    has_res = d_res is not None
    has_prev = f_prev is not None

    def body(*refs):
        refs = list(refs)
        h_ref, gn_ref = refs[0], refs[1]
        pos = 2
        da_ref = dres_ref = t_ref = f_ref = gp_ref = None
        if has_loss:
            t_ref = refs[pos]; pos += 1
        else:
            da_ref = refs[pos]; pos += 1
        if has_res:
            dres_ref = refs[pos]; pos += 1
        if has_prev:
            f_ref, gp_ref = refs[pos], refs[pos + 1]; pos += 2
        dh_ref, dgn_ref = refs[pos], refs[pos + 1]; pos += 2
        df_ref = dgp_ref = loss_ref = None
        if has_prev:
            df_ref, dgp_ref = refs[pos], refs[pos + 1]; pos += 2
        if has_loss:
            loss_ref = refs[pos]

        first = pl.program_id(0) == 0
        hv = h_ref[...]
        gn = gn_ref[...]
        if has_loss:
            err = hv * _rstd(hv) * gn - t_ref[...]
            da = err * (1.0 / d)
            part = 0.5 * jnp.sum(jnp.sum(err * err, axis=-1, keepdims=True) * (1.0 / d))

            @pl.when(first)
            def _():
                loss_ref[...] = jnp.zeros_like(loss_ref)

            loss_ref[...] += part
        else:
            da = da_ref[...].astype(F32)
        dx, dgn = _rms_bwd(hv, gn, da)
        dh = dx + dres_ref[...] if has_res else dx
        dh_ref[...] = dh

        @pl.when(first)
        def _():
            dgn_ref[...] = jnp.zeros_like(dgn_ref)

        dgn_ref[...] += dgn
        if has_prev:
            dfv, dgp = _rms_bwd(f_ref[...], gp_ref[...], dh)
            df_ref[...] = (alpha_prev * dfv).astype(BF)

            @pl.when(first)
            def _():
                dgp_ref[...] = jnp.zeros_like(dgp_ref)

            dgp_ref[...] += alpha_prev * dgp

    row = pl.BlockSpec((tm, d), lambda i: (i, 0))
    vec = pl.BlockSpec((1, d), lambda i: (0, 0))
    ins, in_specs = [h, gn], [row, vec]
    ins.append(target if has_loss else d_a); in_specs.append(row)
    if has_res:
        ins.append(d_res); in_specs.append(row)
    if has_prev:
        ins += [f_prev, gp_prev]; in_specs += [row, vec]
    outs = [jax.ShapeDtypeStruct((m, d), F32), jax.ShapeDtypeStruct((1, d), F32)]
    out_specs = [row, vec]
    names = ["d_h", "d_gn"]
    if has_prev:
        outs += [jax.ShapeDtypeStruct((m, d), BF), jax.ShapeDtypeStruct((1, d), F32)]
        out_specs += [row, vec]
        names += ["d_f", "d_gp"]
    if has_loss:
        outs.append(jax.ShapeDtypeStruct((8, LANES), F32))
        out_specs.append(pl.BlockSpec((8, LANES), lambda i: (0, 0)))
        names.append("loss")
    res = _call(
        body, name="norm_bwd", out_shape=tuple(outs), grid=(m // tm,), in_specs=in_specs,
        out_specs=tuple(out_specs), compiler_params=_cp("arbitrary"),
    )(*ins)
    return dict(zip(names, res))


def ffn_bwd_act(df, gb, which, g, u):
    m, d = df.shape
    nb = g.shape[-1]
    tm = _row_tile(m, 1024)

    def body(df_ref, w_ref, g_ref, u_ref, dg_ref, du_ref):
        parts = _row_parts(tm, 2)
        dhs = [_dot_nt(df_ref[rs, :], w_ref[...]) for rs in parts]
        for rs, dh in zip(parts, dhs):
            gv = g_ref[rs, :].astype(F32)
            uv = u_ref[rs, :].astype(F32)
            s = jax.nn.sigmoid(gv)
            dg_ref[rs, :] = (dh * uv * (s * (1.0 + gv * (1.0 - s)))).astype(BF)
            du_ref[rs, :] = (dh * gv * s).astype(BF)

    blk = jax.ShapeDtypeStruct((N_CHIPS, m, nb), BF)
    aspec = pl.BlockSpec((None, tm, nb), lambda k, i: (k, i, 0))
    return _call(
        body, name="ffn_bwd_act", out_shape=(blk, blk), grid=(N_CHIPS, m // tm),
        in_specs=[pl.BlockSpec((tm, d), lambda k, i: (i, 0)),
                  pl.BlockSpec((None, None, nb, d), lambda k, i: (k, which, 0, 0)), aspec, aspec],
        out_specs=(aspec, aspec), compiler_params=_cp("parallel", "parallel"),
    )(df, gb, g, u)


def mm_tn(a, b, dest, a_spec, b_spec, o_spec, acc_shape, msteps, comm=None):
    def body(a_ref, b_ref, dest_ref, o_ref, acc):
        del dest_ref
        ms = pl.program_id(1)

        @pl.when(ms == 0)
        def _():
            acc[...] = jnp.zeros_like(acc)

        acc[...] += _dot_tn(a_ref[...], b_ref[...])

        @pl.when(ms == msteps - 1)
        def _():
            o_ref[...] = acc[...].astype(o_ref.dtype)

    (out,), extra = _call_with_comm(
        body, comm, (a, b, dest), name="mm_tn", out_shape=(jax.ShapeDtypeStruct(dest.shape, dest.dtype),),
        grid=(N_CHIPS, msteps), in_specs=[a_spec, b_spec, pl.BlockSpec(memory_space=pl.ANY)], out_specs=(o_spec,),
        scratch_shapes=[pltpu.VMEM(acc_shape, F32)], core_aliases={2: 0},
        compiler_params=_cp("arbitrary", "arbitrary") if comm is not None else _cp("parallel", "arbitrary"),
    )
    return out if comm is None else (out, extra)


def _act_spec(arr, tm, nb):
    if arr.ndim == 3:
        return pl.BlockSpec((None, tm, nb), lambda k, ms: (k, ms, 0))
    return pl.BlockSpec((tm, nb), lambda k, ms: (ms, k))


def grad_cb(a, dout, dest, which=None):
    m, kd = a.shape
    nb = dest.shape[-1]
    tm = _row_tile(m, GRAD_ROWS)
    if which is None:
        o_spec = pl.BlockSpec((None, kd, nb), lambda k, ms: (k, 0, 0))
    else:
        o_spec = pl.BlockSpec((None, None, kd, nb), lambda k, ms: (k, which, 0, 0))
    return mm_tn(a, dout, dest, pl.BlockSpec((tm, kd), lambda k, ms: (ms, 0)), _act_spec(dout, tm, nb), o_spec,
                 (kd, nb), m // tm)


def grad_rb(a, dout, dest, which, comm=None):
    m, n = dout.shape
    kb = dest.shape[-2]
    tm = _row_tile(m, GRAD_ROWS)
    o_spec = pl.BlockSpec((None, None, kb, n), lambda k, ms: (k, which, 0, 0))
    return mm_tn(a, dout, dest, _act_spec(a, tm, kb), pl.BlockSpec((tm, n), lambda k, ms: (ms, 0)), o_spec,
                 (kb, n), m // tm, comm=comm)


def _norm_n_in(norm):
    return 3 + (2 if norm.get("f_prev") is not None else 0)


def _norm_join_operands(norm, tm, row_map, vec_map):
    h = norm["h"]
    m, d = h.shape
    row, vec = pl.BlockSpec((tm, d), row_map), pl.BlockSpec((1, d), vec_map)
    ins, in_specs = [h, norm["gn"], norm["d_res"]], [row, vec, row]
    outs = [jax.ShapeDtypeStruct((m, d), F32), jax.ShapeDtypeStruct((1, d), F32)]
    out_specs, names = [row, vec], ["d_h", "d_gn"]
    if norm.get("f_prev") is not None:
        ins += [norm["f_prev"], norm["gp_prev"]]
        in_specs += [row, vec]
        outs += [jax.ShapeDtypeStruct((m, d), BF), jax.ShapeDtypeStruct((1, d), F32)]
        out_specs += [row, vec]
        names += ["d_f", "d_gp"]
    return ins, in_specs, outs, out_specs, names


def _norm_join(da, in_refs, out_refs, norm, first):
    h_ref, gn_ref, dres_ref = in_refs[:3]
    dh_ref, dgn_ref = out_refs[:2]
    has_prev = norm.get("f_prev") is not None
    alpha = norm.get("alpha_prev", 1.0)

    @pl.when(first)
    def _():
        dgn_ref[...] = jnp.zeros_like(dgn_ref)
        if has_prev:
            out_refs[3][...] = jnp.zeros_like(out_refs[3])

    dx, dgn = _rms_bwd(h_ref[...], gn_ref[...], da)
    dh = dx + dres_ref[...]
    dh_ref[...] = dh
    dgn_ref[...] += dgn
    if has_prev:
        dfv, dgp = _rms_bwd(in_refs[3][...], in_refs[4][...], dh)
        out_refs[2][...] = (alpha * dfv).astype(BF)
        out_refs[3][...] += alpha * dgp


def mm_nt_cb(pairs, n, out_dtype, comm=None, norm=None):
    d0 = pairs[0][0]
    m = d0.shape[1] if d0.ndim == 3 else d0.shape[0]
    tm = _row_tile(m, 1024 if norm is None else 512)
    npair = len(pairs)
    n_norm_in = 0 if norm is None else _norm_n_in(norm)

    def body(*refs):
        outs = refs[2 * npair + n_norm_in:-1]
        acc = refs[-1]
        i, k = pl.program_id(0), pl.program_id(1)

        @pl.when(k == 0)
        def _():
            acc[...] = jnp.zeros_like(acc)

        for p in range(npair):
            acc[...] += _dot_nt(refs[2 * p][...], refs[2 * p + 1][...])

        @pl.when(k == N_CHIPS - 1)
        def _():
            if norm is None:
                outs[0][...] = acc[...].astype(out_dtype)
            else:
                _norm_join(acc[...], refs[2 * npair:2 * npair + n_norm_in], outs, norm, i == 0)

    ins, in_specs = [], []
    for dout, w, which in pairs:
        nb = w.shape[-1]
        if dout.ndim == 3:
            in_specs.append(pl.BlockSpec((None, tm, nb), lambda i, k: (k, i, 0)))
        else:
            in_specs.append(pl.BlockSpec((tm, nb), lambda i, k: (i, k)))
        if w.ndim == 4:
            in_specs.append(pl.BlockSpec((None, None, n, nb), lambda i, k, which=which: (k, which, 0, 0)))
        else:
            in_specs.append(pl.BlockSpec((None, n, nb), lambda i, k: (k, 0, 0)))
        ins += [dout, w]
    if norm is None:
        out_shape = (jax.ShapeDtypeStruct((m, n), out_dtype),)
        out_specs = (pl.BlockSpec((tm, n), lambda i, k: (i, 0)),)
        names = None
    else:
        n_ins, n_specs, out_shape, out_specs, names = _norm_join_operands(norm, tm, lambda i, k: (i, 0), lambda i, k: (0, 0))
        ins += n_ins
        in_specs += n_specs
    sequential = comm is not None or norm is not None
    res, extra = _call_with_comm(
        body, comm, tuple(ins), name="mm_nt_cb", out_shape=tuple(out_shape), grid=(m // tm, N_CHIPS),
        in_specs=in_specs, out_specs=tuple(out_specs), scratch_shapes=[pltpu.VMEM((tm, n), F32)],
        compiler_params=_cp("arbitrary", "arbitrary") if sequential else _cp("parallel", "arbitrary"),
    )
    out = res[0] if norm is None else dict(zip(names, res))
    return out if comm is None else (out, extra)


def mm_nt_rb(dout, w, which, out_dtype, norm=None):
    m, n = dout.shape
    kb = w.shape[2]
    tm = _row_tile(m)

    def body(d_ref, w_ref, *rest):
        da = _dot_nt(d_ref[...], w_ref[...].reshape(N_CHIPS * kb, n))
        if norm is None:
            rest[0][...] = da.astype(out_dtype)
        else:
            _norm_join(da, rest[:_norm_n_in(norm)], rest[_norm_n_in(norm):], norm, pl.program_id(0) == 0)

    ins = [dout, w]
    in_specs = [pl.BlockSpec((tm, n), lambda i: (i, 0)), pl.BlockSpec((N_CHIPS, None, kb, n), lambda i: (0, which, 0, 0))]
    if norm is None:
        return _call(
            body, name="mm_nt_rb", out_shape=jax.ShapeDtypeStruct((m, N_CHIPS * kb), out_dtype), grid=(m // tm,),
            in_specs=in_specs, out_specs=pl.BlockSpec((tm, N_CHIPS * kb), lambda i: (i, 0)),
            compiler_params=_cp("parallel"),
        )(*ins)
    n_ins, n_specs, out_shape, out_specs, names = _norm_join_operands(norm, tm, lambda i: (i, 0), lambda i: (0, 0))
    res = _call(
        body, name="mm_nt_rb", out_shape=tuple(out_shape), grid=(m // tm,), in_specs=in_specs + n_specs,
        out_specs=tuple(out_specs), compiler_params=_cp("arbitrary"),
    )(*ins, *n_ins)
    return dict(zip(names, res))


def merge_norm_bwd(oa, ob, dmerged, ga, gb):
    m, w = oa.shape
    tm = _row_tile(m)

    def body(a_ref, b_ref, dm_ref, ga_ref, gb_ref, da_ref, db_ref, dga_ref, dgb_ref):
        @pl.when(pl.program_id(0) == 0)
        def _():
            dga_ref[...] = jnp.zeros_like(dga_ref)
            dgb_ref[...] = jnp.zeros_like(dgb_ref)

        da, dga = _rms_bwd(a_ref[...].astype(F32), ga_ref[...], dm_ref[:, :w].astype(F32))
        db, dgb = _rms_bwd(b_ref[...].astype(F32), gb_ref[...], dm_ref[:, w:].astype(F32))
        da_ref[...] = da
        db_ref[...] = db
        dga_ref[...] += dga
        dgb_ref[...] += dgb

    row = pl.BlockSpec((tm, w), lambda i: (i, 0))
    vec = pl.BlockSpec((1, w), lambda i: (0, 0))
    return _call(
        body, name="merge_norm_bwd", grid=(m // tm,),
        out_shape=(jax.ShapeDtypeStruct((m, w), F32), jax.ShapeDtypeStruct((m, w), F32),
                   jax.ShapeDtypeStruct((1, w), F32), jax.ShapeDtypeStruct((1, w), F32)),
        in_specs=[row, row, pl.BlockSpec((tm, 2 * w), lambda i: (i, 0)), vec, vec],
        out_specs=(row, row, vec, vec), compiler_params=_cp("arbitrary"),
    )(oa, ob, dmerged, ga, gb)


def sgu_bwd(proj, d_oa, wm, wmt, ng, nb, bst):
    m = proj.shape[0]
    tm = _row_tile(m)
    wd = SGU_GROUPS * CHUNK

    def body(u_ref, v_ref, do_ref, wm_ref, wmt_ref, ng_ref, nb_ref, bst_ref,
             dp_ref, dw_ref, dbt_ref, dng_ref, dnb_ref):
        @pl.when(pl.program_id(0) == 0)
        def _():
            dw_ref[...] = jnp.zeros_like(dw_ref)
            dbt_ref[...] = jnp.zeros_like(dbt_ref)
            dng_ref[...] = jnp.zeros_like(dng_ref)
            dnb_ref[...] = jnp.zeros_like(dnb_ref)

        causal = lax.broadcasted_iota(jnp.int32, (CHUNK, CHUNK), 0) >= lax.broadcasted_iota(jnp.int32, (CHUNK, CHUNK), 1)
        for c in range(tm // CHUNK):
            rs = slice(c * CHUNK, (c + 1) * CHUNK)
            for g in range(SGU_GROUPS):
                cs = slice(g * CHUNK, (g + 1) * CHUNK)
                ug, xh, rstd, vn, mixed = _sgu_core(u_ref, v_ref, wm_ref, ng_ref, nb_ref, bst_ref, g, c)
                do = do_ref[rs, cs]
                dug = do * mixed
                dmix = do * ug
                dmb = dmix.astype(BF)
                dbt_ref[g] += jnp.sum(dmix, axis=-1, keepdims=True)
                dw_ref[g] += jnp.where(causal, _dot_nt(dmb, vn.astype(BF)), 0.0)
                dvn = _dot(wmt_ref[g], dmb)
                dng_ref[g:g + 1, :] += jnp.sum(dvn * xh, axis=0, keepdims=True)
                dnb_ref[g:g + 1, :] += jnp.sum(dvn, axis=0, keepdims=True)
                dxh = dvn * ng_ref[g:g + 1, :]
                dvg = rstd * (dxh - jnp.mean(dxh, axis=-1, keepdims=True)
                              - xh * jnp.mean(dxh * xh, axis=-1, keepdims=True))
                dp_ref[rs, cs] = (dug * _gelu_grad(u_ref[rs, cs].astype(F32))).astype(BF)
                dp_ref[rs, wd + g * CHUNK:wd + (g + 1) * CHUNK] = (dvg * _gelu_grad(v_ref[rs, cs].astype(F32))).astype(BF)

    full = lambda shape: pl.BlockSpec(shape, lambda i: (0,) * len(shape))
    return _call(
        body, name="sgu_bwd", grid=(m // tm,),
        out_shape=(jax.ShapeDtypeStruct((m, 2 * wd), BF), jax.ShapeDtypeStruct(wm.shape, F32),
                   jax.ShapeDtypeStruct(bst.shape, F32), jax.ShapeDtypeStruct(ng.shape, F32),
                   jax.ShapeDtypeStruct(nb.shape, F32)),
        in_specs=[pl.BlockSpec((tm, wd), lambda i: (i, 0)), pl.BlockSpec((tm, wd), lambda i: (i, 1)),
                  pl.BlockSpec((tm, wd), lambda i: (i, 0)),
                  full(wm.shape), full(wmt.shape), full(ng.shape), full(nb.shape), full(bst.shape)],
        out_specs=(pl.BlockSpec((tm, 2 * wd), lambda i: (i, 0)), full(wm.shape), full(bst.shape), full(ng.shape),
                   full(nb.shape)),
        compiler_params=_cp("arbitrary"),
    )(proj, proj, d_oa, wm, wmt, ng, nb, bst)


def sb_bwd(proj, tot, d_ob, comm=None):
    m = proj.shape[0]
    tq, tk = _sb_tiles(m)

    ndiag = tq // tk

    def body(q_ref, k_ref, v_ref, tot_ref, do_ref, dq_ref, dk_ref, dv_ref, qs, dos, tots, dqa, cl1, cg):
        i = pl.program_id(1)

        @pl.when(i == 0)
        def _():
            dk_ref[...] = jnp.zeros_like(dk_ref)
            dv_ref[...] = jnp.zeros_like(dv_ref)

        nfull = i * ndiag
        heads = _head_masks()
        r_io = lax.broadcasted_iota(jnp.int32, (tk, tk), 0)
        c_io = lax.broadcasted_iota(jnp.int32, (tk, tk), 1)
        incl = (r_io <= c_io).astype(BF)
        excl = (r_io < c_io).astype(BF)
        qv = q_ref[...]
        dov = do_ref[...].astype(BF)
        totv = tot_ref[...]
        for hd in range(2):
            qs[hd] = jnp.where(heads[hd], qv, jnp.zeros_like(qv)) * SB_SCALE
            dos[hd] = jnp.where(heads[hd], dov, jnp.zeros_like(dov))
            tots[hd] = jnp.max(jnp.where(heads[hd], totv, -jnp.inf), axis=-1, keepdims=True)
        dqa[...] = jnp.zeros_like(dqa)
        cl1[...] = jnp.zeros_like(cl1)
        cg[...] = jnp.zeros_like(cg)

        def block(ks, r0, diag):
            kb = k_ref[pl.ds(ks, tk), :]
            vb = v_ref[pl.ds(ks, tk), :]
            chains = _sb_chains(r0, tq, tq // 2)
            masks = [None if diag is None else _sb_diag_mask(ra, rb, diag, tk) for _, ra, rb in chains]
            qc = [qs[hd, ra:rb, :] for hd, ra, rb in chains]
            doc = [dos[hd, ra:rb, :] for hd, ra, rb in chains]
            zs = [_dot_nt(q, kb) for q in qc]
            das = [_dot_nt(do, vb) for do in doc]
            s1 = []
            for z, mask in zip(zs, masks):
                ls, l1 = _log_sigmoid_pair(z, mask)
                s1.append((ls, l1, _dot2(l1, incl)))
            s2 = []
            for (hd, ra, rb), (ls, l1, pre), da, mask in zip(chains, s1, das, masks):
                a = jnp.exp(ls + (tots[hd, ra:rb, :] - (pre + cl1[hd, ra:rb, :])))
                if mask is not None:
                    a = jnp.where(mask, a, 0.0)
                gmat = a * da
                s2.append((a, gmat, _dot2(gmat, excl)))
            dk_sum = dv_sum = None
            for n, (hd, ra, rb) in enumerate(chains):
                a, gmat, pref = s2[n]
                sg = jnp.exp(s1[n][0])
                dz = gmat * (1.0 - sg) - (pref + cg[hd, ra:rb, :]) * sg
                if masks[n] is not None:
                    dz = jnp.where(masks[n], dz, 0.0)
                dz = dz.astype(BF)
                dqa[hd, ra:rb, :] += _dot(dz, kb)
                dk_t = _dot_tn(dz, qc[n])
                dv_t = _dot_tn(a.astype(BF), doc[n])
                dk_sum = dk_t if dk_sum is None else dk_sum + dk_t
                dv_sum = dv_t if dv_sum is None else dv_sum + dv_t
            for n, (hd, ra, rb) in enumerate(chains):
                cl1[hd, ra:rb, :] += jnp.sum(s1[n][1], axis=-1, keepdims=True)
                cg[hd, ra:rb, :] += jnp.sum(s2[n][1], axis=-1, keepdims=True)
            dk_ref[pl.ds(ks, tk), :] += dk_sum
            dv_ref[pl.ds(ks, tk), :] += dv_sum

        def step(j, carry):
            block(pl.multiple_of(j * tk, tk), 0, None)
            return carry

        lax.fori_loop(0, nfull, step, 0)
        for d in range(ndiag):
            block(pl.multiple_of((nfull + d) * tk, tk), d * tk, d)
        dq_ref[...] = jnp.where(heads[0], dqa[0], dqa[1]) * SB_SCALE

    qb = 2 * 512 // LANES
    tile = pl.BlockSpec((tq, LANES), lambda p, i: (i, p))
    seq = pl.BlockSpec((m, LANES), lambda p, i: (0, p))
    out = jax.ShapeDtypeStruct((m, 512), F32)
    return _call_with_comm(
        body, comm, (proj, proj, proj, tot, d_ob), name="sb_bwd", grid=(4, m // tq), out_shape=(out, out, out),
        in_specs=[pl.BlockSpec((tq, LANES), lambda p, i: (i, qb + p)),
                  pl.BlockSpec((m, LANES), lambda p, i: (0, qb + 4 + p)),
                  pl.BlockSpec((m, LANES), lambda p, i: (0, qb + 8 + p)), tile, tile],
        out_specs=(tile, seq, seq),
        scratch_shapes=[pltpu.VMEM((2, tq, LANES), BF), pltpu.VMEM((2, tq, LANES), BF), pltpu.VMEM((2, tq, 1), F32),
                        pltpu.VMEM((2, tq, LANES), F32), pltpu.VMEM((2, tq, 1), F32), pltpu.VMEM((2, tq, 1), F32)],
        compiler_params=_cp("arbitrary", "arbitrary"),
    )


def xa_bwd(xq, kv, d_o):
    m, d = xq.shape
    mm = kv.shape[0]
    tm = _row_tile(m)

    def body(q_ref, kv_ref, do_ref, dq_ref, dkv_ref):
        @pl.when(pl.program_id(0) == 0)
        def _():
            dkv_ref[...] = jnp.zeros_like(dkv_ref)

        for hd in range(XA_HEADS):
            cs = slice(hd * XA_HEAD_DIM, (hd + 1) * XA_HEAD_DIM)
            vs = slice(d + hd * XA_HEAD_DIM, d + (hd + 1) * XA_HEAD_DIM)
            qh = q_ref[:, cs]
            kh = kv_ref[:, cs]
            doh = do_ref[:, cs]
            p = _xa_probs(qh, kh)
            dp = _dot_nt(doh, kv_ref[:, vs])
            ds = (p * (dp - jnp.sum(p * dp, axis=-1, keepdims=True))).astype(BF)
            dq_ref[:, cs] = (_dot(ds, kh) * XA_SCALE).astype(BF)
            dkv_ref[:, cs] += _dot_tn(ds, qh) * XA_SCALE
            dkv_ref[:, vs] += _dot_tn(p.astype(BF), doh)

    row = pl.BlockSpec((tm, d), lambda i: (i, 0))
    whole = pl.BlockSpec((mm, 2 * d), lambda i: (0, 0))
    return _call(
        body, name="xa_bwd", grid=(m // tm,),
        out_shape=(jax.ShapeDtypeStruct((m, d), BF), jax.ShapeDtypeStruct((mm, 2 * d), F32)),
        in_specs=[row, whole, row], out_specs=(row, whole), compiler_params=_cp("arbitrary"),
    )(xq, kv, d_o)


def adamw(w, g, mom, vel, g_index=None):
    r, c = w.shape
    tr = r
    for cand in range(512, 7, -8):
        if r % cand == 0:
            tr = cand
            break

    def body(w_ref, g_ref, m_ref, v_ref, go_ref, d_ref, nm_ref, nv_ref):
        gv = g_ref[...]
        mn = ADAM_B1 * m_ref[...] + (1.0 - ADAM_B1) * gv
        vn = ADAM_B2 * v_ref[...] + (1.0 - ADAM_B2) * (gv * gv)
        m_hat = mn / (1.0 - ADAM_B1 ** ADAM_STEP)
        v_hat = vn / (1.0 - ADAM_B2 ** ADAM_STEP)
        go_ref[...] = gv
        d_ref[...] = -ADAM_LR * (m_hat / (jnp.sqrt(v_hat) + ADAM_EPS) + ADAM_WD * w_ref[...])
        nm_ref[...] = mn
        nv_ref[...] = vn

    spec = pl.BlockSpec((tr, c), lambda i: (i, 0))
    gspec = spec if g_index is None else pl.BlockSpec((None, tr, c), lambda i: (g_index, i, 0))
    out = jax.ShapeDtypeStruct((r, c), F32)
    return _call(
        body, name="adamw", out_shape=(out, out, out, out), grid=(r // tr,), in_specs=[spec, gspec, spec, spec],
        out_specs=(spec, spec, spec, spec), compiler_params=_cp("parallel"),
    )(w, g, mom, vel)


def _place():
    x, y, c = lax.axis_index("x"), lax.axis_index("y"), lax.axis_index("c")
    others = [(1 - x, y), (x, 1 - y), (1 - x, 1 - y)]
    return x, y, c, others


_HALF = {"A1": (1, 512), "A2": (1, 512), "B1": (0, 352), "B2": (0, 352), "C": (1, 128), "D": (0, 512), "E": (0, 512),
         "A1T": (1, 352), "A2T": (1, 352)}
SET_EARLY = ("A1", "B1", "D")
SET_LATE = ("A2", "B2", "C", "E")
GRAD_EARLY = ("A1T", "B1", "D")
GRAD_LATE = ("A2T", "B2", "C", "E")
_HBM = pl.BlockSpec(memory_space=pl.ANY)


def _half_of(ref, name, hc, lead=0):
    axis, size = _HALF[name]
    idx = [slice(None)] * (lead + axis) + [pl.ds(hc * size, size)]
    return ref.at[tuple(idx)]


def _gather_slots(loc, names):
    me = 2 * lax.axis_index("x") + lax.axis_index("y")
    init = []
    for nm in names:
        full = lax.empty((N_CHIPS,) + loc[nm].shape, loc[nm].dtype)
        init.append(lax.dynamic_update_slice(full, loc[nm][None], (me,) + (0,) * loc[nm].ndim))
    return init


def _gather_ici(names, src, out, send, recv, sends=True, arrivals=True):
    x, y, c, others = _place()
    me = 2 * x + y
    out_sends, out_arrivals = [], []
    for a, nm in enumerate(names):
        for j, (ox, oy) in enumerate(others):
            sems = dict(send_sem=send.at[3 * a + j], recv_sem=recv.at[3 * a + j], device_id_type=MESH)
            if sends:
                out_sends.append(pltpu.make_async_remote_copy(
                    src_ref=_half_of(src[a], nm, c), dst_ref=_half_of(out[a].at[me], nm, c), device_id=(ox, oy, c),
                    **sems))
            if arrivals:
                landed = _half_of(out[a].at[2 * ox + oy], nm, c)
                out_arrivals.append(pltpu.make_async_remote_copy(src_ref=landed, dst_ref=landed, device_id=(x, y, c),
                                                                 **sems))
    return out_sends, out_arrivals


def _gather_d2d(names, given, out, send, recv):
    x, y, c, others = _place()
    sends, arrivals = [], []
    for a, nm in enumerate(names):
        for j, (ox, oy) in enumerate(others):
            sems = dict(send_sem=send.at[3 * a + j], recv_sem=recv.at[3 * a + j], device_id_type=MESH)
            sends.append(pltpu.make_async_remote_copy(
                src_ref=_half_of(given[a].at[2 * ox + oy], nm, c), dst_ref=_half_of(out[a].at[2 * ox + oy], nm, c),
                device_id=(x, y, 1 - c), **sems))
            landed = _half_of(out[a].at[2 * ox + oy], nm, 1 - c)
            arrivals.append(pltpu.make_async_remote_copy(src_ref=landed, dst_ref=landed, device_id=(x, y, c), **sems))
    return sends, arrivals


def gather_weights(loc, names):
    n = len(names)

    def body(*refs):
        src, given, out = refs[:n], refs[n:2 * n], refs[2 * n:3 * n]
        send1, recv1, send2, recv2 = refs[3 * n:3 * n + 4]
        first, landed = _gather_ici(names, src, out, send1, recv1)
        del given
        passed, arrivals = _gather_d2d(names, out, out, send2, recv2)
        for f in first:
            f.start()
        for l, p in zip(landed, passed):
            l.wait_recv()
            p.start()
        for a in arrivals:
            a.wait_recv()
        for f in first + passed:
            f.wait_send()

    init = _gather_slots(loc, names)
    res = _call(
        body, name="gather_weights", out_shape=tuple(jax.ShapeDtypeStruct(t.shape, t.dtype) for t in init),
        in_specs=[_HBM] * (2 * n), out_specs=(_HBM,) * n, input_output_aliases={n + a: a for a in range(n)},
        scratch_shapes=[pltpu.SemaphoreType.DMA((3 * n,))] * 4,
    )(*[loc[nm] for nm in names], *init)
    return dict(zip(names, res))


def gather_forward(bufs, names):
    n = len(names)

    def body(*refs):
        given, out = refs[:n], refs[n:2 * n]
        passed, arrivals = _gather_d2d(names, given, out, refs[2 * n], refs[2 * n + 1])
        for p in passed:
            p.start()
        for a in arrivals:
            a.wait_recv()
        for p in passed:
            p.wait_send()

    res = _call(
        body, name="gather_forward", out_shape=tuple(jax.ShapeDtypeStruct(t.shape, t.dtype) for t in bufs),
        in_specs=[_HBM] * n, out_specs=(_HBM,) * n, input_output_aliases={a: a for a in range(n)},
        scratch_shapes=[pltpu.SemaphoreType.DMA((3 * n,))] * 2,
    )(*bufs)
    return dict(zip(names, res))


class FusedComm:
    def __init__(self, ins, outs, aliases, n_sems, start, finish):
        self.ins, self.outs, self.aliases, self.n_sems, self.start, self.finish = ins, outs, aliases, n_sems, start, finish


def gather_comm(loc, names):
    n = len(names)

    def start(ins, outs, send, recv):
        for f in _gather_ici(names, ins[:n], outs, send, recv, arrivals=False)[0]:
            f.start()

    def finish(ins, outs, send, recv):
        first, landed = _gather_ici(names, ins[:n], outs, send, recv)
        for l in landed:
            l.wait_recv()
        for f in first:
            f.wait_send()

    init = _gather_slots(loc, names)
    return FusedComm([loc[nm] for nm in names] + init, [jax.ShapeDtypeStruct(t.shape, t.dtype) for t in init],
                     {n + a: a for a in range(n)}, 3 * n, start, finish)


def _half_shape(name, shape):
    axis, size = _HALF[name]
    s = list(shape)
    s[axis] = size
    return tuple(s)


def rs_to_sibling(grads, names):
    n = len(names)

    def body(*refs):
        src = dict(zip(names, refs[:n]))
        out = dict(zip(names, refs[n:2 * n]))
        send, recv = refs[2 * n], refs[2 * n + 1]
        x, y, c, _ = _place()
        copies = []
        for a, nm in enumerate(names):
            copies.append(pltpu.make_async_remote_copy(
                src_ref=_half_of(src[nm], nm, 1 - c, lead=1), dst_ref=out[nm], send_sem=send.at[a], recv_sem=recv.at[a],
                device_id=(x, y, 1 - c), device_id_type=MESH))
        for cpy in copies:
            cpy.start()
        for cpy in copies:
            cpy.wait()

    hbm = pl.BlockSpec(memory_space=pl.ANY)
    outs = tuple(jax.ShapeDtypeStruct((N_CHIPS,) + _half_shape(nm, grads[nm].shape[1:]), grads[nm].dtype)
                 for nm in names)
    res = _call(
        body, name="rs_to_sibling", out_shape=outs, in_specs=[hbm] * n, out_specs=(hbm,) * n,
        scratch_shapes=[pltpu.SemaphoreType.DMA((n,)), pltpu.SemaphoreType.DMA((n,))],
    )(*[grads[nm] for nm in names])
    return dict(zip(names, res))


def _tile2(shape):
    lead = shape[:-2]
    return lead, shape[-2:]


def add_halves(name, mine, got, c_idx):
    axis, size = _HALF[name]
    hshape = got.shape
    lead, last2 = hshape[:-2], hshape[-2:]
    nlead = len(lead)
    haxis = 1 + axis

    def body(c_ref, m_ref, g_ref, o_ref):
        del c_ref
        o_ref[...] = (m_ref[...].astype(F32) + g_ref[...].astype(F32)).astype(o_ref.dtype)

    blk = (None,) * nlead + last2

    def got_map(*idx):
        return tuple(idx[:nlead]) + (0, 0)

    def mine_map(*idx):
        lead_idx = list(idx[:nlead])
        c = idx[nlead][0]
        if haxis < nlead:
            lead_idx[haxis] = lead_idx[haxis] + c * size
            return tuple(lead_idx) + (0, 0)
        return tuple(lead_idx) + (c, 0)

    grid_spec = pltpu.PrefetchScalarGridSpec(
        num_scalar_prefetch=1, grid=lead,
        in_specs=[pl.BlockSpec(blk, mine_map), pl.BlockSpec(blk, got_map)],
        out_specs=pl.BlockSpec(blk, got_map))
    return _call(
        body, name="add_halves", out_shape=jax.ShapeDtypeStruct(hshape, got.dtype), grid_spec=grid_spec,
        compiler_params=_cp(*(("parallel",) * nlead)),
    )(c_idx, mine, got)


def _rs_ici(n, src, out, send, recv):
    x, y, c, others = _place()
    copies = []
    for a in range(n):
        for j, (ox, oy) in enumerate(others):
            copies.append(pltpu.make_async_remote_copy(
                src_ref=src[a].at[2 * ox + oy], dst_ref=out[a].at[j], send_sem=send.at[3 * a + j],
                recv_sem=recv.at[3 * a + j], device_id=(ox, oy, c), device_id_type=MESH))
    return copies


def _rs_out_shapes(summed, names):
    return [jax.ShapeDtypeStruct((3,) + summed[nm].shape[1:], summed[nm].dtype) for nm in names]


def rs_comm(summed, names):
    n = len(names)

    def start(ins, outs, send, recv):
        for cpy in _rs_ici(n, ins, outs, send, recv):
            cpy.start()

    def finish(ins, outs, send, recv):
        for cpy in _rs_ici(n, ins, outs, send, recv):
            cpy.wait()

    return FusedComm([summed[nm] for nm in names], _rs_out_shapes(summed, names), {}, 3 * n, start, finish)


def add_chips(name, summed, got, kc_idx, full_shape):
    axis, size = _HALF[name]
    hshape = summed.shape[1:]
    lead, last2 = hshape[:-2], hshape[-2:]
    nlead = len(lead)

    def body(kc_ref, s_ref, g0_ref, g1_ref, g2_ref, o_ref):
        del kc_ref
        o_ref[...] = ((s_ref[...].astype(F32) + g0_ref[...].astype(F32)) + g1_ref[...].astype(F32)) + g2_ref[...].astype(F32)

    blk = (None,) * (nlead + 1) + last2
    oblk = (None,) * nlead + last2

    def got_map(slot):
        return lambda *idx: (slot,) + tuple(idx[:nlead]) + (0, 0)

    def out_map(*idx):
        lead_idx = list(idx[:nlead])
        c = idx[-1][1]
        if axis < nlead:
            lead_idx[axis] = lead_idx[axis] + c * size
            return tuple(lead_idx) + (0, 0)
        return tuple(lead_idx) + (c, 0)

    grid_spec = pltpu.PrefetchScalarGridSpec(
        num_scalar_prefetch=1, grid=lead if nlead else (1,),
        in_specs=[pl.BlockSpec(blk, lambda *idx: (idx[-1][0],) + tuple(idx[:nlead]) + (0, 0)),
                  pl.BlockSpec(blk, got_map(0)), pl.BlockSpec(blk, got_map(1)), pl.BlockSpec(blk, got_map(2))],
        out_specs=pl.BlockSpec(oblk, out_map))
    return _call(
        body, name="add_chips", out_shape=jax.ShapeDtypeStruct(full_shape, F32), grid_spec=grid_spec,
        compiler_params=_cp(*(("parallel",) * max(nlead, 1))),
    )(kc_idx, summed, got, got, got)


def rs_replicate(shards, names):
    n = len(names)

    def body(*refs):
        given = dict(zip(names, refs[:n]))
        buf = dict(zip(names, refs[n:2 * n]))
        send, recv = refs[2 * n], refs[2 * n + 1]
        x, y, c, _ = _place()
        copies = []
        for a, nm in enumerate(names):
            copies.append(pltpu.make_async_remote_copy(
                src_ref=_half_of(given[nm], nm, c), dst_ref=_half_of(buf[nm], nm, c), send_sem=send.at[a],
                recv_sem=recv.at[a], device_id=(x, y, 1 - c), device_id_type=MESH))
        for cpy in copies:
            cpy.start()
        for a, nm in enumerate(names):
            other = _half_of(buf[nm], nm, 1 - c)
            pltpu.make_async_remote_copy(src_ref=other, dst_ref=other, send_sem=send.at[a], recv_sem=recv.at[a],
                                         device_id=(x, y, 1 - c), device_id_type=MESH).wait_recv()
        for cpy in copies:
            cpy.wait_send()

    hbm = pl.BlockSpec(memory_space=pl.ANY)
    outs = tuple(jax.ShapeDtypeStruct(shards[nm].shape, F32) for nm in names)
    res = _call(
        body, name="rs_replicate", out_shape=outs, in_specs=[hbm] * n, out_specs=(hbm,) * n,
        input_output_aliases={a: a for a in range(n)},
        scratch_shapes=[pltpu.SemaphoreType.DMA((n,)), pltpu.SemaphoreType.DMA((n,))],
    )(*[shards[nm] for nm in names])
    return dict(zip(names, res))


def allreduce_small(v):
    r = v.shape[0]
    h = r // 2

    def body(v_ref, o_ref, sib, slots, send, recv):
        x, y, c, others = _place()
        me = 2 * x + y
        sibling = (x, y, 1 - c)
        mine = pl.ds(pl.multiple_of(c * h, 8), h)
        theirs = pl.ds(pl.multiple_of((1 - c) * h, 8), h)
        swap = pltpu.make_async_remote_copy(src_ref=v_ref.at[theirs], dst_ref=sib, send_sem=send.at[0],
                                            recv_sem=recv.at[0], device_id=sibling, device_id_type=MESH)
        swap.start()
        swap.wait()
        slots[me] = v_ref[mine] + sib[...]
        sends = [pltpu.make_async_remote_copy(src_ref=slots.at[me], dst_ref=slots.at[me], send_sem=send.at[1 + j],
                                              recv_sem=recv.at[1 + j], device_id=(ox, oy, c), device_id_type=MESH)
                 for j, (ox, oy) in enumerate(others)]
        for s in sends:
            s.start()
        for j, (ox, oy) in enumerate(others):
            got = slots.at[2 * ox + oy]
            pltpu.make_async_remote_copy(src_ref=got, dst_ref=got, send_sem=send.at[1 + j], recv_sem=recv.at[1 + j],
                                         device_id=(x, y, c), device_id_type=MESH).wait_recv()
        for s in sends:
            s.wait_send()
        o_ref[mine] = (slots[0] + slots[1]) + (slots[2] + slots[3])
        back = pltpu.make_async_remote_copy(src_ref=o_ref.at[mine], dst_ref=o_ref.at[mine], send_sem=send.at[4],
                                            recv_sem=recv.at[4], device_id=sibling, device_id_type=MESH)
        back.start()
        pltpu.make_async_remote_copy(src_ref=o_ref.at[theirs], dst_ref=o_ref.at[theirs], send_sem=send.at[4],
                                     recv_sem=recv.at[4], device_id=sibling, device_id_type=MESH).wait_recv()
        back.wait_send()

    vm = pl.BlockSpec(memory_space=pltpu.VMEM)
    return _call(
        body, name="allreduce_small", out_shape=jax.ShapeDtypeStruct(v.shape, F32), in_specs=[vm], out_specs=vm,
        scratch_shapes=[pltpu.VMEM((h, LANES), F32), pltpu.VMEM((N_CHIPS, h, LANES), F32),
                        pltpu.SemaphoreType.DMA((5,)), pltpu.SemaphoreType.DMA((5,))],
    )(v)


def local_step(x, mem, target, ga1, small, ffn1_up, fwd_sb, bwd_sb, ffn1_mid, ffn1_da):
    causal = jnp.tril(jnp.ones((CHUNK, CHUNK), dtype=bool))
    w_s = jnp.where(causal[None], small["sgu_w_s"], 0.0)
    wm = w_s.astype(BF)
    wmt = jnp.swapaxes(w_s, 1, 2).astype(BF)
    bst = small["sgu_b_s"].reshape(SGU_GROUPS, CHUNK, 1)
    ng, nbias = small["sgu_norm_g"], small["sgu_norm_b"]

    a1 = rms_fwd(x, small["ffn1_pre_g"])
    g1, u1, hid1, rest = ffn1_up(a1)
    gb1, gd = rest["B1"][:, None], rest["D"]
    f1, h1, n1 = mm_res(hid1, gb1, 0, x, small["ffn1_post_g"], 0.5, small["mix_pre_g"])
    proj = mm_cb(n1, gd)
    oa = sgu_fwd(proj, wm, ng, nbias, bst)
    ob, tot, late = fwd_sb(proj)
    ga2, gb2, gc, ge = late["A2"], late["B2"][:, None], late["C"], late["E"]
    merged = merge_norm(oa, ob, small["sgu_out_g"], small["sb_out_g"])
    mo, h2, xn = mm_res(merged, gc, 0, h1, small["mix_post_g"], 1.0, small["xa_pre_g"])
    memn = rms_fwd(mem, small["mem_norm_g"])
    kv = mm_cb(memn, ge)
    xq = mm_res_plain(xn, gc, 1)
    o = xa_fwd(xq, kv)
    cc, h3, a2 = mm_res(o, gc, 2, h2, small["xa_post_g"], 1.0, small["ffn2_pre_g"])
    (g2, u2, hid2), _ = ffn_up(a2, ga2, 0, 1)
    f2, h4, _ = mm_res(hid2, gb2, 0, h3, small["ffn2_post_g"], 0.5, small["final_norm_g"])

    gate_up_t = (N_CHIPS, 2, ga1.shape[3], ga1.shape[2])
    dga1, dga2 = lax.empty(gate_up_t, BF), lax.empty(gate_up_t, BF)
    dgb1, dgb2 = lax.empty(gb1.shape, BF), lax.empty(gb2.shape, BF)
    dgc = lax.empty(gc.shape, BF)
    dgd = lax.empty(gd.shape, BF)
    dge = lax.empty(ge.shape, BF)
    sg = {}

    r = norm_bwd(h4, small["final_norm_g"], target=target, f_prev=f2, gp_prev=small["ffn2_post_g"], alpha_prev=0.5)
    loss_tile, dh4, df2 = r["loss"], r["d_h"], r["d_f"]
    sg["final_norm_g"], sg["ffn2_post_g"] = r["d_gn"], r["d_gp"]

    dg2, du2 = ffn_bwd_act(df2, gb2, 0, g2, u2)
    dgb2 = grad_rb(hid2, df2, dgb2, 0)
    dga2 = grad_rb(dg2, a2, dga2, 0)
    dga2 = grad_rb(du2, a2, dga2, 1)
    r = mm_nt_cb([(dg2, ga2, 0), (du2, ga2, 1)], D_MODEL, F32,
                 norm=dict(h=h3, gn=small["ffn2_pre_g"], d_res=dh4, f_prev=cc, gp_prev=small["xa_post_g"], alpha_prev=1.0))
    dh3, dc = r["d_h"], r["d_f"]
    sg["ffn2_pre_g"], sg["xa_post_g"] = r["d_gn"], r["d_gp"]

    d_o = mm_nt_rb(dc, gc, 2, BF)
    dgc = grad_rb(o, dc, dgc, 2)
    dxq, dkv = xa_bwd(xq, kv, d_o)
    dkvb = dkv.astype(BF)
    dge = grad_cb(memn, dkvb, dge)
    dmemn = mm_nt_cb([(dkvb, ge, None)], D_MODEL, F32)
    sg["mem_norm_g"] = norm_bwd(mem, small["mem_norm_g"], d_a=dmemn)["d_gn"]
    dgc = grad_rb(xn, dxq, dgc, 1)
    r = mm_nt_rb(dxq, gc, 1, F32,
                 norm=dict(h=h2, gn=small["xa_pre_g"], d_res=dh3, f_prev=mo, gp_prev=small["mix_post_g"], alpha_prev=1.0))
    dh2, dmo = r["d_h"], r["d_f"]
    sg["xa_pre_g"], sg["mix_post_g"] = r["d_gn"], r["d_gp"]

    dmerged = mm_nt_rb(dmo, gc, 0, BF)
    dgc = grad_rb(merged, dmo, dgc, 0)
    d_oa, d_ob, sg["sgu_out_g"], sg["sb_out_g"] = merge_norm_bwd(oa, ob, dmerged, small["sgu_out_g"], small["sb_out_g"])
    dp_uv, dws, dbt, sg["sgu_norm_g"], sg["sgu_norm_b"] = sgu_bwd(proj, d_oa, wm, wmt, ng, nbias, bst)
    sg["sgu_w_s"] = dws
    sg["sgu_b_s"] = dbt.reshape(SGU_GROUPS, CHUNK)
    late_grads = {"A2T": dga2, "B2": dgb2.reshape(late["B2"].shape), "C": dgc, "E": dge}
    dq, dk, dv, state = bwd_sb(proj, tot, d_ob, late_grads)
    dproj = jnp.concatenate([dp_uv, dq.astype(BF), dk.astype(BF), dv.astype(BF)], axis=1)
    dgd = grad_cb(n1, dproj, dgd)
    r = mm_nt_cb([(dproj, gd, None)], D_MODEL, F32,
                 norm=dict(h=h1, gn=small["mix_pre_g"], d_res=dh2, f_prev=f1, gp_prev=small["ffn1_post_g"], alpha_prev=0.5))
    dh1, df1 = r["d_h"], r["d_f"]
    sg["mix_pre_g"], sg["ffn1_post_g"] = r["d_gn"], r["d_gp"]

    dg1, du1 = ffn_bwd_act(df1, gb1, 0, g1, u1)
    dgb1 = grad_rb(hid1, df1, dgb1, 0)
    mid_state, comm_b1, comm_d = ffn1_mid({"B1": dgb1.reshape(rest["B1"].shape), "D": dgd})
    res = grad_rb(dg1, a1, dga1, 0, comm=comm_b1)
    dga1, got_b1 = res if comm_b1 is not None else (res, ())
    res = grad_rb(du1, a1, dga1, 1, comm=comm_d)
    dga1, got_d = res if comm_d is not None else (res, ())
    r, da_state = ffn1_da([(dg1, ga1, 0), (du1, ga1, 1)], {"A1T": dga1},
                          dict(h=x, gn=small["ffn1_pre_g"], d_res=dh1))
    state1 = (mid_state, got_b1, got_d, da_state)
    grad_x = r["d_h"]
    sg["ffn1_pre_g"] = r["d_gn"]
    return loss_tile, grad_x, sg, state, state1


def mm_res_plain(a, w, which):
    m = a.shape[0]
    kb, n = w.shape[2], w.shape[3]
    tm = _row_tile(m)

    def body(a_ref, w_ref, o_ref):
        acc = None
        for k in range(N_CHIPS):
            t = _dot(a_ref[:, k * kb:(k + 1) * kb], w_ref[k])
            acc = t if acc is None else acc + t
        o_ref[...] = acc.astype(BF)

    return _call(
        body, name="mm_rb", out_shape=jax.ShapeDtypeStruct((m, n), BF), grid=(m // tm,),
        in_specs=[pl.BlockSpec((tm, N_CHIPS * kb), lambda i: (i, 0)),
                  pl.BlockSpec((N_CHIPS, None, kb, n), lambda i: (0, which, 0, 0))],
        out_specs=pl.BlockSpec((tm, n), lambda i: (i, 0)), compiler_params=_cp("parallel"),
    )(a, w)


_BIG = ("ffn1_w_gate", "ffn1_w_up", "ffn1_w_down", "w_in", "w_out", "xa_w_q", "xa_w_kv", "xa_w_o",
        "ffn2_w_gate", "ffn2_w_up", "ffn2_w_down")
_SMALL = ("ffn1_pre_g", "ffn1_post_g", "mix_pre_g", "mix_post_g", "sgu_norm_g", "sgu_norm_b", "sgu_w_s", "sgu_b_s",
          "sgu_out_g", "sb_out_g", "xa_pre_g", "xa_post_g", "mem_norm_g", "ffn2_pre_g", "ffn2_post_g", "final_norm_g")
_WEIGHTS = ("ffn1_pre_g", "ffn1_post_g", "ffn1_w_gate", "ffn1_w_up", "ffn1_w_down", "mix_pre_g", "mix_post_g", "w_in",
            "sgu_norm_g", "sgu_norm_b", "sgu_w_s", "sgu_b_s", "sgu_out_g", "sb_out_g", "w_out", "xa_pre_g", "xa_post_g",
            "mem_norm_g", "xa_w_q", "xa_w_kv", "xa_w_o", "ffn2_pre_g", "ffn2_post_g", "ffn2_w_gate", "ffn2_w_up",
            "ffn2_w_down", "final_norm_g")
_SLOT = {"ffn1_w_gate": ("A1T", 0, True), "ffn1_w_up": ("A1T", 1, True), "ffn2_w_gate": ("A2T", 0, True),
         "ffn2_w_up": ("A2T", 1, True), "ffn1_w_down": ("B1", None, False), "ffn2_w_down": ("B2", None, False),
         "w_out": ("C", 0, False), "xa_w_q": ("C", 1, False), "xa_w_o": ("C", 2, False), "w_in": ("D", None, False),
         "xa_w_kv": ("E", None, False)}


def _pack_small(vals):
    return jnp.concatenate([vals[nm].reshape(-1, LANES) for nm in _SMALL], axis=0)


def _unpack_small(packed, shapes):
    out, pos = {}, 0
    for nm in _SMALL:
        rows = math.prod(shapes[nm]) // LANES
        out[nm] = packed[pos:pos + rows].reshape(shapes[nm])
        pos += rows
    return out


def kernel(x, mem, ffn1_pre_g, ffn1_post_g, ffn1_w_gate, ffn1_w_up, ffn1_w_down, mix_pre_g, mix_post_g, w_in, sgu_norm_g, sgu_norm_b, sgu_w_s, sgu_b_s, sgu_out_g, sb_out_g, w_out, xa_pre_g, xa_post_g, mem_norm_g, xa_w_q, xa_w_kv, xa_w_o, ffn2_pre_g, ffn2_post_g, ffn2_w_gate, ffn2_w_up, ffn2_w_down, final_norm_g, loss_target, m_ffn1_pre_g, m_ffn1_post_g, m_ffn1_w_gate, m_ffn1_w_up, m_ffn1_w_down, m_mix_pre_g, m_mix_post_g, m_w_in, m_sgu_norm_g, m_sgu_norm_b, m_sgu_w_s, m_sgu_b_s, m_sgu_out_g, m_sb_out_g, m_w_out, m_xa_pre_g, m_xa_post_g, m_mem_norm_g, m_xa_w_q, m_xa_w_kv, m_xa_w_o, m_ffn2_pre_g, m_ffn2_post_g, m_ffn2_w_gate, m_ffn2_w_up, m_ffn2_w_down, m_final_norm_g, v_ffn1_pre_g, v_ffn1_post_g, v_ffn1_w_gate, v_ffn1_w_up, v_ffn1_w_down, v_mix_pre_g, v_mix_post_g, v_w_in, v_sgu_norm_g, v_sgu_norm_b, v_sgu_w_s, v_sgu_b_s, v_sgu_out_g, v_sb_out_g, v_w_out, v_xa_pre_g, v_xa_post_g, v_mem_norm_g, v_xa_w_q, v_xa_w_kv, v_xa_w_o, v_ffn2_pre_g, v_ffn2_post_g, v_ffn2_w_gate, v_ffn2_w_up, v_ffn2_w_down, v_final_norm_g):
    env = dict(locals())
    w = {nm: env[nm] for nm in _WEIGHTS}
    mom = {nm: env["m_" + nm] for nm in _WEIGHTS}
    vel = {nm: env["v_" + nm] for nm in _WEIGHTS}

    loc = {
        "A1": jnp.stack([w["ffn1_w_gate"][0], w["ffn1_w_up"][0]]).astype(BF),
        "A2": jnp.stack([w["ffn2_w_gate"][0], w["ffn2_w_up"][0]]).astype(BF),
        "B1": w["ffn1_w_down"][0].astype(BF),
        "B2": w["ffn2_w_down"][0].astype(BF),
        "C": jnp.stack([w["w_out"][0], w["xa_w_q"][0], w["xa_w_o"][0]]).astype(BF),
        "D": w["w_in"][0].astype(BF),
        "E": w["xa_w_kv"][0].astype(BF),
    }
    ga1 = gather_weights(loc, ("A1",))["A1"]
    c_idx = lax.axis_index("c").astype(jnp.int32).reshape(1)
    kc_idx = jnp.stack([2 * lax.axis_index("x") + lax.axis_index("y"), lax.axis_index("c")]).astype(jnp.int32)
    early_rest = tuple(nm for nm in SET_EARLY if nm != "A1")

    def ffn1_up(a1):
        (g1, u1, hid1), bufs = ffn_up(a1, ga1, 0, 1, comm=gather_comm(loc, early_rest))
        return g1, u1, hid1, gather_forward(bufs, early_rest)

    def fwd_sb(proj):
        (ob, tot), bufs = sb_fwd(proj, comm=gather_comm(loc, SET_LATE))
        return ob, tot, gather_forward(bufs, SET_LATE)

    def reduce_to_pairs(grads, names):
        from_sib = rs_to_sibling(grads, names)
        return {nm: add_halves(nm, grads[nm], from_sib[nm], c_idx) for nm in names}

    def bwd_sb(proj, tot, d_ob, late_grads):
        pairs = reduce_to_pairs(late_grads, GRAD_LATE)
        (dq, dk, dv), got = sb_bwd(proj, tot, d_ob, comm=rs_comm(pairs, GRAD_LATE))
        return dq, dk, dv, (pairs, dict(zip(GRAD_LATE, got)))

    def ffn1_mid(bd_grads):
        pairs = reduce_to_pairs(bd_grads, ("B1", "D"))
        return pairs, rs_comm(pairs, ("B1",)), rs_comm(pairs, ("D",))

    def ffn1_da(mm_pairs, gate_up_grads, norm):
        pairs = reduce_to_pairs(gate_up_grads, ("A1T",))
        res, got = mm_nt_cb(mm_pairs, D_MODEL, F32, comm=rs_comm(pairs, ("A1T",)), norm=norm)
        return res, (pairs, got)

    small = {nm: w[nm][0] for nm in _SMALL}
    for nm in ("ffn1_pre_g", "ffn1_post_g", "mix_pre_g", "mix_post_g", "sgu_out_g", "sb_out_g", "xa_pre_g", "xa_post_g",
               "mem_norm_g", "ffn2_pre_g", "ffn2_post_g", "final_norm_g"):
        small[nm] = w[nm]
    loss_tile, grad_x, small_g, (late_pairs, late_got), (bd_pairs, got_b1, got_d, (a1_pairs, got_a1)) = local_step(
        x[0], mem[0], loss_target[0], ga1, small, ffn1_up, fwd_sb, bwd_sb, ffn1_mid, ffn1_da)
    pair_sum = {**late_pairs, **bd_pairs, **a1_pairs}
    from_chips = {**late_got, "B1": got_b1[0], "D": got_d[0], "A1T": got_a1[0]}

    small_shapes = {nm: w[nm].shape for nm in _SMALL}
    flat = _pack_small(small_g)
    n_small = flat.shape[0]
    pad = jnp.zeros((-n_small % 8, LANES), F32)
    packed = allreduce_small(jnp.concatenate([flat, pad, loss_tile], axis=0))
    loss = packed[-8, 0]
    g_small = _unpack_small(packed[:n_small], small_shapes)

    grad_groups = GRAD_EARLY + GRAD_LATE
    shard_shape = {nm: pair_sum[nm].shape[1:] for nm in grad_groups}
    for nm in grad_groups:
        axis, size = _HALF[nm]
        shard_shape[nm] = shard_shape[nm][:axis] + (2 * size,) + shard_shape[nm][axis + 1:]
    shard = rs_replicate({nm: add_chips(nm, pair_sum[nm], from_chips[nm], kc_idx, shard_shape[nm])
                          for nm in grad_groups}, grad_groups)

    grads, delta, new_m, new_v = {}, {}, {}, {}
    for nm in _BIG:
        grp, idx, transposed = _SLOT[nm]
        shape = w[nm].shape
        if transposed:
            res = adamw(w[nm][0].T, shard[grp], mom[nm][0].T, vel[nm][0].T, g_index=idx)
            res = [t.T for t in res]
        else:
            res = adamw(w[nm][0], shard[grp], mom[nm][0], vel[nm][0], g_index=idx)
        grads[nm], delta[nm], new_m[nm], new_v[nm] = (t.reshape(shape) for t in res)
    _, d, nm_, nv_ = adamw(_pack_small(w), _pack_small(g_small), _pack_small(mom), _pack_small(vel))
    d, nm_, nv_ = (_unpack_small(t, small_shapes) for t in (d, nm_, nv_))
    for nm in _SMALL:
        grads[nm], delta[nm], new_m[nm], new_v[nm] = g_small[nm], d[nm], nm_[nm], nv_[nm]

    return (loss, grad_x[None], *[grads[nm] for nm in _WEIGHTS], *[delta[nm] for nm in _WEIGHTS],
            *[new_m[nm] for nm in _WEIGHTS], *[new_v[nm] for nm in _WEIGHTS])
```

```python
import functools
import math

import jax
import jax.numpy as jnp
from jax import lax
from jax.experimental import pallas as pl
from jax.experimental.pallas import tpu as pltpu

F32 = jnp.float32
BF = jnp.bfloat16
EPS = 1e-6
D_MODEL = 1024
N_CHIPS = 4
FF_BLOCK = 704
IN_BLOCK = 640
KV_BLOCK = 512
ROW_BLOCK = 256
SGU_GROUPS = 4
CHUNK = 128
SB_HEAD_DIM = 64
SB_SCALE = SB_HEAD_DIM ** -0.5
XA_HEADS = 4
XA_HEAD_DIM = 256
XA_SCALE = XA_HEAD_DIM ** -0.5
LANES = 128
VMEM_LIMIT = 56 * 1024 * 1024
MESH = pl.DeviceIdType.MESH

ADAM_LR = 0.001
ADAM_B1 = 0.9
ADAM_B2 = 0.999
ADAM_EPS = 1e-08
ADAM_WD = 0.01
ADAM_STEP = 10

_GELU_C = math.sqrt(2.0 / math.pi)
_GELU_A = 0.044715


def _cp(*sem):
    return pltpu.CompilerParams(dimension_semantics=sem, vmem_limit_bytes=VMEM_LIMIT)


def _call(body, **kw):
    return pl.pallas_call(body, **kw)


def _call_with_comm(core_body, comm, args, *, name, grid, out_shape, in_specs, out_specs, scratch_shapes, compiler_params,
                    core_aliases=None):
    core_aliases = dict(core_aliases or {})
    if comm is None:
        res = _call(core_body, name=name, grid=grid, out_shape=tuple(out_shape), in_specs=list(in_specs),
                    out_specs=tuple(out_specs), scratch_shapes=list(scratch_shapes), input_output_aliases=core_aliases,
                    compiler_params=compiler_params)(*args)
        return res, ()
    n_in, n_out, n_scr = len(in_specs), len(out_shape), len(scratch_shapes)
    ni, no = len(comm.ins), len(comm.outs)

    def body(*refs):
        core_in, cin = refs[:n_in], refs[n_in:n_in + ni]
        core_out = refs[n_in + ni:n_in + ni + n_out]
        cout = refs[n_in + ni + n_out:n_in + ni + n_out + no]
        scr = refs[n_in + ni + n_out + no:]
        core_scr, send, recv = scr[:n_scr], scr[n_scr], scr[n_scr + 1]
        ids = [pl.program_id(a) for a in range(len(grid))]
        first = functools.reduce(jnp.logical_and, [i == 0 for i in ids])
        last = functools.reduce(jnp.logical_and, [i == g - 1 for i, g in zip(ids, grid)])

        @pl.when(first)
        def _():
            comm.start(cin, cout, send, recv)

        core_body(*core_in, *core_out, *core_scr)

        @pl.when(last)
        def _():
            comm.finish(cin, cout, send, recv)

    hbm = pl.BlockSpec(memory_space=pl.ANY)
    res = _call(
        body, name=name, grid=grid, out_shape=tuple(out_shape) + tuple(comm.outs),
        in_specs=list(in_specs) + [hbm] * ni, out_specs=tuple(out_specs) + (hbm,) * no,
        scratch_shapes=list(scratch_shapes) + [pltpu.SemaphoreType.DMA((comm.n_sems,))] * 2,
        input_output_aliases={**core_aliases, **{n_in + i: n_out + o for i, o in comm.aliases.items()}},
        compiler_params=compiler_params,
    )(*args, *comm.ins)
    return res[:n_out], res[n_out:]


def _dot(a, b):
    return jnp.dot(a, b, preferred_element_type=F32)


def _dot_nt(a, b):
    return lax.dot_general(a, b, (((1,), (1,)), ((), ())), preferred_element_type=F32)


def _dot_tn(a, b):
    return lax.dot_general(a, b, (((0,), (0,)), ((), ())), preferred_element_type=F32)


def _rstd(x):
    return lax.rsqrt(jnp.mean(x * x, axis=-1, keepdims=True) + EPS)


def _rms_bwd(x, g, dy):
    r = _rstd(x)
    xh = x * r
    gd = dy * g
    dx = r * (gd - xh * jnp.mean(gd * xh, axis=-1, keepdims=True))
    dg = jnp.sum(dy * xh, axis=0, keepdims=True)
    return dx, dg


def _gelu(x):
    return 0.5 * x * (1.0 + jnp.tanh(_GELU_C * (x + _GELU_A * (x * x * x))))


def _gelu_grad(x):
    t = jnp.tanh(_GELU_C * (x + _GELU_A * (x * x * x)))
    return 0.5 * (1.0 + t) + 0.5 * x * (1.0 - t * t) * (_GELU_C * (1.0 + 3.0 * _GELU_A * x * x))


def _dot2(x, ones_mat):
    hi = x.astype(BF)
    lo = (x - hi.astype(F32)).astype(BF)
    return _dot(hi, ones_mat) + _dot(lo, ones_mat)


GRAD_ROWS = 2048


def _row_tile(m, want=512):
    return min(want, m)


def _row_parts(tm, nparts=4):
    step = tm // nparts
    return [slice(p * step, (p + 1) * step) for p in range(nparts)]


def rms_fwd(x, g):
    m, d = x.shape
    tm = _row_tile(m)

    def body(x_ref, g_ref, o_ref):
        xv = x_ref[...]
        o_ref[...] = (xv * _rstd(xv) * g_ref[...]).astype(BF)

    return _call(
        body, name="rms_fwd", out_shape=jax.ShapeDtypeStruct((m, d), BF), grid=(m // tm,),
        in_specs=[pl.BlockSpec((tm, d), lambda i: (i, 0)), pl.BlockSpec((1, d), lambda i: (0, 0))],
        out_specs=pl.BlockSpec((tm, d), lambda i: (i, 0)), compiler_params=_cp("parallel"),
    )(x, g)


def ffn_up(a, ga, ig, iu, comm=None):
    m, d = a.shape
    tm = _row_tile(m)
    nb = ga.shape[-1]

    def body(a_ref, wg_ref, wu_ref, g_ref, u_ref, h_ref):
        av = a_ref[...]
        g = _dot(av, wg_ref[...])
        u = _dot(av, wu_ref[...])
        g_ref[...] = g.astype(BF)
        u_ref[...] = u.astype(BF)
        h_ref[...] = (g * jax.nn.sigmoid(g) * u).astype(BF)

    blk = jax.ShapeDtypeStruct((N_CHIPS, m, nb), BF)
    ospec = pl.BlockSpec((None, tm, nb), lambda k, i: (k, i, 0))
    return _call_with_comm(
        body, comm, (a, ga, ga), name="ffn_up", out_shape=(blk, blk, blk), grid=(N_CHIPS, m // tm),
        in_specs=[pl.BlockSpec((tm, d), lambda k, i: (i, 0)),
                  pl.BlockSpec((None, None, d, nb), lambda k, i: (k, ig, 0, 0)),
                  pl.BlockSpec((None, None, d, nb), lambda k, i: (k, iu, 0, 0))],
        out_specs=(ospec, ospec, ospec), scratch_shapes=[],
        compiler_params=_cp("arbitrary", "arbitrary") if comm is not None else _cp("parallel", "parallel"),
    )


def mm_res(lhs, w, which, h, gp, alpha, gn):
    blocked = lhs.ndim == 3
    m = lhs.shape[1] if blocked else lhs.shape[0]
    kb, n = w.shape[2], w.shape[3]
    tm = _row_tile(m)

    def body(l_ref, w_ref, h_ref, gp_ref, gn_ref, f_ref, hn_ref, an_ref):
        parts = _row_parts(tm)
        accs = []
        for rs in parts:
            acc = None
            for k in range(N_CHIPS):
                lk = l_ref[k, rs, :] if blocked else l_ref[rs, k * kb:(k + 1) * kb]
                t = _dot(lk, w_ref[k])
                acc = t if acc is None else acc + t
            accs.append(acc)
        for rs, acc in zip(parts, accs):
            f_ref[rs, :] = acc
            hn = h_ref[rs, :] + alpha * (acc * _rstd(acc) * gp_ref[...])
            hn_ref[rs, :] = hn
            an_ref[rs, :] = (hn * _rstd(hn) * gn_ref[...]).astype(BF)

    lspec = (pl.BlockSpec((N_CHIPS, tm, kb), lambda i: (0, i, 0)) if blocked
             else pl.BlockSpec((tm, N_CHIPS * kb), lambda i: (i, 0)))
    row = pl.BlockSpec((tm, n), lambda i: (i, 0))
    vec = pl.BlockSpec((1, n), lambda i: (0, 0))
    return _call(
        body, name="mm_res", grid=(m // tm,),
        out_shape=(jax.ShapeDtypeStruct((m, n), F32), jax.ShapeDtypeStruct((m, n), F32),
                   jax.ShapeDtypeStruct((m, n), BF)),
        in_specs=[lspec, pl.BlockSpec((N_CHIPS, None, kb, n), lambda i: (0, which, 0, 0)), row, vec, vec],
        out_specs=(row, row, row), compiler_params=_cp("parallel"),
    )(lhs, w, h, gp, gn)


def mm_cb(a, w, out_dtype=BF):
    m, kd = a.shape
    nb = w.shape[-1]
    tm = _row_tile(m)

    def body(a_ref, w_ref, o_ref):
        o_ref[...] = _dot(a_ref[...], w_ref[...]).astype(out_dtype)

    return _call(
        body, name="mm_cb", out_shape=jax.ShapeDtypeStruct((m, N_CHIPS * nb), out_dtype),
        grid=(N_CHIPS, m // tm),
        in_specs=[pl.BlockSpec((tm, kd), lambda k, i: (i, 0)), pl.BlockSpec((None, kd, nb), lambda k, i: (k, 0, 0))],
        out_specs=pl.BlockSpec((tm, nb), lambda k, i: (i, k)), compiler_params=_cp("parallel", "parallel"),
    )(a, w)


def _sgu_core(u_pre, vg_pre, wm_ref, ng_ref, nb_ref, bst_ref, g, c):
    rs = slice(c * CHUNK, (c + 1) * CHUNK)
    cs = slice(g * CHUNK, (g + 1) * CHUNK)
    ug = _gelu(u_pre[rs, cs].astype(F32))
    vgl = _gelu(vg_pre[rs, cs].astype(F32))
    mu = jnp.mean(vgl, axis=-1, keepdims=True)
    cen = vgl - mu
    rstd = lax.rsqrt(jnp.mean(cen * cen, axis=-1, keepdims=True) + EPS)
    xh = cen * rstd
    vn = xh * ng_ref[g:g + 1, :] + nb_ref[g:g + 1, :]
    mixed = _dot(wm_ref[g], vn.astype(BF)) + bst_ref[g]
    return ug, xh, rstd, vn, mixed


def sgu_fwd(proj, wm, ng, nb, bst):
    m = proj.shape[0]
    tm = _row_tile(m)
    wd = SGU_GROUPS * CHUNK

    def body(u_ref, v_ref, wm_ref, ng_ref, nb_ref, bst_ref, o_ref):
        for c in range(tm // CHUNK):
            for g in range(SGU_GROUPS):
                ug, _, _, _, mixed = _sgu_core(u_ref, v_ref, wm_ref, ng_ref, nb_ref, bst_ref, g, c)
                o_ref[c * CHUNK:(c + 1) * CHUNK, g * CHUNK:(g + 1) * CHUNK] = (ug * mixed).astype(BF)

    full = lambda shape: pl.BlockSpec(shape, lambda i: (0,) * len(shape))
    return _call(
        body, name="sgu_fwd", out_shape=jax.ShapeDtypeStruct((m, wd), BF), grid=(m // tm,),
        in_specs=[pl.BlockSpec((tm, wd), lambda i: (i, 0)), pl.BlockSpec((tm, wd), lambda i: (i, 1)),
                  full(wm.shape), full(ng.shape), full(nb.shape), full(bst.shape)],
        out_specs=pl.BlockSpec((tm, wd), lambda i: (i, 0)), compiler_params=_cp("parallel"),
    )(proj, proj, wm, ng, nb, bst)


def _sb_tiles(m, tq=512):
    return min(tq, m), min(256, m)


def _log_sigmoid_pair(z, mask):
    ls = jnp.minimum(z, 0.0) - jnp.log(1.0 + jnp.exp(-jnp.abs(z)))
    l1 = ls - z
    return ls, (l1 if mask is None else jnp.where(mask, l1, 0.0))


def _sb_diag_mask(r0, r1, d, tk):
    rows = r0 + lax.broadcasted_iota(jnp.int32, (r1 - r0, tk), 0)
    cols = d * tk + lax.broadcasted_iota(jnp.int32, (r1 - r0, tk), 1)
    return cols < rows


def _sb_chains(r0, r1, rows):
    span = r1 - r0
    part = next((p for p in range(min(rows, span), 0, -256) if span % p == 0), span)
    return [(hd, ra, ra + part) for ra in range(r0, r1, part) for hd in range(2)]


def _head_masks():
    lane = lax.broadcasted_iota(jnp.int32, (1, LANES), 1)
    return [lane < SB_HEAD_DIM, lane >= SB_HEAD_DIM]


def sb_fwd(proj, comm=None):
    m = proj.shape[0]
    tq, tk = _sb_tiles(m, 2048)
    ndiag = tq // tk

    def body(q_ref, k_ref, v_ref, o_ref, tot_ref, qs, acc, car):
        i = pl.program_id(1)
        nfull = i * ndiag
        heads = _head_masks()
        upper = (lax.broadcasted_iota(jnp.int32, (tk, tk), 0) > lax.broadcasted_iota(jnp.int32, (tk, tk), 1)).astype(BF)
        qv = q_ref[...]
        for hd in range(2):
            qs[hd] = jnp.where(heads[hd], qv, jnp.zeros_like(qv)) * SB_SCALE
        acc[...] = jnp.zeros_like(acc)
        car[...] = jnp.zeros_like(car)

        def block(ks, r0, diag):
            kb = k_ref[pl.ds(ks, tk), :]
            vb = v_ref[pl.ds(ks, tk), :]
            chains = _sb_chains(r0, tq, 512)
            masks = [None if diag is None else _sb_diag_mask(ra, rb, diag, tk) for _, ra, rb in chains]
            zs = [_dot_nt(qs[hd, ra:rb, :], kb) for hd, ra, rb in chains]
            mid = []
            for z, mask in zip(zs, masks):
                ls, l1 = _log_sigmoid_pair(z, mask)
                mid.append((ls, l1, _dot2(l1, upper)))
            pvs = []
            for (hd, ra, rb), (ls, l1, cum), mask in zip(chains, mid, masks):
                a = jnp.exp(ls + (cum + car[hd, ra:rb, :]))
                if mask is not None:
                    a = jnp.where(mask, a, 0.0)
                pvs.append(_dot(a.astype(BF), vb))
            for (hd, ra, rb), (ls, l1, cum), pv in zip(chains, mid, pvs):
                acc[hd, ra:rb, :] += pv
                car[hd, ra:rb, :] += jnp.sum(l1, axis=-1, keepdims=True)

        for d in reversed(range(ndiag)):
            block(pl.multiple_of((nfull + d) * tk, tk), d * tk, d)

        def step(jj, carry):
            block(pl.multiple_of((nfull - 1 - jj) * tk, tk), 0, None)
            return carry

        lax.fori_loop(0, nfull, step, 0)
        o_ref[...] = jnp.where(heads[0], acc[0], acc[1]).astype(BF)
        tot_ref[...] = jnp.where(heads[0], car[0], car[1])

    qb = 2 * 512 // LANES
    return _call_with_comm(
        body, comm, (proj, proj, proj), name="sb_fwd", grid=(4, m // tq),
        out_shape=(jax.ShapeDtypeStruct((m, 512), BF), jax.ShapeDtypeStruct((m, 512), F32)),
        in_specs=[pl.BlockSpec((tq, LANES), lambda p, i: (i, qb + p)),
                  pl.BlockSpec((m, LANES), lambda p, i: (0, qb + 4 + p)),
                  pl.BlockSpec((m, LANES), lambda p, i: (0, qb + 8 + p))],
        out_specs=(pl.BlockSpec((tq, LANES), lambda p, i: (i, p)), pl.BlockSpec((tq, LANES), lambda p, i: (i, p))),
        scratch_shapes=[pltpu.VMEM((2, tq, LANES), BF), pltpu.VMEM((2, tq, LANES), F32), pltpu.VMEM((2, tq, 1), F32)],
        compiler_params=_cp("arbitrary", "arbitrary"),
    )


def merge_norm(oa, ob, ga, gb):
    m, w = oa.shape
    tm = _row_tile(m)

    def body(a_ref, b_ref, ga_ref, gb_ref, o_ref):
        av = a_ref[...].astype(F32)
        bv = b_ref[...].astype(F32)
        o_ref[:, :w] = (av * _rstd(av) * ga_ref[...]).astype(BF)
        o_ref[:, w:] = (bv * _rstd(bv) * gb_ref[...]).astype(BF)

    row = pl.BlockSpec((tm, w), lambda i: (i, 0))
    vec = pl.BlockSpec((1, w), lambda i: (0, 0))
    return _call(
        body, name="merge_norm", out_shape=jax.ShapeDtypeStruct((m, 2 * w), BF), grid=(m // tm,),
        in_specs=[row, row, vec, vec], out_specs=pl.BlockSpec((tm, 2 * w), lambda i: (i, 0)),
        compiler_params=_cp("parallel"),
    )(oa, ob, ga, gb)


def _xa_probs(qh, kh):
    logits = _dot_nt(qh, kh) * XA_SCALE
    e = jnp.exp(logits - jnp.max(logits, axis=-1, keepdims=True))
    return e / jnp.sum(e, axis=-1, keepdims=True)


def xa_fwd(xq, kv):
    m, d = xq.shape
    mm = kv.shape[0]
    tm = _row_tile(m)

    def body(q_ref, kv_ref, o_ref):
        for hd in range(XA_HEADS):
            cs = slice(hd * XA_HEAD_DIM, (hd + 1) * XA_HEAD_DIM)
            p = _xa_probs(q_ref[:, cs], kv_ref[:, cs])
            vh = kv_ref[:, d + hd * XA_HEAD_DIM:d + (hd + 1) * XA_HEAD_DIM]
            o_ref[:, cs] = _dot(p.astype(BF), vh).astype(BF)

    return _call(
        body, name="xa_fwd", out_shape=jax.ShapeDtypeStruct((m, d), BF), grid=(m // tm,),
        in_specs=[pl.BlockSpec((tm, d), lambda i: (i, 0)), pl.BlockSpec((mm, 2 * d), lambda i: (0, 0))],
        out_specs=pl.BlockSpec((tm, d), lambda i: (i, 0)), compiler_params=_cp("parallel"),
    )(xq, kv)


def norm_bwd(h, gn, d_a=None, d_res=None, target=None, f_prev=None, gp_prev=None, alpha_prev=1.0):
    m, d = h.shape
    tm = _row_tile(m)
    has_loss = target is not None
    has_res = d_res is not None
    has_prev = f_prev is not None

    def body(*refs):
        refs = list(refs)
        h_ref, gn_ref = refs[0], refs[1]
        pos = 2
        da_ref = dres_ref = t_ref = f_ref = gp_ref = None
        if has_loss:
            t_ref = refs[pos]; pos += 1
        else:
            da_ref = refs[pos]; pos += 1
        if has_res:
            dres_ref = refs[pos]; pos += 1
        if has_prev:
            f_ref, gp_ref = refs[pos], refs[pos + 1]; pos += 2
        dh_ref, dgn_ref = refs[pos], refs[pos + 1]; pos += 2
        df_ref = dgp_ref = loss_ref = None
        if has_prev:
            df_ref, dgp_ref = refs[pos], refs[pos + 1]; pos += 2
        if has_loss:
            loss_ref = refs[pos]

        first = pl.program_id(0) == 0
        hv = h_ref[...]
        gn = gn_ref[...]
        if has_loss:
            err = hv * _rstd(hv) * gn - t_ref[...]
            da = err * (1.0 / d)
            part = 0.5 * jnp.sum(jnp.sum(err * err, axis=-1, keepdims=True) * (1.0 / d))

            @pl.when(first)
            def _():
                loss_ref[...] = jnp.zeros_like(loss_ref)

            loss_ref[...] += part
        else:
            da = da_ref[...].astype(F32)
        dx, dgn = _rms_bwd(hv, gn, da)
        dh = dx + dres_ref[...] if has_res else dx
        dh_ref[...] = dh

        @pl.when(first)
        def _():
            dgn_ref[...] = jnp.zeros_like(dgn_ref)

        dgn_ref[...] += dgn
        if has_prev:
            dfv, dgp = _rms_bwd(f_ref[...], gp_ref[...], dh)
            df_ref[...] = (alpha_prev * dfv).astype(BF)

            @pl.when(first)
            def _():
                dgp_ref[...] = jnp.zeros_like(dgp_ref)

            dgp_ref[...] += alpha_prev * dgp

    row = pl.BlockSpec((tm, d), lambda i: (i, 0))
    vec = pl.BlockSpec((1, d), lambda i: (0, 0))
    ins, in_specs = [h, gn], [row, vec]
    ins.append(target if has_loss else d_a); in_specs.append(row)
    if has_res:
        ins.append(d_res); in_specs.append(row)
    if has_prev:
        ins += [f_prev, gp_prev]; in_specs += [row, vec]
    outs = [jax.ShapeDtypeStruct((m, d), F32), jax.ShapeDtypeStruct((1, d), F32)]
    out_specs = [row, vec]
    names = ["d_h", "d_gn"]
    if has_prev:
        outs += [jax.ShapeDtypeStruct((m, d), BF), jax.ShapeDtypeStruct((1, d), F32)]
        out_specs += [row, vec]
        names += ["d_f", "d_gp"]
    if has_loss:
        outs.append(jax.ShapeDtypeStruct((8, LANES), F32))
        out_specs.append(pl.BlockSpec((8, LANES), lambda i: (0, 0)))
        names.append("loss")
    res = _call(
        body, name="norm_bwd", out_shape=tuple(outs), grid=(m // tm,), in_specs=in_specs,
        out_specs=tuple(out_specs), compiler_params=_cp("arbitrary"),
    )(*ins)
    return dict(zip(names, res))


def ffn_bwd_act(df, gb, which, g, u):
    m, d = df.shape
    nb = g.shape[-1]
    tm = _row_tile(m, 1024)

    def body(df_ref, w_ref, g_ref, u_ref, dg_ref, du_ref):
        parts = _row_parts(tm, 2)
        dhs = [_dot_nt(df_ref[rs, :], w_ref[...]) for rs in parts]
        for rs, dh in zip(parts, dhs):
            gv = g_ref[rs, :].astype(F32)
            uv = u_ref[rs, :].astype(F32)
            s = jax.nn.sigmoid(gv)
            dg_ref[rs, :] = (dh * uv * (s * (1.0 + gv * (1.0 - s)))).astype(BF)
            du_ref[rs, :] = (dh * gv * s).astype(BF)

    blk = jax.ShapeDtypeStruct((N_CHIPS, m, nb), BF)
    aspec = pl.BlockSpec((None, tm, nb), lambda k, i: (k, i, 0))
    return _call(
        body, name="ffn_bwd_act", out_shape=(blk, blk), grid=(N_CHIPS, m // tm),
        in_specs=[pl.BlockSpec((tm, d), lambda k, i: (i, 0)),
                  pl.BlockSpec((None, None, nb, d), lambda k, i: (k, which, 0, 0)), aspec, aspec],
        out_specs=(aspec, aspec), compiler_params=_cp("parallel", "parallel"),
    )(df, gb, g, u)


def mm_tn(a, b, dest, a_spec, b_spec, o_spec, acc_shape, msteps, comm=None):
    def body(a_ref, b_ref, dest_ref, o_ref, acc):
        del dest_ref
        ms = pl.program_id(1)

        @pl.when(ms == 0)
        def _():
            acc[...] = jnp.zeros_like(acc)

        acc[...] += _dot_tn(a_ref[...], b_ref[...])

        @pl.when(ms == msteps - 1)
        def _():
            o_ref[...] = acc[...].astype(o_ref.dtype)

    (out,), extra = _call_with_comm(
        body, comm, (a, b, dest), name="mm_tn", out_shape=(jax.ShapeDtypeStruct(dest.shape, dest.dtype),),
        grid=(N_CHIPS, msteps), in_specs=[a_spec, b_spec, pl.BlockSpec(memory_space=pl.ANY)], out_specs=(o_spec,),
        scratch_shapes=[pltpu.VMEM(acc_shape, F32)], core_aliases={2: 0},
        compiler_params=_cp("arbitrary", "arbitrary") if comm is not None else _cp("parallel", "arbitrary"),
    )
    return out if comm is None else (out, extra)


def _act_spec(arr, tm, nb):
    if arr.ndim == 3:
        return pl.BlockSpec((None, tm, nb), lambda k, ms: (k, ms, 0))
    return pl.BlockSpec((tm, nb), lambda k, ms: (ms, k))


def grad_cb(a, dout, dest, which=None):
    m, kd = a.shape
    nb = dest.shape[-1]
    tm = _row_tile(m, GRAD_ROWS)
    if which is None:
        o_spec = pl.BlockSpec((None, kd, nb), lambda k, ms: (k, 0, 0))
    else:
        o_spec = pl.BlockSpec((None, None, kd, nb), lambda k, ms: (k, which, 0, 0))
    return mm_tn(a, dout, dest, pl.BlockSpec((tm, kd), lambda k, ms: (ms, 0)), _act_spec(dout, tm, nb), o_spec,
                 (kd, nb), m // tm)


def grad_rb(a, dout, dest, which, comm=None):
    m, n = dout.shape
    kb = dest.shape[-2]
    tm = _row_tile(m, GRAD_ROWS)
    o_spec = pl.BlockSpec((None, None, kb, n), lambda k, ms: (k, which, 0, 0))
    return mm_tn(a, dout, dest, _act_spec(a, tm, kb), pl.BlockSpec((tm, n), lambda k, ms: (ms, 0)), o_spec,
                 (kb, n), m // tm, comm=comm)


def _norm_n_in(norm):
    return 3 + (2 if norm.get("f_prev") is not None else 0)


def _norm_join_operands(norm, tm, row_map, vec_map):
    h = norm["h"]
    m, d = h.shape
    row, vec = pl.BlockSpec((tm, d), row_map), pl.BlockSpec((1, d), vec_map)
    ins, in_specs = [h, norm["gn"], norm["d_res"]], [row, vec, row]
    outs = [jax.ShapeDtypeStruct((m, d), F32), jax.ShapeDtypeStruct((1, d), F32)]
    out_specs, names = [row, vec], ["d_h", "d_gn"]
    if norm.get("f_prev") is not None:
        ins += [norm["f_prev"], norm["gp_prev"]]
        in_specs += [row, vec]
        outs += [jax.ShapeDtypeStruct((m, d), BF), jax.ShapeDtypeStruct((1, d), F32)]
        out_specs += [row, vec]
        names += ["d_f", "d_gp"]
    return ins, in_specs, outs, out_specs, names


def _norm_join(da, in_refs, out_refs, norm, first):
    h_ref, gn_ref, dres_ref = in_refs[:3]
    dh_ref, dgn_ref = out_refs[:2]
    has_prev = norm.get("f_prev") is not None
    alpha = norm.get("alpha_prev", 1.0)

    @pl.when(first)
    def _():
        dgn_ref[...] = jnp.zeros_like(dgn_ref)
        if has_prev:
            out_refs[3][...] = jnp.zeros_like(out_refs[3])

    dx, dgn = _rms_bwd(h_ref[...], gn_ref[...], da)
    dh = dx + dres_ref[...]
    dh_ref[...] = dh
    dgn_ref[...] += dgn
    if has_prev:
        dfv, dgp = _rms_bwd(in_refs[3][...], in_refs[4][...], dh)
        out_refs[2][...] = (alpha * dfv).astype(BF)
        out_refs[3][...] += alpha * dgp


def mm_nt_cb(pairs, n, out_dtype, comm=None, norm=None):
    d0 = pairs[0][0]
    m = d0.shape[1] if d0.ndim == 3 else d0.shape[0]
    tm = _row_tile(m, 1024 if norm is None else 512)
    npair = len(pairs)
    n_norm_in = 0 if norm is None else _norm_n_in(norm)

    def body(*refs):
        outs = refs[2 * npair + n_norm_in:-1]
        acc = refs[-1]
        i, k = pl.program_id(0), pl.program_id(1)

        @pl.when(k == 0)
        def _():
            acc[...] = jnp.zeros_like(acc)

        for p in range(npair):
            acc[...] += _dot_nt(refs[2 * p][...], refs[2 * p + 1][...])

        @pl.when(k == N_CHIPS - 1)
        def _():
            if norm is None:
                outs[0][...] = acc[...].astype(out_dtype)
            else:
                _norm_join(acc[...], refs[2 * npair:2 * npair + n_norm_in], outs, norm, i == 0)

    ins, in_specs = [], []
    for dout, w, which in pairs:
        nb = w.shape[-1]
        if dout.ndim == 3:
            in_specs.append(pl.BlockSpec((None, tm, nb), lambda i, k: (k, i, 0)))
        else:
            in_specs.append(pl.BlockSpec((tm, nb), lambda i, k: (i, k)))
        if w.ndim == 4:
            in_specs.append(pl.BlockSpec((None, None, n, nb), lambda i, k, which=which: (k, which, 0, 0)))
        else:
            in_specs.append(pl.BlockSpec((None, n, nb), lambda i, k: (k, 0, 0)))
        ins += [dout, w]
    if norm is None:
        out_shape = (jax.ShapeDtypeStruct((m, n), out_dtype),)
        out_specs = (pl.BlockSpec((tm, n), lambda i, k: (i, 0)),)
        names = None
    else:
        n_ins, n_specs, out_shape, out_specs, names = _norm_join_operands(norm, tm, lambda i, k: (i, 0), lambda i, k: (0, 0))
        ins += n_ins
        in_specs += n_specs
    sequential = comm is not None or norm is not None
    res, extra = _call_with_comm(
        body, comm, tuple(ins), name="mm_nt_cb", out_shape=tuple(out_shape), grid=(m // tm, N_CHIPS),
        in_specs=in_specs, out_specs=tuple(out_specs), scratch_shapes=[pltpu.VMEM((tm, n), F32)],
        compiler_params=_cp("arbitrary", "arbitrary") if sequential else _cp("parallel", "arbitrary"),
    )
    out = res[0] if norm is None else dict(zip(names, res))
    return out if comm is None else (out, extra)


def mm_nt_rb(dout, w, which, out_dtype, norm=None):
    m, n = dout.shape
    kb = w.shape[2]
    tm = _row_tile(m)

    def body(d_ref, w_ref, *rest):
        da = _dot_nt(d_ref[...], w_ref[...].reshape(N_CHIPS * kb, n))
        if norm is None:
            rest[0][...] = da.astype(out_dtype)
        else:
            _norm_join(da, rest[:_norm_n_in(norm)], rest[_norm_n_in(norm):], norm, pl.program_id(0) == 0)

    ins = [dout, w]
    in_specs = [pl.BlockSpec((tm, n), lambda i: (i, 0)), pl.BlockSpec((N_CHIPS, None, kb, n), lambda i: (0, which, 0, 0))]
    if norm is None:
        return _call(
            body, name="mm_nt_rb", out_shape=jax.ShapeDtypeStruct((m, N_CHIPS * kb), out_dtype), grid=(m // tm,),
            in_specs=in_specs, out_specs=pl.BlockSpec((tm, N_CHIPS * kb), lambda i: (i, 0)),
            compiler_params=_cp("parallel"),
        )(*ins)
    n_ins, n_specs, out_shape, out_specs, names = _norm_join_operands(norm, tm, lambda i: (i, 0), lambda i: (0, 0))
    res = _call(
        body, name="mm_nt_rb", out_shape=tuple(out_shape), grid=(m // tm,), in_specs=in_specs + n_specs,
        out_specs=tuple(out_specs), compiler_params=_cp("arbitrary"),
    )(*ins, *n_ins)
    return dict(zip(names, res))


def merge_norm_bwd(oa, ob, dmerged, ga, gb):
    m, w = oa.shape
    tm = _row_tile(m)

    def body(a_ref, b_ref, dm_ref, ga_ref, gb_ref, da_ref, db_ref, dga_ref, dgb_ref):
        @pl.when(pl.program_id(0) == 0)
        def _():
            dga_ref[...] = jnp.zeros_like(dga_ref)
            dgb_ref[...] = jnp.zeros_like(dgb_ref)

        da, dga = _rms_bwd(a_ref[...].astype(F32), ga_ref[...], dm_ref[:, :w].astype(F32))
        db, dgb = _rms_bwd(b_ref[...].astype(F32), gb_ref[...], dm_ref[:, w:].astype(F32))
        da_ref[...] = da
        db_ref[...] = db
        dga_ref[...] += dga
        dgb_ref[...] += dgb

    row = pl.BlockSpec((tm, w), lambda i: (i, 0))
    vec = pl.BlockSpec((1, w), lambda i: (0, 0))
    return _call(
        body, name="merge_norm_bwd", grid=(m // tm,),
        out_shape=(jax.ShapeDtypeStruct((m, w), F32), jax.ShapeDtypeStruct((m, w), F32),
                   jax.ShapeDtypeStruct((1, w), F32), jax.ShapeDtypeStruct((1, w), F32)),
        in_specs=[row, row, pl.BlockSpec((tm, 2 * w), lambda i: (i, 0)), vec, vec],
        out_specs=(row, row, vec, vec), compiler_params=_cp("arbitrary"),
    )(oa, ob, dmerged, ga, gb)


def sgu_bwd(proj, d_oa, wm, wmt, ng, nb, bst):
    m = proj.shape[0]
    tm = _row_tile(m)
    wd = SGU_GROUPS * CHUNK

    def body(u_ref, v_ref, do_ref, wm_ref, wmt_ref, ng_ref, nb_ref, bst_ref,
             dp_ref, dw_ref, dbt_ref, dng_ref, dnb_ref):
        @pl.when(pl.program_id(0) == 0)
        def _():
            dw_ref[...] = jnp.zeros_like(dw_ref)
            dbt_ref[...] = jnp.zeros_like(dbt_ref)
            dng_ref[...] = jnp.zeros_like(dng_ref)
            dnb_ref[...] = jnp.zeros_like(dnb_ref)

        causal = lax.broadcasted_iota(jnp.int32, (CHUNK, CHUNK), 0) >= lax.broadcasted_iota(jnp.int32, (CHUNK, CHUNK), 1)
        for c in range(tm // CHUNK):
            rs = slice(c * CHUNK, (c + 1) * CHUNK)
            for g in range(SGU_GROUPS):
                cs = slice(g * CHUNK, (g + 1) * CHUNK)
                ug, xh, rstd, vn, mixed = _sgu_core(u_ref, v_ref, wm_ref, ng_ref, nb_ref, bst_ref, g, c)
                do = do_ref[rs, cs]
                dug = do * mixed
                dmix = do * ug
                dmb = dmix.astype(BF)
                dbt_ref[g] += jnp.sum(dmix, axis=-1, keepdims=True)
                dw_ref[g] += jnp.where(causal, _dot_nt(dmb, vn.astype(BF)), 0.0)
                dvn = _dot(wmt_ref[g], dmb)
                dng_ref[g:g + 1, :] += jnp.sum(dvn * xh, axis=0, keepdims=True)
                dnb_ref[g:g + 1, :] += jnp.sum(dvn, axis=0, keepdims=True)
                dxh = dvn * ng_ref[g:g + 1, :]
                dvg = rstd * (dxh - jnp.mean(dxh, axis=-1, keepdims=True)
                              - xh * jnp.mean(dxh * xh, axis=-1, keepdims=True))
                dp_ref[rs, cs] = (dug * _gelu_grad(u_ref[rs, cs].astype(F32))).astype(BF)
                dp_ref[rs, wd + g * CHUNK:wd + (g + 1) * CHUNK] = (dvg * _gelu_grad(v_ref[rs, cs].astype(F32))).astype(BF)

    full = lambda shape: pl.BlockSpec(shape, lambda i: (0,) * len(shape))
    return _call(
        body, name="sgu_bwd", grid=(m // tm,),
        out_shape=(jax.ShapeDtypeStruct((m, 2 * wd), BF), jax.ShapeDtypeStruct(wm.shape, F32),
                   jax.ShapeDtypeStruct(bst.shape, F32), jax.ShapeDtypeStruct(ng.shape, F32),
                   jax.ShapeDtypeStruct(nb.shape, F32)),
        in_specs=[pl.BlockSpec((tm, wd), lambda i: (i, 0)), pl.BlockSpec((tm, wd), lambda i: (i, 1)),
                  pl.BlockSpec((tm, wd), lambda i: (i, 0)),
                  full(wm.shape), full(wmt.shape), full(ng.shape), full(nb.shape), full(bst.shape)],
        out_specs=(pl.BlockSpec((tm, 2 * wd), lambda i: (i, 0)), full(wm.shape), full(bst.shape), full(ng.shape),
                   full(nb.shape)),
        compiler_params=_cp("arbitrary"),
    )(proj, proj, d_oa, wm, wmt, ng, nb, bst)


def sb_bwd(proj, tot, d_ob, comm=None):
    m = proj.shape[0]
    tq, tk = _sb_tiles(m)
    ndiag = tq // tk

    def body(q_ref, k_ref, v_ref, tot_ref, do_ref, dq_ref, dk_ref, dv_ref, qs, dos, tots, dqa, cl1, cg):
        i = pl.program_id(1)

        @pl.when(i == 0)
        def _():
            dk_ref[...] = jnp.zeros_like(dk_ref)
            dv_ref[...] = jnp.zeros_like(dv_ref)

        nfull = i * ndiag
        heads = _head_masks()
        r_io = lax.broadcasted_iota(jnp.int32, (tk, tk), 0)
        c_io = lax.broadcasted_iota(jnp.int32, (tk, tk), 1)
        incl = (r_io <= c_io).astype(BF)
        excl = (r_io < c_io).astype(BF)
        qv = q_ref[...]
        dov = do_ref[...].astype(BF)
        totv = tot_ref[...]
        for hd in range(2):
            qs[hd] = jnp.where(heads[hd], qv, jnp.zeros_like(qv)) * SB_SCALE
            dos[hd] = jnp.where(heads[hd], dov, jnp.zeros_like(dov))
            tots[hd] = jnp.max(jnp.where(heads[hd], totv, -jnp.inf), axis=-1, keepdims=True)
        dqa[...] = jnp.zeros_like(dqa)
        cl1[...] = jnp.zeros_like(cl1)
        cg[...] = jnp.zeros_like(cg)

        def block(ks, r0, diag):
            kb = k_ref[pl.ds(ks, tk), :]
            vb = v_ref[pl.ds(ks, tk), :]
            chains = _sb_chains(r0, tq, 256)
            masks = [None if diag is None else _sb_diag_mask(ra, rb, diag, tk) for _, ra, rb in chains]
            qc = [qs[hd, ra:rb, :] for hd, ra, rb in chains]
            doc = [dos[hd, ra:rb, :] for hd, ra, rb in chains]
            zs = [_dot_nt(q, kb) for q in qc]
            das = [_dot_nt(do, vb) for do in doc]
            s1 = []
            for z, mask in zip(zs, masks):
                ls, l1 = _log_sigmoid_pair(z, mask)
                s1.append((ls, l1, _dot2(l1, incl)))
            s2 = []
            for (hd, ra, rb), (ls, l1, pre), da, mask in zip(chains, s1, das, masks):
                a = jnp.exp(ls + (tots[hd, ra:rb, :] - (pre + cl1[hd, ra:rb, :])))
                if mask is not None:
                    a = jnp.where(mask, a, 0.0)
                gmat = a * da
                s2.append((a, gmat, _dot2(gmat, excl)))
            dk_sum = dv_sum = None
            for n, (hd, ra, rb) in enumerate(chains):
                a, gmat, pref = s2[n]
                sg = jnp.exp(s1[n][0])
                dz = gmat * (1.0 - sg) - (pref + cg[hd, ra:rb, :]) * sg
                if masks[n] is not None:
                    dz = jnp.where(masks[n], dz, 0.0)
                dz = dz.astype(BF)
                dqa[hd, ra:rb, :] += _dot(dz, kb)
                dk_t = _dot_tn(dz, qc[n])
                dv_t = _dot_tn(a.astype(BF), doc[n])
                dk_sum = dk_t if dk_sum is None else dk_sum + dk_t
                dv_sum = dv_t if dv_sum is None else dv_sum + dv_t
            for n, (hd, ra, rb) in enumerate(chains):
                cl1[hd, ra:rb, :] += jnp.sum(s1[n][1], axis=-1, keepdims=True)
                cg[hd, ra:rb, :] += jnp.sum(s2[n][1], axis=-1, keepdims=True)
            dk_ref[pl.ds(ks, tk), :] += dk_sum
            dv_ref[pl.ds(ks, tk), :] += dv_sum

        def step(j, carry):
            block(pl.multiple_of(j * tk, tk), 0, None)
            return carry

        lax.fori_loop(0, nfull, step, 0)
        for d in range(ndiag):
            block(pl.multiple_of((nfull + d) * tk, tk), d * tk, d)
        dq_ref[...] = jnp.where(heads[0], dqa[0], dqa[1]) * SB_SCALE

    qb = 2 * 512 // LANES
    tile = pl.BlockSpec((tq, LANES), lambda p, i: (i, p))
    seq = pl.BlockSpec((m, LANES), lambda p, i: (0, p))
    out = jax.ShapeDtypeStruct((m, 512), F32)
    return _call_with_comm(
        body, comm, (proj, proj, proj, tot, d_ob), name="sb_bwd", grid=(4, m // tq), out_shape=(out, out, out),
        in_specs=[pl.BlockSpec((tq, LANES), lambda p, i: (i, qb + p)),
                  pl.BlockSpec((m, LANES), lambda p, i: (0, qb + 4 + p)),
                  pl.BlockSpec((m, LANES), lambda p, i: (0, qb + 8 + p)), tile, tile],
        out_specs=(tile, seq, seq),
        scratch_shapes=[pltpu.VMEM((2, tq, LANES), BF), pltpu.VMEM((2, tq, LANES), BF), pltpu.VMEM((2, tq, 1), F32),
                        pltpu.VMEM((2, tq, LANES), F32), pltpu.VMEM((2, tq, 1), F32), pltpu.VMEM((2, tq, 1), F32)],
        compiler_params=_cp("arbitrary", "arbitrary"),
    )


def xa_bwd(xq, kv, d_o):
    m, d = xq.shape
    mm = kv.shape[0]
    tm = _row_tile(m)

    def body(q_ref, kv_ref, do_ref, dq_ref, dkv_ref):
        @pl.when(pl.program_id(0) == 0)
        def _():
            dkv_ref[...] = jnp.zeros_like(dkv_ref)

        for hd in range(XA_HEADS):
            cs = slice(hd * XA_HEAD_DIM, (hd + 1) * XA_HEAD_DIM)
            vs = slice(d + hd * XA_HEAD_DIM, d + (hd + 1) * XA_HEAD_DIM)
            qh = q_ref[:, cs]
            kh = kv_ref[:, cs]
            doh = do_ref[:, cs]
            p = _xa_probs(qh, kh)
            dp = _dot_nt(doh, kv_ref[:, vs])
            ds = (p * (dp - jnp.sum(p * dp, axis=-1, keepdims=True))).astype(BF)
            dq_ref[:, cs] = (_dot(ds, kh) * XA_SCALE).astype(BF)
            dkv_ref[:, cs] += _dot_tn(ds, qh) * XA_SCALE
            dkv_ref[:, vs] += _dot_tn(p.astype(BF), doh)

    row = pl.BlockSpec((tm, d), lambda i: (i, 0))
    whole = pl.BlockSpec((mm, 2 * d), lambda i: (0, 0))
    return _call(
        body, name="xa_bwd", grid=(m // tm,),
        out_shape=(jax.ShapeDtypeStruct((m, d), BF), jax.ShapeDtypeStruct((mm, 2 * d), F32)),
        in_specs=[row, whole, row], out_specs=(row, whole), compiler_params=_cp("arbitrary"),
    )(xq, kv, d_o)


def adamw(w, g, mom, vel, g_index=None):
    r, c = w.shape
    tr = r
    for cand in range(512, 7, -8):
        if r % cand == 0:
            tr = cand
            break

    def body(w_ref, g_ref, m_ref, v_ref, go_ref, d_ref, nm_ref, nv_ref):
        gv = g_ref[...]
        mn = ADAM_B1 * m_ref[...] + (1.0 - ADAM_B1) * gv
        vn = ADAM_B2 * v_ref[...] + (1.0 - ADAM_B2) * (gv * gv)
        m_hat = mn / (1.0 - ADAM_B1 ** ADAM_STEP)
        v_hat = vn / (1.0 - ADAM_B2 ** ADAM_STEP)
        go_ref[...] = gv
        d_ref[...] = -ADAM_LR * (m_hat / (jnp.sqrt(v_hat) + ADAM_EPS) + ADAM_WD * w_ref[...])
        nm_ref[...] = mn
        nv_ref[...] = vn

    spec = pl.BlockSpec((tr, c), lambda i: (i, 0))
    gspec = spec if g_index is None else pl.BlockSpec((None, tr, c), lambda i: (g_index, i, 0))
    out = jax.ShapeDtypeStruct((r, c), F32)
    return _call(
        body, name="adamw", out_shape=(out, out, out, out), grid=(r // tr,), in_specs=[spec, gspec, spec, spec],
        out_specs=(spec, spec, spec, spec), compiler_params=_cp("parallel"),
    )(w, g, mom, vel)


def _place():
    x, y, c = lax.axis_index("x"), lax.axis_index("y"), lax.axis_index("c")
    others = [(1 - x, y), (x, 1 - y), (1 - x, 1 - y)]
    return x, y, c, others


_HALF = {"A1": (1, 512), "A2": (1, 512), "B1": (0, 352), "B2": (0, 352), "C": (1, 128), "D": (0, 512), "E": (0, 512),
         "A1T": (1, 352), "A2T": (1, 352)}
SET_EARLY = ("A1", "B1", "D")
SET_LATE = ("A2", "B2", "C", "E")
GRAD_EARLY = ("A1T", "B1", "D")
GRAD_LATE = ("A2T", "B2", "C", "E")
_HBM = pl.BlockSpec(memory_space=pl.ANY)


def _half_of(ref, name, hc, lead=0):
    axis, size = _HALF[name]
    idx = [slice(None)] * (lead + axis) + [pl.ds(hc * size, size)]
    return ref.at[tuple(idx)]


def _gather_slots(loc, names):
    me = 2 * lax.axis_index("x") + lax.axis_index("y")
    init = []
    for nm in names:
        full = lax.empty((N_CHIPS,) + loc[nm].shape, loc[nm].dtype)
        init.append(lax.dynamic_update_slice(full, loc[nm][None], (me,) + (0,) * loc[nm].ndim))
    return init


def _gather_ici(names, src, out, send, recv, sends=True, arrivals=True):
    x, y, c, others = _place()
    me = 2 * x + y
    out_sends, out_arrivals = [], []
    for a, nm in enumerate(names):
        for j, (ox, oy) in enumerate(others):
            sems = dict(send_sem=send.at[3 * a + j], recv_sem=recv.at[3 * a + j], device_id_type=MESH)
            if sends:
                out_sends.append(pltpu.make_async_remote_copy(
                    src_ref=_half_of(src[a], nm, c), dst_ref=_half_of(out[a].at[me], nm, c), device_id=(ox, oy, c),
                    **sems))
            if arrivals:
                landed = _half_of(out[a].at[2 * ox + oy], nm, c)
                out_arrivals.append(pltpu.make_async_remote_copy(src_ref=landed, dst_ref=landed, device_id=(x, y, c),
                                                                 **sems))
    return out_sends, out_arrivals


def _gather_d2d(names, given, out, send, recv):
    x, y, c, others = _place()
    sends, arrivals = [], []
    for a, nm in enumerate(names):
        for j, (ox, oy) in enumerate(others):
            sems = dict(send_sem=send.at[3 * a + j], recv_sem=recv.at[3 * a + j], device_id_type=MESH)
            sends.append(pltpu.make_async_remote_copy(
                src_ref=_half_of(given[a].at[2 * ox + oy], nm, c), dst_ref=_half_of(out[a].at[2 * ox + oy], nm, c),
                device_id=(x, y, 1 - c), **sems))
            landed = _half_of(out[a].at[2 * ox + oy], nm, 1 - c)
            arrivals.append(pltpu.make_async_remote_copy(src_ref=landed, dst_ref=landed, device_id=(x, y, c), **sems))
    return sends, arrivals


def gather_weights(loc, names):
    n = len(names)

    def body(*refs):
        src, given, out = refs[:n], refs[n:2 * n], refs[2 * n:3 * n]
        send1, recv1, send2, recv2 = refs[3 * n:3 * n + 4]
        first, landed = _gather_ici(names, src, out, send1, recv1)
        del given
        passed, arrivals = _gather_d2d(names, out, out, send2, recv2)
        for f in first:
            f.start()
        for l, p in zip(landed, passed):
            l.wait_recv()
            p.start()
        for a in arrivals:
            a.wait_recv()
        for f in first + passed:
            f.wait_send()

    init = _gather_slots(loc, names)
    res = _call(
        body, name="gather_weights", out_shape=tuple(jax.ShapeDtypeStruct(t.shape, t.dtype) for t in init),
        in_specs=[_HBM] * (2 * n), out_specs=(_HBM,) * n, input_output_aliases={n + a: a for a in range(n)},
        scratch_shapes=[pltpu.SemaphoreType.DMA((3 * n,))] * 4,
    )(*[loc[nm] for nm in names], *init)
    return dict(zip(names, res))


def gather_forward(bufs, names):
    n = len(names)

    def body(*refs):
        given, out = refs[:n], refs[n:2 * n]
        passed, arrivals = _gather_d2d(names, given, out, refs[2 * n], refs[2 * n + 1])
        for p in passed:
            p.start()
        for a in arrivals:
            a.wait_recv()
        for p in passed:
            p.wait_send()

    res = _call(
        body, name="gather_forward", out_shape=tuple(jax.ShapeDtypeStruct(t.shape, t.dtype) for t in bufs),
        in_specs=[_HBM] * n, out_specs=(_HBM,) * n, input_output_aliases={a: a for a in range(n)},
        scratch_shapes=[pltpu.SemaphoreType.DMA((3 * n,))] * 2,
    )(*bufs)
    return dict(zip(names, res))


class FusedComm:
    def __init__(self, ins, outs, aliases, n_sems, start, finish):
        self.ins, self.outs, self.aliases, self.n_sems, self.start, self.finish = ins, outs, aliases, n_sems, start, finish


def gather_comm(loc, names):
    n = len(names)

    def start(ins, outs, send, recv):
        for f in _gather_ici(names, ins[:n], outs, send, recv, arrivals=False)[0]:
            f.start()

    def finish(ins, outs, send, recv):
        first, landed = _gather_ici(names, ins[:n], outs, send, recv)
        for l in landed:
            l.wait_recv()
        for f in first:
            f.wait_send()

    init = _gather_slots(loc, names)
    return FusedComm([loc[nm] for nm in names] + init, [jax.ShapeDtypeStruct(t.shape, t.dtype) for t in init],
                     {n + a: a for a in range(n)}, 3 * n, start, finish)


def _half_shape(name, shape):
    axis, size = _HALF[name]
    s = list(shape)
    s[axis] = size
    return tuple(s)


def rs_to_sibling(grads, names):
    n = len(names)

    def body(*refs):
        src = dict(zip(names, refs[:n]))
        out = dict(zip(names, refs[n:2 * n]))
        send, recv = refs[2 * n], refs[2 * n + 1]
        x, y, c, _ = _place()
        copies = []
        for a, nm in enumerate(names):
            copies.append(pltpu.make_async_remote_copy(
                src_ref=_half_of(src[nm], nm, 1 - c, lead=1), dst_ref=out[nm], send_sem=send.at[a], recv_sem=recv.at[a],
                device_id=(x, y, 1 - c), device_id_type=MESH))
        for cpy in copies:
            cpy.start()
        for cpy in copies:
            cpy.wait()

    hbm = pl.BlockSpec(memory_space=pl.ANY)
    outs = tuple(jax.ShapeDtypeStruct((N_CHIPS,) + _half_shape(nm, grads[nm].shape[1:]), grads[nm].dtype)
                 for nm in names)
    res = _call(
        body, name="rs_to_sibling", out_shape=outs, in_specs=[hbm] * n, out_specs=(hbm,) * n,
        scratch_shapes=[pltpu.SemaphoreType.DMA((n,)), pltpu.SemaphoreType.DMA((n,))],
    )(*[grads[nm] for nm in names])
    return dict(zip(names, res))


def _tile2(shape):
    lead = shape[:-2]
    return lead, shape[-2:]


def add_halves(name, mine, got, c_idx):
    axis, size = _HALF[name]
    hshape = got.shape
    lead, last2 = hshape[:-2], hshape[-2:]
    nlead = len(lead)
    haxis = 1 + axis

    def body(c_ref, m_ref, g_ref, o_ref):
        del c_ref
        o_ref[...] = (m_ref[...].astype(F32) + g_ref[...].astype(F32)).astype(o_ref.dtype)

    blk = (None,) * nlead + last2

    def got_map(*idx):
        return tuple(idx[:nlead]) + (0, 0)

    def mine_map(*idx):
        lead_idx = list(idx[:nlead])
        c = idx[nlead][0]
        if haxis < nlead:
            lead_idx[haxis] = lead_idx[haxis] + c * size
            return tuple(lead_idx) + (0, 0)
        return tuple(lead_idx) + (c, 0)

    grid_spec = pltpu.PrefetchScalarGridSpec(
        num_scalar_prefetch=1, grid=lead,
        in_specs=[pl.BlockSpec(blk, mine_map), pl.BlockSpec(blk, got_map)],
        out_specs=pl.BlockSpec(blk, got_map))
    return _call(
        body, name="add_halves", out_shape=jax.ShapeDtypeStruct(hshape, got.dtype), grid_spec=grid_spec,
        compiler_params=_cp(*(("parallel",) * nlead)),
    )(c_idx, mine, got)


def _rs_ici(n, src, out, send, recv):
    x, y, c, others = _place()
    copies = []
    for a in range(n):
        for j, (ox, oy) in enumerate(others):
            copies.append(pltpu.make_async_remote_copy(
                src_ref=src[a].at[2 * ox + oy], dst_ref=out[a].at[j], send_sem=send.at[3 * a + j],
                recv_sem=recv.at[3 * a + j], device_id=(ox, oy, c), device_id_type=MESH))
    return copies


def _rs_out_shapes(summed, names):
    return [jax.ShapeDtypeStruct((3,) + summed[nm].shape[1:], summed[nm].dtype) for nm in names]


def rs_comm(summed, names):
    n = len(names)

    def start(ins, outs, send, recv):
        for cpy in _rs_ici(n, ins, outs, send, recv):
            cpy.start()

    def finish(ins, outs, send, recv):
        for cpy in _rs_ici(n, ins, outs, send, recv):
            cpy.wait()

    return FusedComm([summed[nm] for nm in names], _rs_out_shapes(summed, names), {}, 3 * n, start, finish)


def add_chips(name, summed, got, kc_idx, full_shape):
    axis, size = _HALF[name]
    hshape = summed.shape[1:]
    lead, last2 = hshape[:-2], hshape[-2:]
    nlead = len(lead)

    def body(kc_ref, s_ref, g0_ref, g1_ref, g2_ref, o_ref):
        del kc_ref
        o_ref[...] = ((s_ref[...].astype(F32) + g0_ref[...].astype(F32)) + g1_ref[...].astype(F32)) + g2_ref[...].astype(F32)

    blk = (None,) * (nlead + 1) + last2
    oblk = (None,) * nlead + last2

    def got_map(slot):
        return lambda *idx: (slot,) + tuple(idx[:nlead]) + (0, 0)

    def out_map(*idx):
        lead_idx = list(idx[:nlead])
        c = idx[-1][1]
        if axis < nlead:
            lead_idx[axis] = lead_idx[axis] + c * size
            return tuple(lead_idx) + (0, 0)
        return tuple(lead_idx) + (c, 0)

    grid_spec = pltpu.PrefetchScalarGridSpec(
        num_scalar_prefetch=1, grid=lead if nlead else (1,),
        in_specs=[pl.BlockSpec(blk, lambda *idx: (idx[-1][0],) + tuple(idx[:nlead]) + (0, 0)),
                  pl.BlockSpec(blk, got_map(0)), pl.BlockSpec(blk, got_map(1)), pl.BlockSpec(blk, got_map(2))],
        out_specs=pl.BlockSpec(oblk, out_map))
    return _call(
        body, name="add_chips", out_shape=jax.ShapeDtypeStruct(full_shape, F32), grid_spec=grid_spec,
        compiler_params=_cp(*(("parallel",) * max(nlead, 1))),
    )(kc_idx, summed, got, got, got)


def rs_replicate(shards, names):
    n = len(names)

    def body(*refs):
        given = dict(zip(names, refs[:n]))
        buf = dict(zip(names, refs[n:2 * n]))
        send, recv = refs[2 * n], refs[2 * n + 1]
        x, y, c, _ = _place()
        copies = []
        for a, nm in enumerate(names):
            copies.append(pltpu.make_async_remote_copy(
                src_ref=_half_of(given[nm], nm, c), dst_ref=_half_of(buf[nm], nm, c), send_sem=send.at[a],
                recv_sem=recv.at[a], device_id=(x, y, 1 - c), device_id_type=MESH))
        for cpy in copies:
            cpy.start()
        for a, nm in enumerate(names):
            other = _half_of(buf[nm], nm, 1 - c)
            pltpu.make_async_remote_copy(src_ref=other, dst_ref=other, send_sem=send.at[a], recv_sem=recv.at[a],
                                         device_id=(x, y, 1 - c), device_id_type=MESH).wait_recv()
        for cpy in copies:
            cpy.wait_send()

    hbm = pl.BlockSpec(memory_space=pl.ANY)
    outs = tuple(jax.ShapeDtypeStruct(shards[nm].shape, F32) for nm in names)
    res = _call(
        body, name="rs_replicate", out_shape=outs, in_specs=[hbm] * n, out_specs=(hbm,) * n,
        input_output_aliases={a: a for a in range(n)},
        scratch_shapes=[pltpu.SemaphoreType.DMA((n,)), pltpu.SemaphoreType.DMA((n,))],
    )(*[shards[nm] for nm in names])
    return dict(zip(names, res))


def allreduce_small(v):
    r = v.shape[0]
    h = r // 2

    def body(v_ref, o_ref, sib, slots, send, recv):
        x, y, c, others = _place()
        me = 2 * x + y
        sibling = (x, y, 1 - c)
        mine = pl.ds(pl.multiple_of(c * h, 8), h)
        theirs = pl.ds(pl.multiple_of((1 - c) * h, 8), h)
        swap = pltpu.make_async_remote_copy(src_ref=v_ref.at[theirs], dst_ref=sib, send_sem=send.at[0],
                                            recv_sem=recv.at[0], device_id=sibling, device_id_type=MESH)
        swap.start()
        swap.wait()
        slots[me] = v_ref[mine] + sib[...]
        sends = [pltpu.make_async_remote_copy(src_ref=slots.at[me], dst_ref=slots.at[me], send_sem=send.at[1 + j],
                                              recv_sem=recv.at[1 + j], device_id=(ox, oy, c), device_id_type=MESH)
                 for j, (ox, oy) in enumerate(others)]
        for s in sends:
            s.start()
        for j, (ox, oy) in enumerate(others):
            got = slots.at[2 * ox + oy]
            pltpu.make_async_remote_copy(src_ref=got, dst_ref=got, send_sem=send.at[1 + j], recv_sem=recv.at[1 + j],
                                         device_id=(x, y, c), device_id_type=MESH).wait_recv()
        for s in sends:
            s.wait_send()
        o_ref[mine] = (slots[0] + slots[1]) + (slots[2] + slots[3])
        back = pltpu.make_async_remote_copy(src_ref=o_ref.at[mine], dst_ref=o_ref.at[mine], send_sem=send.at[4],
                                            recv_sem=recv.at[4], device_id=sibling, device_id_type=MESH)
        back.start()
        pltpu.make_async_remote_copy(src_ref=o_ref.at[theirs], dst_ref=o_ref.at[theirs], send_sem=send.at[4],
                                     recv_sem=recv.at[4], device_id=sibling, device_id_type=MESH).wait_recv()
        back.wait_send()

    vm = pl.BlockSpec(memory_space=pltpu.VMEM)
    return _call(
        body, name="allreduce_small", out_shape=jax.ShapeDtypeStruct(v.shape, F32), in_specs=[vm], out_specs=vm,
        scratch_shapes=[pltpu.VMEM((h, LANES), F32), pltpu.VMEM((N_CHIPS, h, LANES), F32),
                        pltpu.SemaphoreType.DMA((5,)), pltpu.SemaphoreType.DMA((5,))],
    )(v)


def local_step(x, mem, target, ga1, small, ffn1_up, fwd_sb, bwd_sb, ffn1_mid, ffn1_da):
    causal = jnp.tril(jnp.ones((CHUNK, CHUNK), dtype=bool))
    w_s = jnp.where(causal[None], small["sgu_w_s"], 0.0)
    wm = w_s.astype(BF)
    wmt = jnp.swapaxes(w_s, 1, 2).astype(BF)
    bst = small["sgu_b_s"].reshape(SGU_GROUPS, CHUNK, 1)
    ng, nbias = small["sgu_norm_g"], small["sgu_norm_b"]

    a1 = rms_fwd(x, small["ffn1_pre_g"])
    g1, u1, hid1, rest = ffn1_up(a1)
    gb1, gd = rest["B1"][:, None], rest["D"]
    f1, h1, n1 = mm_res(hid1, gb1, 0, x, small["ffn1_post_g"], 0.5, small["mix_pre_g"])
    proj = mm_cb(n1, gd)
    oa = sgu_fwd(proj, wm, ng, nbias, bst)
    ob, tot, late = fwd_sb(proj)
    ga2, gb2, gc, ge = late["A2"], late["B2"][:, None], late["C"], late["E"]
    merged = merge_norm(oa, ob, small["sgu_out_g"], small["sb_out_g"])
    mo, h2, xn = mm_res(merged, gc, 0, h1, small["mix_post_g"], 1.0, small["xa_pre_g"])
    memn = rms_fwd(mem, small["mem_norm_g"])
    kv = mm_cb(memn, ge)
    xq = mm_res_plain(xn, gc, 1)
    o = xa_fwd(xq, kv)
    cc, h3, a2 = mm_res(o, gc, 2, h2, small["xa_post_g"], 1.0, small["ffn2_pre_g"])
    (g2, u2, hid2), _ = ffn_up(a2, ga2, 0, 1)
    f2, h4, _ = mm_res(hid2, gb2, 0, h3, small["ffn2_post_g"], 0.5, small["final_norm_g"])

    gate_up_t = (N_CHIPS, 2, ga1.shape[3], ga1.shape[2])
    dga1, dga2 = lax.empty(gate_up_t, BF), lax.empty(gate_up_t, BF)
    dgb1, dgb2 = lax.empty(gb1.shape, BF), lax.empty(gb2.shape, BF)
    dgc = lax.empty(gc.shape, BF)
    dgd = lax.empty(gd.shape, BF)
    dge = lax.empty(ge.shape, BF)
    sg = {}

    r = norm_bwd(h4, small["final_norm_g"], target=target, f_prev=f2, gp_prev=small["ffn2_post_g"], alpha_prev=0.5)
    loss_tile, dh4, df2 = r["loss"], r["d_h"], r["d_f"]
    sg["final_norm_g"], sg["ffn2_post_g"] = r["d_gn"], r["d_gp"]

    dg2, du2 = ffn_bwd_act(df2, gb2, 0, g2, u2)
    dgb2 = grad_rb(hid2, df2, dgb2, 0)
    dga2 = grad_rb(dg2, a2, dga2, 0)
    dga2 = grad_rb(du2, a2, dga2, 1)
    r = mm_nt_cb([(dg2, ga2, 0), (du2, ga2, 1)], D_MODEL, F32,
                 norm=dict(h=h3, gn=small["ffn2_pre_g"], d_res=dh4, f_prev=cc, gp_prev=small["xa_post_g"], alpha_prev=1.0))
    dh3, dc = r["d_h"], r["d_f"]
    sg["ffn2_pre_g"], sg["xa_post_g"] = r["d_gn"], r["d_gp"]

    d_o = mm_nt_rb(dc, gc, 2, BF)
    dgc = grad_rb(o, dc, dgc, 2)
    dxq, dkv = xa_bwd(xq, kv, d_o)
    dkvb = dkv.astype(BF)
    dge = grad_cb(memn, dkvb, dge)
    dmemn = mm_nt_cb([(dkvb, ge, None)], D_MODEL, F32)
    sg["mem_norm_g"] = norm_bwd(mem, small["mem_norm_g"], d_a=dmemn)["d_gn"]
    dgc = grad_rb(xn, dxq, dgc, 1)
    r = mm_nt_rb(dxq, gc, 1, F32,
                 norm=dict(h=h2, gn=small["xa_pre_g"], d_res=dh3, f_prev=mo, gp_prev=small["mix_post_g"], alpha_prev=1.0))
    dh2, dmo = r["d_h"], r["d_f"]
    sg["xa_pre_g"], sg["mix_post_g"] = r["d_gn"], r["d_gp"]

    dmerged = mm_nt_rb(dmo, gc, 0, BF)
    dgc = grad_rb(merged, dmo, dgc, 0)
    d_oa, d_ob, sg["sgu_out_g"], sg["sb_out_g"] = merge_norm_bwd(oa, ob, dmerged, small["sgu_out_g"], small["sb_out_g"])
    dp_uv, dws, dbt, sg["sgu_norm_g"], sg["sgu_norm_b"] = sgu_bwd(proj, d_oa, wm, wmt, ng, nbias, bst)
    sg["sgu_w_s"] = dws
    sg["sgu_b_s"] = dbt.reshape(SGU_GROUPS, CHUNK)
    late_grads = {"A2T": dga2, "B2": dgb2.reshape(late["B2"].shape), "C": dgc, "E": dge}
    dq, dk, dv, state = bwd_sb(proj, tot, d_ob, late_grads)
    dproj = jnp.concatenate([dp_uv, dq.astype(BF), dk.astype(BF), dv.astype(BF)], axis=1)
    dgd = grad_cb(n1, dproj, dgd)
    r = mm_nt_cb([(dproj, gd, None)], D_MODEL, F32,
                 norm=dict(h=h1, gn=small["mix_pre_g"], d_res=dh2, f_prev=f1, gp_prev=small["ffn1_post_g"], alpha_prev=0.5))
    dh1, df1 = r["d_h"], r["d_f"]
    sg["mix_pre_g"], sg["ffn1_post_g"] = r["d_gn"], r["d_gp"]

    dg1, du1 = ffn_bwd_act(df1, gb1, 0, g1, u1)
    dgb1 = grad_rb(hid1, df1, dgb1, 0)
    mid_state, comm_b1, comm_d = ffn1_mid({"B1": dgb1.reshape(rest["B1"].shape), "D": dgd})
    res = grad_rb(dg1, a1, dga1, 0, comm=comm_b1)
    dga1, got_b1 = res if comm_b1 is not None else (res, ())
    res = grad_rb(du1, a1, dga1, 1, comm=comm_d)
    dga1, got_d = res if comm_d is not None else (res, ())
    r, da_state = ffn1_da([(dg1, ga1, 0), (du1, ga1, 1)], {"A1T": dga1},
                          dict(h=x, gn=small["ffn1_pre_g"], d_res=dh1))
    state1 = (mid_state, got_b1, got_d, da_state)
    grad_x = r["d_h"]
    sg["ffn1_pre_g"] = r["d_gn"]
    return loss_tile, grad_x, sg, state, state1


def mm_res_plain(a, w, which):
    m = a.shape[0]
    kb, n = w.shape[2], w.shape[3]
    tm = _row_tile(m)

    def body(a_ref, w_ref, o_ref):
        acc = None
        for k in range(N_CHIPS):
            t = _dot(a_ref[:, k * kb:(k + 1) * kb], w_ref[k])
            acc = t if acc is None else acc + t
        o_ref[...] = acc.astype(BF)

    return _call(
        body, name="mm_rb", out_shape=jax.ShapeDtypeStruct((m, n), BF), grid=(m // tm,),
        in_specs=[pl.BlockSpec((tm, N_CHIPS * kb), lambda i: (i, 0)),
                  pl.BlockSpec((N_CHIPS, None, kb, n), lambda i: (0, which, 0, 0))],
        out_specs=pl.BlockSpec((tm, n), lambda i: (i, 0)), compiler_params=_cp("parallel"),
    )(a, w)


_BIG = ("ffn1_w_gate", "ffn1_w_up", "ffn1_w_down", "w_in", "w_out", "xa_w_q", "xa_w_kv", "xa_w_o",
        "ffn2_w_gate", "ffn2_w_up", "ffn2_w_down")
_SMALL = ("ffn1_pre_g", "ffn1_post_g", "mix_pre_g", "mix_post_g", "sgu_norm_g", "sgu_norm_b", "sgu_w_s", "sgu_b_s",
          "sgu_out_g", "sb_out_g", "xa_pre_g", "xa_post_g", "mem_norm_g", "ffn2_pre_g", "ffn2_post_g", "final_norm_g")
_WEIGHTS = ("ffn1_pre_g", "ffn1_post_g", "ffn1_w_gate", "ffn1_w_up", "ffn1_w_down", "mix_pre_g", "mix_post_g", "w_in",
            "sgu_norm_g", "sgu_norm_b", "sgu_w_s", "sgu_b_s", "sgu_out_g", "sb_out_g", "w_out", "xa_pre_g", "xa_post_g",
            "mem_norm_g", "xa_w_q", "xa_w_kv", "xa_w_o", "ffn2_pre_g", "ffn2_post_g", "ffn2_w_gate", "ffn2_w_up",
            "ffn2_w_down", "final_norm_g")
_SLOT = {"ffn1_w_gate": ("A1T", 0, True), "ffn1_w_up": ("A1T", 1, True), "ffn2_w_gate": ("A2T", 0, True),
         "ffn2_w_up": ("A2T", 1, True), "ffn1_w_down": ("B1", None, False), "ffn2_w_down": ("B2", None, False),
         "w_out": ("C", 0, False), "xa_w_q": ("C", 1, False), "xa_w_o": ("C", 2, False), "w_in": ("D", None, False),
         "xa_w_kv": ("E", None, False)}


def _pack_small(vals):
    return jnp.concatenate([vals[nm].reshape(-1, LANES) for nm in _SMALL], axis=0)


def _unpack_small(packed, shapes):
    out, pos = {}, 0
    for nm in _SMALL:
        rows = math.prod(shapes[nm]) // LANES
        out[nm] = packed[pos:pos + rows].reshape(shapes[nm])
        pos += rows
    return out


def kernel(x, mem, ffn1_pre_g, ffn1_post_g, ffn1_w_gate, ffn1_w_up, ffn1_w_down, mix_pre_g, mix_post_g, w_in, sgu_norm_g, sgu_norm_b, sgu_w_s, sgu_b_s, sgu_out_g, sb_out_g, w_out, xa_pre_g, xa_post_g, mem_norm_g, xa_w_q, xa_w_kv, xa_w_o, ffn2_pre_g, ffn2_post_g, ffn2_w_gate, ffn2_w_up, ffn2_w_down, final_norm_g, loss_target, m_ffn1_pre_g, m_ffn1_post_g, m_ffn1_w_gate, m_ffn1_w_up, m_ffn1_w_down, m_mix_pre_g, m_mix_post_g, m_w_in, m_sgu_norm_g, m_sgu_norm_b, m_sgu_w_s, m_sgu_b_s, m_sgu_out_g, m_sb_out_g, m_w_out, m_xa_pre_g, m_xa_post_g, m_mem_norm_g, m_xa_w_q, m_xa_w_kv, m_xa_w_o, m_ffn2_pre_g, m_ffn2_post_g, m_ffn2_w_gate, m_ffn2_w_up, m_ffn2_w_down, m_final_norm_g, v_ffn1_pre_g, v_ffn1_post_g, v_ffn1_w_gate, v_ffn1_w_up, v_ffn1_w_down, v_mix_pre_g, v_mix_post_g, v_w_in, v_sgu_norm_g, v_sgu_norm_b, v_sgu_w_s, v_sgu_b_s, v_sgu_out_g, v_sb_out_g, v_w_out, v_xa_pre_g, v_xa_post_g, v_mem_norm_g, v_xa_w_q, v_xa_w_kv, v_xa_w_o, v_ffn2_pre_g, v_ffn2_post_g, v_ffn2_w_gate, v_ffn2_w_up, v_ffn2_w_down, v_final_norm_g):
    env = dict(locals())
    w = {nm: env[nm] for nm in _WEIGHTS}
    mom = {nm: env["m_" + nm] for nm in _WEIGHTS}
    vel = {nm: env["v_" + nm] for nm in _WEIGHTS}

    loc = {
        "A1": jnp.stack([w["ffn1_w_gate"][0], w["ffn1_w_up"][0]]).astype(BF),
        "A2": jnp.stack([w["ffn2_w_gate"][0], w["ffn2_w_up"][0]]).astype(BF),
        "B1": w["ffn1_w_down"][0].astype(BF),
        "B2": w["ffn2_w_down"][0].astype(BF),
        "C": jnp.stack([w["w_out"][0], w["xa_w_q"][0], w["xa_w_o"][0]]).astype(BF),
        "D": w["w_in"][0].astype(BF),
        "E": w["xa_w_kv"][0].astype(BF),
    }
    ga1 = gather_weights(loc, ("A1",))["A1"]
    c_idx = lax.axis_index("c").astype(jnp.int32).reshape(1)
    kc_idx = jnp.stack([2 * lax.axis_index("x") + lax.axis_index("y"), lax.axis_index("c")]).astype(jnp.int32)
    early_rest = tuple(nm for nm in SET_EARLY if nm != "A1")

    def ffn1_up(a1):
        (g1, u1, hid1), bufs = ffn_up(a1, ga1, 0, 1, comm=gather_comm(loc, early_rest))
        return g1, u1, hid1, gather_forward(bufs, early_rest)

    def fwd_sb(proj):
        (ob, tot), bufs = sb_fwd(proj, comm=gather_comm(loc, SET_LATE))
        return ob, tot, gather_forward(bufs, SET_LATE)

    def reduce_to_pairs(grads, names):
        from_sib = rs_to_sibling(grads, names)
        return {nm: add_halves(nm, grads[nm], from_sib[nm], c_idx) for nm in names}

    def bwd_sb(proj, tot, d_ob, late_grads):
        pairs = reduce_to_pairs(late_grads, GRAD_LATE)
        (dq, dk, dv), got = sb_bwd(proj, tot, d_ob, comm=rs_comm(pairs, GRAD_LATE))
        return dq, dk, dv, (pairs, dict(zip(GRAD_LATE, got)))

    def ffn1_mid(bd_grads):
        pairs = reduce_to_pairs(bd_grads, ("B1", "D"))
        return pairs, rs_comm(pairs, ("B1",)), rs_comm(pairs, ("D",))

    def ffn1_da(mm_pairs, gate_up_grads, norm):
        pairs = reduce_to_pairs(gate_up_grads, ("A1T",))
        da1, got = mm_nt_cb(mm_pairs, D_MODEL, F32, comm=rs_comm(pairs, ("A1T",)))
        return norm_bwd(norm["h"], norm["gn"], d_a=da1, d_res=norm["d_res"]), (pairs, got)

    small = {nm: w[nm][0] for nm in _SMALL}
    for nm in ("ffn1_pre_g", "ffn1_post_g", "mix_pre_g", "mix_post_g", "sgu_out_g", "sb_out_g", "xa_pre_g", "xa_post_g",
               "mem_norm_g", "ffn2_pre_g", "ffn2_post_g", "final_norm_g"):
        small[nm] = w[nm]
    loss_tile, grad_x, small_g, (late_pairs, late_got), (bd_pairs, got_b1, got_d, (a1_pairs, got_a1)) = local_step(
        x[0], mem[0], loss_target[0], ga1, small, ffn1_up, fwd_sb, bwd_sb, ffn1_mid, ffn1_da)
    pair_sum = {**late_pairs, **bd_pairs, **a1_pairs}
    from_chips = {**late_got, "B1": got_b1[0], "D": got_d[0], "A1T": got_a1[0]}

    small_shapes = {nm: w[nm].shape for nm in _SMALL}
    flat = _pack_small(small_g)
    n_small = flat.shape[0]
    pad = jnp.zeros((-n_small % 8, LANES), F32)
    packed = allreduce_small(jnp.concatenate([flat, pad, loss_tile], axis=0))
    loss = packed[-8, 0]
    g_small = _unpack_small(packed[:n_small], small_shapes)

    grad_groups = GRAD_EARLY + GRAD_LATE
    shard_shape = {nm: pair_sum[nm].shape[1:] for nm in grad_groups}
    for nm in grad_groups:
        axis, size = _HALF[nm]
        shard_shape[nm] = shard_shape[nm][:axis] + (2 * size,) + shard_shape[nm][axis + 1:]
    shard = rs_replicate({nm: add_chips(nm, pair_sum[nm], from_chips[nm], kc_idx, shard_shape[nm])
                          for nm in grad_groups}, grad_groups)

    grads, delta, new_m, new_v = {}, {}, {}, {}
    for nm in _BIG:
        grp, idx, transposed = _SLOT[nm]
        shape = w[nm].shape
        if transposed:
            res = adamw(w[nm][0].T, shard[grp], mom[nm][0].T, vel[nm][0].T, g_index=idx)
            res = [t.T for t in res]
        else:
            res = adamw(w[nm][0], shard[grp], mom[nm][0], vel[nm][0], g_index=idx)
        grads[nm], delta[nm], new_m[nm], new_v[nm] = (t.reshape(shape) for t in res)
    _, d, nm_, nv_ = adamw(_pack_small(w), _pack_small(g_small), _pack_small(mom), _pack_small(vel))
    d, nm_, nv_ = (_unpack_small(t, small_shapes) for t in (d, nm_, nv_))
    for nm in _SMALL:
        grads[nm], delta[nm], new_m[nm], new_v[nm] = g_small[nm], d[nm], nm_[nm], nv_[nm]

    return (loss, grad_x[None], *[grads[nm] for nm in _WEIGHTS], *[delta[nm] for nm in _WEIGHTS],
            *[new_m[nm] for nm in _WEIGHTS], *[new_v[nm] for nm in _WEIGHTS])
```

```python
import functools
import math

import jax
import jax.numpy as jnp
from jax import lax
from jax.experimental import pallas as pl
from jax.experimental.pallas import tpu as pltpu

F32 = jnp.float32
BF = jnp.bfloat16
EPS = 1e-6
D_MODEL = 1024
N_CHIPS = 4
FF_BLOCK = 704
IN_BLOCK = 640
KV_BLOCK = 512
ROW_BLOCK = 256
SGU_GROUPS = 4
CHUNK = 128
SB_HEAD_DIM = 64
SB_SCALE = SB_HEAD_DIM ** -0.5
XA_HEADS = 4
XA_HEAD_DIM = 256
XA_SCALE = XA_HEAD_DIM ** -0.5
LANES = 128
VMEM_LIMIT = 56 * 1024 * 1024
MESH = pl.DeviceIdType.MESH

ADAM_LR = 0.001
ADAM_B1 = 0.9
ADAM_B2 = 0.999
ADAM_EPS = 1e-08
ADAM_WD = 0.01
ADAM_STEP = 10

_GELU_C = math.sqrt(2.0 / math.pi)
_GELU_A = 0.044715


def _cp(*sem):
    return pltpu.CompilerParams(dimension_semantics=sem, vmem_limit_bytes=VMEM_LIMIT)


def _call(body, **kw):
    return pl.pallas_call(body, **kw)


def _call_with_comm(core_body, comm, args, *, name, grid, out_shape, in_specs, out_specs, scratch_shapes, compiler_params,
                    core_aliases=None):
    core_aliases = dict(core_aliases or {})
    if comm is None:
        res = _call(core_body, name=name, grid=grid, out_shape=tuple(out_shape), in_specs=list(in_specs),
                    out_specs=tuple(out_specs), scratch_shapes=list(scratch_shapes), input_output_aliases=core_aliases,
                    compiler_params=compiler_params)(*args)
        return res, ()
    n_in, n_out, n_scr = len(in_specs), len(out_shape), len(scratch_shapes)
    ni, no = len(comm.ins), len(comm.outs)

    def body(*refs):
        core_in, cin = refs[:n_in], refs[n_in:n_in + ni]
        core_out = refs[n_in + ni:n_in + ni + n_out]
        cout = refs[n_in + ni + n_out:n_in + ni + n_out + no]
        scr = refs[n_in + ni + n_out + no:]
        core_scr, send, recv = scr[:n_scr], scr[n_scr], scr[n_scr + 1]
        ids = [pl.program_id(a) for a in range(len(grid))]
        first = functools.reduce(jnp.logical_and, [i == 0 for i in ids])
        last = functools.reduce(jnp.logical_and, [i == g - 1 for i, g in zip(ids, grid)])

        @pl.when(first)
        def _():
            comm.start(cin, cout, send, recv)

        core_body(*core_in, *core_out, *core_scr)

        @pl.when(last)
        def _():
            comm.finish(cin, cout, send, recv)

    hbm = pl.BlockSpec(memory_space=pl.ANY)
    res = _call(
        body, name=name, grid=grid, out_shape=tuple(out_shape) + tuple(comm.outs),
        in_specs=list(in_specs) + [hbm] * ni, out_specs=tuple(out_specs) + (hbm,) * no,
        scratch_shapes=list(scratch_shapes) + [pltpu.SemaphoreType.DMA((comm.n_sems,))] * 2,
        input_output_aliases={**core_aliases, **{n_in + i: n_out + o for i, o in comm.aliases.items()}},
        compiler_params=compiler_params,
    )(*args, *comm.ins)
    return res[:n_out], res[n_out:]


def _dot(a, b):
    return jnp.dot(a, b, preferred_element_type=F32)


def _dot_nt(a, b):
    return lax.dot_general(a, b, (((1,), (1,)), ((), ())), preferred_element_type=F32)


def _dot_tn(a, b):
    return lax.dot_general(a, b, (((0,), (0,)), ((), ())), preferred_element_type=F32)


def _rstd(x):
    return lax.rsqrt(jnp.mean(x * x, axis=-1, keepdims=True) + EPS)


def _rms_bwd(x, g, dy):
    r = _rstd(x)
    xh = x * r
    gd = dy * g
    dx = r * (gd - xh * jnp.mean(gd * xh, axis=-1, keepdims=True))
    dg = jnp.sum(dy * xh, axis=0, keepdims=True)
    return dx, dg


def _gelu(x):
    return 0.5 * x * (1.0 + jnp.tanh(_GELU_C * (x + _GELU_A * (x * x * x))))


def _gelu_grad(x):
    t = jnp.tanh(_GELU_C * (x + _GELU_A * (x * x * x)))
    return 0.5 * (1.0 + t) + 0.5 * x * (1.0 - t * t) * (_GELU_C * (1.0 + 3.0 * _GELU_A * x * x))


def _dot2(x, ones_mat):
    hi = x.astype(BF)
    lo = (x - hi.astype(F32)).astype(BF)
    return _dot(hi, ones_mat) + _dot(lo, ones_mat)


GRAD_ROWS = 2048


def _row_tile(m, want=512):
    return min(want, m)


def _row_parts(tm, nparts=4):
    step = tm // nparts
    return [slice(p * step, (p + 1) * step) for p in range(nparts)]


def rms_fwd(x, g):
    m, d = x.shape
    tm = _row_tile(m)

    def body(x_ref, g_ref, o_ref):
        xv = x_ref[...]
        o_ref[...] = (xv * _rstd(xv) * g_ref[...]).astype(BF)

    return _call(
        body, name="rms_fwd", out_shape=jax.ShapeDtypeStruct((m, d), BF), grid=(m // tm,),
        in_specs=[pl.BlockSpec((tm, d), lambda i: (i, 0)), pl.BlockSpec((1, d), lambda i: (0, 0))],
        out_specs=pl.BlockSpec((tm, d), lambda i: (i, 0)), compiler_params=_cp("parallel"),
    )(x, g)


def ffn_up(a, ga, ig, iu, comm=None):
    m, d = a.shape
    tm = _row_tile(m, 1024)
    nb = ga.shape[-1]

    def body(a_ref, wg_ref, wu_ref, g_ref, u_ref, h_ref):
        av = a_ref[...]
        g = _dot(av, wg_ref[...])
        u = _dot(av, wu_ref[...])
        g_ref[...] = g.astype(BF)
        u_ref[...] = u.astype(BF)
        h_ref[...] = (g * jax.nn.sigmoid(g) * u).astype(BF)

    blk = jax.ShapeDtypeStruct((N_CHIPS, m, nb), BF)
    ospec = pl.BlockSpec((None, tm, nb), lambda k, i: (k, i, 0))
    return _call_with_comm(
        body, comm, (a, ga, ga), name="ffn_up", out_shape=(blk, blk, blk), grid=(N_CHIPS, m // tm),
        in_specs=[pl.BlockSpec((tm, d), lambda k, i: (i, 0)),
                  pl.BlockSpec((None, None, d, nb), lambda k, i: (k, ig, 0, 0)),
                  pl.BlockSpec((None, None, d, nb), lambda k, i: (k, iu, 0, 0))],
        out_specs=(ospec, ospec, ospec), scratch_shapes=[],
        compiler_params=_cp("arbitrary", "arbitrary") if comm is not None else _cp("parallel", "parallel"),
    )


def mm_res(lhs, w, which, h, gp, alpha, gn):
    blocked = lhs.ndim == 3
    m = lhs.shape[1] if blocked else lhs.shape[0]
    kb, n = w.shape[2], w.shape[3]
    tm = _row_tile(m)

    def body(l_ref, w_ref, h_ref, gp_ref, gn_ref, f_ref, hn_ref, an_ref):
        parts = _row_parts(tm)
        accs = []
        for rs in parts:
            acc = None
            for k in range(N_CHIPS):
                lk = l_ref[k, rs, :] if blocked else l_ref[rs, k * kb:(k + 1) * kb]
                t = _dot(lk, w_ref[k])
                acc = t if acc is None else acc + t
            accs.append(acc)
        for rs, acc in zip(parts, accs):
            f_ref[rs, :] = acc
            hn = h_ref[rs, :] + alpha * (acc * _rstd(acc) * gp_ref[...])
            hn_ref[rs, :] = hn
            an_ref[rs, :] = (hn * _rstd(hn) * gn_ref[...]).astype(BF)

    lspec = (pl.BlockSpec((N_CHIPS, tm, kb), lambda i: (0, i, 0)) if blocked
             else pl.BlockSpec((tm, N_CHIPS * kb), lambda i: (i, 0)))
    row = pl.BlockSpec((tm, n), lambda i: (i, 0))
    vec = pl.BlockSpec((1, n), lambda i: (0, 0))
    return _call(
        body, name="mm_res", grid=(m // tm,),
        out_shape=(jax.ShapeDtypeStruct((m, n), F32), jax.ShapeDtypeStruct((m, n), F32),
                   jax.ShapeDtypeStruct((m, n), BF)),
        in_specs=[lspec, pl.BlockSpec((N_CHIPS, None, kb, n), lambda i: (0, which, 0, 0)), row, vec, vec],
        out_specs=(row, row, row), compiler_params=_cp("parallel"),
    )(lhs, w, h, gp, gn)


def mm_cb(a, w, out_dtype=BF):
    m, kd = a.shape
    nb = w.shape[-1]
    tm = _row_tile(m, 1024)

    def body(a_ref, w_ref, o_ref):
        o_ref[...] = _dot(a_ref[...], w_ref[...]).astype(out_dtype)

    return _call(
        body, name="mm_cb", out_shape=jax.ShapeDtypeStruct((m, N_CHIPS * nb), out_dtype),
        grid=(N_CHIPS, m // tm),
        in_specs=[pl.BlockSpec((tm, kd), lambda k, i: (i, 0)), pl.BlockSpec((None, kd, nb), lambda k, i: (k, 0, 0))],
        out_specs=pl.BlockSpec((tm, nb), lambda k, i: (i, k)), compiler_params=_cp("parallel", "parallel"),
    )(a, w)


def _sgu_core(u_pre, vg_pre, wm_ref, ng_ref, nb_ref, bst_ref, g, c):
    rs = slice(c * CHUNK, (c + 1) * CHUNK)
    cs = slice(g * CHUNK, (g + 1) * CHUNK)
    ug = _gelu(u_pre[rs, cs].astype(F32))
    vgl = _gelu(vg_pre[rs, cs].astype(F32))
    mu = jnp.mean(vgl, axis=-1, keepdims=True)
    cen = vgl - mu
    rstd = lax.rsqrt(jnp.mean(cen * cen, axis=-1, keepdims=True) + EPS)
    xh = cen * rstd
    vn = xh * ng_ref[g:g + 1, :] + nb_ref[g:g + 1, :]
    mixed = _dot(wm_ref[g], vn.astype(BF)) + bst_ref[g]
    return ug, xh, rstd, vn, mixed


def sgu_fwd(proj, wm, ng, nb, bst):
    m = proj.shape[0]
    tm = _row_tile(m)
    wd = SGU_GROUPS * CHUNK

    def body(u_ref, v_ref, wm_ref, ng_ref, nb_ref, bst_ref, o_ref):
        for c in range(tm // CHUNK):
            for g in range(SGU_GROUPS):
                ug, _, _, _, mixed = _sgu_core(u_ref, v_ref, wm_ref, ng_ref, nb_ref, bst_ref, g, c)
                o_ref[c * CHUNK:(c + 1) * CHUNK, g * CHUNK:(g + 1) * CHUNK] = (ug * mixed).astype(BF)

    full = lambda shape: pl.BlockSpec(shape, lambda i: (0,) * len(shape))
    return _call(
        body, name="sgu_fwd", out_shape=jax.ShapeDtypeStruct((m, wd), BF), grid=(m // tm,),
        in_specs=[pl.BlockSpec((tm, wd), lambda i: (i, 0)), pl.BlockSpec((tm, wd), lambda i: (i, 1)),
                  full(wm.shape), full(ng.shape), full(nb.shape), full(bst.shape)],
        out_specs=pl.BlockSpec((tm, wd), lambda i: (i, 0)), compiler_params=_cp("parallel"),
    )(proj, proj, wm, ng, nb, bst)


def _sb_tiles(m, tq=512):
    return min(tq, m), min(256, m)


def _log_sigmoid_pair(z, mask):
    ls = jnp.minimum(z, 0.0) - jnp.log(1.0 + jnp.exp(-jnp.abs(z)))
    l1 = ls - z
    return ls, (l1 if mask is None else jnp.where(mask, l1, 0.0))


def _sb_diag_mask(r0, r1, d, tk):
    rows = r0 + lax.broadcasted_iota(jnp.int32, (r1 - r0, tk), 0)
    cols = d * tk + lax.broadcasted_iota(jnp.int32, (r1 - r0, tk), 1)
    return cols < rows


def _sb_chains(r0, r1, rows):
    span = r1 - r0
    part = next((p for p in range(min(rows, span), 0, -256) if span % p == 0), span)
    return [(hd, ra, ra + part) for ra in range(r0, r1, part) for hd in range(2)]


def _head_masks():
    lane = lax.broadcasted_iota(jnp.int32, (1, LANES), 1)
    return [lane < SB_HEAD_DIM, lane >= SB_HEAD_DIM]


def sb_fwd(proj, comm=None):
    m = proj.shape[0]
    tq, tk = _sb_tiles(m, 2048)
    ndiag = tq // tk

    def body(q_ref, k_ref, v_ref, o_ref, tot_ref, qs, acc, car):
        i = pl.program_id(1)
        nfull = i * ndiag
        heads = _head_masks()
        upper = (lax.broadcasted_iota(jnp.int32, (tk, tk), 0) > lax.broadcasted_iota(jnp.int32, (tk, tk), 1)).astype(BF)
        qv = q_ref[...]
        for hd in range(2):
            qs[hd] = jnp.where(heads[hd], qv, jnp.zeros_like(qv)) * SB_SCALE
        acc[...] = jnp.zeros_like(acc)
        car[...] = jnp.zeros_like(car)

        def block(ks, r0, diag):
            kb = k_ref[pl.ds(ks, tk), :]
            vb = v_ref[pl.ds(ks, tk), :]
            chains = _sb_chains(r0, tq, 512)
            masks = [None if diag is None else _sb_diag_mask(ra, rb, diag, tk) for _, ra, rb in chains]
            zs = [_dot_nt(qs[hd, ra:rb, :], kb) for hd, ra, rb in chains]
            mid = []
            for z, mask in zip(zs, masks):
                ls, l1 = _log_sigmoid_pair(z, mask)
                mid.append((ls, l1, _dot2(l1, upper)))
            pvs = []
            for (hd, ra, rb), (ls, l1, cum), mask in zip(chains, mid, masks):
                a = jnp.exp(ls + (cum + car[hd, ra:rb, :]))
                if mask is not None:
                    a = jnp.where(mask, a, 0.0)
                pvs.append(_dot(a.astype(BF), vb))
            for (hd, ra, rb), (ls, l1, cum), pv in zip(chains, mid, pvs):
                acc[hd, ra:rb, :] += pv
                car[hd, ra:rb, :] += jnp.sum(l1, axis=-1, keepdims=True)

        for d in reversed(range(ndiag)):
            block(pl.multiple_of((nfull + d) * tk, tk), d * tk, d)

        def step(jj, carry):
            block(pl.multiple_of((nfull - 1 - jj) * tk, tk), 0, None)
            return carry

        lax.fori_loop(0, nfull, step, 0)
        o_ref[...] = jnp.where(heads[0], acc[0], acc[1]).astype(BF)
        tot_ref[...] = jnp.where(heads[0], car[0], car[1])

    qb = 2 * 512 // LANES
    return _call_with_comm(
        body, comm, (proj, proj, proj), name="sb_fwd", grid=(4, m // tq),
        out_shape=(jax.ShapeDtypeStruct((m, 512), BF), jax.ShapeDtypeStruct((m, 512), F32)),
        in_specs=[pl.BlockSpec((tq, LANES), lambda p, i: (i, qb + p)),
                  pl.BlockSpec((m, LANES), lambda p, i: (0, qb + 4 + p)),
                  pl.BlockSpec((m, LANES), lambda p, i: (0, qb + 8 + p))],
        out_specs=(pl.BlockSpec((tq, LANES), lambda p, i: (i, p)), pl.BlockSpec((tq, LANES), lambda p, i: (i, p))),
        scratch_shapes=[pltpu.VMEM((2, tq, LANES), BF), pltpu.VMEM((2, tq, LANES), F32), pltpu.VMEM((2, tq, 1), F32)],
        compiler_params=_cp("arbitrary", "arbitrary"),
    )


def merge_norm(oa, ob, ga, gb):
    m, w = oa.shape
    tm = _row_tile(m)

    def body(a_ref, b_ref, ga_ref, gb_ref, o_ref):
        av = a_ref[...].astype(F32)
        bv = b_ref[...].astype(F32)
        o_ref[:, :w] = (av * _rstd(av) * ga_ref[...]).astype(BF)
        o_ref[:, w:] = (bv * _rstd(bv) * gb_ref[...]).astype(BF)

    row = pl.BlockSpec((tm, w), lambda i: (i, 0))
    vec = pl.BlockSpec((1, w), lambda i: (0, 0))
    return _call(
        body, name="merge_norm", out_shape=jax.ShapeDtypeStruct((m, 2 * w), BF), grid=(m // tm,),
        in_specs=[row, row, vec, vec], out_specs=pl.BlockSpec((tm, 2 * w), lambda i: (i, 0)),
        compiler_params=_cp("parallel"),
    )(oa, ob, ga, gb)


def _xa_probs(qh, kh):
    logits = _dot_nt(qh, kh) * XA_SCALE
    e = jnp.exp(logits - jnp.max(logits, axis=-1, keepdims=True))
    return e / jnp.sum(e, axis=-1, keepdims=True)


def xa_fwd(xq, kv):
    m, d = xq.shape
    mm = kv.shape[0]
    tm = _row_tile(m)

    def body(q_ref, kv_ref, o_ref):
        for hd in range(XA_HEADS):
            cs = slice(hd * XA_HEAD_DIM, (hd + 1) * XA_HEAD_DIM)
            p = _xa_probs(q_ref[:, cs], kv_ref[:, cs])
            vh = kv_ref[:, d + hd * XA_HEAD_DIM:d + (hd + 1) * XA_HEAD_DIM]
            o_ref[:, cs] = _dot(p.astype(BF), vh).astype(BF)

    return _call(
        body, name="xa_fwd", out_shape=jax.ShapeDtypeStruct((m, d), BF), grid=(m // tm,),
        in_specs=[pl.BlockSpec((tm, d), lambda i: (i, 0)), pl.BlockSpec((mm, 2 * d), lambda i: (0, 0))],
        out_specs=pl.BlockSpec((tm, d), lambda i: (i, 0)), compiler_params=_cp("parallel"),
    )(xq, kv)


def norm_bwd(h, gn, d_a=None, d_res=None, target=None, f_prev=None, gp_prev=None, alpha_prev=1.0):
    m, d = h.shape
    tm = _row_tile(m)
    has_loss = target is not None
    has_res = d_res is not None
    has_prev = f_prev is not None

    def body(*refs):
        refs = list(refs)
        h_ref, gn_ref = refs[0], refs[1]
        pos = 2
        da_ref = dres_ref = t_ref = f_ref = gp_ref = None
        if has_loss:
            t_ref = refs[pos]; pos += 1
        else:
            da_ref = refs[pos]; pos += 1
        if has_res:
            dres_ref = refs[pos]; pos += 1
        if has_prev:
            f_ref, gp_ref = refs[pos], refs[pos + 1]; pos += 2
        dh_ref, dgn_ref = refs[pos], refs[pos + 1]; pos += 2
        df_ref = dgp_ref = loss_ref = None
        if has_prev:
            df_ref, dgp_ref = refs[pos], refs[pos + 1]; pos += 2
        if has_loss:
            loss_ref = refs[pos]

        first = pl.program_id(0) == 0
        hv = h_ref[...]
        gn = gn_ref[...]
        if has_loss:
            err = hv * _rstd(hv) * gn - t_ref[...]
            da = err * (1.0 / d)
            part = 0.5 * jnp.sum(jnp.sum(err * err, axis=-1, keepdims=True) * (1.0 / d))

            @pl.when(first)
            def _():
                loss_ref[...] = jnp.zeros_like(loss_ref)

            loss_ref[...] += part
        else:
            da = da_ref[...].astype(F32)
        dx, dgn = _rms_bwd(hv, gn, da)
        dh = dx + dres_ref[...] if has_res else dx
        dh_ref[...] = dh

        @pl.when(first)
        def _():
            dgn_ref[...] = jnp.zeros_like(dgn_ref)

        dgn_ref[...] += dgn
        if has_prev:
            dfv, dgp = _rms_bwd(f_ref[...], gp_ref[...], dh)
            df_ref[...] = (alpha_prev * dfv).astype(BF)

            @pl.when(first)
            def _():
                dgp_ref[...] = jnp.zeros_like(dgp_ref)

            dgp_ref[...] += alpha_prev * dgp

    row = pl.BlockSpec((tm, d), lambda i: (i, 0))
    vec = pl.BlockSpec((1, d), lambda i: (0, 0))
    ins, in_specs = [h, gn], [row, vec]
    ins.append(target if has_loss else d_a); in_specs.append(row)
    if has_res:
        ins.append(d_res); in_specs.append(row)
    if has_prev:
        ins += [f_prev, gp_prev]; in_specs += [row, vec]
    outs = [jax.ShapeDtypeStruct((m, d), F32), jax.ShapeDtypeStruct((1, d), F32)]
    out_specs = [row, vec]
    names = ["d_h", "d_gn"]
    if has_prev:
        outs += [jax.ShapeDtypeStruct((m, d), BF), jax.ShapeDtypeStruct((1, d), F32)]
        out_specs += [row, vec]
        names += ["d_f", "d_gp"]
    if has_loss:
        outs.append(jax.ShapeDtypeStruct((8, LANES), F32))
        out_specs.append(pl.BlockSpec((8, LANES), lambda i: (0, 0)))
        names.append("loss")
    res = _call(
        body, name="norm_bwd", out_shape=tuple(outs), grid=(m // tm,), in_specs=in_specs,
        out_specs=tuple(out_specs), compiler_params=_cp("arbitrary"),
    )(*ins)
    return dict(zip(names, res))


def ffn_bwd_act(df, gb, which, g, u):
    m, d = df.shape
    nb = g.shape[-1]
    tm = _row_tile(m, 1024)

    def body(df_ref, w_ref, g_ref, u_ref, dg_ref, du_ref):
        parts = _row_parts(tm, 2)
        dhs = [_dot_nt(df_ref[rs, :], w_ref[...]) for rs in parts]
        for rs, dh in zip(parts, dhs):
            gv = g_ref[rs, :].astype(F32)
            uv = u_ref[rs, :].astype(F32)
            s = jax.nn.sigmoid(gv)
            dg_ref[rs, :] = (dh * uv * (s * (1.0 + gv * (1.0 - s)))).astype(BF)
            du_ref[rs, :] = (dh * gv * s).astype(BF)

    blk = jax.ShapeDtypeStruct((N_CHIPS, m, nb), BF)
    aspec = pl.BlockSpec((None, tm, nb), lambda k, i: (k, i, 0))
    return _call(
        body, name="ffn_bwd_act", out_shape=(blk, blk), grid=(N_CHIPS, m // tm),
        in_specs=[pl.BlockSpec((tm, d), lambda k, i: (i, 0)),
                  pl.BlockSpec((None, None, nb, d), lambda k, i: (k, which, 0, 0)), aspec, aspec],
        out_specs=(aspec, aspec), compiler_params=_cp("parallel", "parallel"),
    )(df, gb, g, u)


def mm_tn(a, b, dest, a_spec, b_spec, o_spec, acc_shape, msteps, comm=None):
    def body(a_ref, b_ref, dest_ref, o_ref, acc):
        del dest_ref
        ms = pl.program_id(1)

        @pl.when(ms == 0)
        def _():
            acc[...] = jnp.zeros_like(acc)

        acc[...] += _dot_tn(a_ref[...], b_ref[...])

        @pl.when(ms == msteps - 1)
        def _():
            o_ref[...] = acc[...].astype(o_ref.dtype)

    (out,), extra = _call_with_comm(
        body, comm, (a, b, dest), name="mm_tn", out_shape=(jax.ShapeDtypeStruct(dest.shape, dest.dtype),),
        grid=(N_CHIPS, msteps), in_specs=[a_spec, b_spec, pl.BlockSpec(memory_space=pl.ANY)], out_specs=(o_spec,),
        scratch_shapes=[pltpu.VMEM(acc_shape, F32)], core_aliases={2: 0},
        compiler_params=_cp("arbitrary", "arbitrary") if comm is not None else _cp("parallel", "arbitrary"),
    )
    return out if comm is None else (out, extra)


def _act_spec(arr, tm, nb):
    if arr.ndim == 3:
        return pl.BlockSpec((None, tm, nb), lambda k, ms: (k, ms, 0))
    return pl.BlockSpec((tm, nb), lambda k, ms: (ms, k))


def grad_cb(a, dout, dest, which=None):
    m, kd = a.shape
    nb = dest.shape[-1]
    tm = _row_tile(m, GRAD_ROWS)
    if which is None:
        o_spec = pl.BlockSpec((None, kd, nb), lambda k, ms: (k, 0, 0))
    else:
        o_spec = pl.BlockSpec((None, None, kd, nb), lambda k, ms: (k, which, 0, 0))
    return mm_tn(a, dout, dest, pl.BlockSpec((tm, kd), lambda k, ms: (ms, 0)), _act_spec(dout, tm, nb), o_spec,
                 (kd, nb), m // tm)


def grad_rb(a, dout, dest, which, comm=None):
    m, n = dout.shape
    kb = dest.shape[-2]
    tm = _row_tile(m, GRAD_ROWS)
    o_spec = pl.BlockSpec((None, None, kb, n), lambda k, ms: (k, which, 0, 0))
    return mm_tn(a, dout, dest, _act_spec(a, tm, kb), pl.BlockSpec((tm, n), lambda k, ms: (ms, 0)), o_spec,
                 (kb, n), m // tm, comm=comm)


def _norm_n_in(norm):
    return 3 + (2 if norm.get("f_prev") is not None else 0)


def _norm_join_operands(norm, tm, row_map, vec_map):
    h = norm["h"]
    m, d = h.shape
    row, vec = pl.BlockSpec((tm, d), row_map), pl.BlockSpec((1, d), vec_map)
    ins, in_specs = [h, norm["gn"], norm["d_res"]], [row, vec, row]
    outs = [jax.ShapeDtypeStruct((m, d), F32), jax.ShapeDtypeStruct((1, d), F32)]
    out_specs, names = [row, vec], ["d_h", "d_gn"]
    if norm.get("f_prev") is not None:
        ins += [norm["f_prev"], norm["gp_prev"]]
        in_specs += [row, vec]
        outs += [jax.ShapeDtypeStruct((m, d), BF), jax.ShapeDtypeStruct((1, d), F32)]
        out_specs += [row, vec]
        names += ["d_f", "d_gp"]
    return ins, in_specs, outs, out_specs, names


def _norm_join(da, in_refs, out_refs, norm, first):
    h_ref, gn_ref, dres_ref = in_refs[:3]
    dh_ref, dgn_ref = out_refs[:2]
    has_prev = norm.get("f_prev") is not None
    alpha = norm.get("alpha_prev", 1.0)

    @pl.when(first)
    def _():
        dgn_ref[...] = jnp.zeros_like(dgn_ref)
        if has_prev:
            out_refs[3][...] = jnp.zeros_like(out_refs[3])

    dx, dgn = _rms_bwd(h_ref[...], gn_ref[...], da)
    dh = dx + dres_ref[...]
    dh_ref[...] = dh
    dgn_ref[...] += dgn
    if has_prev:
        dfv, dgp = _rms_bwd(in_refs[3][...], in_refs[4][...], dh)
        out_refs[2][...] = (alpha * dfv).astype(BF)
        out_refs[3][...] += alpha * dgp


def mm_nt_cb(pairs, n, out_dtype, comm=None, norm=None):
    d0 = pairs[0][0]
    m = d0.shape[1] if d0.ndim == 3 else d0.shape[0]
    tm = _row_tile(m, 1024 if norm is None else 512)
    npair = len(pairs)
    n_norm_in = 0 if norm is None else _norm_n_in(norm)

    def body(*refs):
        outs = refs[2 * npair + n_norm_in:-1]
        acc = refs[-1]
        i, k = pl.program_id(0), pl.program_id(1)

        @pl.when(k == 0)
        def _():
            acc[...] = jnp.zeros_like(acc)

        for p in range(npair):
            acc[...] += _dot_nt(refs[2 * p][...], refs[2 * p + 1][...])

        @pl.when(k == N_CHIPS - 1)
        def _():
            if norm is None:
                outs[0][...] = acc[...].astype(out_dtype)
            else:
                _norm_join(acc[...], refs[2 * npair:2 * npair + n_norm_in], outs, norm, i == 0)

    ins, in_specs = [], []
    for dout, w, which in pairs:
        nb = w.shape[-1]
        if dout.ndim == 3:
            in_specs.append(pl.BlockSpec((None, tm, nb), lambda i, k: (k, i, 0)))
        else:
            in_specs.append(pl.BlockSpec((tm, nb), lambda i, k: (i, k)))
        if w.ndim == 4:
            in_specs.append(pl.BlockSpec((None, None, n, nb), lambda i, k, which=which: (k, which, 0, 0)))
        else:
            in_specs.append(pl.BlockSpec((None, n, nb), lambda i, k: (k, 0, 0)))
        ins += [dout, w]
    if norm is None:
        out_shape = (jax.ShapeDtypeStruct((m, n), out_dtype),)
        out_specs = (pl.BlockSpec((tm, n), lambda i, k: (i, 0)),)
        names = None
    else:
        n_ins, n_specs, out_shape, out_specs, names = _norm_join_operands(norm, tm, lambda i, k: (i, 0), lambda i, k: (0, 0))
        ins += n_ins
        in_specs += n_specs
    sequential = comm is not None or norm is not None
    res, extra = _call_with_comm(
        body, comm, tuple(ins), name="mm_nt_cb", out_shape=tuple(out_shape), grid=(m // tm, N_CHIPS),
        in_specs=in_specs, out_specs=tuple(out_specs), scratch_shapes=[pltpu.VMEM((tm, n), F32)],
        compiler_params=_cp("arbitrary", "arbitrary") if sequential else _cp("parallel", "arbitrary"),
    )
    out = res[0] if norm is None else dict(zip(names, res))
    return out if comm is None else (out, extra)


def mm_nt_rb(dout, w, which, out_dtype, norm=None):
    m, n = dout.shape
    kb = w.shape[2]
    tm = _row_tile(m)

    def body(d_ref, w_ref, *rest):
        da = _dot_nt(d_ref[...], w_ref[...].reshape(N_CHIPS * kb, n))
        if norm is None:
            rest[0][...] = da.astype(out_dtype)
        else:
            _norm_join(da, rest[:_norm_n_in(norm)], rest[_norm_n_in(norm):], norm, pl.program_id(0) == 0)

    ins = [dout, w]
    in_specs = [pl.BlockSpec((tm, n), lambda i: (i, 0)), pl.BlockSpec((N_CHIPS, None, kb, n), lambda i: (0, which, 0, 0))]
    if norm is None:
        return _call(
            body, name="mm_nt_rb", out_shape=jax.ShapeDtypeStruct((m, N_CHIPS * kb), out_dtype), grid=(m // tm,),
            in_specs=in_specs, out_specs=pl.BlockSpec((tm, N_CHIPS * kb), lambda i: (i, 0)),
            compiler_params=_cp("parallel"),
        )(*ins)
    n_ins, n_specs, out_shape, out_specs, names = _norm_join_operands(norm, tm, lambda i: (i, 0), lambda i: (0, 0))
    res = _call(
        body, name="mm_nt_rb", out_shape=tuple(out_shape), grid=(m // tm,), in_specs=in_specs + n_specs,
        out_specs=tuple(out_specs), compiler_params=_cp("arbitrary"),
    )(*ins, *n_ins)
    return dict(zip(names, res))


def merge_norm_bwd(oa, ob, dmerged, ga, gb):
    m, w = oa.shape
    tm = _row_tile(m)

    def body(a_ref, b_ref, dm_ref, ga_ref, gb_ref, da_ref, db_ref, dga_ref, dgb_ref):
        @pl.when(pl.program_id(0) == 0)
        def _():
            dga_ref[...] = jnp.zeros_like(dga_ref)
            dgb_ref[...] = jnp.zeros_like(dgb_ref)

        da, dga = _rms_bwd(a_ref[...].astype(F32), ga_ref[...], dm_ref[:, :w].astype(F32))
        db, dgb = _rms_bwd(b_ref[...].astype(F32), gb_ref[...], dm_ref[:, w:].astype(F32))
        da_ref[...] = da
        db_ref[...] = db
        dga_ref[...] += dga
        dgb_ref[...] += dgb

    row = pl.BlockSpec((tm, w), lambda i: (i, 0))
    vec = pl.BlockSpec((1, w), lambda i: (0, 0))
    return _call(
        body, name="merge_norm_bwd", grid=(m // tm,),
        out_shape=(jax.ShapeDtypeStruct((m, w), F32), jax.ShapeDtypeStruct((m, w), F32),
                   jax.ShapeDtypeStruct((1, w), F32), jax.ShapeDtypeStruct((1, w), F32)),
        in_specs=[row, row, pl.BlockSpec((tm, 2 * w), lambda i: (i, 0)), vec, vec],
        out_specs=(row, row, vec, vec), compiler_params=_cp("arbitrary"),
    )(oa, ob, dmerged, ga, gb)


def sgu_bwd(proj, d_oa, wm, wmt, ng, nb, bst):
    m = proj.shape[0]
    tm = _row_tile(m)
    wd = SGU_GROUPS * CHUNK

    def body(u_ref, v_ref, do_ref, wm_ref, wmt_ref, ng_ref, nb_ref, bst_ref,
             dp_ref, dw_ref, dbt_ref, dng_ref, dnb_ref):
        @pl.when(pl.program_id(0) == 0)
        def _():
            dw_ref[...] = jnp.zeros_like(dw_ref)
            dbt_ref[...] = jnp.zeros_like(dbt_ref)
            dng_ref[...] = jnp.zeros_like(dng_ref)
            dnb_ref[...] = jnp.zeros_like(dnb_ref)

        causal = lax.broadcasted_iota(jnp.int32, (CHUNK, CHUNK), 0) >= lax.broadcasted_iota(jnp.int32, (CHUNK, CHUNK), 1)
        for c in range(tm // CHUNK):
            rs = slice(c * CHUNK, (c + 1) * CHUNK)
            for g in range(SGU_GROUPS):
                cs = slice(g * CHUNK, (g + 1) * CHUNK)
                ug, xh, rstd, vn, mixed = _sgu_core(u_ref, v_ref, wm_ref, ng_ref, nb_ref, bst_ref, g, c)
                do = do_ref[rs, cs]
                dug = do * mixed
                dmix = do * ug
                dmb = dmix.astype(BF)
                dbt_ref[g] += jnp.sum(dmix, axis=-1, keepdims=True)
                dw_ref[g] += jnp.where(causal, _dot_nt(dmb, vn.astype(BF)), 0.0)
                dvn = _dot(wmt_ref[g], dmb)
                dng_ref[g:g + 1, :] += jnp.sum(dvn * xh, axis=0, keepdims=True)
                dnb_ref[g:g + 1, :] += jnp.sum(dvn, axis=0, keepdims=True)
                dxh = dvn * ng_ref[g:g + 1, :]
                dvg = rstd * (dxh - jnp.mean(dxh, axis=-1, keepdims=True)
                              - xh * jnp.mean(dxh * xh, axis=-1, keepdims=True))
                dp_ref[rs, cs] = (dug * _gelu_grad(u_ref[rs, cs].astype(F32))).astype(BF)
                dp_ref[rs, wd + g * CHUNK:wd + (g + 1) * CHUNK] = (dvg * _gelu_grad(v_ref[rs, cs].astype(F32))).astype(BF)

    full = lambda shape: pl.BlockSpec(shape, lambda i: (0,) * len(shape))
    return _call(
        body, name="sgu_bwd", grid=(m // tm,),
        out_shape=(jax.ShapeDtypeStruct((m, 2 * wd), BF), jax.ShapeDtypeStruct(wm.shape, F32),
                   jax.ShapeDtypeStruct(bst.shape, F32), jax.ShapeDtypeStruct(ng.shape, F32),
                   jax.ShapeDtypeStruct(nb.shape, F32)),
        in_specs=[pl.BlockSpec((tm, wd), lambda i: (i, 0)), pl.BlockSpec((tm, wd), lambda i: (i, 1)),
                  pl.BlockSpec((tm, wd), lambda i: (i, 0)),
                  full(wm.shape), full(wmt.shape), full(ng.shape), full(nb.shape), full(bst.shape)],
        out_specs=(pl.BlockSpec((tm, 2 * wd), lambda i: (i, 0)), full(wm.shape), full(bst.shape), full(ng.shape),
                   full(nb.shape)),
        compiler_params=_cp("arbitrary"),
    )(proj, proj, d_oa, wm, wmt, ng, nb, bst)


def sb_bwd(proj, tot, d_ob, comm=None):
    m = proj.shape[0]
    tq, tk = _sb_tiles(m)
    ndiag = tq // tk

    def body(q_ref, k_ref, v_ref, tot_ref, do_ref, dq_ref, dk_ref, dv_ref, qs, dos, tots, dqa, cl1, cg):
        i = pl.program_id(1)

        @pl.when(i == 0)
        def _():
            dk_ref[...] = jnp.zeros_like(dk_ref)
            dv_ref[...] = jnp.zeros_like(dv_ref)

        nfull = i * ndiag
        heads = _head_masks()
        r_io = lax.broadcasted_iota(jnp.int32, (tk, tk), 0)
        c_io = lax.broadcasted_iota(jnp.int32, (tk, tk), 1)
        incl = (r_io <= c_io).astype(BF)
        excl = (r_io < c_io).astype(BF)
        qv = q_ref[...]
        dov = do_ref[...].astype(BF)
        totv = tot_ref[...]
        for hd in range(2):
            qs[hd] = jnp.where(heads[hd], qv, jnp.zeros_like(qv)) * SB_SCALE
            dos[hd] = jnp.where(heads[hd], dov, jnp.zeros_like(dov))
            tots[hd] = jnp.max(jnp.where(heads[hd], totv, -jnp.inf), axis=-1, keepdims=True)
        dqa[...] = jnp.zeros_like(dqa)
        cl1[...] = jnp.zeros_like(cl1)
        cg[...] = jnp.zeros_like(cg)

        def block(ks, r0, diag):
            kb = k_ref[pl.ds(ks, tk), :]
            vb = v_ref[pl.ds(ks, tk), :]
            chains = _sb_chains(r0, tq, 256)
            masks = [None if diag is None else _sb_diag_mask(ra, rb, diag, tk) for _, ra, rb in chains]
            qc = [qs[hd, ra:rb, :] for hd, ra, rb in chains]
            doc = [dos[hd, ra:rb, :] for hd, ra, rb in chains]
            zs = [_dot_nt(q, kb) for q in qc]
            das = [_dot_nt(do, vb) for do in doc]
            s1 = []
            for z, mask in zip(zs, masks):
                ls, l1 = _log_sigmoid_pair(z, mask)
                s1.append((ls, l1, _dot2(l1, incl)))
            s2 = []
            for (hd, ra, rb), (ls, l1, pre), da, mask in zip(chains, s1, das, masks):
                a = jnp.exp(ls + (tots[hd, ra:rb, :] - (pre + cl1[hd, ra:rb, :])))
                if mask is not None:
                    a = jnp.where(mask, a, 0.0)
                gmat = a * da
                s2.append((a, gmat, _dot2(gmat, excl)))
            dk_sum = dv_sum = None
            for n, (hd, ra, rb) in enumerate(chains):
                a, gmat, pref = s2[n]
                sg = jnp.exp(s1[n][0])
                dz = gmat * (1.0 - sg) - (pref + cg[hd, ra:rb, :]) * sg
                if masks[n] is not None:
                    dz = jnp.where(masks[n], dz, 0.0)
                dz = dz.astype(BF)
                dqa[hd, ra:rb, :] += _dot(dz, kb)
                dk_t = _dot_tn(dz, qc[n])
                dv_t = _dot_tn(a.astype(BF), doc[n])
                dk_sum = dk_t if dk_sum is None else dk_sum + dk_t
                dv_sum = dv_t if dv_sum is None else dv_sum + dv_t
            for n, (hd, ra, rb) in enumerate(chains):
                cl1[hd, ra:rb, :] += jnp.sum(s1[n][1], axis=-1, keepdims=True)
                cg[hd, ra:rb, :] += jnp.sum(s2[n][1], axis=-1, keepdims=True)
            dk_ref[pl.ds(ks, tk), :] += dk_sum
            dv_ref[pl.ds(ks, tk), :] += dv_sum

        def step(j, carry):
            block(pl.multiple_of(j * tk, tk), 0, None)
            return carry

        lax.fori_loop(0, nfull, step, 0)
        for d in range(ndiag):
            block(pl.multiple_of((nfull + d) * tk, tk), d * tk, d)
        dq_ref[...] = jnp.where(heads[0], dqa[0], dqa[1]) * SB_SCALE

    qb = 2 * 512 // LANES
    tile = pl.BlockSpec((tq, LANES), lambda p, i: (i, p))
    seq = pl.BlockSpec((m, LANES), lambda p, i: (0, p))
    out = jax.ShapeDtypeStruct((m, 512), F32)
    return _call_with_comm(
        body, comm, (proj, proj, proj, tot, d_ob), name="sb_bwd", grid=(4, m // tq), out_shape=(out, out, out),
        in_specs=[pl.BlockSpec((tq, LANES), lambda p, i: (i, qb + p)),
                  pl.BlockSpec((m, LANES), lambda p, i: (0, qb + 4 + p)),
                  pl.BlockSpec((m, LANES), lambda p, i: (0, qb + 8 + p)), tile, tile],
        out_specs=(tile, seq, seq),
        scratch_shapes=[pltpu.VMEM((2, tq, LANES), BF), pltpu.VMEM((2, tq, LANES), BF), pltpu.VMEM((2, tq, 1), F32),
                        pltpu.VMEM((2, tq, LANES), F32), pltpu.VMEM((2, tq, 1), F32), pltpu.VMEM((2, tq, 1), F32)],
        compiler_params=_cp("arbitrary", "arbitrary"),
    )


def xa_bwd(xq, kv, d_o):
    m, d = xq.shape
    mm = kv.shape[0]
    tm = _row_tile(m)

    def body(q_ref, kv_ref, do_ref, dq_ref, dkv_ref):
        @pl.when(pl.program_id(0) == 0)
        def _():
            dkv_ref[...] = jnp.zeros_like(dkv_ref)

        for hd in range(XA_HEADS):
            cs = slice(hd * XA_HEAD_DIM, (hd + 1) * XA_HEAD_DIM)
            vs = slice(d + hd * XA_HEAD_DIM, d + (hd + 1) * XA_HEAD_DIM)
            qh = q_ref[:, cs]
            kh = kv_ref[:, cs]
            doh = do_ref[:, cs]
            p = _xa_probs(qh, kh)
            dp = _dot_nt(doh, kv_ref[:, vs])
            ds = (p * (dp - jnp.sum(p * dp, axis=-1, keepdims=True))).astype(BF)
            dq_ref[:, cs] = (_dot(ds, kh) * XA_SCALE).astype(BF)
            dkv_ref[:, cs] += _dot_tn(ds, qh) * XA_SCALE
            dkv_ref[:, vs] += _dot_tn(p.astype(BF), doh)

    row = pl.BlockSpec((tm, d), lambda i: (i, 0))
    whole = pl.BlockSpec((mm, 2 * d), lambda i: (0, 0))
    return _call(
        body, name="xa_bwd", grid=(m // tm,),
        out_shape=(jax.ShapeDtypeStruct((m, d), BF), jax.ShapeDtypeStruct((mm, 2 * d), F32)),
        in_specs=[row, whole, row], out_specs=(row, whole), compiler_params=_cp("arbitrary"),
    )(xq, kv, d_o)


def adamw(w, g, mom, vel, g_index=None):
    r, c = w.shape
    tr = r
    for cand in range(512, 7, -8):
        if r % cand == 0:
            tr = cand
            break

    def body(w_ref, g_ref, m_ref, v_ref, go_ref, d_ref, nm_ref, nv_ref):
        gv = g_ref[...]
        mn = ADAM_B1 * m_ref[...] + (1.0 - ADAM_B1) * gv
        vn = ADAM_B2 * v_ref[...] + (1.0 - ADAM_B2) * (gv * gv)
        m_hat = mn / (1.0 - ADAM_B1 ** ADAM_STEP)
        v_hat = vn / (1.0 - ADAM_B2 ** ADAM_STEP)
        go_ref[...] = gv
        d_ref[...] = -ADAM_LR * (m_hat / (jnp.sqrt(v_hat) + ADAM_EPS) + ADAM_WD * w_ref[...])
        nm_ref[...] = mn
        nv_ref[...] = vn

    spec = pl.BlockSpec((tr, c), lambda i: (i, 0))
    gspec = spec if g_index is None else pl.BlockSpec((None, tr, c), lambda i: (g_index, i, 0))
    out = jax.ShapeDtypeStruct((r, c), F32)
    return _call(
        body, name="adamw", out_shape=(out, out, out, out), grid=(r // tr,), in_specs=[spec, gspec, spec, spec],
        out_specs=(spec, spec, spec, spec), compiler_params=_cp("parallel"),
    )(w, g, mom, vel)


def _place():
    x, y, c = lax.axis_index("x"), lax.axis_index("y"), lax.axis_index("c")
    others = [(1 - x, y), (x, 1 - y), (1 - x, 1 - y)]
    return x, y, c, others


_HALF = {"A1": (1, 512), "A2": (1, 512), "B1": (0, 352), "B2": (0, 352), "C": (1, 128), "D": (0, 512), "E": (0, 512),
         "A1T": (1, 352), "A2T": (1, 352)}
SET_EARLY = ("A1", "B1", "D")
SET_LATE = ("A2", "B2", "C", "E")
GRAD_EARLY = ("A1T", "B1", "D")
GRAD_LATE = ("A2T", "B2", "C", "E")
_HBM = pl.BlockSpec(memory_space=pl.ANY)


def _half_of(ref, name, hc, lead=0):
    axis, size = _HALF[name]
    idx = [slice(None)] * (lead + axis) + [pl.ds(hc * size, size)]
    return ref.at[tuple(idx)]


def _gather_slots(loc, names):
    me = 2 * lax.axis_index("x") + lax.axis_index("y")
    init = []
    for nm in names:
        full = lax.empty((N_CHIPS,) + loc[nm].shape, loc[nm].dtype)
        init.append(lax.dynamic_update_slice(full, loc[nm][None], (me,) + (0,) * loc[nm].ndim))
    return init


def _gather_ici(names, src, out, send, recv, sends=True, arrivals=True):
    x, y, c, others = _place()
    me = 2 * x + y
    out_sends, out_arrivals = [], []
    for a, nm in enumerate(names):
        for j, (ox, oy) in enumerate(others):
            sems = dict(send_sem=send.at[3 * a + j], recv_sem=recv.at[3 * a + j], device_id_type=MESH)
            if sends:
                out_sends.append(pltpu.make_async_remote_copy(
                    src_ref=_half_of(src[a], nm, c), dst_ref=_half_of(out[a].at[me], nm, c), device_id=(ox, oy, c),
                    **sems))
            if arrivals:
                landed = _half_of(out[a].at[2 * ox + oy], nm, c)
                out_arrivals.append(pltpu.make_async_remote_copy(src_ref=landed, dst_ref=landed, device_id=(x, y, c),
                                                                 **sems))
    return out_sends, out_arrivals


def _gather_d2d(names, given, out, send, recv):
    x, y, c, others = _place()
    sends, arrivals = [], []
    for a, nm in enumerate(names):
        for j, (ox, oy) in enumerate(others):
            sems = dict(send_sem=send.at[3 * a + j], recv_sem=recv.at[3 * a + j], device_id_type=MESH)
            sends.append(pltpu.make_async_remote_copy(
                src_ref=_half_of(given[a].at[2 * ox + oy], nm, c), dst_ref=_half_of(out[a].at[2 * ox + oy], nm, c),
                device_id=(x, y, 1 - c), **sems))
            landed = _half_of(out[a].at[2 * ox + oy], nm, 1 - c)
            arrivals.append(pltpu.make_async_remote_copy(src_ref=landed, dst_ref=landed, device_id=(x, y, c), **sems))
    return sends, arrivals


def gather_weights(loc, names):
    n = len(names)

    def body(*refs):
        src, given, out = refs[:n], refs[n:2 * n], refs[2 * n:3 * n]
        send1, recv1, send2, recv2 = refs[3 * n:3 * n + 4]
        first, landed = _gather_ici(names, src, out, send1, recv1)
        del given
        passed, arrivals = _gather_d2d(names, out, out, send2, recv2)
        for f in first:
            f.start()
        for l, p in zip(landed, passed):
            l.wait_recv()
            p.start()
        for a in arrivals:
            a.wait_recv()
        for f in first + passed:
            f.wait_send()

    init = _gather_slots(loc, names)
    res = _call(
        body, name="gather_weights", out_shape=tuple(jax.ShapeDtypeStruct(t.shape, t.dtype) for t in init),
        in_specs=[_HBM] * (2 * n), out_specs=(_HBM,) * n, input_output_aliases={n + a: a for a in range(n)},
        scratch_shapes=[pltpu.SemaphoreType.DMA((3 * n,))] * 4,
    )(*[loc[nm] for nm in names], *init)
    return dict(zip(names, res))


def gather_forward(bufs, names):
    n = len(names)

    def body(*refs):
        given, out = refs[:n], refs[n:2 * n]
        passed, arrivals = _gather_d2d(names, given, out, refs[2 * n], refs[2 * n + 1])
        for p in passed:
            p.start()
        for a in arrivals:
            a.wait_recv()
        for p in passed:
            p.wait_send()

    res = _call(
        body, name="gather_forward", out_shape=tuple(jax.ShapeDtypeStruct(t.shape, t.dtype) for t in bufs),
        in_specs=[_HBM] * n, out_specs=(_HBM,) * n, input_output_aliases={a: a for a in range(n)},
        scratch_shapes=[pltpu.SemaphoreType.DMA((3 * n,))] * 2,
    )(*bufs)
    return dict(zip(names, res))


class FusedComm:
    def __init__(self, ins, outs, aliases, n_sems, start, finish):
        self.ins, self.outs, self.aliases, self.n_sems, self.start, self.finish = ins, outs, aliases, n_sems, start, finish


def gather_comm(loc, names):
    n = len(names)

    def start(ins, outs, send, recv):
        for f in _gather_ici(names, ins[:n], outs, send, recv, arrivals=False)[0]:
            f.start()

    def finish(ins, outs, send, recv):
        first, landed = _gather_ici(names, ins[:n], outs, send, recv)
        for l in landed:
            l.wait_recv()
        for f in first:
            f.wait_send()

    init = _gather_slots(loc, names)
    return FusedComm([loc[nm] for nm in names] + init, [jax.ShapeDtypeStruct(t.shape, t.dtype) for t in init],
                     {n + a: a for a in range(n)}, 3 * n, start, finish)


def _half_shape(name, shape):
    axis, size = _HALF[name]
    s = list(shape)
    s[axis] = size
    return tuple(s)


def rs_to_sibling(grads, names):
    n = len(names)

    def body(*refs):
        src = dict(zip(names, refs[:n]))
        out = dict(zip(names, refs[n:2 * n]))
        send, recv = refs[2 * n], refs[2 * n + 1]
        x, y, c, _ = _place()
        copies = []
        for a, nm in enumerate(names):
            copies.append(pltpu.make_async_remote_copy(
                src_ref=_half_of(src[nm], nm, 1 - c, lead=1), dst_ref=out[nm], send_sem=send.at[a], recv_sem=recv.at[a],
                device_id=(x, y, 1 - c), device_id_type=MESH))
        for cpy in copies:
            cpy.start()
        for cpy in copies:
            cpy.wait()

    hbm = pl.BlockSpec(memory_space=pl.ANY)
    outs = tuple(jax.ShapeDtypeStruct((N_CHIPS,) + _half_shape(nm, grads[nm].shape[1:]), grads[nm].dtype)
                 for nm in names)
    res = _call(
        body, name="rs_to_sibling", out_shape=outs, in_specs=[hbm] * n, out_specs=(hbm,) * n,
        scratch_shapes=[pltpu.SemaphoreType.DMA((n,)), pltpu.SemaphoreType.DMA((n,))],
    )(*[grads[nm] for nm in names])
    return dict(zip(names, res))


def _tile2(shape):
    lead = shape[:-2]
    return lead, shape[-2:]


def add_halves(name, mine, got, c_idx):
    axis, size = _HALF[name]
    hshape = got.shape
    lead, last2 = hshape[:-2], hshape[-2:]
    nlead = len(lead)
    haxis = 1 + axis

    def body(c_ref, m_ref, g_ref, o_ref):
        del c_ref
        o_ref[...] = (m_ref[...].astype(F32) + g_ref[...].astype(F32)).astype(o_ref.dtype)

    blk = (None,) * nlead + last2

    def got_map(*idx):
        return tuple(idx[:nlead]) + (0, 0)

    def mine_map(*idx):
        lead_idx = list(idx[:nlead])
        c = idx[nlead][0]
        if haxis < nlead:
            lead_idx[haxis] = lead_idx[haxis] + c * size
            return tuple(lead_idx) + (0, 0)
        return tuple(lead_idx) + (c, 0)

    grid_spec = pltpu.PrefetchScalarGridSpec(
        num_scalar_prefetch=1, grid=lead,
        in_specs=[pl.BlockSpec(blk, mine_map), pl.BlockSpec(blk, got_map)],
        out_specs=pl.BlockSpec(blk, got_map))
    return _call(
        body, name="add_halves", out_shape=jax.ShapeDtypeStruct(hshape, got.dtype), grid_spec=grid_spec,
        compiler_params=_cp(*(("parallel",) * nlead)),
    )(c_idx, mine, got)


def _rs_ici(n, src, out, send, recv):
    x, y, c, others = _place()
    copies = []
    for a in range(n):
        for j, (ox, oy) in enumerate(others):
            copies.append(pltpu.make_async_remote_copy(
                src_ref=src[a].at[2 * ox + oy], dst_ref=out[a].at[j], send_sem=send.at[3 * a + j],
                recv_sem=recv.at[3 * a + j], device_id=(ox, oy, c), device_id_type=MESH))
    return copies


def _rs_out_shapes(summed, names):
    return [jax.ShapeDtypeStruct((3,) + summed[nm].shape[1:], summed[nm].dtype) for nm in names]


def rs_comm(summed, names):
    n = len(names)

    def start(ins, outs, send, recv):
        for cpy in _rs_ici(n, ins, outs, send, recv):
            cpy.start()

    def finish(ins, outs, send, recv):
        for cpy in _rs_ici(n, ins, outs, send, recv):
            cpy.wait()

    return FusedComm([summed[nm] for nm in names], _rs_out_shapes(summed, names), {}, 3 * n, start, finish)


def add_chips(name, summed, got, kc_idx, full_shape):
    axis, size = _HALF[name]
    hshape = summed.shape[1:]
    lead, last2 = hshape[:-2], hshape[-2:]
    nlead = len(lead)

    def body(kc_ref, s_ref, g0_ref, g1_ref, g2_ref, o_ref):
        del kc_ref
        o_ref[...] = ((s_ref[...].astype(F32) + g0_ref[...].astype(F32)) + g1_ref[...].astype(F32)) + g2_ref[...].astype(F32)

    blk = (None,) * (nlead + 1) + last2
    oblk = (None,) * nlead + last2

    def got_map(slot):
        return lambda *idx: (slot,) + tuple(idx[:nlead]) + (0, 0)

    def out_map(*idx):
        lead_idx = list(idx[:nlead])
        c = idx[-1][1]
        if axis < nlead:
            lead_idx[axis] = lead_idx[axis] + c * size
            return tuple(lead_idx) + (0, 0)
        return tuple(lead_idx) + (c, 0)

    grid_spec = pltpu.PrefetchScalarGridSpec(
        num_scalar_prefetch=1, grid=lead if nlead else (1,),
        in_specs=[pl.BlockSpec(blk, lambda *idx: (idx[-1][0],) + tuple(idx[:nlead]) + (0, 0)),
                  pl.BlockSpec(blk, got_map(0)), pl.BlockSpec(blk, got_map(1)), pl.BlockSpec(blk, got_map(2))],
        out_specs=pl.BlockSpec(oblk, out_map))
    return _call(
        body, name="add_chips", out_shape=jax.ShapeDtypeStruct(full_shape, F32), grid_spec=grid_spec,
        compiler_params=_cp(*(("parallel",) * max(nlead, 1))),
    )(kc_idx, summed, got, got, got)


def rs_replicate(shards, names):
    n = len(names)

    def body(*refs):
        given = dict(zip(names, refs[:n]))
        buf = dict(zip(names, refs[n:2 * n]))
        send, recv = refs[2 * n], refs[2 * n + 1]
        x, y, c, _ = _place()
        copies = []
        for a, nm in enumerate(names):
            copies.append(pltpu.make_async_remote_copy(
                src_ref=_half_of(given[nm], nm, c), dst_ref=_half_of(buf[nm], nm, c), send_sem=send.at[a],
                recv_sem=recv.at[a], device_id=(x, y, 1 - c), device_id_type=MESH))
        for cpy in copies:
            cpy.start()
        for a, nm in enumerate(names):
            other = _half_of(buf[nm], nm, 1 - c)
            pltpu.make_async_remote_copy(src_ref=other, dst_ref=other, send_sem=send.at[a], recv_sem=recv.at[a],
                                         device_id=(x, y, 1 - c), device_id_type=MESH).wait_recv()
        for cpy in copies:
            cpy.wait_send()

    hbm = pl.BlockSpec(memory_space=pl.ANY)
    outs = tuple(jax.ShapeDtypeStruct(shards[nm].shape, F32) for nm in names)
    res = _call(
        body, name="rs_replicate", out_shape=outs, in_specs=[hbm] * n, out_specs=(hbm,) * n,
        input_output_aliases={a: a for a in range(n)},
        scratch_shapes=[pltpu.SemaphoreType.DMA((n,)), pltpu.SemaphoreType.DMA((n,))],
    )(*[shards[nm] for nm in names])
    return dict(zip(names, res))


def allreduce_small(v):
    r = v.shape[0]
    h = r // 2

    def body(v_ref, o_ref, sib, slots, send, recv):
        x, y, c, others = _place()
        me = 2 * x + y
        sibling = (x, y, 1 - c)
        mine = pl.ds(pl.multiple_of(c * h, 8), h)
        theirs = pl.ds(pl.multiple_of((1 - c) * h, 8), h)
        swap = pltpu.make_async_remote_copy(src_ref=v_ref.at[theirs], dst_ref=sib, send_sem=send.at[0],
                                            recv_sem=recv.at[0], device_id=sibling, device_id_type=MESH)
        swap.start()
        swap.wait()
        slots[me] = v_ref[mine] + sib[...]
        sends = [pltpu.make_async_remote_copy(src_ref=slots.at[me], dst_ref=slots.at[me], send_sem=send.at[1 + j],
                                              recv_sem=recv.at[1 + j], device_id=(ox, oy, c), device_id_type=MESH)
                 for j, (ox, oy) in enumerate(others)]
        for s in sends:
            s.start()
        for j, (ox, oy) in enumerate(others):
            got = slots.at[2 * ox + oy]
            pltpu.make_async_remote_copy(src_ref=got, dst_ref=got, send_sem=send.at[1 + j], recv_sem=recv.at[1 + j],
                                         device_id=(x, y, c), device_id_type=MESH).wait_recv()
        for s in sends:
            s.wait_send()
        o_ref[mine] = (slots[0] + slots[1]) + (slots[2] + slots[3])
        back = pltpu.make_async_remote_copy(src_ref=o_ref.at[mine], dst_ref=o_ref.at[mine], send_sem=send.at[4],
                                            recv_sem=recv.at[4], device_id=sibling, device_id_type=MESH)
        back.start()
        pltpu.make_async_remote_copy(src_ref=o_ref.at[theirs], dst_ref=o_ref.at[theirs], send_sem=send.at[4],
                                     recv_sem=recv.at[4], device_id=sibling, device_id_type=MESH).wait_recv()
        back.wait_send()

    vm = pl.BlockSpec(memory_space=pltpu.VMEM)
    return _call(
        body, name="allreduce_small", out_shape=jax.ShapeDtypeStruct(v.shape, F32), in_specs=[vm], out_specs=vm,
        scratch_shapes=[pltpu.VMEM((h, LANES), F32), pltpu.VMEM((N_CHIPS, h, LANES), F32),
                        pltpu.SemaphoreType.DMA((5,)), pltpu.SemaphoreType.DMA((5,))],
    )(v)


def local_step(x, mem, target, ga1, small, ffn1_up, fwd_sb, bwd_sb, ffn1_mid, ffn1_da):
    causal = jnp.tril(jnp.ones((CHUNK, CHUNK), dtype=bool))
    w_s = jnp.where(causal[None], small["sgu_w_s"], 0.0)
    wm = w_s.astype(BF)
    wmt = jnp.swapaxes(w_s, 1, 2).astype(BF)
    bst = small["sgu_b_s"].reshape(SGU_GROUPS, CHUNK, 1)
    ng, nbias = small["sgu_norm_g"], small["sgu_norm_b"]

    a1 = rms_fwd(x, small["ffn1_pre_g"])
    g1, u1, hid1, rest = ffn1_up(a1)
    gb1, gd = rest["B1"][:, None], rest["D"]
    f1, h1, n1 = mm_res(hid1, gb1, 0, x, small["ffn1_post_g"], 0.5, small["mix_pre_g"])
    proj = mm_cb(n1, gd)
    oa = sgu_fwd(proj, wm, ng, nbias, bst)
    ob, tot, late = fwd_sb(proj)
    ga2, gb2, gc, ge = late["A2"], late["B2"][:, None], late["C"], late["E"]
    merged = merge_norm(oa, ob, small["sgu_out_g"], small["sb_out_g"])
    mo, h2, xn = mm_res(merged, gc, 0, h1, small["mix_post_g"], 1.0, small["xa_pre_g"])
    memn = rms_fwd(mem, small["mem_norm_g"])
    kv = mm_cb(memn, ge)
    xq = mm_res_plain(xn, gc, 1)
    o = xa_fwd(xq, kv)
    cc, h3, a2 = mm_res(o, gc, 2, h2, small["xa_post_g"], 1.0, small["ffn2_pre_g"])
    (g2, u2, hid2), _ = ffn_up(a2, ga2, 0, 1)
    f2, h4, _ = mm_res(hid2, gb2, 0, h3, small["ffn2_post_g"], 0.5, small["final_norm_g"])

    gate_up_t = (N_CHIPS, 2, ga1.shape[3], ga1.shape[2])
    dga1, dga2 = lax.empty(gate_up_t, BF), lax.empty(gate_up_t, BF)
    dgb1, dgb2 = lax.empty(gb1.shape, BF), lax.empty(gb2.shape, BF)
    dgc = lax.empty(gc.shape, BF)
    dgd = lax.empty(gd.shape, BF)
    dge = lax.empty(ge.shape, BF)
    sg = {}

    r = norm_bwd(h4, small["final_norm_g"], target=target, f_prev=f2, gp_prev=small["ffn2_post_g"], alpha_prev=0.5)
    loss_tile, dh4, df2 = r["loss"], r["d_h"], r["d_f"]
    sg["final_norm_g"], sg["ffn2_post_g"] = r["d_gn"], r["d_gp"]

    dg2, du2 = ffn_bwd_act(df2, gb2, 0, g2, u2)
    dgb2 = grad_rb(hid2, df2, dgb2, 0)
    dga2 = grad_rb(dg2, a2, dga2, 0)
    dga2 = grad_rb(du2, a2, dga2, 1)
    r = mm_nt_cb([(dg2, ga2, 0), (du2, ga2, 1)], D_MODEL, F32,
                 norm=dict(h=h3, gn=small["ffn2_pre_g"], d_res=dh4, f_prev=cc, gp_prev=small["xa_post_g"], alpha_prev=1.0))
    dh3, dc = r["d_h"], r["d_f"]
    sg["ffn2_pre_g"], sg["xa_post_g"] = r["d_gn"], r["d_gp"]

    d_o = mm_nt_rb(dc, gc, 2, BF)
    dgc = grad_rb(o, dc, dgc, 2)
    dxq, dkv = xa_bwd(xq, kv, d_o)
    dkvb = dkv.astype(BF)
    dge = grad_cb(memn, dkvb, dge)
    dmemn = mm_nt_cb([(dkvb, ge, None)], D_MODEL, F32)
    sg["mem_norm_g"] = norm_bwd(mem, small["mem_norm_g"], d_a=dmemn)["d_gn"]
    dgc = grad_rb(xn, dxq, dgc, 1)
    r = mm_nt_rb(dxq, gc, 1, F32,
                 norm=dict(h=h2, gn=small["xa_pre_g"], d_res=dh3, f_prev=mo, gp_prev=small["mix_post_g"], alpha_prev=1.0))
    dh2, dmo = r["d_h"], r["d_f"]
    sg["xa_pre_g"], sg["mix_post_g"] = r["d_gn"], r["d_gp"]

    dmerged = mm_nt_rb(dmo, gc, 0, BF)
    dgc = grad_rb(merged, dmo, dgc, 0)
    d_oa, d_ob, sg["sgu_out_g"], sg["sb_out_g"] = merge_norm_bwd(oa, ob, dmerged, small["sgu_out_g"], small["sb_out_g"])
    dp_uv, dws, dbt, sg["sgu_norm_g"], sg["sgu_norm_b"] = sgu_bwd(proj, d_oa, wm, wmt, ng, nbias, bst)
    sg["sgu_w_s"] = dws
    sg["sgu_b_s"] = dbt.reshape(SGU_GROUPS, CHUNK)
    late_grads = {"A2T": dga2, "B2": dgb2.reshape(late["B2"].shape), "C": dgc, "E": dge}
    dq, dk, dv, state = bwd_sb(proj, tot, d_ob, late_grads)
    dproj = jnp.concatenate([dp_uv, dq.astype(BF), dk.astype(BF), dv.astype(BF)], axis=1)
    dgd = grad_cb(n1, dproj, dgd)
    r = mm_nt_cb([(dproj, gd, None)], D_MODEL, F32,
                 norm=dict(h=h1, gn=small["mix_pre_g"], d_res=dh2, f_prev=f1, gp_prev=small["ffn1_post_g"], alpha_prev=0.5))
    dh1, df1 = r["d_h"], r["d_f"]
    sg["mix_pre_g"], sg["ffn1_post_g"] = r["d_gn"], r["d_gp"]

    dg1, du1 = ffn_bwd_act(df1, gb1, 0, g1, u1)
    dgb1 = grad_rb(hid1, df1, dgb1, 0)
    mid_state, comm_b1, comm_d = ffn1_mid({"B1": dgb1.reshape(rest["B1"].shape), "D": dgd})
    res = grad_rb(dg1, a1, dga1, 0, comm=comm_b1)
    dga1, got_b1 = res if comm_b1 is not None else (res, ())
    res = grad_rb(du1, a1, dga1, 1, comm=comm_d)
    dga1, got_d = res if comm_d is not None else (res, ())
    r, da_state = ffn1_da([(dg1, ga1, 0), (du1, ga1, 1)], {"A1T": dga1},
                          dict(h=x, gn=small["ffn1_pre_g"], d_res=dh1))
    state1 = (mid_state, got_b1, got_d, da_state)
    grad_x = r["d_h"]
    sg["ffn1_pre_g"] = r["d_gn"]
    return loss_tile, grad_x, sg, state, state1


def mm_res_plain(a, w, which):
    m = a.shape[0]
    kb, n = w.shape[2], w.shape[3]
    tm = _row_tile(m)

    def body(a_ref, w_ref, o_ref):
        acc = None
        for k in range(N_CHIPS):
            t = _dot(a_ref[:, k * kb:(k + 1) * kb], w_ref[k])
            acc = t if acc is None else acc + t
        o_ref[...] = acc.astype(BF)

    return _call(
        body, name="mm_rb", out_shape=jax.ShapeDtypeStruct((m, n), BF), grid=(m // tm,),
        in_specs=[pl.BlockSpec((tm, N_CHIPS * kb), lambda i: (i, 0)),
                  pl.BlockSpec((N_CHIPS, None, kb, n), lambda i: (0, which, 0, 0))],
        out_specs=pl.BlockSpec((tm, n), lambda i: (i, 0)), compiler_params=_cp("parallel"),
    )(a, w)


_BIG = ("ffn1_w_gate", "ffn1_w_up", "ffn1_w_down", "w_in", "w_out", "xa_w_q", "xa_w_kv", "xa_w_o",
        "ffn2_w_gate", "ffn2_w_up", "ffn2_w_down")
_SMALL = ("ffn1_pre_g", "ffn1_post_g", "mix_pre_g", "mix_post_g", "sgu_norm_g", "sgu_norm_b", "sgu_w_s", "sgu_b_s",
          "sgu_out_g", "sb_out_g", "xa_pre_g", "xa_post_g", "mem_norm_g", "ffn2_pre_g", "ffn2_post_g", "final_norm_g")
_WEIGHTS = ("ffn1_pre_g", "ffn1_post_g", "ffn1_w_gate", "ffn1_w_up", "ffn1_w_down", "mix_pre_g", "mix_post_g", "w_in",
            "sgu_norm_g", "sgu_norm_b", "sgu_w_s", "sgu_b_s", "sgu_out_g", "sb_out_g", "w_out", "xa_pre_g", "xa_post_g",
            "mem_norm_g", "xa_w_q", "xa_w_kv", "xa_w_o", "ffn2_pre_g", "ffn2_post_g", "ffn2_w_gate", "ffn2_w_up",
            "ffn2_w_down", "final_norm_g")
_SLOT = {"ffn1_w_gate": ("A1T", 0, True), "ffn1_w_up": ("A1T", 1, True), "ffn2_w_gate": ("A2T", 0, True),
         "ffn2_w_up": ("A2T", 1, True), "ffn1_w_down": ("B1", None, False), "ffn2_w_down": ("B2", None, False),
         "w_out": ("C", 0, False), "xa_w_q": ("C", 1, False), "xa_w_o": ("C", 2, False), "w_in": ("D", None, False),
         "xa_w_kv": ("E", None, False)}


def _pack_small(vals):
    return jnp.concatenate([vals[nm].reshape(-1, LANES) for nm in _SMALL], axis=0)


def _unpack_small(packed, shapes):
    out, pos = {}, 0
    for nm in _SMALL:
        rows = math.prod(shapes[nm]) // LANES
        out[nm] = packed[pos:pos + rows].reshape(shapes[nm])
        pos += rows
    return out


def kernel(x, mem, ffn1_pre_g, ffn1_post_g, ffn1_w_gate, ffn1_w_up, ffn1_w_down, mix_pre_g, mix_post_g, w_in, sgu_norm_g, sgu_norm_b, sgu_w_s, sgu_b_s, sgu_out_g, sb_out_g, w_out, xa_pre_g, xa_post_g, mem_norm_g, xa_w_q, xa_w_kv, xa_w_o, ffn2_pre_g, ffn2_post_g, ffn2_w_gate, ffn2_w_up, ffn2_w_down, final_norm_g, loss_target, m_ffn1_pre_g, m_ffn1_post_g, m_ffn1_w_gate, m_ffn1_w_up, m_ffn1_w_down, m_mix_pre_g, m_mix_post_g, m_w_in, m_sgu_norm_g, m_sgu_norm_b, m_sgu_w_s, m_sgu_b_s, m_sgu_out_g, m_sb_out_g, m_w_out, m_xa_pre_g, m_xa_post_g, m_mem_norm_g, m_xa_w_q, m_xa_w_kv, m_xa_w_o, m_ffn2_pre_g, m_ffn2_post_g, m_ffn2_w_gate, m_ffn2_w_up, m_ffn2_w_down, m_final_norm_g, v_ffn1_pre_g, v_ffn1_post_g, v_ffn1_w_gate, v_ffn1_w_up, v_ffn1_w_down, v_mix_pre_g, v_mix_post_g, v_w_in, v_sgu_norm_g, v_sgu_norm_b, v_sgu_w_s, v_sgu_b_s, v_sgu_out_g, v_sb_out_g, v_w_out, v_xa_pre_g, v_xa_post_g, v_mem_norm_g, v_xa_w_q, v_xa_w_kv, v_xa_w_o, v_ffn2_pre_g, v_ffn2_post_g, v_ffn2_w_gate, v_ffn2_w_up, v_ffn2_w_down, v_final_norm_g):
    env = dict(locals())
    w = {nm: env[nm] for nm in _WEIGHTS}
    mom = {nm: env["m_" + nm] for nm in _WEIGHTS}
    vel = {nm: env["v_" + nm] for nm in _WEIGHTS}

    loc = {
        "A1": jnp.stack([w["ffn1_w_gate"][0], w["ffn1_w_up"][0]]).astype(BF),
        "A2": jnp.stack([w["ffn2_w_gate"][0], w["ffn2_w_up"][0]]).astype(BF),
        "B1": w["ffn1_w_down"][0].astype(BF),
        "B2": w["ffn2_w_down"][0].astype(BF),
        "C": jnp.stack([w["w_out"][0], w["xa_w_q"][0], w["xa_w_o"][0]]).astype(BF),
        "D": w["w_in"][0].astype(BF),
        "E": w["xa_w_kv"][0].astype(BF),
    }
    ga1 = gather_weights(loc, ("A1",))["A1"]
    c_idx = lax.axis_index("c").astype(jnp.int32).reshape(1)
    kc_idx = jnp.stack([2 * lax.axis_index("x") + lax.axis_index("y"), lax.axis_index("c")]).astype(jnp.int32)
    early_rest = tuple(nm for nm in SET_EARLY if nm != "A1")

    def ffn1_up(a1):
        (g1, u1, hid1), bufs = ffn_up(a1, ga1, 0, 1, comm=gather_comm(loc, early_rest))
        return g1, u1, hid1, gather_forward(bufs, early_rest)

    def fwd_sb(proj):
        (ob, tot), bufs = sb_fwd(proj, comm=gather_comm(loc, SET_LATE))
        return ob, tot, gather_forward(bufs, SET_LATE)

    def reduce_to_pairs(grads, names):
        from_sib = rs_to_sibling(grads, names)
        return {nm: add_halves(nm, grads[nm], from_sib[nm], c_idx) for nm in names}

    def bwd_sb(proj, tot, d_ob, late_grads):
        pairs = reduce_to_pairs(late_grads, GRAD_LATE)
        (dq, dk, dv), got = sb_bwd(proj, tot, d_ob, comm=rs_comm(pairs, GRAD_LATE))
        return dq, dk, dv, (pairs, dict(zip(GRAD_LATE, got)))

    def ffn1_mid(bd_grads):
        pairs = reduce_to_pairs(bd_grads, ("B1", "D"))
        return pairs, rs_comm(pairs, ("B1",)), rs_comm(pairs, ("D",))

    def ffn1_da(mm_pairs, gate_up_grads, norm):
        pairs = reduce_to_pairs(gate_up_grads, ("A1T",))
        da1, got = mm_nt_cb(mm_pairs, D_MODEL, F32, comm=rs_comm(pairs, ("A1T",)))
        return norm_bwd(norm["h"], norm["gn"], d_a=da1, d_res=norm["d_res"]), (pairs, got)

    small = {nm: w[nm][0] for nm in _SMALL}
    for nm in ("ffn1_pre_g", "ffn1_post_g", "mix_pre_g", "mix_post_g", "sgu_out_g", "sb_out_g", "xa_pre_g", "xa_post_g",
               "mem_norm_g", "ffn2_pre_g", "ffn2_post_g", "final_norm_g"):
        small[nm] = w[nm]
    loss_tile, grad_x, small_g, (late_pairs, late_got), (bd_pairs, got_b1, got_d, (a1_pairs, got_a1)) = local_step(
        x[0], mem[0], loss_target[0], ga1, small, ffn1_up, fwd_sb, bwd_sb, ffn1_mid, ffn1_da)
    pair_sum = {**late_pairs, **bd_pairs, **a1_pairs}
    from_chips = {**late_got, "B1": got_b1[0], "D": got_d[0], "A1T": got_a1[0]}

    small_shapes = {nm: w[nm].shape for nm in _SMALL}
    flat = _pack_small(small_g)
    n_small = flat.shape[0]
    pad = jnp.zeros((-n_small % 8, LANES), F32)
    packed = allreduce_small(jnp.concatenate([flat, pad, loss_tile], axis=0))
    loss = packed[-8, 0]
    g_small = _unpack_small(packed[:n_small], small_shapes)

    grad_groups = GRAD_EARLY + GRAD_LATE
    shard_shape = {nm: pair_sum[nm].shape[1:] for nm in grad_groups}
    for nm in grad_groups:
        axis, size = _HALF[nm]
        shard_shape[nm] = shard_shape[nm][:axis] + (2 * size,) + shard_shape[nm][axis + 1:]
    shard = rs_replicate({nm: add_chips(nm, pair_sum[nm], from_chips[nm], kc_idx, shard_shape[nm])
                          for nm in grad_groups}, grad_groups)

    grads, delta, new_m, new_v = {}, {}, {}, {}
    for nm in _BIG:
        grp, idx, transposed = _SLOT[nm]
        shape = w[nm].shape
        if transposed:
            res = adamw(w[nm][0].T, shard[grp], mom[nm][0].T, vel[nm][0].T, g_index=idx)
            res = [t.T for t in res]
        else:
            res = adamw(w[nm][0], shard[grp], mom[nm][0], vel[nm][0], g_index=idx)
        grads[nm], delta[nm], new_m[nm], new_v[nm] = (t.reshape(shape) for t in res)
    _, d, nm_, nv_ = adamw(_pack_small(w), _pack_small(g_small), _pack_small(mom), _pack_small(vel))
    d, nm_, nv_ = (_unpack_small(t, small_shapes) for t in (d, nm_, nv_))
    for nm in _SMALL:
        grads[nm], delta[nm], new_m[nm], new_v[nm] = g_small[nm], d[nm], nm_[nm], nv_[nm]

    return (loss, grad_x[None], *[grads[nm] for nm in _WEIGHTS], *[delta[nm] for nm in _WEIGHTS],
            *[new_m[nm] for nm in _WEIGHTS], *[new_v[nm] for nm in _WEIGHTS])
```

```python
import functools
import math

import jax
import jax.numpy as jnp
from jax import lax
from jax.experimental import pallas as pl
from jax.experimental.pallas import tpu as pltpu

F32 = jnp.float32
BF = jnp.bfloat16
EPS = 1e-6
D_MODEL = 1024
N_CHIPS = 4
FF_BLOCK = 704
IN_BLOCK = 640
KV_BLOCK = 512
ROW_BLOCK = 256
SGU_GROUPS = 4
CHUNK = 128
SB_HEAD_DIM = 64
SB_SCALE = SB_HEAD_DIM ** -0.5
XA_HEADS = 4
XA_HEAD_DIM = 256
XA_SCALE = XA_HEAD_DIM ** -0.5
LANES = 128
VMEM_LIMIT = 56 * 1024 * 1024
MESH = pl.DeviceIdType.MESH

ADAM_LR = 0.001
ADAM_B1 = 0.9
ADAM_B2 = 0.999
ADAM_EPS = 1e-08
ADAM_WD = 0.01
ADAM_STEP = 10

_GELU_C = math.sqrt(2.0 / math.pi)
_GELU_A = 0.044715


def _cp(*sem):
    return pltpu.CompilerParams(dimension_semantics=sem, vmem_limit_bytes=VMEM_LIMIT)


def _call(body, **kw):
    return pl.pallas_call(body, **kw)


def _call_with_comm(core_body, comm, args, *, name, grid, out_shape, in_specs, out_specs, scratch_shapes, compiler_params,
                    core_aliases=None):
    core_aliases = dict(core_aliases or {})
    if comm is None:
        res = _call(core_body, name=name, grid=grid, out_shape=tuple(out_shape), in_specs=list(in_specs),
                    out_specs=tuple(out_specs), scratch_shapes=list(scratch_shapes), input_output_aliases=core_aliases,
                    compiler_params=compiler_params)(*args)
        return res, ()
    n_in, n_out, n_scr = len(in_specs), len(out_shape), len(scratch_shapes)
    ni, no = len(comm.ins), len(comm.outs)

    def body(*refs):
        core_in, cin = refs[:n_in], refs[n_in:n_in + ni]
        core_out = refs[n_in + ni:n_in + ni + n_out]
        cout = refs[n_in + ni + n_out:n_in + ni + n_out + no]
        scr = refs[n_in + ni + n_out + no:]
        core_scr, send, recv = scr[:n_scr], scr[n_scr], scr[n_scr + 1]
        ids = [pl.program_id(a) for a in range(len(grid))]
        first = functools.reduce(jnp.logical_and, [i == 0 for i in ids])
        last = functools.reduce(jnp.logical_and, [i == g - 1 for i, g in zip(ids, grid)])

        @pl.when(first)
        def _():
            comm.start(cin, cout, send, recv)

        core_body(*core_in, *core_out, *core_scr)

        @pl.when(last)
        def _():
            comm.finish(cin, cout, send, recv)

    hbm = pl.BlockSpec(memory_space=pl.ANY)
    res = _call(
        body, name=name, grid=grid, out_shape=tuple(out_shape) + tuple(comm.outs),
        in_specs=list(in_specs) + [hbm] * ni, out_specs=tuple(out_specs) + (hbm,) * no,
        scratch_shapes=list(scratch_shapes) + [pltpu.SemaphoreType.DMA((comm.n_sems,))] * 2,
        input_output_aliases={**core_aliases, **{n_in + i: n_out + o for i, o in comm.aliases.items()}},
        compiler_params=compiler_params,
    )(*args, *comm.ins)
    return res[:n_out], res[n_out:]


def _dot(a, b):
    return jnp.dot(a, b, preferred_element_type=F32)


def _dot_nt(a, b):
    return lax.dot_general(a, b, (((1,), (1,)), ((), ())), preferred_element_type=F32)


def _dot_tn(a, b):
    return lax.dot_general(a, b, (((0,), (0,)), ((), ())), preferred_element_type=F32)


def _rstd(x):
    return lax.rsqrt(jnp.mean(x * x, axis=-1, keepdims=True) + EPS)


def _rms_bwd(x, g, dy):
    r = _rstd(x)
    xh = x * r
    gd = dy * g
    dx = r * (gd - xh * jnp.mean(gd * xh, axis=-1, keepdims=True))
    dg = jnp.sum(dy * xh, axis=0, keepdims=True)
    return dx, dg


def _gelu(x):
    return 0.5 * x * (1.0 + jnp.tanh(_GELU_C * (x + _GELU_A * (x * x * x))))


def _gelu_grad(x):
    t = jnp.tanh(_GELU_C * (x + _GELU_A * (x * x * x)))
    return 0.5 * (1.0 + t) + 0.5 * x * (1.0 - t * t) * (_GELU_C * (1.0 + 3.0 * _GELU_A * x * x))


def _dot2(x, ones_mat):
    hi = x.astype(BF)
    lo = (x - hi.astype(F32)).astype(BF)
    return _dot(hi, ones_mat) + _dot(lo, ones_mat)


GRAD_ROWS = 2048


def _row_tile(m, want=512):
    return min(want, m)


def _row_parts(tm, nparts=4):
    step = tm // nparts
    return [slice(p * step, (p + 1) * step) for p in range(nparts)]


def rms_fwd(x, g):
    m, d = x.shape
    tm = _row_tile(m, 1024)

    def body(x_ref, g_ref, o_ref):
        xv = x_ref[...]
        o_ref[...] = (xv * _rstd(xv) * g_ref[...]).astype(BF)

    return _call(
        body, name="rms_fwd", out_shape=jax.ShapeDtypeStruct((m, d), BF), grid=(m // tm,),
        in_specs=[pl.BlockSpec((tm, d), lambda i: (i, 0)), pl.BlockSpec((1, d), lambda i: (0, 0))],
        out_specs=pl.BlockSpec((tm, d), lambda i: (i, 0)), compiler_params=_cp("parallel"),
    )(x, g)


def ffn_up(a, ga, ig, iu, comm=None):
    m, d = a.shape
    tm = _row_tile(m, 1024)
    nb = ga.shape[-1]

    def body(a_ref, wg_ref, wu_ref, g_ref, u_ref, h_ref):
        av = a_ref[...]
        g = _dot(av, wg_ref[...])
        u = _dot(av, wu_ref[...])
        g_ref[...] = g.astype(BF)
        u_ref[...] = u.astype(BF)
        h_ref[...] = (g * jax.nn.sigmoid(g) * u).astype(BF)

    blk = jax.ShapeDtypeStruct((N_CHIPS, m, nb), BF)
    ospec = pl.BlockSpec((None, tm, nb), lambda k, i: (k, i, 0))
    return _call_with_comm(
        body, comm, (a, ga, ga), name="ffn_up", out_shape=(blk, blk, blk), grid=(N_CHIPS, m // tm),
        in_specs=[pl.BlockSpec((tm, d), lambda k, i: (i, 0)),
                  pl.BlockSpec((None, None, d, nb), lambda k, i: (k, ig, 0, 0)),
                  pl.BlockSpec((None, None, d, nb), lambda k, i: (k, iu, 0, 0))],
        out_specs=(ospec, ospec, ospec), scratch_shapes=[],
        compiler_params=_cp("arbitrary", "arbitrary") if comm is not None else _cp("parallel", "parallel"),
    )


def mm_res(lhs, w, which, h, gp, alpha, gn):
    blocked = lhs.ndim == 3
    m = lhs.shape[1] if blocked else lhs.shape[0]
    kb, n = w.shape[2], w.shape[3]
    tm = _row_tile(m)

    def body(l_ref, w_ref, h_ref, gp_ref, gn_ref, f_ref, hn_ref, an_ref):
        parts = _row_parts(tm)
        accs = []
        for rs in parts:
            acc = None
            for k in range(N_CHIPS):
                lk = l_ref[k, rs, :] if blocked else l_ref[rs, k * kb:(k + 1) * kb]
                t = _dot(lk, w_ref[k])
                acc = t if acc is None else acc + t
            accs.append(acc)
        for rs, acc in zip(parts, accs):
            f_ref[rs, :] = acc
            hn = h_ref[rs, :] + alpha * (acc * _rstd(acc) * gp_ref[...])
            hn_ref[rs, :] = hn
            an_ref[rs, :] = (hn * _rstd(hn) * gn_ref[...]).astype(BF)

    lspec = (pl.BlockSpec((N_CHIPS, tm, kb), lambda i: (0, i, 0)) if blocked
             else pl.BlockSpec((tm, N_CHIPS * kb), lambda i: (i, 0)))
    row = pl.BlockSpec((tm, n), lambda i: (i, 0))
    vec = pl.BlockSpec((1, n), lambda i: (0, 0))
    return _call(
        body, name="mm_res", grid=(m // tm,),
        out_shape=(jax.ShapeDtypeStruct((m, n), F32), jax.ShapeDtypeStruct((m, n), F32),
                   jax.ShapeDtypeStruct((m, n), BF)),
        in_specs=[lspec, pl.BlockSpec((N_CHIPS, None, kb, n), lambda i: (0, which, 0, 0)), row, vec, vec],
        out_specs=(row, row, row), compiler_params=_cp("parallel"),
    )(lhs, w, h, gp, gn)


def mm_cb(a, w, out_dtype=BF):
    m, kd = a.shape
    nb = w.shape[-1]
    tm = _row_tile(m, 1024)

    def body(a_ref, w_ref, o_ref):
        o_ref[...] = _dot(a_ref[...], w_ref[...]).astype(out_dtype)

    return _call(
        body, name="mm_cb", out_shape=jax.ShapeDtypeStruct((m, N_CHIPS * nb), out_dtype),
        grid=(N_CHIPS, m // tm),
        in_specs=[pl.BlockSpec((tm, kd), lambda k, i: (i, 0)), pl.BlockSpec((None, kd, nb), lambda k, i: (k, 0, 0))],
        out_specs=pl.BlockSpec((tm, nb), lambda k, i: (i, k)), compiler_params=_cp("parallel", "parallel"),
    )(a, w)


def _sgu_core(u_pre, vg_pre, wm_ref, ng_ref, nb_ref, bst_ref, g, c):
    rs = slice(c * CHUNK, (c + 1) * CHUNK)
    cs = slice(g * CHUNK, (g + 1) * CHUNK)
    ug = _gelu(u_pre[rs, cs].astype(F32))
    vgl = _gelu(vg_pre[rs, cs].astype(F32))
    mu = jnp.mean(vgl, axis=-1, keepdims=True)
    cen = vgl - mu
    rstd = lax.rsqrt(jnp.mean(cen * cen, axis=-1, keepdims=True) + EPS)
    xh = cen * rstd
    vn = xh * ng_ref[g:g + 1, :] + nb_ref[g:g + 1, :]
    mixed = _dot(wm_ref[g], vn.astype(BF)) + bst_ref[g]
    return ug, xh, rstd, vn, mixed


def sgu_fwd(proj, wm, ng, nb, bst):
    m = proj.shape[0]
    tm = _row_tile(m)
    wd = SGU_GROUPS * CHUNK

    def body(u_ref, v_ref, wm_ref, ng_ref, nb_ref, bst_ref, o_ref):
        for c in range(tm // CHUNK):
            for g in range(SGU_GROUPS):
                ug, _, _, _, mixed = _sgu_core(u_ref, v_ref, wm_ref, ng_ref, nb_ref, bst_ref, g, c)
                o_ref[c * CHUNK:(c + 1) * CHUNK, g * CHUNK:(g + 1) * CHUNK] = (ug * mixed).astype(BF)

    full = lambda shape: pl.BlockSpec(shape, lambda i: (0,) * len(shape))
    return _call(
        body, name="sgu_fwd", out_shape=jax.ShapeDtypeStruct((m, wd), BF), grid=(m // tm,),
        in_specs=[pl.BlockSpec((tm, wd), lambda i: (i, 0)), pl.BlockSpec((tm, wd), lambda i: (i, 1)),
                  full(wm.shape), full(ng.shape), full(nb.shape), full(bst.shape)],
        out_specs=pl.BlockSpec((tm, wd), lambda i: (i, 0)), compiler_params=_cp("parallel"),
    )(proj, proj, wm, ng, nb, bst)


def _sb_tiles(m, tq=512):
    return min(tq, m), min(256, m)


def _log_sigmoid_pair(z, mask):
    ls = jnp.minimum(z, 0.0) - jnp.log(1.0 + jnp.exp(-jnp.abs(z)))
    l1 = ls - z
    return ls, (l1 if mask is None else jnp.where(mask, l1, 0.0))


def _sb_diag_mask(r0, r1, d, tk):
    rows = r0 + lax.broadcasted_iota(jnp.int32, (r1 - r0, tk), 0)
    cols = d * tk + lax.broadcasted_iota(jnp.int32, (r1 - r0, tk), 1)
    return cols < rows


def _sb_chains(r0, r1, rows):
    span = r1 - r0
    part = next((p for p in range(min(rows, span), 0, -256) if span % p == 0), span)
    return [(hd, ra, ra + part) for ra in range(r0, r1, part) for hd in range(2)]


def _head_masks():
    lane = lax.broadcasted_iota(jnp.int32, (1, LANES), 1)
    return [lane < SB_HEAD_DIM, lane >= SB_HEAD_DIM]


def sb_fwd(proj, comm=None):
    m = proj.shape[0]
    tq, tk = _sb_tiles(m, 2048)
    ndiag = tq // tk

    def body(q_ref, k_ref, v_ref, o_ref, tot_ref, qs, acc, car):
        i = pl.program_id(1)
        nfull = i * ndiag
        heads = _head_masks()
        upper = (lax.broadcasted_iota(jnp.int32, (tk, tk), 0) > lax.broadcasted_iota(jnp.int32, (tk, tk), 1)).astype(BF)
        qv = q_ref[...]
        for hd in range(2):
            qs[hd] = jnp.where(heads[hd], qv, jnp.zeros_like(qv)) * SB_SCALE
        acc[...] = jnp.zeros_like(acc)
        car[...] = jnp.zeros_like(car)

        def block(ks, r0, diag):
            kb = k_ref[pl.ds(ks, tk), :]
            vb = v_ref[pl.ds(ks, tk), :]
            chains = _sb_chains(r0, tq, 512)
            masks = [None if diag is None else _sb_diag_mask(ra, rb, diag, tk) for _, ra, rb in chains]
            zs = [_dot_nt(qs[hd, ra:rb, :], kb) for hd, ra, rb in chains]
            mid = []
            for z, mask in zip(zs, masks):
                ls, l1 = _log_sigmoid_pair(z, mask)
                mid.append((ls, l1, _dot2(l1, upper)))
            pvs = []
            for (hd, ra, rb), (ls, l1, cum), mask in zip(chains, mid, masks):
                a = jnp.exp(ls + (cum + car[hd, ra:rb, :]))
                if mask is not None:
                    a = jnp.where(mask, a, 0.0)
                pvs.append(_dot(a.astype(BF), vb))
            for (hd, ra, rb), (ls, l1, cum), pv in zip(chains, mid, pvs):
                acc[hd, ra:rb, :] += pv
                car[hd, ra:rb, :] += jnp.sum(l1, axis=-1, keepdims=True)

        for d in reversed(range(ndiag)):
            block(pl.multiple_of((nfull + d) * tk, tk), d * tk, d)

        def step(jj, carry):
            block(pl.multiple_of((nfull - 1 - jj) * tk, tk), 0, None)
            return carry

        lax.fori_loop(0, nfull, step, 0)
        o_ref[...] = jnp.where(heads[0], acc[0], acc[1]).astype(BF)
        tot_ref[...] = jnp.where(heads[0], car[0], car[1])

    qb = 2 * 512 // LANES
    return _call_with_comm(
        body, comm, (proj, proj, proj), name="sb_fwd", grid=(4, m // tq),
        out_shape=(jax.ShapeDtypeStruct((m, 512), BF), jax.ShapeDtypeStruct((m, 512), F32)),
        in_specs=[pl.BlockSpec((tq, LANES), lambda p, i: (i, qb + p)),
                  pl.BlockSpec((m, LANES), lambda p, i: (0, qb + 4 + p)),
                  pl.BlockSpec((m, LANES), lambda p, i: (0, qb + 8 + p))],
        out_specs=(pl.BlockSpec((tq, LANES), lambda p, i: (i, p)), pl.BlockSpec((tq, LANES), lambda p, i: (i, p))),
        scratch_shapes=[pltpu.VMEM((2, tq, LANES), BF), pltpu.VMEM((2, tq, LANES), F32), pltpu.VMEM((2, tq, 1), F32)],
        compiler_params=_cp("arbitrary", "arbitrary"),
    )


def merge_norm(oa, ob, ga, gb):
    m, w = oa.shape
    tm = _row_tile(m, 1024)

    def body(a_ref, b_ref, ga_ref, gb_ref, o_ref):
        av = a_ref[...].astype(F32)
        bv = b_ref[...].astype(F32)
        o_ref[:, :w] = (av * _rstd(av) * ga_ref[...]).astype(BF)
        o_ref[:, w:] = (bv * _rstd(bv) * gb_ref[...]).astype(BF)

    row = pl.BlockSpec((tm, w), lambda i: (i, 0))
    vec = pl.BlockSpec((1, w), lambda i: (0, 0))
    return _call(
        body, name="merge_norm", out_shape=jax.ShapeDtypeStruct((m, 2 * w), BF), grid=(m // tm,),
        in_specs=[row, row, vec, vec], out_specs=pl.BlockSpec((tm, 2 * w), lambda i: (i, 0)),
        compiler_params=_cp("parallel"),
    )(oa, ob, ga, gb)


def _xa_probs(qh, kh):
    logits = _dot_nt(qh, kh) * XA_SCALE
    e = jnp.exp(logits - jnp.max(logits, axis=-1, keepdims=True))
    return e / jnp.sum(e, axis=-1, keepdims=True)


def xa_fwd(xq, kv):
    m, d = xq.shape
    mm = kv.shape[0]
    tm = _row_tile(m)

    def body(q_ref, kv_ref, o_ref):
        for hd in range(XA_HEADS):
            cs = slice(hd * XA_HEAD_DIM, (hd + 1) * XA_HEAD_DIM)
            p = _xa_probs(q_ref[:, cs], kv_ref[:, cs])
            vh = kv_ref[:, d + hd * XA_HEAD_DIM:d + (hd + 1) * XA_HEAD_DIM]
            o_ref[:, cs] = _dot(p.astype(BF), vh).astype(BF)

    return _call(
        body, name="xa_fwd", out_shape=jax.ShapeDtypeStruct((m, d), BF), grid=(m // tm,),
        in_specs=[pl.BlockSpec((tm, d), lambda i: (i, 0)), pl.BlockSpec((mm, 2 * d), lambda i: (0, 0))],
        out_specs=pl.BlockSpec((tm, d), lambda i: (i, 0)), compiler_params=_cp("parallel"),
    )(xq, kv)


def norm_bwd(h, gn, d_a=None, d_res=None, target=None, f_prev=None, gp_prev=None, alpha_prev=1.0):
    m, d = h.shape
    tm = _row_tile(m)
    has_loss = target is not None
    has_res = d_res is not None
    has_prev = f_prev is not None

    def body(*refs):
        refs = list(refs)
        h_ref, gn_ref = refs[0], refs[1]
        pos = 2
        da_ref = dres_ref = t_ref = f_ref = gp_ref = None
        if has_loss:
            t_ref = refs[pos]; pos += 1
        else:
            da_ref = refs[pos]; pos += 1
        if has_res:
            dres_ref = refs[pos]; pos += 1
        if has_prev:
            f_ref, gp_ref = refs[pos], refs[pos + 1]; pos += 2
        dh_ref, dgn_ref = refs[pos], refs[pos + 1]; pos += 2
        df_ref = dgp_ref = loss_ref = None
        if has_prev:
            df_ref, dgp_ref = refs[pos], refs[pos + 1]; pos += 2
        if has_loss:
            loss_ref = refs[pos]

        first = pl.program_id(0) == 0
        hv = h_ref[...]
        gn = gn_ref[...]
        if has_loss:
            err = hv * _rstd(hv) * gn - t_ref[...]
            da = err * (1.0 / d)
            part = 0.5 * jnp.sum(jnp.sum(err * err, axis=-1, keepdims=True) * (1.0 / d))

            @pl.when(first)
            def _():
                loss_ref[...] = jnp.zeros_like(loss_ref)

            loss_ref[...] += part
        else:
            da = da_ref[...].astype(F32)
        dx, dgn = _rms_bwd(hv, gn, da)
        dh = dx + dres_ref[...] if has_res else dx
        dh_ref[...] = dh

        @pl.when(first)
        def _():
            dgn_ref[...] = jnp.zeros_like(dgn_ref)

        dgn_ref[...] += dgn
        if has_prev:
            dfv, dgp = _rms_bwd(f_ref[...], gp_ref[...], dh)
            df_ref[...] = (alpha_prev * dfv).astype(BF)

            @pl.when(first)
            def _():
                dgp_ref[...] = jnp.zeros_like(dgp_ref)

            dgp_ref[...] += alpha_prev * dgp

    row = pl.BlockSpec((tm, d), lambda i: (i, 0))
    vec = pl.BlockSpec((1, d), lambda i: (0, 0))
    ins, in_specs = [h, gn], [row, vec]
    ins.append(target if has_loss else d_a); in_specs.append(row)
    if has_res:
        ins.append(d_res); in_specs.append(row)
    if has_prev:
        ins += [f_prev, gp_prev]; in_specs += [row, vec]
    outs = [jax.ShapeDtypeStruct((m, d), F32), jax.ShapeDtypeStruct((1, d), F32)]
    out_specs = [row, vec]
    names = ["d_h", "d_gn"]
    if has_prev:
        outs += [jax.ShapeDtypeStruct((m, d), BF), jax.ShapeDtypeStruct((1, d), F32)]
        out_specs += [row, vec]
        names += ["d_f", "d_gp"]
    if has_loss:
        outs.append(jax.ShapeDtypeStruct((8, LANES), F32))
        out_specs.append(pl.BlockSpec((8, LANES), lambda i: (0, 0)))
        names.append("loss")
    res = _call(
        body, name="norm_bwd", out_shape=tuple(outs), grid=(m // tm,), in_specs=in_specs,
        out_specs=tuple(out_specs), compiler_params=_cp("arbitrary"),
    )(*ins)
    return dict(zip(names, res))


def ffn_bwd_act(df, gb, which, g, u):
    m, d = df.shape
    nb = g.shape[-1]
    tm = _row_tile(m, 1024)

    def body(df_ref, w_ref, g_ref, u_ref, dg_ref, du_ref):
        parts = _row_parts(tm, 2)
        dhs = [_dot_nt(df_ref[rs, :], w_ref[...]) for rs in parts]
        for rs, dh in zip(parts, dhs):
            gv = g_ref[rs, :].astype(F32)
            uv = u_ref[rs, :].astype(F32)
            s = jax.nn.sigmoid(gv)
            dg_ref[rs, :] = (dh * uv * (s * (1.0 + gv * (1.0 - s)))).astype(BF)
            du_ref[rs, :] = (dh * gv * s).astype(BF)

    blk = jax.ShapeDtypeStruct((N_CHIPS, m, nb), BF)
    aspec = pl.BlockSpec((None, tm, nb), lambda k, i: (k, i, 0))
    return _call(
        body, name="ffn_bwd_act", out_shape=(blk, blk), grid=(N_CHIPS, m // tm),
        in_specs=[pl.BlockSpec((tm, d), lambda k, i: (i, 0)),
                  pl.BlockSpec((None, None, nb, d), lambda k, i: (k, which, 0, 0)), aspec, aspec],
        out_specs=(aspec, aspec), compiler_params=_cp("parallel", "parallel"),
    )(df, gb, g, u)


def mm_tn(a, b, dest, a_spec, b_spec, o_spec, acc_shape, msteps, comm=None):
    def body(a_ref, b_ref, dest_ref, o_ref, acc):
        del dest_ref
        ms = pl.program_id(1)

        @pl.when(ms == 0)
        def _():
            acc[...] = jnp.zeros_like(acc)

        acc[...] += _dot_tn(a_ref[...], b_ref[...])

        @pl.when(ms == msteps - 1)
        def _():
            o_ref[...] = acc[...].astype(o_ref.dtype)

    (out,), extra = _call_with_comm(
        body, comm, (a, b, dest), name="mm_tn", out_shape=(jax.ShapeDtypeStruct(dest.shape, dest.dtype),),
        grid=(N_CHIPS, msteps), in_specs=[a_spec, b_spec, pl.BlockSpec(memory_space=pl.ANY)], out_specs=(o_spec,),
        scratch_shapes=[pltpu.VMEM(acc_shape, F32)], core_aliases={2: 0},
        compiler_params=_cp("arbitrary", "arbitrary") if comm is not None else _cp("parallel", "arbitrary"),
    )
    return out if comm is None else (out, extra)


def _act_spec(arr, tm, nb):
    if arr.ndim == 3:
        return pl.BlockSpec((None, tm, nb), lambda k, ms: (k, ms, 0))
    return pl.BlockSpec((tm, nb), lambda k, ms: (ms, k))


def grad_cb(a, dout, dest, which=None):
    m, kd = a.shape
    nb = dest.shape[-1]
    tm = _row_tile(m, GRAD_ROWS)
    if which is None:
        o_spec = pl.BlockSpec((None, kd, nb), lambda k, ms: (k, 0, 0))
    else:
        o_spec = pl.BlockSpec((None, None, kd, nb), lambda k, ms: (k, which, 0, 0))
    return mm_tn(a, dout, dest, pl.BlockSpec((tm, kd), lambda k, ms: (ms, 0)), _act_spec(dout, tm, nb), o_spec,
                 (kd, nb), m // tm)


def grad_rb(a, dout, dest, which, comm=None):
    m, n = dout.shape
    kb = dest.shape[-2]
    tm = _row_tile(m, GRAD_ROWS)
    o_spec = pl.BlockSpec((None, None, kb, n), lambda k, ms: (k, which, 0, 0))
    return mm_tn(a, dout, dest, _act_spec(a, tm, kb), pl.BlockSpec((tm, n), lambda k, ms: (ms, 0)), o_spec,
                 (kb, n), m // tm, comm=comm)


def _norm_n_in(norm):
    return 3 + (2 if norm.get("f_prev") is not None else 0)


def _norm_join_operands(norm, tm, row_map, vec_map):
    h = norm["h"]
    m, d = h.shape
    row, vec = pl.BlockSpec((tm, d), row_map), pl.BlockSpec((1, d), vec_map)
    ins, in_specs = [h, norm["gn"], norm["d_res"]], [row, vec, row]
    outs = [jax.ShapeDtypeStruct((m, d), F32), jax.ShapeDtypeStruct((1, d), F32)]
    out_specs, names = [row, vec], ["d_h", "d_gn"]
    if norm.get("f_prev") is not None:
        ins += [norm["f_prev"], norm["gp_prev"]]
        in_specs += [row, vec]
        outs += [jax.ShapeDtypeStruct((m, d), BF), jax.ShapeDtypeStruct((1, d), F32)]
        out_specs += [row, vec]
        names += ["d_f", "d_gp"]
    return ins, in_specs, outs, out_specs, names


def _norm_join(da, in_refs, out_refs, norm, first):
    h_ref, gn_ref, dres_ref = in_refs[:3]
    dh_ref, dgn_ref = out_refs[:2]
    has_prev = norm.get("f_prev") is not None
    alpha = norm.get("alpha_prev", 1.0)

    @pl.when(first)
    def _():
        dgn_ref[...] = jnp.zeros_like(dgn_ref)
        if has_prev:
            out_refs[3][...] = jnp.zeros_like(out_refs[3])

    dx, dgn = _rms_bwd(h_ref[...], gn_ref[...], da)
    dh = dx + dres_ref[...]
    dh_ref[...] = dh
    dgn_ref[...] += dgn
    if has_prev:
        dfv, dgp = _rms_bwd(in_refs[3][...], in_refs[4][...], dh)
        out_refs[2][...] = (alpha * dfv).astype(BF)
        out_refs[3][...] += alpha * dgp


def mm_nt_cb(pairs, n, out_dtype, comm=None, norm=None):
    d0 = pairs[0][0]
    m = d0.shape[1] if d0.ndim == 3 else d0.shape[0]
    tm = _row_tile(m, 1024 if norm is None else 512)
    npair = len(pairs)
    n_norm_in = 0 if norm is None else _norm_n_in(norm)

    def body(*refs):
        outs = refs[2 * npair + n_norm_in:-1]
        acc = refs[-1]
        i, k = pl.program_id(0), pl.program_id(1)

        @pl.when(k == 0)
        def _():
            acc[...] = jnp.zeros_like(acc)

        for p in range(npair):
            acc[...] += _dot_nt(refs[2 * p][...], refs[2 * p + 1][...])

        @pl.when(k == N_CHIPS - 1)
        def _():
            if norm is None:
                outs[0][...] = acc[...].astype(out_dtype)
            else:
                _norm_join(acc[...], refs[2 * npair:2 * npair + n_norm_in], outs, norm, i == 0)

    ins, in_specs = [], []
    for dout, w, which in pairs:
        nb = w.shape[-1]
        if dout.ndim == 3:
            in_specs.append(pl.BlockSpec((None, tm, nb), lambda i, k: (k, i, 0)))
        else:
            in_specs.append(pl.BlockSpec((tm, nb), lambda i, k: (i, k)))
        if w.ndim == 4:
            in_specs.append(pl.BlockSpec((None, None, n, nb), lambda i, k, which=which: (k, which, 0, 0)))
        else:
            in_specs.append(pl.BlockSpec((None, n, nb), lambda i, k: (k, 0, 0)))
        ins += [dout, w]
    if norm is None:
        out_shape = (jax.ShapeDtypeStruct((m, n), out_dtype),)
        out_specs = (pl.BlockSpec((tm, n), lambda i, k: (i, 0)),)
        names = None
    else:
        n_ins, n_specs, out_shape, out_specs, names = _norm_join_operands(norm, tm, lambda i, k: (i, 0), lambda i, k: (0, 0))
        ins += n_ins
        in_specs += n_specs
    sequential = comm is not None or norm is not None
    res, extra = _call_with_comm(
        body, comm, tuple(ins), name="mm_nt_cb", out_shape=tuple(out_shape), grid=(m // tm, N_CHIPS),
        in_specs=in_specs, out_specs=tuple(out_specs), scratch_shapes=[pltpu.VMEM((tm, n), F32)],
        compiler_params=_cp("arbitrary", "arbitrary") if sequential else _cp("parallel", "arbitrary"),
    )
    out = res[0] if norm is None else dict(zip(names, res))
    return out if comm is None else (out, extra)


def mm_nt_rb(dout, w, which, out_dtype, norm=None):
    m, n = dout.shape
    kb = w.shape[2]
    tm = _row_tile(m, 1024 if norm is None else 512)

    def body(d_ref, w_ref, *rest):
        da = _dot_nt(d_ref[...], w_ref[...].reshape(N_CHIPS * kb, n))
        if norm is None:
            rest[0][...] = da.astype(out_dtype)
        else:
            _norm_join(da, rest[:_norm_n_in(norm)], rest[_norm_n_in(norm):], norm, pl.program_id(0) == 0)

    ins = [dout, w]
    in_specs = [pl.BlockSpec((tm, n), lambda i: (i, 0)), pl.BlockSpec((N_CHIPS, None, kb, n), lambda i: (0, which, 0, 0))]
    if norm is None:
        return _call(
            body, name="mm_nt_rb", out_shape=jax.ShapeDtypeStruct((m, N_CHIPS * kb), out_dtype), grid=(m // tm,),
            in_specs=in_specs, out_specs=pl.BlockSpec((tm, N_CHIPS * kb), lambda i: (i, 0)),
            compiler_params=_cp("parallel"),
        )(*ins)
    n_ins, n_specs, out_shape, out_specs, names = _norm_join_operands(norm, tm, lambda i: (i, 0), lambda i: (0, 0))
    res = _call(
        body, name="mm_nt_rb", out_shape=tuple(out_shape), grid=(m // tm,), in_specs=in_specs + n_specs,
        out_specs=tuple(out_specs), compiler_params=_cp("arbitrary"),
    )(*ins, *n_ins)
    return dict(zip(names, res))


def merge_norm_bwd(oa, ob, dmerged, ga, gb):
    m, w = oa.shape
    tm = _row_tile(m)

    def body(a_ref, b_ref, dm_ref, ga_ref, gb_ref, da_ref, db_ref, dga_ref, dgb_ref):
        @pl.when(pl.program_id(0) == 0)
        def _():
            dga_ref[...] = jnp.zeros_like(dga_ref)
            dgb_ref[...] = jnp.zeros_like(dgb_ref)

        da, dga = _rms_bwd(a_ref[...].astype(F32), ga_ref[...], dm_ref[:, :w].astype(F32))
        db, dgb = _rms_bwd(b_ref[...].astype(F32), gb_ref[...], dm_ref[:, w:].astype(F32))
        da_ref[...] = da
        db_ref[...] = db
        dga_ref[...] += dga
        dgb_ref[...] += dgb

    row = pl.BlockSpec((tm, w), lambda i: (i, 0))
    vec = pl.BlockSpec((1, w), lambda i: (0, 0))
    return _call(
        body, name="merge_norm_bwd", grid=(m // tm,),
        out_shape=(jax.ShapeDtypeStruct((m, w), F32), jax.ShapeDtypeStruct((m, w), F32),
                   jax.ShapeDtypeStruct((1, w), F32), jax.ShapeDtypeStruct((1, w), F32)),
        in_specs=[row, row, pl.BlockSpec((tm, 2 * w), lambda i: (i, 0)), vec, vec],
        out_specs=(row, row, vec, vec), compiler_params=_cp("arbitrary"),
    )(oa, ob, dmerged, ga, gb)


def sgu_bwd(proj, d_oa, wm, wmt, ng, nb, bst):
    m = proj.shape[0]
    tm = _row_tile(m)
    wd = SGU_GROUPS * CHUNK

    def body(u_ref, v_ref, do_ref, wm_ref, wmt_ref, ng_ref, nb_ref, bst_ref,
             dp_ref, dw_ref, dbt_ref, dng_ref, dnb_ref):
        @pl.when(pl.program_id(0) == 0)
        def _():
            dw_ref[...] = jnp.zeros_like(dw_ref)
            dbt_ref[...] = jnp.zeros_like(dbt_ref)
            dng_ref[...] = jnp.zeros_like(dng_ref)
            dnb_ref[...] = jnp.zeros_like(dnb_ref)

        causal = lax.broadcasted_iota(jnp.int32, (CHUNK, CHUNK), 0) >= lax.broadcasted_iota(jnp.int32, (CHUNK, CHUNK), 1)
        for c in range(tm // CHUNK):
            rs = slice(c * CHUNK, (c + 1) * CHUNK)
            for g in range(SGU_GROUPS):
                cs = slice(g * CHUNK, (g + 1) * CHUNK)
                ug, xh, rstd, vn, mixed = _sgu_core(u_ref, v_ref, wm_ref, ng_ref, nb_ref, bst_ref, g, c)
                do = do_ref[rs, cs]
                dug = do * mixed
                dmix = do * ug
                dmb = dmix.astype(BF)
                dbt_ref[g] += jnp.sum(dmix, axis=-1, keepdims=True)
                dw_ref[g] += jnp.where(causal, _dot_nt(dmb, vn.astype(BF)), 0.0)
                dvn = _dot(wmt_ref[g], dmb)
                dng_ref[g:g + 1, :] += jnp.sum(dvn * xh, axis=0, keepdims=True)
                dnb_ref[g:g + 1, :] += jnp.sum(dvn, axis=0, keepdims=True)
                dxh = dvn * ng_ref[g:g + 1, :]
                dvg = rstd * (dxh - jnp.mean(dxh, axis=-1, keepdims=True)
                              - xh * jnp.mean(dxh * xh, axis=-1, keepdims=True))
                dp_ref[rs, cs] = (dug * _gelu_grad(u_ref[rs, cs].astype(F32))).astype(BF)
                dp_ref[rs, wd + g * CHUNK:wd + (g + 1) * CHUNK] = (dvg * _gelu_grad(v_ref[rs, cs].astype(F32))).astype(BF)

    full = lambda shape: pl.BlockSpec(shape, lambda i: (0,) * len(shape))
    return _call(
        body, name="sgu_bwd", grid=(m // tm,),
        out_shape=(jax.ShapeDtypeStruct((m, 2 * wd), BF), jax.ShapeDtypeStruct(wm.shape, F32),
                   jax.ShapeDtypeStruct(bst.shape, F32), jax.ShapeDtypeStruct(ng.shape, F32),
                   jax.ShapeDtypeStruct(nb.shape, F32)),
        in_specs=[pl.BlockSpec((tm, wd), lambda i: (i, 0)), pl.BlockSpec((tm, wd), lambda i: (i, 1)),
                  pl.BlockSpec((tm, wd), lambda i: (i, 0)),
                  full(wm.shape), full(wmt.shape), full(ng.shape), full(nb.shape), full(bst.shape)],
        out_specs=(pl.BlockSpec((tm, 2 * wd), lambda i: (i, 0)), full(wm.shape), full(bst.shape), full(ng.shape),
                   full(nb.shape)),
        compiler_params=_cp("arbitrary"),
    )(proj, proj, d_oa, wm, wmt, ng, nb, bst)


def sb_bwd(proj, tot, d_ob, comm=None):
    m = proj.shape[0]
    tq, tk = _sb_tiles(m)
    ndiag = tq // tk

    def body(q_ref, k_ref, v_ref, tot_ref, do_ref, dq_ref, dk_ref, dv_ref, qs, dos, tots, dqa, cl1, cg):
        i = pl.program_id(1)

        @pl.when(i == 0)
        def _():
            dk_ref[...] = jnp.zeros_like(dk_ref)
            dv_ref[...] = jnp.zeros_like(dv_ref)

        nfull = i * ndiag
        heads = _head_masks()
        r_io = lax.broadcasted_iota(jnp.int32, (tk, tk), 0)
        c_io = lax.broadcasted_iota(jnp.int32, (tk, tk), 1)
        incl = (r_io <= c_io).astype(BF)
        excl = (r_io < c_io).astype(BF)
        qv = q_ref[...]
        dov = do_ref[...].astype(BF)
        totv = tot_ref[...]
        for hd in range(2):
            qs[hd] = jnp.where(heads[hd], qv, jnp.zeros_like(qv)) * SB_SCALE
            dos[hd] = jnp.where(heads[hd], dov, jnp.zeros_like(dov))
            tots[hd] = jnp.max(jnp.where(heads[hd], totv, -jnp.inf), axis=-1, keepdims=True)
        dqa[...] = jnp.zeros_like(dqa)
        cl1[...] = jnp.zeros_like(cl1)
        cg[...] = jnp.zeros_like(cg)

        def block(ks, r0, diag):
            kb = k_ref[pl.ds(ks, tk), :]
            vb = v_ref[pl.ds(ks, tk), :]
            chains = _sb_chains(r0, tq, 256)
            masks = [None if diag is None else _sb_diag_mask(ra, rb, diag, tk) for _, ra, rb in chains]
            qc = [qs[hd, ra:rb, :] for hd, ra, rb in chains]
            doc = [dos[hd, ra:rb, :] for hd, ra, rb in chains]
            zs = [_dot_nt(q, kb) for q in qc]
            das = [_dot_nt(do, vb) for do in doc]
            s1 = []
            for z, mask in zip(zs, masks):
                ls, l1 = _log_sigmoid_pair(z, mask)
                s1.append((ls, l1, _dot2(l1, incl)))
            s2 = []
            for (hd, ra, rb), (ls, l1, pre), da, mask in zip(chains, s1, das, masks):
                a = jnp.exp(ls + (tots[hd, ra:rb, :] - (pre + cl1[hd, ra:rb, :])))
                if mask is not None:
                    a = jnp.where(mask, a, 0.0)
                gmat = a * da
                s2.append((a, gmat, _dot2(gmat, excl)))
            dk_sum = dv_sum = None
            for n, (hd, ra, rb) in enumerate(chains):
                a, gmat, pref = s2[n]
                sg = jnp.exp(s1[n][0])
                dz = gmat * (1.0 - sg) - (pref + cg[hd, ra:rb, :]) * sg
                if masks[n] is not None:
                    dz = jnp.where(masks[n], dz, 0.0)
                dz = dz.astype(BF)
                dqa[hd, ra:rb, :] += _dot(dz, kb)
                dk_t = _dot_tn(dz, qc[n])
                dv_t = _dot_tn(a.astype(BF), doc[n])
                dk_sum = dk_t if dk_sum is None else dk_sum + dk_t
                dv_sum = dv_t if dv_sum is None else dv_sum + dv_t
            for n, (hd, ra, rb) in enumerate(chains):
                cl1[hd, ra:rb, :] += jnp.sum(s1[n][1], axis=-1, keepdims=True)
                cg[hd, ra:rb, :] += jnp.sum(s2[n][1], axis=-1, keepdims=True)
            dk_ref[pl.ds(ks, tk), :] += dk_sum
            dv_ref[pl.ds(ks, tk), :] += dv_sum

        def step(j, carry):
            block(pl.multiple_of(j * tk, tk), 0, None)
            return carry

        lax.fori_loop(0, nfull, step, 0)
        for d in range(ndiag):
            block(pl.multiple_of((nfull + d) * tk, tk), d * tk, d)
        dq_ref[...] = jnp.where(heads[0], dqa[0], dqa[1]) * SB_SCALE

    qb = 2 * 512 // LANES
    tile = pl.BlockSpec((tq, LANES), lambda p, i: (i, p))
    seq = pl.BlockSpec((m, LANES), lambda p, i: (0, p))
    out = jax.ShapeDtypeStruct((m, 512), F32)
    return _call_with_comm(
        body, comm, (proj, proj, proj, tot, d_ob), name="sb_bwd", grid=(4, m // tq), out_shape=(out, out, out),
        in_specs=[pl.BlockSpec((tq, LANES), lambda p, i: (i, qb + p)),
                  pl.BlockSpec((m, LANES), lambda p, i: (0, qb + 4 + p)),
                  pl.BlockSpec((m, LANES), lambda p, i: (0, qb + 8 + p)), tile, tile],
        out_specs=(tile, seq, seq),
        scratch_shapes=[pltpu.VMEM((2, tq, LANES), BF), pltpu.VMEM((2, tq, LANES), BF), pltpu.VMEM((2, tq, 1), F32),
                        pltpu.VMEM((2, tq, LANES), F32), pltpu.VMEM((2, tq, 1), F32), pltpu.VMEM((2, tq, 1), F32)],
        compiler_params=_cp("arbitrary", "arbitrary"),
    )


def xa_bwd(xq, kv, d_o):
    m, d = xq.shape
    mm = kv.shape[0]
    tm = _row_tile(m)

    def body(q_ref, kv_ref, do_ref, dq_ref, dkv_ref):
        @pl.when(pl.program_id(0) == 0)
        def _():
            dkv_ref[...] = jnp.zeros_like(dkv_ref)

        for hd in range(XA_HEADS):
            cs = slice(hd * XA_HEAD_DIM, (hd + 1) * XA_HEAD_DIM)
            vs = slice(d + hd * XA_HEAD_DIM, d + (hd + 1) * XA_HEAD_DIM)
            qh = q_ref[:, cs]
            kh = kv_ref[:, cs]
            doh = do_ref[:, cs]
            p = _xa_probs(qh, kh)
            dp = _dot_nt(doh, kv_ref[:, vs])
            ds = (p * (dp - jnp.sum(p * dp, axis=-1, keepdims=True))).astype(BF)
            dq_ref[:, cs] = (_dot(ds, kh) * XA_SCALE).astype(BF)
            dkv_ref[:, cs] += _dot_tn(ds, qh) * XA_SCALE
            dkv_ref[:, vs] += _dot_tn(p.astype(BF), doh)

    row = pl.BlockSpec((tm, d), lambda i: (i, 0))
    whole = pl.BlockSpec((mm, 2 * d), lambda i: (0, 0))
    return _call(
        body, name="xa_bwd", grid=(m // tm,),
        out_shape=(jax.ShapeDtypeStruct((m, d), BF), jax.ShapeDtypeStruct((mm, 2 * d), F32)),
        in_specs=[row, whole, row], out_specs=(row, whole), compiler_params=_cp("arbitrary"),
    )(xq, kv, d_o)


def adamw(w, g, mom, vel, g_index=None):
    r, c = w.shape
    tr = r
    for cand in range(512, 7, -8):
        if r % cand == 0:
            tr = cand
            break

    def body(w_ref, g_ref, m_ref, v_ref, go_ref, d_ref, nm_ref, nv_ref):
        gv = g_ref[...]
        mn = ADAM_B1 * m_ref[...] + (1.0 - ADAM_B1) * gv
        vn = ADAM_B2 * v_ref[...] + (1.0 - ADAM_B2) * (gv * gv)
        m_hat = mn / (1.0 - ADAM_B1 ** ADAM_STEP)
        v_hat = vn / (1.0 - ADAM_B2 ** ADAM_STEP)
        go_ref[...] = gv
        d_ref[...] = -ADAM_LR * (m_hat / (jnp.sqrt(v_hat) + ADAM_EPS) + ADAM_WD * w_ref[...])
        nm_ref[...] = mn
        nv_ref[...] = vn

    spec = pl.BlockSpec((tr, c), lambda i: (i, 0))
    gspec = spec if g_index is None else pl.BlockSpec((None, tr, c), lambda i: (g_index, i, 0))
    out = jax.ShapeDtypeStruct((r, c), F32)
    return _call(
        body, name="adamw", out_shape=(out, out, out, out), grid=(r // tr,), in_specs=[spec, gspec, spec, spec],
        out_specs=(spec, spec, spec, spec), compiler_params=_cp("parallel"),
    )(w, g, mom, vel)


def _place():
    x, y, c = lax.axis_index("x"), lax.axis_index("y"), lax.axis_index("c")
    others = [(1 - x, y), (x, 1 - y), (1 - x, 1 - y)]
    return x, y, c, others


_HALF = {"A1": (1, 512), "A2": (1, 512), "B1": (0, 352), "B2": (0, 352), "C": (1, 128), "D": (0, 512), "E": (0, 512),
         "A1T": (1, 352), "A2T": (1, 352)}
SET_EARLY = ("A1", "B1", "D")
SET_LATE = ("A2", "B2", "C", "E")
GRAD_EARLY = ("A1T", "B1", "D")
GRAD_LATE = ("A2T", "B2", "C", "E")
_HBM = pl.BlockSpec(memory_space=pl.ANY)


def _half_of(ref, name, hc, lead=0):
    axis, size = _HALF[name]
    idx = [slice(None)] * (lead + axis) + [pl.ds(hc * size, size)]
    return ref.at[tuple(idx)]


def _gather_slots(loc, names):
    me = 2 * lax.axis_index("x") + lax.axis_index("y")
    init = []
    for nm in names:
        full = lax.empty((N_CHIPS,) + loc[nm].shape, loc[nm].dtype)
        init.append(lax.dynamic_update_slice(full, loc[nm][None], (me,) + (0,) * loc[nm].ndim))
    return init


def _gather_ici(names, src, out, send, recv, sends=True, arrivals=True):
    x, y, c, others = _place()
    me = 2 * x + y
    out_sends, out_arrivals = [], []
    for a, nm in enumerate(names):
        for j, (ox, oy) in enumerate(others):
            sems = dict(send_sem=send.at[3 * a + j], recv_sem=recv.at[3 * a + j], device_id_type=MESH)
            if sends:
                out_sends.append(pltpu.make_async_remote_copy(
                    src_ref=_half_of(src[a], nm, c), dst_ref=_half_of(out[a].at[me], nm, c), device_id=(ox, oy, c),
                    **sems))
            if arrivals:
                landed = _half_of(out[a].at[2 * ox + oy], nm, c)
                out_arrivals.append(pltpu.make_async_remote_copy(src_ref=landed, dst_ref=landed, device_id=(x, y, c),
                                                                 **sems))
    return out_sends, out_arrivals


def _gather_d2d(names, given, out, send, recv):
    x, y, c, others = _place()
    sends, arrivals = [], []
    for a, nm in enumerate(names):
        for j, (ox, oy) in enumerate(others):
            sems = dict(send_sem=send.at[3 * a + j], recv_sem=recv.at[3 * a + j], device_id_type=MESH)
            sends.append(pltpu.make_async_remote_copy(
                src_ref=_half_of(given[a].at[2 * ox + oy], nm, c), dst_ref=_half_of(out[a].at[2 * ox + oy], nm, c),
                device_id=(x, y, 1 - c), **sems))
            landed = _half_of(out[a].at[2 * ox + oy], nm, 1 - c)
            arrivals.append(pltpu.make_async_remote_copy(src_ref=landed, dst_ref=landed, device_id=(x, y, c), **sems))
    return sends, arrivals


def gather_weights(loc, names):
    n = len(names)

    def body(*refs):
        src, given, out = refs[:n], refs[n:2 * n], refs[2 * n:3 * n]
        send1, recv1, send2, recv2 = refs[3 * n:3 * n + 4]
        first, landed = _gather_ici(names, src, out, send1, recv1)
        del given
        passed, arrivals = _gather_d2d(names, out, out, send2, recv2)
        for f in first:
            f.start()
        for l, p in zip(landed, passed):
            l.wait_recv()
            p.start()
        for a in arrivals:
            a.wait_recv()
        for f in first + passed:
            f.wait_send()

    init = _gather_slots(loc, names)
    res = _call(
        body, name="gather_weights", out_shape=tuple(jax.ShapeDtypeStruct(t.shape, t.dtype) for t in init),
        in_specs=[_HBM] * (2 * n), out_specs=(_HBM,) * n, input_output_aliases={n + a: a for a in range(n)},
        scratch_shapes=[pltpu.SemaphoreType.DMA((3 * n,))] * 4,
    )(*[loc[nm] for nm in names], *init)
    return dict(zip(names, res))


def gather_forward(bufs, names):
    n = len(names)

    def body(*refs):
        given, out = refs[:n], refs[n:2 * n]
        passed, arrivals = _gather_d2d(names, given, out, refs[2 * n], refs[2 * n + 1])
        for p in passed:
            p.start()
        for a in arrivals:
            a.wait_recv()
        for p in passed:
            p.wait_send()

    res = _call(
        body, name="gather_forward", out_shape=tuple(jax.ShapeDtypeStruct(t.shape, t.dtype) for t in bufs),
        in_specs=[_HBM] * n, out_specs=(_HBM,) * n, input_output_aliases={a: a for a in range(n)},
        scratch_shapes=[pltpu.SemaphoreType.DMA((3 * n,))] * 2,
    )(*bufs)
    return dict(zip(names, res))


class FusedComm:
    def __init__(self, ins, outs, aliases, n_sems, start, finish):
        self.ins, self.outs, self.aliases, self.n_sems, self.start, self.finish = ins, outs, aliases, n_sems, start, finish


def gather_comm(loc, names):
    n = len(names)

    def start(ins, outs, send, recv):
        for f in _gather_ici(names, ins[:n], outs, send, recv, arrivals=False)[0]:
            f.start()

    def finish(ins, outs, send, recv):
        first, landed = _gather_ici(names, ins[:n], outs, send, recv)
        for l in landed:
            l.wait_recv()
        for f in first:
            f.wait_send()

    init = _gather_slots(loc, names)
    return FusedComm([loc[nm] for nm in names] + init, [jax.ShapeDtypeStruct(t.shape, t.dtype) for t in init],
                     {n + a: a for a in range(n)}, 3 * n, start, finish)


def _half_shape(name, shape):
    axis, size = _HALF[name]
    s = list(shape)
    s[axis] = size
    return tuple(s)


def rs_to_sibling(grads, names):
    n = len(names)

    def body(*refs):
        src = dict(zip(names, refs[:n]))
        out = dict(zip(names, refs[n:2 * n]))
        send, recv = refs[2 * n], refs[2 * n + 1]
        x, y, c, _ = _place()
        copies = []
        for a, nm in enumerate(names):
            copies.append(pltpu.make_async_remote_copy(
                src_ref=_half_of(src[nm], nm, 1 - c, lead=1), dst_ref=out[nm], send_sem=send.at[a], recv_sem=recv.at[a],
                device_id=(x, y, 1 - c), device_id_type=MESH))
        for cpy in copies:
            cpy.start()
        for cpy in copies:
            cpy.wait()

    hbm = pl.BlockSpec(memory_space=pl.ANY)
    outs = tuple(jax.ShapeDtypeStruct((N_CHIPS,) + _half_shape(nm, grads[nm].shape[1:]), grads[nm].dtype)
                 for nm in names)
    res = _call(
        body, name="rs_to_sibling", out_shape=outs, in_specs=[hbm] * n, out_specs=(hbm,) * n,
        scratch_shapes=[pltpu.SemaphoreType.DMA((n,)), pltpu.SemaphoreType.DMA((n,))],
    )(*[grads[nm] for nm in names])
    return dict(zip(names, res))


def _tile2(shape):
    lead = shape[:-2]
    return lead, shape[-2:]


def add_halves(name, mine, got, c_idx):
    axis, size = _HALF[name]
    hshape = got.shape
    lead, last2 = hshape[:-2], hshape[-2:]
    nlead = len(lead)
    haxis = 1 + axis

    def body(c_ref, m_ref, g_ref, o_ref):
        del c_ref
        o_ref[...] = (m_ref[...].astype(F32) + g_ref[...].astype(F32)).astype(o_ref.dtype)

    blk = (None,) * nlead + last2

    def got_map(*idx):
        return tuple(idx[:nlead]) + (0, 0)

    def mine_map(*idx):
        lead_idx = list(idx[:nlead])
        c = idx[nlead][0]
        if haxis < nlead:
            lead_idx[haxis] = lead_idx[haxis] + c * size
            return tuple(lead_idx) + (0, 0)
        return tuple(lead_idx) + (c, 0)

    grid_spec = pltpu.PrefetchScalarGridSpec(
        num_scalar_prefetch=1, grid=lead,
        in_specs=[pl.BlockSpec(blk, mine_map), pl.BlockSpec(blk, got_map)],
        out_specs=pl.BlockSpec(blk, got_map))
    return _call(
        body, name="add_halves", out_shape=jax.ShapeDtypeStruct(hshape, got.dtype), grid_spec=grid_spec,
        compiler_params=_cp(*(("parallel",) * nlead)),
    )(c_idx, mine, got)


def _rs_ici(n, src, out, send, recv):
    x, y, c, others = _place()
    copies = []
    for a in range(n):
        for j, (ox, oy) in enumerate(others):
            copies.append(pltpu.make_async_remote_copy(
                src_ref=src[a].at[2 * ox + oy], dst_ref=out[a].at[j], send_sem=send.at[3 * a + j],
                recv_sem=recv.at[3 * a + j], device_id=(ox, oy, c), device_id_type=MESH))
    return copies


def _rs_out_shapes(summed, names):
    return [jax.ShapeDtypeStruct((3,) + summed[nm].shape[1:], summed[nm].dtype) for nm in names]


def rs_comm(summed, names):
    n = len(names)

    def start(ins, outs, send, recv):
        for cpy in _rs_ici(n, ins, outs, send, recv):
            cpy.start()

    def finish(ins, outs, send, recv):
        for cpy in _rs_ici(n, ins, outs, send, recv):
            cpy.wait()

    return FusedComm([summed[nm] for nm in names], _rs_out_shapes(summed, names), {}, 3 * n, start, finish)


def add_chips(name, summed, got, kc_idx, full_shape):
    axis, size = _HALF[name]
    hshape = summed.shape[1:]
    lead, last2 = hshape[:-2], hshape[-2:]
    nlead = len(lead)

    def body(kc_ref, s_ref, g0_ref, g1_ref, g2_ref, o_ref):
        del kc_ref
        o_ref[...] = ((s_ref[...].astype(F32) + g0_ref[...].astype(F32)) + g1_ref[...].astype(F32)) + g2_ref[...].astype(F32)

    blk = (None,) * (nlead + 1) + last2
    oblk = (None,) * nlead + last2

    def got_map(slot):
        return lambda *idx: (slot,) + tuple(idx[:nlead]) + (0, 0)

    def out_map(*idx):
        lead_idx = list(idx[:nlead])
        c = idx[-1][1]
        if axis < nlead:
            lead_idx[axis] = lead_idx[axis] + c * size
            return tuple(lead_idx) + (0, 0)
        return tuple(lead_idx) + (c, 0)

    grid_spec = pltpu.PrefetchScalarGridSpec(
        num_scalar_prefetch=1, grid=lead if nlead else (1,),
        in_specs=[pl.BlockSpec(blk, lambda *idx: (idx[-1][0],) + tuple(idx[:nlead]) + (0, 0)),
                  pl.BlockSpec(blk, got_map(0)), pl.BlockSpec(blk, got_map(1)), pl.BlockSpec(blk, got_map(2))],
        out_specs=pl.BlockSpec(oblk, out_map))
    return _call(
        body, name="add_chips", out_shape=jax.ShapeDtypeStruct(full_shape, F32), grid_spec=grid_spec,
        compiler_params=_cp(*(("parallel",) * max(nlead, 1))),
    )(kc_idx, summed, got, got, got)


def rs_replicate(shards, names):
    n = len(names)

    def body(*refs):
        given = dict(zip(names, refs[:n]))
        buf = dict(zip(names, refs[n:2 * n]))
        send, recv = refs[2 * n], refs[2 * n + 1]
        x, y, c, _ = _place()
        copies = []
        for a, nm in enumerate(names):
            copies.append(pltpu.make_async_remote_copy(
                src_ref=_half_of(given[nm], nm, c), dst_ref=_half_of(buf[nm], nm, c), send_sem=send.at[a],
                recv_sem=recv.at[a], device_id=(x, y, 1 - c), device_id_type=MESH))
        for cpy in copies:
            cpy.start()
        for a, nm in enumerate(names):
            other = _half_of(buf[nm], nm, 1 - c)
            pltpu.make_async_remote_copy(src_ref=other, dst_ref=other, send_sem=send.at[a], recv_sem=recv.at[a],
                                         device_id=(x, y, 1 - c), device_id_type=MESH).wait_recv()
        for cpy in copies:
            cpy.wait_send()

    hbm = pl.BlockSpec(memory_space=pl.ANY)
    outs = tuple(jax.ShapeDtypeStruct(shards[nm].shape, F32) for nm in names)
    res = _call(
        body, name="rs_replicate", out_shape=outs, in_specs=[hbm] * n, out_specs=(hbm,) * n,
        input_output_aliases={a: a for a in range(n)},
        scratch_shapes=[pltpu.SemaphoreType.DMA((n,)), pltpu.SemaphoreType.DMA((n,))],
    )(*[shards[nm] for nm in names])
    return dict(zip(names, res))


def allreduce_small(v):
    r = v.shape[0]
    h = r // 2

    def body(v_ref, o_ref, sib, slots, send, recv):
        x, y, c, others = _place()
        me = 2 * x + y
        sibling = (x, y, 1 - c)
        mine = pl.ds(pl.multiple_of(c * h, 8), h)
        theirs = pl.ds(pl.multiple_of((1 - c) * h, 8), h)
        swap = pltpu.make_async_remote_copy(src_ref=v_ref.at[theirs], dst_ref=sib, send_sem=send.at[0],
                                            recv_sem=recv.at[0], device_id=sibling, device_id_type=MESH)
        swap.start()
        swap.wait()
        slots[me] = v_ref[mine] + sib[...]
        sends = [pltpu.make_async_remote_copy(src_ref=slots.at[me], dst_ref=slots.at[me], send_sem=send.at[1 + j],
                                              recv_sem=recv.at[1 + j], device_id=(ox, oy, c), device_id_type=MESH)
                 for j, (ox, oy) in enumerate(others)]
        for s in sends:
            s.start()
        for j, (ox, oy) in enumerate(others):
            got = slots.at[2 * ox + oy]
            pltpu.make_async_remote_copy(src_ref=got, dst_ref=got, send_sem=send.at[1 + j], recv_sem=recv.at[1 + j],
                                         device_id=(x, y, c), device_id_type=MESH).wait_recv()
        for s in sends:
            s.wait_send()
        o_ref[mine] = (slots[0] + slots[1]) + (slots[2] + slots[3])
        back = pltpu.make_async_remote_copy(src_ref=o_ref.at[mine], dst_ref=o_ref.at[mine], send_sem=send.at[4],
                                            recv_sem=recv.at[4], device_id=sibling, device_id_type=MESH)
        back.start()
        pltpu.make_async_remote_copy(src_ref=o_ref.at[theirs], dst_ref=o_ref.at[theirs], send_sem=send.at[4],
                                     recv_sem=recv.at[4], device_id=sibling, device_id_type=MESH).wait_recv()
        back.wait_send()

    vm = pl.BlockSpec(memory_space=pltpu.VMEM)
    return _call(
        body, name="allreduce_small", out_shape=jax.ShapeDtypeStruct(v.shape, F32), in_specs=[vm], out_specs=vm,
        scratch_shapes=[pltpu.VMEM((h, LANES), F32), pltpu.VMEM((N_CHIPS, h, LANES), F32),
                        pltpu.SemaphoreType.DMA((5,)), pltpu.SemaphoreType.DMA((5,))],
    )(v)


def local_step(x, mem, target, ga1, small, ffn1_up, fwd_sb, bwd_sb, ffn1_mid, ffn1_da):
    causal = jnp.tril(jnp.ones((CHUNK, CHUNK), dtype=bool))
    w_s = jnp.where(causal[None], small["sgu_w_s"], 0.0)
    wm = w_s.astype(BF)
    wmt = jnp.swapaxes(w_s, 1, 2).astype(BF)
    bst = small["sgu_b_s"].reshape(SGU_GROUPS, CHUNK, 1)
    ng, nbias = small["sgu_norm_g"], small["sgu_norm_b"]

    a1 = rms_fwd(x, small["ffn1_pre_g"])
    g1, u1, hid1, rest = ffn1_up(a1)
    gb1, gd = rest["B1"][:, None], rest["D"]
    f1, h1, n1 = mm_res(hid1, gb1, 0, x, small["ffn1_post_g"], 0.5, small["mix_pre_g"])
    proj = mm_cb(n1, gd)
    oa = sgu_fwd(proj, wm, ng, nbias, bst)
    ob, tot, late = fwd_sb(proj)
    ga2, gb2, gc, ge = late["A2"], late["B2"][:, None], late["C"], late["E"]
    merged = merge_norm(oa, ob, small["sgu_out_g"], small["sb_out_g"])
    mo, h2, xn = mm_res(merged, gc, 0, h1, small["mix_post_g"], 1.0, small["xa_pre_g"])
    memn = rms_fwd(mem, small["mem_norm_g"])
    kv = mm_cb(memn, ge)
    xq = mm_res_plain(xn, gc, 1)
    o = xa_fwd(xq, kv)
    cc, h3, a2 = mm_res(o, gc, 2, h2, small["xa_post_g"], 1.0, small["ffn2_pre_g"])
    (g2, u2, hid2), _ = ffn_up(a2, ga2, 0, 1)
    f2, h4, _ = mm_res(hid2, gb2, 0, h3, small["ffn2_post_g"], 0.5, small["final_norm_g"])

    gate_up_t = (N_CHIPS, 2, ga1.shape[3], ga1.shape[2])
    dga1, dga2 = lax.empty(gate_up_t, BF), lax.empty(gate_up_t, BF)
    dgb1, dgb2 = lax.empty(gb1.shape, BF), lax.empty(gb2.shape, BF)
    dgc = lax.empty(gc.shape, BF)
    dgd = lax.empty(gd.shape, BF)
    dge = lax.empty(ge.shape, BF)
    sg = {}

    r = norm_bwd(h4, small["final_norm_g"], target=target, f_prev=f2, gp_prev=small["ffn2_post_g"], alpha_prev=0.5)
    loss_tile, dh4, df2 = r["loss"], r["d_h"], r["d_f"]
    sg["final_norm_g"], sg["ffn2_post_g"] = r["d_gn"], r["d_gp"]

    dg2, du2 = ffn_bwd_act(df2, gb2, 0, g2, u2)
    dgb2 = grad_rb(hid2, df2, dgb2, 0)
    dga2 = grad_rb(dg2, a2, dga2, 0)
    dga2 = grad_rb(du2, a2, dga2, 1)
    r = mm_nt_cb([(dg2, ga2, 0), (du2, ga2, 1)], D_MODEL, F32,
                 norm=dict(h=h3, gn=small["ffn2_pre_g"], d_res=dh4, f_prev=cc, gp_prev=small["xa_post_g"], alpha_prev=1.0))
    dh3, dc = r["d_h"], r["d_f"]
    sg["ffn2_pre_g"], sg["xa_post_g"] = r["d_gn"], r["d_gp"]

    d_o = mm_nt_rb(dc, gc, 2, BF)
    dgc = grad_rb(o, dc, dgc, 2)
    dxq, dkv = xa_bwd(xq, kv, d_o)
    dkvb = dkv.astype(BF)
    dge = grad_cb(memn, dkvb, dge)
    dmemn = mm_nt_cb([(dkvb, ge, None)], D_MODEL, F32)
    sg["mem_norm_g"] = norm_bwd(mem, small["mem_norm_g"], d_a=dmemn)["d_gn"]
    dgc = grad_rb(xn, dxq, dgc, 1)
    r = mm_nt_rb(dxq, gc, 1, F32,
                 norm=dict(h=h2, gn=small["xa_pre_g"], d_res=dh3, f_prev=mo, gp_prev=small["mix_post_g"], alpha_prev=1.0))
    dh2, dmo = r["d_h"], r["d_f"]
    sg["xa_pre_g"], sg["mix_post_g"] = r["d_gn"], r["d_gp"]

    dmerged = mm_nt_rb(dmo, gc, 0, BF)
    dgc = grad_rb(merged, dmo, dgc, 0)
    d_oa, d_ob, sg["sgu_out_g"], sg["sb_out_g"] = merge_norm_bwd(oa, ob, dmerged, small["sgu_out_g"], small["sb_out_g"])
    dp_uv, dws, dbt, sg["sgu_norm_g"], sg["sgu_norm_b"] = sgu_bwd(proj, d_oa, wm, wmt, ng, nbias, bst)
    sg["sgu_w_s"] = dws
    sg["sgu_b_s"] = dbt.reshape(SGU_GROUPS, CHUNK)
    late_grads = {"A2T": dga2, "B2": dgb2.reshape(late["B2"].shape), "C": dgc, "E": dge}
    dq, dk, dv, state = bwd_sb(proj, tot, d_ob, late_grads)
    dproj = jnp.concatenate([dp_uv, dq.astype(BF), dk.astype(BF), dv.astype(BF)], axis=1)
    dgd = grad_cb(n1, dproj, dgd)
    r = mm_nt_cb([(dproj, gd, None)], D_MODEL, F32,
                 norm=dict(h=h1, gn=small["mix_pre_g"], d_res=dh2, f_prev=f1, gp_prev=small["ffn1_post_g"], alpha_prev=0.5))
    dh1, df1 = r["d_h"], r["d_f"]
    sg["mix_pre_g"], sg["ffn1_post_g"] = r["d_gn"], r["d_gp"]

    dg1, du1 = ffn_bwd_act(df1, gb1, 0, g1, u1)
    dgb1 = grad_rb(hid1, df1, dgb1, 0)
    mid_state, comm_b1, comm_d = ffn1_mid({"B1": dgb1.reshape(rest["B1"].shape), "D": dgd})
    res = grad_rb(dg1, a1, dga1, 0, comm=comm_b1)
    dga1, got_b1 = res if comm_b1 is not None else (res, ())
    res = grad_rb(du1, a1, dga1, 1, comm=comm_d)
    dga1, got_d = res if comm_d is not None else (res, ())
    r, da_state = ffn1_da([(dg1, ga1, 0), (du1, ga1, 1)], {"A1T": dga1},
                          dict(h=x, gn=small["ffn1_pre_g"], d_res=dh1))
    state1 = (mid_state, got_b1, got_d, da_state)
    grad_x = r["d_h"]
    sg["ffn1_pre_g"] = r["d_gn"]
    return loss_tile, grad_x, sg, state, state1


def mm_res_plain(a, w, which):
    m = a.shape[0]
    kb, n = w.shape[2], w.shape[3]
    tm = _row_tile(m, 1024)

    def body(a_ref, w_ref, o_ref):
        acc = None
        for k in range(N_CHIPS):
            t = _dot(a_ref[:, k * kb:(k + 1) * kb], w_ref[k])
            acc = t if acc is None else acc + t
        o_ref[...] = acc.astype(BF)

    return _call(
        body, name="mm_rb", out_shape=jax.ShapeDtypeStruct((m, n), BF), grid=(m // tm,),
        in_specs=[pl.BlockSpec((tm, N_CHIPS * kb), lambda i: (i, 0)),
                  pl.BlockSpec((N_CHIPS, None, kb, n), lambda i: (0, which, 0, 0))],
        out_specs=pl.BlockSpec((tm, n), lambda i: (i, 0)), compiler_params=_cp("parallel"),
    )(a, w)


_BIG = ("ffn1_w_gate", "ffn1_w_up", "ffn1_w_down", "w_in", "w_out", "xa_w_q", "xa_w_kv", "xa_w_o",
        "ffn2_w_gate", "ffn2_w_up", "ffn2_w_down")
_SMALL = ("ffn1_pre_g", "ffn1_post_g", "mix_pre_g", "mix_post_g", "sgu_norm_g", "sgu_norm_b", "sgu_w_s", "sgu_b_s",
          "sgu_out_g", "sb_out_g", "xa_pre_g", "xa_post_g", "mem_norm_g", "ffn2_pre_g", "ffn2_post_g", "final_norm_g")
_WEIGHTS = ("ffn1_pre_g", "ffn1_post_g", "ffn1_w_gate", "ffn1_w_up", "ffn1_w_down", "mix_pre_g", "mix_post_g", "w_in",
            "sgu_norm_g", "sgu_norm_b", "sgu_w_s", "sgu_b_s", "sgu_out_g", "sb_out_g", "w_out", "xa_pre_g", "xa_post_g",
            "mem_norm_g", "xa_w_q", "xa_w_kv", "xa_w_o", "ffn2_pre_g", "ffn2_post_g", "ffn2_w_gate", "ffn2_w_up",
            "ffn2_w_down", "final_norm_g")
_SLOT = {"ffn1_w_gate": ("A1T", 0, True), "ffn1_w_up": ("A1T", 1, True), "ffn2_w_gate": ("A2T", 0, True),
         "ffn2_w_up": ("A2T", 1, True), "ffn1_w_down": ("B1", None, False), "ffn2_w_down": ("B2", None, False),
         "w_out": ("C", 0, False), "xa_w_q": ("C", 1, False), "xa_w_o": ("C", 2, False), "w_in": ("D", None, False),
         "xa_w_kv": ("E", None, False)}


def _pack_small(vals):
    return jnp.concatenate([vals[nm].reshape(-1, LANES) for nm in _SMALL], axis=0)


def _unpack_small(packed, shapes):
    out, pos = {}, 0
    for nm in _SMALL:
        rows = math.prod(shapes[nm]) // LANES
        out[nm] = packed[pos:pos + rows].reshape(shapes[nm])
        pos += rows
    return out


def kernel(x, mem, ffn1_pre_g, ffn1_post_g, ffn1_w_gate, ffn1_w_up, ffn1_w_down, mix_pre_g, mix_post_g, w_in, sgu_norm_g, sgu_norm_b, sgu_w_s, sgu_b_s, sgu_out_g, sb_out_g, w_out, xa_pre_g, xa_post_g, mem_norm_g, xa_w_q, xa_w_kv, xa_w_o, ffn2_pre_g, ffn2_post_g, ffn2_w_gate, ffn2_w_up, ffn2_w_down, final_norm_g, loss_target, m_ffn1_pre_g, m_ffn1_post_g, m_ffn1_w_gate, m_ffn1_w_up, m_ffn1_w_down, m_mix_pre_g, m_mix_post_g, m_w_in, m_sgu_norm_g, m_sgu_norm_b, m_sgu_w_s, m_sgu_b_s, m_sgu_out_g, m_sb_out_g, m_w_out, m_xa_pre_g, m_xa_post_g, m_mem_norm_g, m_xa_w_q, m_xa_w_kv, m_xa_w_o, m_ffn2_pre_g, m_ffn2_post_g, m_ffn2_w_gate, m_ffn2_w_up, m_ffn2_w_down, m_final_norm_g, v_ffn1_pre_g, v_ffn1_post_g, v_ffn1_w_gate, v_ffn1_w_up, v_ffn1_w_down, v_mix_pre_g, v_mix_post_g, v_w_in, v_sgu_norm_g, v_sgu_norm_b, v_sgu_w_s, v_sgu_b_s, v_sgu_out_g, v_sb_out_g, v_w_out, v_xa_pre_g, v_xa_post_g, v_mem_norm_g, v_xa_w_q, v_xa_w_kv, v_xa_w_o, v_ffn2_pre_g, v_ffn2_post_g, v_ffn2_w_gate, v_ffn2_w_up, v_ffn2_w_down, v_final_norm_g):
    env = dict(locals())
    w = {nm: env[nm] for nm in _WEIGHTS}
    mom = {nm: env["m_" + nm] for nm in _WEIGHTS}
    vel = {nm: env["v_" + nm] for nm in _WEIGHTS}

    loc = {
        "A1": jnp.stack([w["ffn1_w_gate"][0], w["ffn1_w_up"][0]]).astype(BF),
        "A2": jnp.stack([w["ffn2_w_gate"][0], w["ffn2_w_up"][0]]).astype(BF),
        "B1": w["ffn1_w_down"][0].astype(BF),
        "B2": w["ffn2_w_down"][0].astype(BF),
        "C": jnp.stack([w["w_out"][0], w["xa_w_q"][0], w["xa_w_o"][0]]).astype(BF),
        "D": w["w_in"][0].astype(BF),
        "E": w["xa_w_kv"][0].astype(BF),
    }
    ga1 = gather_weights(loc, ("A1",))["A1"]
    c_idx = lax.axis_index("c").astype(jnp.int32).reshape(1)
    kc_idx = jnp.stack([2 * lax.axis_index("x") + lax.axis_index("y"), lax.axis_index("c")]).astype(jnp.int32)
    early_rest = tuple(nm for nm in SET_EARLY if nm != "A1")

    def ffn1_up(a1):
        (g1, u1, hid1), bufs = ffn_up(a1, ga1, 0, 1, comm=gather_comm(loc, early_rest))
        return g1, u1, hid1, gather_forward(bufs, early_rest)

    def fwd_sb(proj):
        (ob, tot), bufs = sb_fwd(proj, comm=gather_comm(loc, SET_LATE))
        return ob, tot, gather_forward(bufs, SET_LATE)

    def reduce_to_pairs(grads, names):
        from_sib = rs_to_sibling(grads, names)
        return {nm: add_halves(nm, grads[nm], from_sib[nm], c_idx) for nm in names}

    def bwd_sb(proj, tot, d_ob, late_grads):
        pairs = reduce_to_pairs(late_grads, GRAD_LATE)
        (dq, dk, dv), got = sb_bwd(proj, tot, d_ob, comm=rs_comm(pairs, GRAD_LATE))
        return dq, dk, dv, (pairs, dict(zip(GRAD_LATE, got)))

    def ffn1_mid(bd_grads):
        pairs = reduce_to_pairs(bd_grads, ("B1", "D"))
        return pairs, rs_comm(pairs, ("B1",)), rs_comm(pairs, ("D",))

    def ffn1_da(mm_pairs, gate_up_grads, norm):
        pairs = reduce_to_pairs(gate_up_grads, ("A1T",))
        da1, got = mm_nt_cb(mm_pairs, D_MODEL, F32, comm=rs_comm(pairs, ("A1T",)))
        return norm_bwd(norm["h"], norm["gn"], d_a=da1, d_res=norm["d_res"]), (pairs, got)

    small = {nm: w[nm][0] for nm in _SMALL}
    for nm in ("ffn1_pre_g", "ffn1_post_g", "mix_pre_g", "mix_post_g", "sgu_out_g", "sb_out_g", "xa_pre_g", "xa_post_g",
               "mem_norm_g", "ffn2_pre_g", "ffn2_post_g", "final_norm_g"):
        small[nm] = w[nm]
    loss_tile, grad_x, small_g, (late_pairs, late_got), (bd_pairs, got_b1, got_d, (a1_pairs, got_a1)) = local_step(
        x[0], mem[0], loss_target[0], ga1, small, ffn1_up, fwd_sb, bwd_sb, ffn1_mid, ffn1_da)
    pair_sum = {**late_pairs, **bd_pairs, **a1_pairs}
    from_chips = {**late_got, "B1": got_b1[0], "D": got_d[0], "A1T": got_a1[0]}

    small_shapes = {nm: w[nm].shape for nm in _SMALL}
    flat = _pack_small(small_g)
    n_small = flat.shape[0]
    pad = jnp.zeros((-n_small % 8, LANES), F32)
    packed = allreduce_small(jnp.concatenate([flat, pad, loss_tile], axis=0))
    loss = packed[-8, 0]
    g_small = _unpack_small(packed[:n_small], small_shapes)

    grad_groups = GRAD_EARLY + GRAD_LATE
    shard_shape = {nm: pair_sum[nm].shape[1:] for nm in grad_groups}
    for nm in grad_groups:
        axis, size = _HALF[nm]
        shard_shape[nm] = shard_shape[nm][:axis] + (2 * size,) + shard_shape[nm][axis + 1:]
    shard = rs_replicate({nm: add_chips(nm, pair_sum[nm], from_chips[nm], kc_idx, shard_shape[nm])
                          for nm in grad_groups}, grad_groups)

    grads, delta, new_m, new_v = {}, {}, {}, {}
    for nm in _BIG:
        grp, idx, transposed = _SLOT[nm]
        shape = w[nm].shape
        if transposed:
            res = adamw(w[nm][0].T, shard[grp], mom[nm][0].T, vel[nm][0].T, g_index=idx)
            res = [t.T for t in res]
        else:
            res = adamw(w[nm][0], shard[grp], mom[nm][0], vel[nm][0], g_index=idx)
        grads[nm], delta[nm], new_m[nm], new_v[nm] = (t.reshape(shape) for t in res)
    _, d, nm_, nv_ = adamw(_pack_small(w), _pack_small(g_small), _pack_small(mom), _pack_small(vel))
    d, nm_, nv_ = (_unpack_small(t, small_shapes) for t in (d, nm_, nv_))
    for nm in _SMALL:
        grads[nm], delta[nm], new_m[nm], new_v[nm] = g_small[nm], d[nm], nm_[nm], nv_[nm]

    return (loss, grad_x[None], *[grads[nm] for nm in _WEIGHTS], *[delta[nm] for nm in _WEIGHTS],
            *[new_m[nm] for nm in _WEIGHTS], *[new_v[nm] for nm in _WEIGHTS])
```
